```python
import jax, jax.numpy as jnp
from jax import lax
import numpy as np

D_MODEL = 1024
BATCH = 8
SEQ = 2048
DEPTH = 2
DEC_BATCH = 16
DEC_SEQ = 16
PAST_LEN = 4096

CHUNK = 64
N_HEADS = 8
HEAD_DIM = 64
ATTN_WIDTH = N_HEADS * HEAD_DIM
LEFT_CHUNKS = 8
ATTN_REACH = LEFT_CHUNKS * CHUNK
REL_CLIP = 256
ATTN_SCALE = HEAD_DIM ** -0.5
POOL_WIDTH = 512
POOL_WINDOWS = (2, 4, 8, 16)
N_POOL_GROUPS = len(POOL_WINDOWS)
POOL_GROUP = POOL_WIDTH // N_POOL_GROUPS
POOL_HIST = max(POOL_WINDOWS) - 1
D_FF = 2816
N_EXPERTS = 8
TOP_K = 2
D_FF_EXPERT = 2816
N_DENSE = (DEPTH + 1) // 2
N_MOE = DEPTH // 2
IN_WIDTH = POOL_WIDTH + 3 * ATTN_WIDTH + 2 * D_MODEL
DN_ALPHA = (2 * DEPTH) ** 0.25
DN_BETA = (8 * DEPTH) ** -0.25
LN_EPS = 1e-5
NEG_INF = -1e30

kernel_name = "gated_pool_chunkattn_deepnorm_step"


def layer_norm(x, g, b):
    xf = x.astype(jnp.float32)
    mu = jnp.mean(xf, axis=-1, keepdims=True)
    var = jnp.mean(jnp.square(xf - mu), axis=-1, keepdims=True)
    return ((xf - mu) * lax.rsqrt(var + LN_EPS) * g.astype(jnp.float32) + b.astype(jnp.float32)).astype(x.dtype)


def rel_bias(table, qpos, kpos):
    idx = jnp.clip(qpos[:, None] - kpos[None, :], -REL_CLIP, REL_CLIP) + REL_CLIP
    return table[:, idx].astype(jnp.float32)


def in_proj(x, w_in, b_in):
    B, T, _ = x.shape
    z = x @ w_in + b_in
    o1 = POOL_WIDTH
    o2 = o1 + ATTN_WIDTH
    o3 = o2 + ATTN_WIDTH
    o4 = o3 + ATTN_WIDTH
    o5 = o4 + D_MODEL
    heads = lambda t: t.reshape(B, T, N_HEADS, HEAD_DIM)
    return (z[..., :o1], heads(z[..., o1:o2]), heads(z[..., o2:o3]), heads(z[..., o3:o4]),
            z[..., o4:o5], z[..., o5:])


def pool_mix(v_all, pos0, w_grp, scale):
    B, L, P = v_all.shape
    T = L - POOL_HIST
    vf = v_all.astype(jnp.float32)
    cz = jnp.concatenate([jnp.zeros((B, 1, P), jnp.float32), jnp.cumsum(vf, axis=1)], axis=1)
    end = cz[:, POOL_HIST + 1:]
    pos = pos0 + jnp.arange(T)
    outs = []
    for g, w in enumerate(POOL_WINDOWS):
        sl = slice(g * POOL_GROUP, (g + 1) * POOL_GROUP)
        start = cz[:, POOL_HIST + 1 - w:POOL_HIST + 1 - w + T, sl]
        cnt = jnp.minimum(pos + 1, w).astype(jnp.float32)[None, :, None]
        outs.append((end[..., sl] - start) / cnt)
    pooled = (jnp.concatenate(outs, axis=-1) - vf[:, POOL_HIST:]).astype(v_all.dtype)
    y = jnp.einsum('btgc,gcd->btgd', pooled.reshape(B, T, N_POOL_GROUPS, POOL_GROUP), w_grp)
    return y.reshape(B, T, P) * scale


def band_attention(q, k, v, table):
    B, T, H, Dh = q.shape
    nc = T // CHUNK
    band = (LEFT_CHUNKS + 1) * CHUNK

    def bands(t):
        tc = t.reshape(B, nc, CHUNK, H, Dh)
        tp = jnp.concatenate([jnp.zeros((B, LEFT_CHUNKS, CHUNK, H, Dh), t.dtype), tc], axis=1)
        return jnp.concatenate([tp[:, i:i + nc] for i in range(LEFT_CHUNKS + 1)], axis=2)

    kb, vb = bands(k), bands(v)
    qc = q.reshape(B, nc, CHUNK, H, Dh)
    s = jnp.einsum('bnqhd,bnkhd->bnhqk', qc, kb).astype(jnp.float32) * ATTN_SCALE
    kk = jnp.arange(band)
    bias = rel_bias(table, jnp.arange(CHUNK) + ATTN_REACH, kk)
    valid = (jnp.arange(nc)[:, None] * CHUNK - ATTN_REACH + kk[None, :]) >= 0
    s = jnp.where(valid[None, :, None, None, :], s + bias[None, None], NEG_INF)
    p = jax.nn.softmax(s, axis=-1).astype(v.dtype)
    o = jnp.einsum('bnhqk,bnkhd->bnqhd', p, vb)
    return o.reshape(B, T, H * Dh)


def cached_attention(q, k_all, v_all, table, pos0):
    B, T, H, Dh = q.shape
    L = k_all.shape[1]
    qpos = pos0 + jnp.arange(T)
    kpos = pos0 + T - L + jnp.arange(L)
    s = jnp.einsum('bqhd,bkhd->bhqk', q, k_all).astype(jnp.float32) * ATTN_SCALE
    s = s + rel_bias(table, qpos, kpos)[None]
    p = jax.nn.softmax(s, axis=-1).astype(v_all.dtype)
    o = jnp.einsum('bhqk,bkhd->bqhd', p, v_all)
    return o.reshape(B, T, H * Dh)


def merge_branches(pool_y, attn_y, gp, ga, w_pool_br, w_attn_br, w_out):
    m = jax.nn.sigmoid(gp) * (pool_y @ w_pool_br) + jax.nn.sigmoid(ga) * (attn_y @ w_attn_br)
    return m @ w_out


def swiglu(x, w1, w3, w2):
    return (jax.nn.silu(x @ w1) * (x @ w3)) @ w2


def moe_ffn(x, w_r, b_r, w1, w3, w2):
    B, T, D = x.shape
    xt = x.reshape(B * T, D)
    logits = (xt @ w_r).astype(jnp.float32) + b_r.astype(jnp.float32)
    vals, idx = lax.top_k(logits, TOP_K)
    wts = jax.nn.softmax(vals, axis=-1)
    gate = jnp.sum(jax.nn.one_hot(idx, N_EXPERTS, dtype=jnp.float32) * wts[..., None], axis=1)
    y = jnp.zeros_like(xt)
    for e in range(N_EXPERTS):
        y = y + gate[:, e:e + 1].astype(xt.dtype) * swiglu(xt, w1[e], w3[e], w2[e])
    return y.reshape(B, T, D)


def setup_inputs(seed: int = 0) -> dict:
    key = jax.random.key(seed)
    ks = jax.random.split(key, 26)

    def nrm(k, shape, scale):
        return jax.random.normal(k, shape, jnp.float32) * scale

    a_buf = min(ATTN_REACH, PAST_LEN)
    return {
        'x_prompt': nrm(ks[0], (BATCH, SEQ, D_MODEL), 1.0),
        'x_sample': nrm(ks[1], (DEC_BATCH, DEC_SEQ, D_MODEL), 1.0),
        'cache_k': nrm(ks[2], (DEPTH, DEC_BATCH, a_buf, N_HEADS, HEAD_DIM), 1.0),
        'cache_v': nrm(ks[3], (DEPTH, DEC_BATCH, a_buf, N_HEADS, HEAD_DIM), 1.0),
        'state_pool': nrm(ks[4], (DEPTH, DEC_BATCH, POOL_HIST, POOL_WIDTH), 1.0),
        'w_in': nrm(ks[5], (DEPTH, D_MODEL, IN_WIDTH), D_MODEL ** -0.5),
        'b_in': nrm(ks[6], (DEPTH, IN_WIDTH), 0.02),
        'w_pool_grp': nrm(ks[7], (DEPTH, N_POOL_GROUPS, POOL_GROUP, POOL_GROUP), POOL_GROUP ** -0.5),
        'pool_scale': 1.0 + nrm(ks[8], (DEPTH, POOL_WIDTH), 0.1),
        'rel_table': nrm(ks[9], (DEPTH, N_HEADS, 2 * REL_CLIP + 1), 0.5),
        'w_pool_br': nrm(ks[10], (DEPTH, POOL_WIDTH, D_MODEL), POOL_WIDTH ** -0.5),
        'w_attn_br': nrm(ks[11], (DEPTH, ATTN_WIDTH, D_MODEL), ATTN_WIDTH ** -0.5),
        'w_out': nrm(ks[12], (DEPTH, D_MODEL, D_MODEL), DN_BETA * D_MODEL ** -0.5),
        'ln1_g': 1.0 + nrm(ks[13], (DEPTH, D_MODEL), 0.05),
        'ln1_b': nrm(ks[14], (DEPTH, D_MODEL), 0.02),
        'ln2_g': 1.0 + nrm(ks[15], (DEPTH, D_MODEL), 0.05),
        'ln2_b': nrm(ks[16], (DEPTH, D_MODEL), 0.02),
        'w1_dense': nrm(ks[17], (N_DENSE, D_MODEL, D_FF), D_MODEL ** -0.5),
        'w3_dense': nrm(ks[18], (N_DENSE, D_MODEL, D_FF), D_MODEL ** -0.5),
        'w2_dense': nrm(ks[19], (N_DENSE, D_FF, D_MODEL), DN_BETA * D_FF ** -0.5),
        'w_router': nrm(ks[20], (N_MOE, D_MODEL, N_EXPERTS), D_MODEL ** -0.5),
        'b_router': nrm(ks[21], (N_MOE, N_EXPERTS), 0.01),
        'w1_exp': nrm(ks[22], (N_MOE, N_EXPERTS, D_MODEL, D_FF_EXPERT), D_MODEL ** -0.5),
        'w3_exp': nrm(ks[23], (N_MOE, N_EXPERTS, D_MODEL, D_FF_EXPERT), D_MODEL ** -0.5),
        'w2_exp': nrm(ks[24], (N_MOE, N_EXPERTS, D_FF_EXPERT, D_MODEL), DN_BETA * D_FF_EXPERT ** -0.5),
    }


def reference(x_prompt, x_sample, cache_k, cache_v, state_pool, w_in, b_in, w_pool_grp, pool_scale,
              rel_table, w_pool_br, w_attn_br, w_out, ln1_g, ln1_b, ln2_g, ln2_b,
              w1_dense, w3_dense, w2_dense, w_router, b_router, w1_exp, w3_exp, w2_exp):
    xp, xs = x_prompt, x_sample
    Bp, Tp, _ = xp.shape
    keep_p = min(ATTN_REACH, Tp)
    keep_s = cache_k.shape[2]
    hist0 = jnp.zeros((Bp, POOL_HIST, POOL_WIDTH), xp.dtype)
    kp_new, vp_new, pp_new, ks_new, vs_new, ps_new = [], [], [], [], [], []
    for l in range(DEPTH):
        u, q, k, v, gp, ga = in_proj(xp, w_in[l], b_in[l])
        pool_y = pool_mix(jnp.concatenate([hist0, u], axis=1), 0, w_pool_grp[l], pool_scale[l])
        attn_y = band_attention(q, k, v, rel_table[l])
        mix_p = merge_branches(pool_y, attn_y, gp, ga, w_pool_br[l], w_attn_br[l], w_out[l])
        kp_new.append(k[:, Tp - keep_p:])
        vp_new.append(v[:, Tp - keep_p:])
        pp_new.append(u[:, Tp - POOL_HIST:])
        u_s, q_s, k_s, v_s, gp_s, ga_s = in_proj(xs, w_in[l], b_in[l])
        u_all = jnp.concatenate([state_pool[l], u_s], axis=1)
        pool_ys = pool_mix(u_all, PAST_LEN, w_pool_grp[l], pool_scale[l])
        k_all = jnp.concatenate([cache_k[l], k_s], axis=1)
        v_all = jnp.concatenate([cache_v[l], v_s], axis=1)
        attn_ys = cached_attention(q_s, k_all, v_all, rel_table[l], PAST_LEN)
        mix_s = merge_branches(pool_ys, attn_ys, gp_s, ga_s, w_pool_br[l], w_attn_br[l], w_out[l])
        ks_new.append(k_all[:, -keep_s:])
        vs_new.append(v_all[:, -keep_s:])
        ps_new.append(u_all[:, -POOL_HIST:])
        xp = layer_norm(DN_ALPHA * xp + mix_p, ln1_g[l], ln1_b[l])
        xs = layer_norm(DN_ALPHA * xs + mix_s, ln1_g[l], ln1_b[l])
        j = l // 2
        if l % 2 == 0:
            fp = swiglu(xp, w1_dense[j], w3_dense[j], w2_dense[j])
            fs = swiglu(xs, w1_dense[j], w3_dense[j], w2_dense[j])
        else:
            fp = moe_ffn(xp, w_router[j], b_router[j], w1_exp[j], w3_exp[j], w2_exp[j])
            fs = moe_ffn(xs, w_router[j], b_router[j], w1_exp[j], w3_exp[j], w2_exp[j])
        xp = layer_norm(DN_ALPHA * xp + fp, ln2_g[l], ln2_b[l])
        xs = layer_norm(DN_ALPHA * xs + fs, ln2_g[l], ln2_b[l])
    return (xp, xs, jnp.stack(kp_new), jnp.stack(vp_new), jnp.stack(pp_new),
            jnp.stack(ks_new), jnp.stack(vs_new), jnp.stack(ps_new))
```

```python
import functools

import jax
import jax.numpy as jnp
from jax import lax
from jax.experimental import pallas as pl
from jax.experimental.pallas import tpu as pltpu

F32 = jnp.float32
BF16 = jnp.bfloat16

D_MODEL = 1024
N_HEADS = 8
HEAD_DIM = 64
ATTN_WIDTH = N_HEADS * HEAD_DIM
CHUNK = 64
LEFT_CHUNKS = 8
BAND = (LEFT_CHUNKS + 1) * CHUNK
ATTN_REACH = LEFT_CHUNKS * CHUNK
REL_CLIP = 256
ATTN_SCALE = HEAD_DIM ** -0.5
POOL_WIDTH = 512
POOL_WINDOWS = (2, 4, 8, 16)
POOL_GROUP = POOL_WIDTH // len(POOL_WINDOWS)
POOL_HIST = max(POOL_WINDOWS) - 1
HIST_ROWS = POOL_HIST + 1
D_FF = 2816
N_EXPERTS = 8
PAST_LEN = 4096
DEPTH = 2
DN_ALPHA = (2 * DEPTH) ** 0.25
LN_EPS = 1e-5
NEG_INF = -1e30
IN_WIDTH = POOL_WIDTH + 3 * ATTN_WIDTH + 2 * D_MODEL
HEAD_PAIRS = N_HEADS // 2
PAIR_W = 2 * HEAD_DIM
LANES = 128

COL_U, COL_Q, COL_K, COL_V = 0, 1, 2, 3

EXPERT_TILE = 512
FF_SPLIT = 2
VMEM_LIMIT = 56 * 1024 * 1024


def _pick(n, candidates):
    for c in candidates:
        if n % c == 0:
            return c
    raise ValueError(f"no tile in {candidates} divides {n}")


def _params(sem, vmem=None):
    return pltpu.CompilerParams(dimension_semantics=sem, vmem_limit_bytes=vmem or VMEM_LIMIT)


def _layer_norm(r, g, b):
    mu = jnp.mean(r, axis=-1, keepdims=True)
    c = r - mu
    var = jnp.mean(c * c, axis=-1, keepdims=True)
    return c * lax.rsqrt(var + LN_EPS) * g + b


def _inproj_body(x_ref, w_ref, b_ref, z_ref):
    x = x_ref[...].astype(BF16)
    z_ref[...] = jnp.dot(x, w_ref[...], preferred_element_type=F32) + b_ref[...]


def _in_proj(x, w_bf, b):
    n = x.shape[0]
    tm = _pick(n, (1280, 640, 256))
    tn = 1024
    return pl.pallas_call(
        _inproj_body,
        out_shape=jax.ShapeDtypeStruct((n, IN_WIDTH), F32),
        grid=(n // tm, IN_WIDTH // tn),
        in_specs=[
            pl.BlockSpec((tm, D_MODEL), lambda i, j: (i, 0)),
            pl.BlockSpec((D_MODEL, tn), lambda i, j: (0, j)),
            pl.BlockSpec((1, tn), lambda i, j: (0, j)),
        ],
        out_specs=pl.BlockSpec((tm, tn), lambda i, j: (i, j)),
        compiler_params=_params(("arbitrary", "arbitrary")),
        name="in_proj",
    )(x, w_bf, b)


def _pair_scores(q_pair, k_pair, bias):
    lane = lax.broadcasted_iota(jnp.int32, q_pair.shape, 1)
    qs = q_pair * ATTN_SCALE
    q2 = jnp.concatenate([jnp.where(lane < HEAD_DIM, qs, 0.0), jnp.where(lane >= HEAD_DIM, qs, 0.0)], axis=0)
    s = lax.dot_general(q2.astype(BF16), k_pair, (((1,), (1,)), ((), ())), preferred_element_type=F32)
    return s + bias


def _pair_output(s, v_pair):
    rows = s.shape[0] // 2
    m = jnp.max(s, axis=-1, keepdims=True)
    e = jnp.exp(s - m)
    l = jnp.sum(e, axis=-1, keepdims=True)
    o2 = jnp.dot(e.astype(BF16), v_pair, preferred_element_type=F32) / l
    lane = lax.broadcasted_iota(jnp.int32, (rows, PAIR_W), 1)
    return jnp.where(lane < HEAD_DIM, o2[:rows], o2[rows:])


def _attn_prompt_body(q_ref, kp_ref, kc_ref, vp_ref, vc_ref, bias_ref, o_ref, kext_ref, vext_ref):
    blk = q_ref.shape[0]
    j = pl.program_id(1)
    kext_ref[0:blk, :] = kp_ref[...].astype(BF16)
    kext_ref[blk:2 * blk, :] = kc_ref[...].astype(BF16)
    vext_ref[0:blk, :] = vp_ref[...].astype(BF16)
    vext_ref[blk:2 * blk, :] = vc_ref[...].astype(BF16)
    col = lax.broadcasted_iota(jnp.int32, (2 * CHUNK, BAND), 1)

    def chunk(c, carry):
        q0 = pl.multiple_of(c * CHUNK, CHUNK)
        first_valid = jnp.where(j == 0, (LEFT_CHUNKS - c) * CHUNK, 0)
        for hp in range(HEAD_PAIRS):
            lanes = slice(hp * PAIR_W, (hp + 1) * PAIR_W)
            q_pair = q_ref[pl.ds(q0, CHUNK), lanes]
            k_pair = kext_ref[pl.ds(q0, BAND), lanes]
            v_pair = vext_ref[pl.ds(q0, BAND), lanes]
            s = _pair_scores(q_pair, k_pair, bias_ref[hp])
            s = jnp.where(col >= first_valid, s, NEG_INF)
            o_ref[pl.ds(q0, CHUNK), lanes] = _pair_output(s, v_pair)
        return carry

    lax.fori_loop(0, blk // CHUNK, chunk, 0)


def _attn_prompt(z, bias_pairs, batch, seq):
    blk = ATTN_REACH
    per_seq = seq // blk
    rows = batch * seq

    def cur(col):
        return lambda b, j: (b * per_seq + j, col)

    def prev(col):
        return lambda b, j: (b * per_seq + jnp.maximum(j - 1, 0), col)

    return pl.pallas_call(
        _attn_prompt_body,
        out_shape=jax.ShapeDtypeStruct((rows, ATTN_WIDTH), F32),
        grid=(batch, per_seq),
        in_specs=[
            pl.BlockSpec((blk, ATTN_WIDTH), cur(COL_Q)),
            pl.BlockSpec((blk, ATTN_WIDTH), prev(COL_K)),
            pl.BlockSpec((blk, ATTN_WIDTH), cur(COL_K)),
            pl.BlockSpec((blk, ATTN_WIDTH), prev(COL_V)),
            pl.BlockSpec((blk, ATTN_WIDTH), cur(COL_V)),
            pl.BlockSpec((HEAD_PAIRS, 2 * CHUNK, BAND), lambda b, j: (0, 0, 0)),
        ],
        out_specs=pl.BlockSpec((blk, ATTN_WIDTH), lambda b, j: (b * per_seq + j, 0)),
        scratch_shapes=[pltpu.VMEM((2 * blk, ATTN_WIDTH), BF16), pltpu.VMEM((2 * blk, ATTN_WIDTH), BF16)],
        compiler_params=_params(("arbitrary", "arbitrary")),
        name="attn_prompt",
    )(z, z, z, z, z, bias_pairs)


def _attn_sample_body(q_ref, kn_ref, vn_ref, ck_ref, cv_ref, bias_ref, *rest):
    o_ref, ko_ref, vo_ref, kall_ref, vall_ref = rest[-5:]
    keep = ck_ref.shape[2]
    t = q_ref.shape[0]
    ck = ck_ref[0, 0]
    cv = cv_ref[0, 0]
    kn = kn_ref[...]
    vn = vn_ref[...]
    kall_ref[0:keep, :] = ck.astype(BF16)
    kall_ref[keep:keep + t, :] = kn.astype(BF16)
    vall_ref[0:keep, :] = cv.astype(BF16)
    vall_ref[keep:keep + t, :] = vn.astype(BF16)
    for hp in range(HEAD_PAIRS):
        lanes = slice(hp * PAIR_W, (hp + 1) * PAIR_W)
        s = _pair_scores(q_ref[:, lanes], kall_ref[:, lanes], bias_ref[hp])
        o_ref[:, lanes] = _pair_output(s, vall_ref[:, lanes])
    ko_ref[0, 0, 0:keep - t, :] = ck[t:keep]
    ko_ref[0, 0, keep - t:keep, :] = kn
    vo_ref[0, 0, 0:keep - t, :] = cv[t:keep]
    vo_ref[0, 0, keep - t:keep, :] = vn
    for later in range(1, ko_ref.shape[0]):
        ko_ref[later] = jnp.zeros(ko_ref.shape[1:], F32)
        vo_ref[later] = jnp.zeros(vo_ref.shape[1:], F32)


def _attn_sample(z, cache_k, cache_v, bias_pairs, layer, row0, streams, t, k_buf, v_buf):
    depth, _, keep, _ = cache_k.shape
    blk0 = row0 // t
    ins = [z, z, z, cache_k, cache_v, bias_pairs]
    in_specs = [
        pl.BlockSpec((t, ATTN_WIDTH), lambda s: (blk0 + s, COL_Q)),
        pl.BlockSpec((t, ATTN_WIDTH), lambda s: (blk0 + s, COL_K)),
        pl.BlockSpec((t, ATTN_WIDTH), lambda s: (blk0 + s, COL_V)),
        pl.BlockSpec((1, 1, keep, ATTN_WIDTH), lambda s: (layer, s, 0, 0)),
        pl.BlockSpec((1, 1, keep, ATTN_WIDTH), lambda s: (layer, s, 0, 0)),
        pl.BlockSpec((HEAD_PAIRS, 2 * t, keep + t), lambda s: (0, 0, 0)),
    ]
    aliases = {}
    if k_buf is None:
        assert layer == 0
        buf_spec = pl.BlockSpec((depth, 1, keep, ATTN_WIDTH), lambda s: (0, s, 0, 0))
    else:
        aliases = {len(ins): 1, len(ins) + 1: 2}
        ins += [k_buf, v_buf]
        in_specs += [pl.BlockSpec(memory_space=pl.ANY), pl.BlockSpec(memory_space=pl.ANY)]
        buf_spec = pl.BlockSpec((1, 1, keep, ATTN_WIDTH), lambda s: (layer, s, 0, 0))
    buf = jax.ShapeDtypeStruct(cache_k.shape, F32)
    return pl.pallas_call(
        _attn_sample_body,
        out_shape=(jax.ShapeDtypeStruct((streams * t, ATTN_WIDTH), F32), buf, buf),
        grid=(streams,),
        in_specs=in_specs,
        out_specs=(
            pl.BlockSpec((t, ATTN_WIDTH), lambda s: (s, 0)),
            buf_spec,
            buf_spec,
        ),
        scratch_shapes=[pltpu.VMEM((keep + t, ATTN_WIDTH), BF16), pltpu.VMEM((keep + t, ATTN_WIDTH), BF16)],
        input_output_aliases=aliases,
        compiler_params=_params(("arbitrary",)),
        name="attn_sample",
    )(*ins)


def _merge(pooled, u, attn, gp, ga, x, wgrp_ref, scale_ref, wp_ref, wa_ref, wo_ref, g_ref, b_ref):
    pooled = pooled - u
    pool_y = jnp.dot(pooled.astype(BF16), wgrp_ref[...], preferred_element_type=F32) * scale_ref[...]
    mp = jnp.dot(pool_y.astype(BF16), wp_ref[...], preferred_element_type=F32)
    ma = jnp.dot(attn.astype(BF16), wa_ref[...], preferred_element_type=F32)
    m = jax.nn.sigmoid(gp) * mp + jax.nn.sigmoid(ga) * ma
    y = jnp.dot(m.astype(BF16), wo_ref[...], preferred_element_type=F32)
    return _layer_norm(DN_ALPHA * x + y, g_ref[...], b_ref[...])


def _window_means(read, pos, shape_out):
    outs = []
    for g, w in enumerate(POOL_WINDOWS):
        lanes = slice(g * POOL_GROUP, (g + 1) * POOL_GROUP)
        s = read(0, lanes)
        for back in range(1, w):
            s = s + read(back, lanes)
        outs.append((s / jnp.minimum(pos + 1, w).astype(F32)).reshape(shape_out))
    return jnp.concatenate(outs, axis=-1)


def _mix_body(n_prompt_tiles, tiles_per_seq, u_ref, up_ref, hist_ref, attn_p_ref, attn_s_ref, gp_ref, ga_ref, x_ref,
              wgrp_ref, scale_ref, wp_ref, wa_ref, wo_ref, g_ref, b_ref, o_ref,
              buf_ref, sbuf_ref, pooled_ref, attn_ref):
    t = u_ref.shape[0]
    i = pl.program_id(0)
    u = u_ref[...]

    @pl.when(i < n_prompt_tiles)
    def _():
        tile = i % tiles_per_seq
        buf_ref[0:HIST_ROWS, :] = jnp.where(tile == 0, 0.0, up_ref[...])
        buf_ref[HIST_ROWS:HIST_ROWS + t, :] = u
        pos = tile * t + lax.broadcasted_iota(jnp.int32, (t, 1), 0)
        read = lambda back, lanes: buf_ref[HIST_ROWS - back:HIST_ROWS - back + t, lanes]
        pooled_ref[...] = _window_means(read, pos, (t, POOL_GROUP))
        attn_ref[...] = attn_p_ref[...]

    @pl.when(i == n_prompt_tiles)
    def _():
        streams, hrows, _ = hist_ref.shape
        ts = t // streams
        sbuf_ref[:, 0:hrows, :] = hist_ref[...]
        sbuf_ref[:, hrows:hrows + ts, :] = u.reshape(streams, ts, POOL_WIDTH)
        pos = PAST_LEN + lax.broadcasted_iota(jnp.int32, (1, ts, 1), 1)
        read = lambda back, lanes: sbuf_ref[:, hrows - back:hrows - back + ts, lanes]
        pooled_ref[...] = _window_means(read, pos, (t, POOL_GROUP))
        attn_ref[...] = attn_s_ref[...]

    o_ref[...] = _merge(pooled_ref[...], u, attn_ref[...], gp_ref[...], ga_ref[...], x_ref[...],
                        wgrp_ref, scale_ref, wp_ref, wa_ref, wo_ref, g_ref, b_ref)


def _mix(z, attn_p, attn_s, x, hist, weights, n_prompt, seq):
    n = x.shape[0]
    t = n - n_prompt
    n_prompt_tiles = n_prompt // t
    streams, hrows, _ = hist.shape
    hist_per_tile = t // HIST_ROWS
    last_p = n_prompt_tiles - 1
    zero = lambda i: (0, 0)
    return pl.pallas_call(
        functools.partial(_mix_body, n_prompt_tiles, seq // t),
        out_shape=jax.ShapeDtypeStruct((n, D_MODEL), F32),
        grid=(n_prompt_tiles + 1,),
        in_specs=[
            pl.BlockSpec((t, POOL_WIDTH), lambda i: (i, COL_U)),
            pl.BlockSpec((HIST_ROWS, POOL_WIDTH), lambda i: (jnp.maximum(i * hist_per_tile - 1, 0), COL_U)),
            pl.BlockSpec((streams, hrows, POOL_WIDTH), lambda i: (0, 0, 0)),
            pl.BlockSpec((t, ATTN_WIDTH), lambda i: (jnp.minimum(i, last_p), 0)),
            pl.BlockSpec((t, ATTN_WIDTH), zero),
            pl.BlockSpec((t, D_MODEL), lambda i: (i, 2)),
            pl.BlockSpec((t, D_MODEL), lambda i: (i, 3)),
            pl.BlockSpec((t, D_MODEL), lambda i: (i, 0)),
            pl.BlockSpec((POOL_WIDTH, POOL_WIDTH), zero),
            pl.BlockSpec((1, POOL_WIDTH), zero),
            pl.BlockSpec((POOL_WIDTH, D_MODEL), zero),
            pl.BlockSpec((ATTN_WIDTH, D_MODEL), zero),
            pl.BlockSpec((D_MODEL, D_MODEL), zero),
            pl.BlockSpec((1, D_MODEL), zero),
            pl.BlockSpec((1, D_MODEL), zero),
        ],
        out_specs=pl.BlockSpec((t, D_MODEL), lambda i: (i, 0)),
        scratch_shapes=[
            pltpu.VMEM((HIST_ROWS + t, POOL_WIDTH), F32),
            pltpu.VMEM((streams, hrows + t // streams, POOL_WIDTH), F32),
            pltpu.VMEM((t, POOL_WIDTH), F32),
            pltpu.VMEM((t, ATTN_WIDTH), F32),
        ],
        compiler_params=_params(("arbitrary",)),
        name="mix",
    )(z, z, hist, attn_p, attn_s, z, z, x, *weights)


def _ffn_dense_body(x_ref, w1_ref, w3_ref, w2_ref, g_ref, b_ref, o_ref):
    x = x_ref[...]
    xb = x.astype(BF16)
    a = jnp.dot(xb, w1_ref[...], preferred_element_type=F32)
    c = jnp.dot(xb, w3_ref[...], preferred_element_type=F32)
    h = (jax.nn.silu(a) * c).astype(BF16)
    f = jnp.dot(h, w2_ref[...], preferred_element_type=F32)
    o_ref[...] = _layer_norm(DN_ALPHA * x + f, g_ref[...], b_ref[...])


def _ffn_dense(x, w1, w3, w2, g, b):
    n = x.shape[0]
    t = 256
    zero = lambda i: (0, 0)
    return pl.pallas_call(
        _ffn_dense_body,
        out_shape=jax.ShapeDtypeStruct((n, D_MODEL), F32),
        grid=(n // t,),
        in_specs=[
            pl.BlockSpec((t, D_MODEL), lambda i: (i, 0)),
            pl.BlockSpec((D_MODEL, D_FF), zero),
            pl.BlockSpec((D_MODEL, D_FF), zero),
            pl.BlockSpec((D_FF, D_MODEL), zero),
            pl.BlockSpec((1, D_MODEL), zero),
            pl.BlockSpec((1, D_MODEL), zero),
        ],
        out_specs=pl.BlockSpec((t, D_MODEL), lambda i: (i, 0)),
        compiler_params=_params(("arbitrary",)),
        name="ffn_dense",
    )(x, w1, w3, w2, g, b)


def _router_body(x_ref, wr_ref, br_ref, mi_ref, mw_ref, cnt_ref, carry_ref):
    t = x_ref.shape[0]
    i = pl.program_id(0)

    @pl.when(i == 0)
    def _():
        carry_ref[...] = jnp.zeros_like(carry_ref)

    logits = jnp.dot(x_ref[...], wr_ref[...], preferred_element_type=F32,
                     precision=lax.Precision.HIGHEST) + br_ref[...]
    lane = lax.broadcasted_iota(jnp.int32, (t, LANES), 1)
    logits = jnp.where(lane < N_EXPERTS, logits, -jnp.inf)
    lane_f = lane.astype(F32)
    v0 = jnp.max(logits, axis=-1, keepdims=True)
    e0 = jnp.min(jnp.where(logits == v0, lane_f, float(LANES)), axis=-1, keepdims=True)
    rest = jnp.where(lane_f == e0, -jnp.inf, logits)
    v1 = jnp.max(rest, axis=-1, keepdims=True)
    e1 = jnp.min(jnp.where(rest == v1, lane_f, float(LANES)), axis=-1, keepdims=True)
    ex = jnp.exp(v1 - v0)
    w0 = 1.0 / (1.0 + ex)
    w1 = ex / (1.0 + ex)
    oh0 = (lane_f == e0).astype(F32)
    oh1 = (lane_f == e1).astype(F32)
    r_i = lax.broadcasted_iota(jnp.int32, (t, t), 0)
    c_i = lax.broadcasted_iota(jnp.int32, (t, t), 1)
    tri = (c_i < r_i).astype(BF16)
    pre0 = jnp.dot(tri, oh0.astype(BF16), preferred_element_type=F32)
    pre1 = jnp.dot(tri, oh1.astype(BF16), preferred_element_type=F32)
    cnt0 = jnp.sum(oh0, axis=0, keepdims=True)
    cnt1 = jnp.sum(oh1, axis=0, keepdims=True)
    carry = carry_ref[...]
    rank0 = jnp.sum(oh0 * (carry + pre0), axis=-1, keepdims=True)
    rank1 = jnp.sum(oh1 * (carry + cnt0 + pre1), axis=-1, keepdims=True)
    carry = carry + cnt0 + cnt1
    carry_ref[...] = carry
    cnt_ref[...] = carry.astype(jnp.int32)
    mi = jnp.where(lane == 0, e0, jnp.where(lane == 1, e1, 0.0))
    mi = jnp.where(lane == 2, rank0, jnp.where(lane == 3, rank1, mi))
    mi_ref[...] = mi.astype(jnp.int32)
    mw_ref[...] = jnp.where(lane == 0, w0, jnp.where(lane == 1, w1, 0.0))


def _router(x, w_r, b_r):
    n = x.shape[0]
    t = 256
    wr = jnp.zeros((D_MODEL, LANES), F32).at[:, :N_EXPERTS].set(w_r)
    br = jnp.zeros((1, LANES), F32).at[:, :N_EXPERTS].set(b_r[None, :])
    zero = lambda i: (0, 0)
    return pl.pallas_call(
        _router_body,
        out_shape=(jax.ShapeDtypeStruct((n, LANES), jnp.int32),
                   jax.ShapeDtypeStruct((n, LANES), F32),
                   jax.ShapeDtypeStruct((1, LANES), jnp.int32)),
        grid=(n // t,),
        in_specs=[
            pl.BlockSpec((t, D_MODEL), lambda i: (i, 0)),
            pl.BlockSpec((D_MODEL, LANES), zero),
            pl.BlockSpec((1, LANES), zero),
        ],
        out_specs=(pl.BlockSpec((t, LANES), lambda i: (i, 0)),
                   pl.BlockSpec((t, LANES), lambda i: (i, 0)),
                   pl.BlockSpec((1, LANES), zero)),
        scratch_shapes=[pltpu.VMEM((1, LANES), F32)],
        compiler_params=_params(("arbitrary",)),
        name="router",
    )(x, wr, br)


def _dispatch_body(dest_ref, last_ref, nused_ref, x_ref, xs_ref, zero_ref, sem_ref):
    t = x_ref.shape[0]
    tm = zero_ref.shape[0]
    n_tiles = xs_ref.shape[0] // tm
    i = pl.program_id(0)

    @pl.when(i == 0)
    def _():
        zero_ref[...] = jnp.zeros_like(zero_ref)

        def fill(tile):
            cp = pltpu.make_async_copy(zero_ref, xs_ref.at[pl.ds(pl.multiple_of(tile * tm, tm), tm), :], sem_ref.at[1])
            cp.start()
            cp.wait()

        for e in range(N_EXPERTS):
            @pl.when(last_ref[e] >= 0)
            def _():
                fill(last_ref[e])

        def unused(tile, carry):
            fill(tile)
            return carry

        lax.fori_loop(nused_ref[0], n_tiles, unused, 0)

    def issue(r, carry):
        for slot in range(2):
            d = dest_ref[2 * r + slot]
            pltpu.make_async_copy(x_ref.at[pl.ds(r, 1), :], xs_ref.at[pl.ds(d, 1), :], sem_ref.at[0]).start()
        return carry

    lax.fori_loop(0, t, issue, 0)
    pltpu.make_async_copy(xs_ref.at[pl.ds(0, 2 * t), :], xs_ref.at[pl.ds(0, 2 * t), :], sem_ref.at[0]).wait()


def _dispatch(x, dest, last_tile, n_used, n_tiles):
    n = x.shape[0]
    t = 256
    tm = EXPERT_TILE
    return pl.pallas_call(
        _dispatch_body,
        out_shape=jax.ShapeDtypeStruct((n_tiles * tm, D_MODEL), F32),
        grid=(n // t,),
        in_specs=[
            pl.BlockSpec((2 * t,), lambda i: (i,), memory_space=pltpu.SMEM),
            pl.BlockSpec(memory_space=pltpu.SMEM),
            pl.BlockSpec(memory_space=pltpu.SMEM),
            pl.BlockSpec((t, D_MODEL), lambda i: (i, 0)),
        ],
        out_specs=pl.BlockSpec(memory_space=pl.ANY),
        scratch_shapes=[pltpu.VMEM((tm, D_MODEL), F32), pltpu.SemaphoreType.DMA((2,))],
        compiler_params=_params(("arbitrary",)),
        name="dispatch",
    )(dest, last_tile, n_used, x)


def _experts_body(te_ref, nused_ref, x_ref, w1_ref, w3_ref, w2_ref, o_ref):
    del te_ref
    i = pl.program_id(0)

    @pl.when(i < nused_ref[0])
    def _():
        xb = x_ref[...].astype(BF16)
        f = None
        for k in range(FF_SPLIT):
            cols = slice(k * (D_FF // FF_SPLIT), (k + 1) * (D_FF // FF_SPLIT))
            a = jnp.dot(xb, w1_ref[0, :, cols], preferred_element_type=F32)
            c = jnp.dot(xb, w3_ref[0, :, cols], preferred_element_type=F32)
            h = (jax.nn.silu(a) * c).astype(BF16)
            part = jnp.dot(h, w2_ref[0, cols, :], preferred_element_type=F32)
            f = part if f is None else f + part
        o_ref[...] = f

    @pl.when(i >= nused_ref[0])
    def _():
        o_ref[...] = jnp.zeros_like(o_ref)


def _experts(xs, tile_expert, n_used, w1, w3, w2):
    tm = EXPERT_TILE
    n_tiles = xs.shape[0] // tm

    def xmap(i, te, nu):
        return (jnp.minimum(i, nu[0] - 1), 0)

    def wmap(i, te, nu):
        return (te[jnp.minimum(i, nu[0] - 1)], 0, 0)

    return pl.pallas_call(
        _experts_body,
        out_shape=jax.ShapeDtypeStruct(xs.shape, F32),
        grid_spec=pltpu.PrefetchScalarGridSpec(
            num_scalar_prefetch=2,
            grid=(n_tiles,),
            in_specs=[
                pl.BlockSpec((tm, D_MODEL), xmap),
                pl.BlockSpec((1, D_MODEL, D_FF), wmap),
                pl.BlockSpec((1, D_MODEL, D_FF), wmap),
                pl.BlockSpec((1, D_FF, D_MODEL), wmap),
            ],
            out_specs=pl.BlockSpec((tm, D_MODEL), lambda i, te, nu: (i, 0)),
        ),
        compiler_params=_params(("arbitrary",), 60 * 1024 * 1024),
        name="experts",
    )(tile_expert, n_used, xs, w1, w3, w2)


def _combine_body(dest_ref, x_ref, mw_ref, g_ref, b_ref, ys_ref, o_ref, y0_ref, y1_ref, sem_ref):
    t = x_ref.shape[0]

    def issue(r, carry):
        pltpu.make_async_copy(ys_ref.at[pl.ds(dest_ref[2 * r], 1), :], y0_ref.at[pl.ds(r, 1), :], sem_ref.at[0]).start()
        pltpu.make_async_copy(ys_ref.at[pl.ds(dest_ref[2 * r + 1], 1), :], y1_ref.at[pl.ds(r, 1), :], sem_ref.at[0]).start()
        return carry

    lax.fori_loop(0, t, issue, 0)
    pltpu.make_async_copy(ys_ref.at[pl.ds(0, t), :], y0_ref, sem_ref.at[0]).wait()
    pltpu.make_async_copy(ys_ref.at[pl.ds(0, t), :], y1_ref, sem_ref.at[0]).wait()
    mw = mw_ref[...]
    f = mw[:, 0:1] * y0_ref[...] + mw[:, 1:2] * y1_ref[...]
    o_ref[...] = _layer_norm(DN_ALPHA * x_ref[...] + f, g_ref[...], b_ref[...])


def _combine(x, ys, dest, mw, g, b):
    n = x.shape[0]
    t = 256
    zero = lambda i: (0, 0)
    return pl.pallas_call(
        _combine_body,
        out_shape=jax.ShapeDtypeStruct((n, D_MODEL), F32),
        grid=(n // t,),
        in_specs=[
            pl.BlockSpec((2 * t,), lambda i: (i,), memory_space=pltpu.SMEM),
            pl.BlockSpec((t, D_MODEL), lambda i: (i, 0)),
            pl.BlockSpec((t, LANES), lambda i: (i, 0)),
            pl.BlockSpec((1, D_MODEL), zero),
            pl.BlockSpec((1, D_MODEL), zero),
            pl.BlockSpec(memory_space=pl.ANY),
        ],
        out_specs=pl.BlockSpec((t, D_MODEL), lambda i: (i, 0)),
        scratch_shapes=[pltpu.VMEM((t, D_MODEL), F32), pltpu.VMEM((t, D_MODEL), F32),
                        pltpu.SemaphoreType.DMA((1,))],
        compiler_params=_params(("arbitrary",)),
        name="combine",
    )(dest, x, mw, g, b, ys)


def _moe(x, w_r, b_r, w1, w3, w2, g, b):
    n = x.shape[0]
    tm = EXPERT_TILE
    n_tiles = (2 * n + N_EXPERTS * (tm - 1)) // tm
    mi, mw, cnt = _router(x, w_r, b_r)
    counts = cnt[0, :N_EXPERTS]
    tiles = (counts + tm - 1) // tm
    tile_end = jnp.cumsum(tiles)
    start = (tile_end - tiles) * tm
    n_used = tile_end[-1:].astype(jnp.int32)
    tile_expert = jnp.sum((tile_end[None, :] <= jnp.arange(n_tiles)[:, None]).astype(jnp.int32), axis=1)
    tile_expert = jnp.minimum(tile_expert, N_EXPERTS - 1)
    last_tile = jnp.where(tiles > 0, tile_end - 1, -1).astype(jnp.int32)
    dest = (start[mi[:, 0:2]] + mi[:, 2:4]).astype(jnp.int32).reshape(2 * n)
    xs = _dispatch(x, dest, last_tile, n_used, n_tiles)
    ys = _experts(xs, tile_expert, n_used, w1, w3, w2)
    return _combine(x, ys, dest, mw, g, b)


def _block_diag(w):
    g, c, _ = w.shape
    out = jnp.zeros((g * c, g * c), w.dtype)
    for i in range(g):
        out = out.at[i * c:(i + 1) * c, i * c:(i + 1) * c].set(w[i])
    return out


def _bias_pairs(table, t_q, n_keys, offset):
    rel = (jnp.arange(t_q)[:, None] + offset) - jnp.arange(n_keys)[None, :]
    idx = jnp.clip(rel, -REL_CLIP, REL_CLIP) + REL_CLIP
    return table[:, idx].astype(F32).reshape(HEAD_PAIRS, 2 * t_q, n_keys)


def kernel(x_prompt, x_sample, cache_k, cache_v, state_pool, w_in, b_in, w_pool_grp, pool_scale,
           rel_table, w_pool_br, w_attn_br, w_out, ln1_g, ln1_b, ln2_g, ln2_b,
           w1_dense, w3_dense, w2_dense, w_router, b_router, w1_exp, w3_exp, w2_exp):
    bp, tp, d = x_prompt.shape
    bs, ts, _ = x_sample.shape
    n_p, n_s = bp * tp, bs * ts
    n = n_p + n_s
    depth = w_in.shape[0]
    keep_s = cache_k.shape[2]
    keep_p = min(ATTN_REACH, tp)

    x = jnp.concatenate([x_prompt.reshape(n_p, d), x_sample.reshape(n_s, d)], axis=0)
    ck = cache_k.reshape(depth, bs, keep_s, ATTN_WIDTH)
    cv = cache_v.reshape(depth, bs, keep_s, ATTN_WIDTH)
    hist = jnp.pad(state_pool, ((0, 0), (0, 0), (HIST_ROWS - POOL_HIST, 0), (0, 0)))
    row = lambda v: v[None, :].astype(F32)

    k_buf = v_buf = None
    kp_new, vp_new, pp_new, ps_new = [], [], [], []
    for l in range(depth):
        wgrp = _block_diag(w_pool_grp[l]).astype(BF16)
        weights = (wgrp, row(pool_scale[l]), w_pool_br[l].astype(BF16), w_attn_br[l].astype(BF16),
                   w_out[l].astype(BF16), row(ln1_g[l]), row(ln1_b[l]))
        z = _in_proj(x, w_in[l].astype(BF16), row(b_in[l]))
        attn_p = _attn_prompt(z, _bias_pairs(rel_table[l], CHUNK, BAND, ATTN_REACH), bp, tp)
        attn_s, k_buf, v_buf = _attn_sample(z, ck, cv, _bias_pairs(rel_table[l], ts, keep_s + ts, keep_s),
                                            l, n_p, bs, ts, k_buf, v_buf)
        x1 = _mix(z, attn_p, attn_s, x, hist[l], weights, n_p, tp)
        j = l // 2
        if l % 2 == 0:
            x = _ffn_dense(x1, w1_dense[j].astype(BF16), w3_dense[j].astype(BF16), w2_dense[j].astype(BF16),
                           row(ln2_g[l]), row(ln2_b[l]))
        else:
            x = _moe(x1, w_router[j], b_router[j], w1_exp[j].astype(BF16), w3_exp[j].astype(BF16),
                     w2_exp[j].astype(BF16), row(ln2_g[l]), row(ln2_b[l]))
        zp = z[:n_p].reshape(bp, tp, IN_WIDTH)
        kp_new.append(zp[:, tp - keep_p:, 2 * ATTN_WIDTH:3 * ATTN_WIDTH].reshape(bp, keep_p, N_HEADS, HEAD_DIM))
        vp_new.append(zp[:, tp - keep_p:, 3 * ATTN_WIDTH:4 * ATTN_WIDTH].reshape(bp, keep_p, N_HEADS, HEAD_DIM))
        pp_new.append(zp[:, tp - POOL_HIST:, :POOL_WIDTH])
        us = z[n_p:, :POOL_WIDTH].reshape(bs, ts, POOL_WIDTH)
        ps_new.append(jnp.concatenate([state_pool[l], us], axis=1)[:, -POOL_HIST:])

    shape_s = (depth, bs, keep_s, N_HEADS, HEAD_DIM)
    return (x[:n_p].reshape(bp, tp, d), x[n_p:].reshape(bs, ts, d),
            jnp.stack(kp_new), jnp.stack(vp_new), jnp.stack(pp_new),
            k_buf.reshape(shape_s), v_buf.reshape(shape_s), jnp.stack(ps_new))
```

```python
import functools

import jax
import jax.numpy as jnp
import numpy as np
from jax import lax
from jax.experimental import pallas as pl
from jax.experimental.pallas import tpu as pltpu

F32 = jnp.float32
BF16 = jnp.bfloat16

D_MODEL = 1024
N_HEADS = 8
HEAD_DIM = 64
ATTN_WIDTH = N_HEADS * HEAD_DIM
CHUNK = 64
LEFT_CHUNKS = 8
BAND = (LEFT_CHUNKS + 1) * CHUNK
ATTN_REACH = LEFT_CHUNKS * CHUNK
REL_CLIP = 256
ATTN_SCALE = HEAD_DIM ** -0.5
POOL_WIDTH = 512
POOL_WINDOWS = (2, 4, 8, 16)
POOL_GROUP = POOL_WIDTH // len(POOL_WINDOWS)
POOL_HIST = max(POOL_WINDOWS) - 1
HIST_ROWS = POOL_HIST + 1
D_FF = 2816
N_EXPERTS = 8
PAST_LEN = 4096
DEPTH = 2
DN_ALPHA = (2 * DEPTH) ** 0.25
LN_EPS = 1e-5
NEG_INF = -1e30
IN_WIDTH = POOL_WIDTH + 3 * ATTN_WIDTH + 2 * D_MODEL
HEAD_PAIRS = N_HEADS // 2
PAIR_W = 2 * HEAD_DIM
LANES = 128

COL_U, COL_Q, COL_K, COL_V = 0, 1, 2, 3

EXPERT_TILE = 512
FF_SPLIT = 2
VMEM_LIMIT = 56 * 1024 * 1024


def _pick(n, candidates):
    for c in candidates:
        if n % c == 0:
            return c
    raise ValueError(f"no tile in {candidates} divides {n}")


def _params(sem, vmem=None):
    return pltpu.CompilerParams(dimension_semantics=sem, vmem_limit_bytes=vmem or VMEM_LIMIT)


def _layer_norm(r, g, b):
    mu = jnp.mean(r, axis=-1, keepdims=True)
    c = r - mu
    var = jnp.mean(c * c, axis=-1, keepdims=True)
    return c * lax.rsqrt(var + LN_EPS) * g + b


def _inproj_body(x_ref, w_ref, b_ref, z_ref):
    x = x_ref[...].astype(BF16)
    z_ref[...] = jnp.dot(x, w_ref[...], preferred_element_type=F32) + b_ref[...]


def _in_proj(x, w_bf, b):
    n = x.shape[0]
    tm = _pick(n, (1280, 640, 256))
    tn = 1024
    return pl.pallas_call(
        _inproj_body,
        out_shape=jax.ShapeDtypeStruct((n, IN_WIDTH), F32),
        grid=(n // tm, IN_WIDTH // tn),
        in_specs=[
            pl.BlockSpec((tm, D_MODEL), lambda i, j: (i, 0)),
            pl.BlockSpec((D_MODEL, tn), lambda i, j: (0, j)),
            pl.BlockSpec((1, tn), lambda i, j: (0, j)),
        ],
        out_specs=pl.BlockSpec((tm, tn), lambda i, j: (i, j)),
        compiler_params=_params(("arbitrary", "arbitrary")),
        name="in_proj",
    )(x, w_bf, b)


def _pair_scores(q_pair, k_pair, bias):
    lane = lax.broadcasted_iota(jnp.int32, q_pair.shape, 1)
    qs = q_pair * ATTN_SCALE
    q2 = jnp.concatenate([jnp.where(lane < HEAD_DIM, qs, 0.0), jnp.where(lane >= HEAD_DIM, qs, 0.0)], axis=0)
    s = lax.dot_general(q2.astype(BF16), k_pair, (((1,), (1,)), ((), ())), preferred_element_type=F32)
    return s + bias


def _pair_output(s, v_pair):
    rows = s.shape[0] // 2
    m = jnp.max(s, axis=-1, keepdims=True)
    e = jnp.exp(s - m)
    l = jnp.sum(e, axis=-1, keepdims=True)
    o2 = jnp.dot(e.astype(BF16), v_pair, preferred_element_type=F32) / l
    lane = lax.broadcasted_iota(jnp.int32, (rows, PAIR_W), 1)
    return jnp.where(lane < HEAD_DIM, o2[:rows], o2[rows:])


def _attn_prompt_body(q_ref, kp_ref, kc_ref, vp_ref, vc_ref, bias_ref, o_ref, kext_ref, vext_ref):
    blk = q_ref.shape[0]
    j = pl.program_id(1)
    kext_ref[0:blk, :] = kp_ref[...].astype(BF16)
    kext_ref[blk:2 * blk, :] = kc_ref[...].astype(BF16)
    vext_ref[0:blk, :] = vp_ref[...].astype(BF16)
    vext_ref[blk:2 * blk, :] = vc_ref[...].astype(BF16)
    col = lax.broadcasted_iota(jnp.int32, (2 * CHUNK, BAND), 1)

    def chunk(c, carry):
        q0 = pl.multiple_of(c * CHUNK, CHUNK)
        first_valid = jnp.where(j == 0, (LEFT_CHUNKS - c) * CHUNK, 0)
        for hp in range(HEAD_PAIRS):
            lanes = slice(hp * PAIR_W, (hp + 1) * PAIR_W)
            q_pair = q_ref[pl.ds(q0, CHUNK), lanes]
            k_pair = kext_ref[pl.ds(q0, BAND), lanes]
            v_pair = vext_ref[pl.ds(q0, BAND), lanes]
            s = _pair_scores(q_pair, k_pair, bias_ref[hp])
            s = jnp.where(col >= first_valid, s, NEG_INF)
            o_ref[pl.ds(q0, CHUNK), lanes] = _pair_output(s, v_pair)
        return carry

    lax.fori_loop(0, blk // CHUNK, chunk, 0)


def _attn_prompt(z, bias_pairs, batch, seq):
    blk = ATTN_REACH
    per_seq = seq // blk
    rows = batch * seq

    def cur(col):
        return lambda b, j: (b * per_seq + j, col)

    def prev(col):
        return lambda b, j: (b * per_seq + jnp.maximum(j - 1, 0), col)

    return pl.pallas_call(
        _attn_prompt_body,
        out_shape=jax.ShapeDtypeStruct((rows, ATTN_WIDTH), F32),
        grid=(batch, per_seq),
        in_specs=[
            pl.BlockSpec((blk, ATTN_WIDTH), cur(COL_Q)),
            pl.BlockSpec((blk, ATTN_WIDTH), prev(COL_K)),
            pl.BlockSpec((blk, ATTN_WIDTH), cur(COL_K)),
            pl.BlockSpec((blk, ATTN_WIDTH), prev(COL_V)),
            pl.BlockSpec((blk, ATTN_WIDTH), cur(COL_V)),
            pl.BlockSpec((HEAD_PAIRS, 2 * CHUNK, BAND), lambda b, j: (0, 0, 0)),
        ],
        out_specs=pl.BlockSpec((blk, ATTN_WIDTH), lambda b, j: (b * per_seq + j, 0)),
        scratch_shapes=[pltpu.VMEM((2 * blk, ATTN_WIDTH), BF16), pltpu.VMEM((2 * blk, ATTN_WIDTH), BF16)],
        compiler_params=_params(("arbitrary", "arbitrary")),
        name="attn_prompt",
    )(z, z, z, z, z, bias_pairs)


def _attn_sample_body(q_ref, kn_ref, vn_ref, ck_ref, cv_ref, bias_ref, *rest):
    o_ref, ko_ref, vo_ref, kall_ref, vall_ref = rest[-5:]
    keep = ck_ref.shape[2]
    t = q_ref.shape[0]
    ck = ck_ref[0, 0]
    cv = cv_ref[0, 0]
    kn = kn_ref[...]
    vn = vn_ref[...]
    kall_ref[0:keep, :] = ck.astype(BF16)
    kall_ref[keep:keep + t, :] = kn.astype(BF16)
    vall_ref[0:keep, :] = cv.astype(BF16)
    vall_ref[keep:keep + t, :] = vn.astype(BF16)
    for hp in range(HEAD_PAIRS):
        lanes = slice(hp * PAIR_W, (hp + 1) * PAIR_W)
        s = _pair_scores(q_ref[:, lanes], kall_ref[:, lanes], bias_ref[hp])
        o_ref[:, lanes] = _pair_output(s, vall_ref[:, lanes])
    ko_ref[0, 0, 0:keep - t, :] = ck[t:keep]
    ko_ref[0, 0, keep - t:keep, :] = kn
    vo_ref[0, 0, 0:keep - t, :] = cv[t:keep]
    vo_ref[0, 0, keep - t:keep, :] = vn
    for later in range(1, ko_ref.shape[0]):
        ko_ref[later] = jnp.zeros(ko_ref.shape[1:], F32)
        vo_ref[later] = jnp.zeros(vo_ref.shape[1:], F32)


def _attn_sample(z, cache_k, cache_v, bias_pairs, layer, row0, streams, t, k_buf, v_buf):
    depth, _, keep, _ = cache_k.shape
    blk0 = row0 // t
    ins = [z, z, z, cache_k, cache_v, bias_pairs]
    in_specs = [
        pl.BlockSpec((t, ATTN_WIDTH), lambda s: (blk0 + s, COL_Q)),
        pl.BlockSpec((t, ATTN_WIDTH), lambda s: (blk0 + s, COL_K)),
        pl.BlockSpec((t, ATTN_WIDTH), lambda s: (blk0 + s, COL_V)),
        pl.BlockSpec((1, 1, keep, ATTN_WIDTH), lambda s: (layer, s, 0, 0)),
        pl.BlockSpec((1, 1, keep, ATTN_WIDTH), lambda s: (layer, s, 0, 0)),
        pl.BlockSpec((HEAD_PAIRS, 2 * t, keep + t), lambda s: (0, 0, 0)),
    ]
    aliases = {}
    if k_buf is None:
        assert layer == 0
        buf_spec = pl.BlockSpec((depth, 1, keep, ATTN_WIDTH), lambda s: (0, s, 0, 0))
    else:
        aliases = {len(ins): 1, len(ins) + 1: 2}
        ins += [k_buf, v_buf]
        in_specs += [pl.BlockSpec(memory_space=pl.ANY), pl.BlockSpec(memory_space=pl.ANY)]
        buf_spec = pl.BlockSpec((1, 1, keep, ATTN_WIDTH), lambda s: (layer, s, 0, 0))
    buf = jax.ShapeDtypeStruct(cache_k.shape, F32)
    return pl.pallas_call(
        _attn_sample_body,
        out_shape=(jax.ShapeDtypeStruct((streams * t, ATTN_WIDTH), F32), buf, buf),
        grid=(streams,),
        in_specs=in_specs,
        out_specs=(
            pl.BlockSpec((t, ATTN_WIDTH), lambda s: (s, 0)),
            buf_spec,
            buf_spec,
        ),
        scratch_shapes=[pltpu.VMEM((keep + t, ATTN_WIDTH), BF16), pltpu.VMEM((keep + t, ATTN_WIDTH), BF16)],
        input_output_aliases=aliases,
        compiler_params=_params(("arbitrary",)),
        name="attn_sample",
    )(*ins)


def _merge(pooled, u, attn, gp, ga, x, wgrp_ref, scale_ref, wp_ref, wa_ref, wo_ref, g_ref, b_ref):
    pooled = pooled - u
    pool_y = jnp.dot(pooled.astype(BF16), wgrp_ref[...], preferred_element_type=F32) * scale_ref[...]
    mp = jnp.dot(pool_y.astype(BF16), wp_ref[...], preferred_element_type=F32)
    ma = jnp.dot(attn.astype(BF16), wa_ref[...], preferred_element_type=F32)
    m = jax.nn.sigmoid(gp) * mp + jax.nn.sigmoid(ga) * ma
    y = jnp.dot(m.astype(BF16), wo_ref[...], preferred_element_type=F32)
    return _layer_norm(DN_ALPHA * x + y, g_ref[...], b_ref[...])


def _window_means(read, pos, shape_out):
    outs = []
    for g, w in enumerate(POOL_WINDOWS):
        lanes = slice(g * POOL_GROUP, (g + 1) * POOL_GROUP)
        s = read(0, lanes)
        for back in range(1, w):
            s = s + read(back, lanes)
        outs.append((s / jnp.minimum(pos + 1, w).astype(F32)).reshape(shape_out))
    return jnp.concatenate(outs, axis=-1)


def _mix_body(n_prompt_tiles, tiles_per_seq, with_router, u_ref, up_ref, hist_ref, attn_p_ref, attn_s_ref,
              gp_ref, ga_ref, x_ref, wgrp_ref, scale_ref, wp_ref, wa_ref, wo_ref, g_ref, b_ref, *rest):
    if with_router:
        (wr_hi_ref, wr_lo_ref, br_ref, o_ref, mi_ref, mw_ref, cnt_ref,
         buf_ref, sbuf_ref, pooled_ref, attn_ref, carry_ref) = rest
    else:
        o_ref, buf_ref, sbuf_ref, pooled_ref, attn_ref = rest
    t = u_ref.shape[0]
    i = pl.program_id(0)
    u = u_ref[...]

    @pl.when(i < n_prompt_tiles)
    def _():
        tile = i % tiles_per_seq
        buf_ref[0:HIST_ROWS, :] = jnp.where(tile == 0, 0.0, up_ref[...])
        buf_ref[HIST_ROWS:HIST_ROWS + t, :] = u
        pos = tile * t + lax.broadcasted_iota(jnp.int32, (t, 1), 0)
        read = lambda back, lanes: buf_ref[HIST_ROWS - back:HIST_ROWS - back + t, lanes]
        pooled_ref[...] = _window_means(read, pos, (t, POOL_GROUP))
        attn_ref[...] = attn_p_ref[...]

    @pl.when(i == n_prompt_tiles)
    def _():
        streams, hrows, _ = hist_ref.shape
        ts = t // streams
        sbuf_ref[:, 0:hrows, :] = hist_ref[...]
        sbuf_ref[:, hrows:hrows + ts, :] = u.reshape(streams, ts, POOL_WIDTH)
        pos = PAST_LEN + lax.broadcasted_iota(jnp.int32, (1, ts, 1), 1)
        read = lambda back, lanes: sbuf_ref[:, hrows - back:hrows - back + ts, lanes]
        pooled_ref[...] = _window_means(read, pos, (t, POOL_GROUP))
        attn_ref[...] = attn_s_ref[...]

    x1 = _merge(pooled_ref[...], u, attn_ref[...], gp_ref[...], ga_ref[...], x_ref[...],
                wgrp_ref, scale_ref, wp_ref, wa_ref, wo_ref, g_ref, b_ref)
    o_ref[...] = x1
    if with_router:
        _route(x1, wr_hi_ref, wr_lo_ref, br_ref, mi_ref, mw_ref, cnt_ref, carry_ref)


def _mix(z, attn_p, attn_s, x, hist, weights, n_prompt, seq, router=None):
    n = x.shape[0]
    t = n - n_prompt
    n_prompt_tiles = n_prompt // t
    streams, hrows, _ = hist.shape
    hist_per_tile = t // HIST_ROWS
    last_p = n_prompt_tiles - 1
    zero = lambda i: (0, 0)
    rows = lambda width: pl.BlockSpec((t, width), lambda i: (i, 0))
    out_shape = [jax.ShapeDtypeStruct((n, D_MODEL), F32)]
    out_specs = [rows(D_MODEL)]
    router_specs, router_scratch = [], []
    if router is not None:
        router_specs = [pl.BlockSpec((D_MODEL, LANES), zero), pl.BlockSpec((D_MODEL, LANES), zero),
                        pl.BlockSpec((1, LANES), zero)]
        out_shape += [jax.ShapeDtypeStruct((n, LANES), jnp.int32), jax.ShapeDtypeStruct((n, LANES), F32),
                      jax.ShapeDtypeStruct((1, LANES), jnp.int32)]
        out_specs += [rows(LANES), rows(LANES), pl.BlockSpec((1, LANES), zero)]
        router_scratch = [pltpu.VMEM((1, LANES), F32)]
    out = pl.pallas_call(
        functools.partial(_mix_body, n_prompt_tiles, seq // t, router is not None),
        out_shape=out_shape,
        grid=(n_prompt_tiles + 1,),
        in_specs=[
            pl.BlockSpec((t, POOL_WIDTH), lambda i: (i, COL_U)),
            pl.BlockSpec((HIST_ROWS, POOL_WIDTH), lambda i: (jnp.maximum(i * hist_per_tile - 1, 0), COL_U)),
            pl.BlockSpec((streams, hrows, POOL_WIDTH), lambda i: (0, 0, 0)),
            pl.BlockSpec((t, ATTN_WIDTH), lambda i: (jnp.minimum(i, last_p), 0)),
            pl.BlockSpec((t, ATTN_WIDTH), zero),
            pl.BlockSpec((t, D_MODEL), lambda i: (i, 2)),
            pl.BlockSpec((t, D_MODEL), lambda i: (i, 3)),
            pl.BlockSpec((t, D_MODEL), lambda i: (i, 0)),
            pl.BlockSpec((POOL_WIDTH, POOL_WIDTH), zero),
            pl.BlockSpec((1, POOL_WIDTH), zero),
            pl.BlockSpec((POOL_WIDTH, D_MODEL), zero),
            pl.BlockSpec((ATTN_WIDTH, D_MODEL), zero),
            pl.BlockSpec((D_MODEL, D_MODEL), zero),
            pl.BlockSpec((1, D_MODEL), zero),
            pl.BlockSpec((1, D_MODEL), zero),
        ] + router_specs,
        out_specs=out_specs,
        scratch_shapes=[
            pltpu.VMEM((HIST_ROWS + t, POOL_WIDTH), F32),
            pltpu.VMEM((streams, hrows + t // streams, POOL_WIDTH), F32),
            pltpu.VMEM((t, POOL_WIDTH), F32),
            pltpu.VMEM((t, ATTN_WIDTH), F32),
        ] + router_scratch,
        compiler_params=_params(("arbitrary",)),
        name="mix",
    )(z, z, hist, attn_p, attn_s, z, z, x, *weights, *(router or ()))
    return out[0] if router is None else out


def _ffn_dense_body(x_ref, w1_ref, w3_ref, w2_ref, g_ref, b_ref, o_ref):
    x = x_ref[...]
    xb = x.astype(BF16)
    a = jnp.dot(xb, w1_ref[...], preferred_element_type=F32)
    c = jnp.dot(xb, w3_ref[...], preferred_element_type=F32)
    h = (jax.nn.silu(a) * c).astype(BF16)
    f = jnp.dot(h, w2_ref[...], preferred_element_type=F32)
    o_ref[...] = _layer_norm(DN_ALPHA * x + f, g_ref[...], b_ref[...])


def _ffn_dense(x, w1, w3, w2, g, b):
    n = x.shape[0]
    t = 256
    zero = lambda i: (0, 0)
    return pl.pallas_call(
        _ffn_dense_body,
        out_shape=jax.ShapeDtypeStruct((n, D_MODEL), F32),
        grid=(n // t,),
        in_specs=[
            pl.BlockSpec((t, D_MODEL), lambda i: (i, 0)),
            pl.BlockSpec((D_MODEL, D_FF), zero),
            pl.BlockSpec((D_MODEL, D_FF), zero),
            pl.BlockSpec((D_FF, D_MODEL), zero),
            pl.BlockSpec((1, D_MODEL), zero),
            pl.BlockSpec((1, D_MODEL), zero),
        ],
        out_specs=pl.BlockSpec((t, D_MODEL), lambda i: (i, 0)),
        compiler_params=_params(("arbitrary",)),
        name="ffn_dense",
    )(x, w1, w3, w2, g, b)


def _route(x, wr_hi_ref, wr_lo_ref, br_ref, mi_ref, mw_ref, cnt_ref, carry_ref):
    t = x.shape[0]

    @pl.when(pl.program_id(0) == 0)
    def _():
        carry_ref[...] = jnp.zeros_like(carry_ref)

    x_hi = x.astype(BF16)
    x_lo = (x - x_hi.astype(F32)).astype(BF16)
    logits = (jnp.dot(x_hi, wr_hi_ref[...], preferred_element_type=F32)
              + jnp.dot(x_lo, wr_hi_ref[...], preferred_element_type=F32)
              + jnp.dot(x_hi, wr_lo_ref[...], preferred_element_type=F32)) + br_ref[...]
    lane = lax.broadcasted_iota(jnp.int32, (t, LANES), 1)
    logits = jnp.where(lane < N_EXPERTS, logits, -jnp.inf)
    lane_f = lane.astype(F32)
    v0 = jnp.max(logits, axis=-1, keepdims=True)
    e0 = jnp.min(jnp.where(logits == v0, lane_f, float(LANES)), axis=-1, keepdims=True)
    rest = jnp.where(lane_f == e0, -jnp.inf, logits)
    v1 = jnp.max(rest, axis=-1, keepdims=True)
    e1 = jnp.min(jnp.where(rest == v1, lane_f, float(LANES)), axis=-1, keepdims=True)
    ex = jnp.exp(v1 - v0)
    w0 = 1.0 / (1.0 + ex)
    w1 = ex / (1.0 + ex)
    oh0 = (lane_f == e0).astype(F32)
    oh1 = (lane_f == e1).astype(F32)
    r_i = lax.broadcasted_iota(jnp.int32, (t, t), 0)
    c_i = lax.broadcasted_iota(jnp.int32, (t, t), 1)
    tri = (c_i < r_i).astype(BF16)
    pre0 = jnp.dot(tri, oh0.astype(BF16), preferred_element_type=F32)
    pre1 = jnp.dot(tri, oh1.astype(BF16), preferred_element_type=F32)
    cnt0 = jnp.sum(oh0, axis=0, keepdims=True)
    cnt1 = jnp.sum(oh1, axis=0, keepdims=True)
    carry = carry_ref[...]
    rank0 = jnp.sum(oh0 * (carry + pre0), axis=-1, keepdims=True)
    rank1 = jnp.sum(oh1 * (carry + cnt0 + pre1), axis=-1, keepdims=True)
    carry = carry + cnt0 + cnt1
    carry_ref[...] = carry
    cnt_ref[...] = carry.astype(jnp.int32)
    mi = jnp.where(lane == 0, e0, jnp.where(lane == 1, e1, 0.0))
    mi = jnp.where(lane == 2, rank0, jnp.where(lane == 3, rank1, mi))
    mi_ref[...] = mi.astype(jnp.int32)
    mw_ref[...] = jnp.where(lane == 0, w0, jnp.where(lane == 1, w1, 0.0))


def _router_operands(w_r, b_r):
    wr = jnp.zeros((D_MODEL, LANES), F32).at[:, :N_EXPERTS].set(w_r)
    br = jnp.zeros((1, LANES), F32).at[:, :N_EXPERTS].set(b_r[None, :])
    wr_hi = wr.astype(BF16)
    wr_lo = (wr - wr_hi.astype(F32)).astype(BF16)
    return wr_hi, wr_lo, br


def _dispatch_body(dest_ref, last_ref, nused_ref, x_ref, xs_ref, zero_ref, sem_ref):
    t = x_ref.shape[0]
    tm = zero_ref.shape[0]
    n_tiles = xs_ref.shape[0] // tm
    i = pl.program_id(0)

    @pl.when(i == 0)
    def _():
        zero_ref[...] = jnp.zeros_like(zero_ref)

        def fill(tile):
            cp = pltpu.make_async_copy(zero_ref, xs_ref.at[pl.ds(pl.multiple_of(tile * tm, tm), tm), :], sem_ref.at[1])
            cp.start()
            cp.wait()

        for e in range(N_EXPERTS):
            @pl.when(last_ref[e] >= 0)
            def _():
                fill(last_ref[e])

        def unused(tile, carry):
            fill(tile)
            return carry

        lax.fori_loop(nused_ref[0], n_tiles, unused, 0)

    def issue(r, carry):
        for slot in range(2):
            d = dest_ref[2 * r + slot]
            pltpu.make_async_copy(x_ref.at[pl.ds(r, 1), :], xs_ref.at[pl.ds(d, 1), :], sem_ref.at[0]).start()
        return carry

    lax.fori_loop(0, t, issue, 0)
    pltpu.make_async_copy(xs_ref.at[pl.ds(0, 2 * t), :], xs_ref.at[pl.ds(0, 2 * t), :], sem_ref.at[0]).wait()


def _dispatch(x, dest, last_tile, n_used, n_tiles):
    n = x.shape[0]
    t = 256
    tm = EXPERT_TILE
    return pl.pallas_call(
        _dispatch_body,
        out_shape=jax.ShapeDtypeStruct((n_tiles * tm, D_MODEL), F32),
        grid=(n // t,),
        in_specs=[
            pl.BlockSpec((2 * t,), lambda i: (i,), memory_space=pltpu.SMEM),
            pl.BlockSpec(memory_space=pltpu.SMEM),
            pl.BlockSpec(memory_space=pltpu.SMEM),
            pl.BlockSpec((t, D_MODEL), lambda i: (i, 0)),
        ],
        out_specs=pl.BlockSpec(memory_space=pl.ANY),
        scratch_shapes=[pltpu.VMEM((tm, D_MODEL), F32), pltpu.SemaphoreType.DMA((2,))],
        compiler_params=_params(("arbitrary",)),
        name="dispatch",
    )(dest, last_tile, n_used, x)


def _experts_body(te_ref, nused_ref, x_ref, w1_ref, w3_ref, w2_ref, o_ref):
    del te_ref
    i = pl.program_id(0)

    @pl.when(i < nused_ref[0])
    def _():
        xb = x_ref[...].astype(BF16)
        f = None
        for k in range(FF_SPLIT):
            cols = slice(k * (D_FF // FF_SPLIT), (k + 1) * (D_FF // FF_SPLIT))
            a = jnp.dot(xb, w1_ref[0, :, cols], preferred_element_type=F32)
            c = jnp.dot(xb, w3_ref[0, :, cols], preferred_element_type=F32)
            h = (jax.nn.silu(a) * c).astype(BF16)
            part = jnp.dot(h, w2_ref[0, cols, :], preferred_element_type=F32)
            f = part if f is None else f + part
        o_ref[...] = f

    @pl.when(i >= nused_ref[0])
    def _():
        o_ref[...] = jnp.zeros_like(o_ref)


def _experts(xs, tile_expert, n_used, w1, w3, w2):
    tm = EXPERT_TILE
    n_tiles = xs.shape[0] // tm

    def xmap(i, te, nu):
        return (jnp.minimum(i, nu[0] - 1), 0)

    def wmap(i, te, nu):
        return (te[jnp.minimum(i, nu[0] - 1)], 0, 0)

    return pl.pallas_call(
        _experts_body,
        out_shape=jax.ShapeDtypeStruct(xs.shape, F32),
        grid_spec=pltpu.PrefetchScalarGridSpec(
            num_scalar_prefetch=2,
            grid=(n_tiles,),
            in_specs=[
                pl.BlockSpec((tm, D_MODEL), xmap),
                pl.BlockSpec((1, D_MODEL, D_FF), wmap),
                pl.BlockSpec((1, D_MODEL, D_FF), wmap),
                pl.BlockSpec((1, D_FF, D_MODEL), wmap),
            ],
            out_specs=pl.BlockSpec((tm, D_MODEL), lambda i, te, nu: (i, 0)),
        ),
        compiler_params=_params(("arbitrary",), 60 * 1024 * 1024),
        name="experts",
    )(tile_expert, n_used, xs, w1, w3, w2)


def _combine_body(dest_ref, x_ref, mw_ref, g_ref, b_ref, ys_ref, o_ref, y0_ref, y1_ref, sem_ref):
    t = x_ref.shape[0]

    def issue(r, carry):
        pltpu.make_async_copy(ys_ref.at[pl.ds(dest_ref[2 * r], 1), :], y0_ref.at[pl.ds(r, 1), :], sem_ref.at[0]).start()
        pltpu.make_async_copy(ys_ref.at[pl.ds(dest_ref[2 * r + 1], 1), :], y1_ref.at[pl.ds(r, 1), :], sem_ref.at[0]).start()
        return carry

    lax.fori_loop(0, t, issue, 0)
    pltpu.make_async_copy(ys_ref.at[pl.ds(0, t), :], y0_ref, sem_ref.at[0]).wait()
    pltpu.make_async_copy(ys_ref.at[pl.ds(0, t), :], y1_ref, sem_ref.at[0]).wait()
    mw = mw_ref[...]
    f = mw[:, 0:1] * y0_ref[...] + mw[:, 1:2] * y1_ref[...]
    o_ref[...] = _layer_norm(DN_ALPHA * x_ref[...] + f, g_ref[...], b_ref[...])


def _combine(x, ys, dest, mw, g, b):
    n = x.shape[0]
    t = 256
    zero = lambda i: (0, 0)
    return pl.pallas_call(
        _combine_body,
        out_shape=jax.ShapeDtypeStruct((n, D_MODEL), F32),
        grid=(n // t,),
        in_specs=[
            pl.BlockSpec((2 * t,), lambda i: (i,), memory_space=pltpu.SMEM),
            pl.BlockSpec((t, D_MODEL), lambda i: (i, 0)),
            pl.BlockSpec((t, LANES), lambda i: (i, 0)),
            pl.BlockSpec((1, D_MODEL), zero),
            pl.BlockSpec((1, D_MODEL), zero),
            pl.BlockSpec(memory_space=pl.ANY),
        ],
        out_specs=pl.BlockSpec((t, D_MODEL), lambda i: (i, 0)),
        scratch_shapes=[pltpu.VMEM((t, D_MODEL), F32), pltpu.VMEM((t, D_MODEL), F32),
                        pltpu.SemaphoreType.DMA((1,))],
        compiler_params=_params(("arbitrary",)),
        name="combine",
    )(dest, x, mw, g, b, ys)


def _moe(x, mi, mw, cnt, w1, w3, w2, g, b):
    n = x.shape[0]
    tm = EXPERT_TILE
    n_tiles = (2 * n + N_EXPERTS * (tm - 1)) // tm
    counts = cnt[0, :N_EXPERTS]
    tiles = (counts + tm - 1) // tm
    tile_end = jnp.cumsum(tiles)
    start = (tile_end - tiles) * tm
    n_used = tile_end[-1:].astype(jnp.int32)
    tile_expert = jnp.sum((tile_end[None, :] <= jnp.arange(n_tiles)[:, None]).astype(jnp.int32), axis=1)
    tile_expert = jnp.minimum(tile_expert, N_EXPERTS - 1)
    last_tile = jnp.where(tiles > 0, tile_end - 1, -1).astype(jnp.int32)
    experts = mi[:, 0:2]
    group_start = sum(jnp.where(experts == e, start[e], 0) for e in range(N_EXPERTS))
    dest = (group_start + mi[:, 2:4]).astype(jnp.int32).reshape(2 * n)
    xs = _dispatch(x, dest, last_tile, n_used, n_tiles)
    ys = _experts(xs, tile_expert, n_used, w1, w3, w2)
    return _combine(x, ys, dest, mw, g, b)


def _block_diag(w):
    g, c, _ = w.shape
    out = jnp.zeros((g * c, g * c), w.dtype)
    for i in range(g):
        out = out.at[i * c:(i + 1) * c, i * c:(i + 1) * c].set(w[i])
    return out


def _bias_pairs(table, t_q, n_keys, offset):
    hi = t_q - 1 + offset
    span = t_q - 1 + n_keys
    cols = np.clip(hi - np.arange(span), -REL_CLIP, REL_CLIP) + REL_CLIP
    rev = table[:, cols]
    rows = [rev[:, t_q - 1 - i:t_q - 1 - i + n_keys] for i in range(t_q)]
    return jnp.stack(rows, axis=1).astype(F32).reshape(HEAD_PAIRS, 2 * t_q, n_keys)


def kernel(x_prompt, x_sample, cache_k, cache_v, state_pool, w_in, b_in, w_pool_grp, pool_scale,
           rel_table, w_pool_br, w_attn_br, w_out, ln1_g, ln1_b, ln2_g, ln2_b,
           w1_dense, w3_dense, w2_dense, w_router, b_router, w1_exp, w3_exp, w2_exp):
    bp, tp, d = x_prompt.shape
    bs, ts, _ = x_sample.shape
    n_p, n_s = bp * tp, bs * ts
    n = n_p + n_s
    depth = w_in.shape[0]
    keep_s = cache_k.shape[2]
    keep_p = min(ATTN_REACH, tp)

    x = jnp.concatenate([x_prompt.reshape(n_p, d), x_sample.reshape(n_s, d)], axis=0)
    ck = cache_k.reshape(depth, bs, keep_s, ATTN_WIDTH)
    cv = cache_v.reshape(depth, bs, keep_s, ATTN_WIDTH)
    hist = jnp.pad(state_pool, ((0, 0), (0, 0), (HIST_ROWS - POOL_HIST, 0), (0, 0)))
    row = lambda v: v[None, :].astype(F32)

    k_buf = v_buf = None
    kp_new, vp_new, pp_new, ps_new = [], [], [], []
    for l in range(depth):
        wgrp = _block_diag(w_pool_grp[l]).astype(BF16)
        weights = (wgrp, row(pool_scale[l]), w_pool_br[l].astype(BF16), w_attn_br[l].astype(BF16),
                   w_out[l].astype(BF16), row(ln1_g[l]), row(ln1_b[l]))
        z = _in_proj(x, w_in[l].astype(BF16), row(b_in[l]))
        attn_p = _attn_prompt(z, _bias_pairs(rel_table[l], CHUNK, BAND, ATTN_REACH), bp, tp)
        attn_s, k_buf, v_buf = _attn_sample(z, ck, cv, _bias_pairs(rel_table[l], ts, keep_s + ts, keep_s),
                                            l, n_p, bs, ts, k_buf, v_buf)
        j = l // 2
        if l % 2 == 0:
            x1 = _mix(z, attn_p, attn_s, x, hist[l], weights, n_p, tp)
            x = _ffn_dense(x1, w1_dense[j].astype(BF16), w3_dense[j].astype(BF16), w2_dense[j].astype(BF16),
                           row(ln2_g[l]), row(ln2_b[l]))
        else:
            x1, mi, mw, cnt = _mix(z, attn_p, attn_s, x, hist[l], weights, n_p, tp,
                                   router=_router_operands(w_router[j], b_router[j]))
            x = _moe(x1, mi, mw, cnt, w1_exp[j].astype(BF16), w3_exp[j].astype(BF16),
                     w2_exp[j].astype(BF16), row(ln2_g[l]), row(ln2_b[l]))
        def tail(rows, col0, width, z=z):
            return jnp.stack([lax.slice(z, ((b + 1) * tp - rows, col0), ((b + 1) * tp, col0 + width))
                              for b in range(bp)])

        kp_new.append(tail(keep_p, COL_K * ATTN_WIDTH, ATTN_WIDTH).reshape(bp, keep_p, N_HEADS, HEAD_DIM))
        vp_new.append(tail(keep_p, COL_V * ATTN_WIDTH, ATTN_WIDTH).reshape(bp, keep_p, N_HEADS, HEAD_DIM))
        pp_new.append(tail(POOL_HIST, 0, POOL_WIDTH))
        us = z[n_p:, :POOL_WIDTH].reshape(bs, ts, POOL_WIDTH)
        ps_new.append(jnp.concatenate([state_pool[l], us], axis=1)[:, -POOL_HIST:])

    shape_s = (depth, bs, keep_s, N_HEADS, HEAD_DIM)
    return (x[:n_p].reshape(bp, tp, d), x[n_p:].reshape(bs, ts, d),
            jnp.stack(kp_new), jnp.stack(vp_new), jnp.stack(pp_new),
            k_buf.reshape(shape_s), v_buf.reshape(shape_s), jnp.stack(ps_new))
```

```python
import functools

import jax
import jax.numpy as jnp
import numpy as np
from jax import lax
from jax.experimental import pallas as pl
from jax.experimental.pallas import tpu as pltpu

F32 = jnp.float32
BF16 = jnp.bfloat16

D_MODEL = 1024
N_HEADS = 8
HEAD_DIM = 64
ATTN_WIDTH = N_HEADS * HEAD_DIM
CHUNK = 64
LEFT_CHUNKS = 8
BAND = (LEFT_CHUNKS + 1) * CHUNK
ATTN_REACH = LEFT_CHUNKS * CHUNK
REL_CLIP = 256
ATTN_SCALE = HEAD_DIM ** -0.5
POOL_WIDTH = 512
POOL_WINDOWS = (2, 4, 8, 16)
POOL_GROUP = POOL_WIDTH // len(POOL_WINDOWS)
POOL_HIST = max(POOL_WINDOWS) - 1
HIST_ROWS = POOL_HIST + 1
D_FF = 2816
N_EXPERTS = 8
PAST_LEN = 4096
DEPTH = 2
DN_ALPHA = (2 * DEPTH) ** 0.25
LN_EPS = 1e-5
NEG_INF = -1e30
IN_WIDTH = POOL_WIDTH + 3 * ATTN_WIDTH + 2 * D_MODEL
HEAD_PAIRS = N_HEADS // 2
PAIR_W = 2 * HEAD_DIM
LANES = 128

COL_U, COL_Q, COL_K, COL_V = 0, 1, 2, 3

EXPERT_TILE = 512
FF_SPLIT = 2
ISSUE_UNROLL = 8
VMEM_LIMIT = 56 * 1024 * 1024


def _pick(n, candidates):
    for c in candidates:
        if n % c == 0:
            return c
    raise ValueError(f"no tile in {candidates} divides {n}")


def _params(sem, vmem=None):
    return pltpu.CompilerParams(dimension_semantics=sem, vmem_limit_bytes=vmem or VMEM_LIMIT)


def _layer_norm(r, g, b):
    mu = jnp.mean(r, axis=-1, keepdims=True)
    c = r - mu
    var = jnp.mean(c * c, axis=-1, keepdims=True)
    return c * lax.rsqrt(var + LN_EPS) * g + b


def _inproj_body(x_ref, w_ref, b_ref, z_ref):
    x = x_ref[...].astype(BF16)
    z_ref[...] = jnp.dot(x, w_ref[...], preferred_element_type=F32) + b_ref[...]


def _in_proj(x, w_bf, b):
    n = x.shape[0]
    tm = _pick(n, (1280, 640, 256))
    tn = 1024
    return pl.pallas_call(
        _inproj_body,
        out_shape=jax.ShapeDtypeStruct((n, IN_WIDTH), F32),
        grid=(n // tm, IN_WIDTH // tn),
        in_specs=[
            pl.BlockSpec((tm, D_MODEL), lambda i, j: (i, 0)),
            pl.BlockSpec((D_MODEL, tn), lambda i, j: (0, j)),
            pl.BlockSpec((1, tn), lambda i, j: (0, j)),
        ],
        out_specs=pl.BlockSpec((tm, tn), lambda i, j: (i, j)),
        compiler_params=_params(("arbitrary", "arbitrary")),
        name="in_proj",
    )(x, w_bf, b)


def _pair_scores(q_pair, k_pair, bias):
    lane = lax.broadcasted_iota(jnp.int32, q_pair.shape, 1)
    qs = q_pair * ATTN_SCALE
    q2 = jnp.concatenate([jnp.where(lane < HEAD_DIM, qs, 0.0), jnp.where(lane >= HEAD_DIM, qs, 0.0)], axis=0)
    s = lax.dot_general(q2.astype(BF16), k_pair, (((1,), (1,)), ((), ())), preferred_element_type=F32)
    return s + bias


def _pair_output(s, v_pair):
    rows = s.shape[0] // 2
    m = jnp.max(s, axis=-1, keepdims=True)
    e = jnp.exp(s - m)
    l = jnp.sum(e, axis=-1, keepdims=True)
    o2 = jnp.dot(e.astype(BF16), v_pair, preferred_element_type=F32) / l
    lane = lax.broadcasted_iota(jnp.int32, (rows, PAIR_W), 1)
    return jnp.where(lane < HEAD_DIM, o2[:rows], o2[rows:])


def _attn_prompt_body(q_ref, kp_ref, kc_ref, vp_ref, vc_ref, bias_ref, o_ref, kext_ref, vext_ref):
    blk = q_ref.shape[0]
    j = pl.program_id(1)
    kext_ref[0:blk, :] = kp_ref[...].astype(BF16)
    kext_ref[blk:2 * blk, :] = kc_ref[...].astype(BF16)
    vext_ref[0:blk, :] = vp_ref[...].astype(BF16)
    vext_ref[blk:2 * blk, :] = vc_ref[...].astype(BF16)
    col = lax.broadcasted_iota(jnp.int32, (2 * CHUNK, BAND), 1)

    def chunk(c, carry):
        q0 = pl.multiple_of(c * CHUNK, CHUNK)
        first_valid = jnp.where(j == 0, (LEFT_CHUNKS - c) * CHUNK, 0)
        for hp in range(HEAD_PAIRS):
            lanes = slice(hp * PAIR_W, (hp + 1) * PAIR_W)
            q_pair = q_ref[pl.ds(q0, CHUNK), lanes]
            k_pair = kext_ref[pl.ds(q0, BAND), lanes]
            v_pair = vext_ref[pl.ds(q0, BAND), lanes]
            s = _pair_scores(q_pair, k_pair, bias_ref[hp])
            s = jnp.where(col >= first_valid, s, NEG_INF)
            o_ref[pl.ds(q0, CHUNK), lanes] = _pair_output(s, v_pair)
        return carry

    lax.fori_loop(0, blk // CHUNK, chunk, 0, unroll=2)


def _attn_prompt(z, bias_pairs, batch, seq):
    blk = ATTN_REACH
    per_seq = seq // blk
    rows = batch * seq

    def cur(col):
        return lambda b, j: (b * per_seq + j, col)

    def prev(col):
        return lambda b, j: (b * per_seq + jnp.maximum(j - 1, 0), col)

    return pl.pallas_call(
        _attn_prompt_body,
        out_shape=jax.ShapeDtypeStruct((rows, ATTN_WIDTH), F32),
        grid=(batch, per_seq),
        in_specs=[
            pl.BlockSpec((blk, ATTN_WIDTH), cur(COL_Q)),
            pl.BlockSpec((blk, ATTN_WIDTH), prev(COL_K)),
            pl.BlockSpec((blk, ATTN_WIDTH), cur(COL_K)),
            pl.BlockSpec((blk, ATTN_WIDTH), prev(COL_V)),
            pl.BlockSpec((blk, ATTN_WIDTH), cur(COL_V)),
            pl.BlockSpec((HEAD_PAIRS, 2 * CHUNK, BAND), lambda b, j: (0, 0, 0)),
        ],
        out_specs=pl.BlockSpec((blk, ATTN_WIDTH), lambda b, j: (b * per_seq + j, 0)),
        scratch_shapes=[pltpu.VMEM((2 * blk, ATTN_WIDTH), BF16), pltpu.VMEM((2 * blk, ATTN_WIDTH), BF16)],
        compiler_params=_params(("arbitrary", "arbitrary")),
        name="attn_prompt",
    )(z, z, z, z, z, bias_pairs)


def _attn_sample_body(q_ref, kn_ref, vn_ref, ck_ref, cv_ref, bias_ref, *rest):
    o_ref, ko_ref, vo_ref, kall_ref, vall_ref = rest[-5:]
    keep = ck_ref.shape[2]
    t = q_ref.shape[0]
    ck = ck_ref[0, 0]
    cv = cv_ref[0, 0]
    kn = kn_ref[...]
    vn = vn_ref[...]
    kall_ref[0:keep, :] = ck.astype(BF16)
    kall_ref[keep:keep + t, :] = kn.astype(BF16)
    vall_ref[0:keep, :] = cv.astype(BF16)
    vall_ref[keep:keep + t, :] = vn.astype(BF16)
    for hp in range(HEAD_PAIRS):
        lanes = slice(hp * PAIR_W, (hp + 1) * PAIR_W)
        s = _pair_scores(q_ref[:, lanes], kall_ref[:, lanes], bias_ref[hp])
        o_ref[:, lanes] = _pair_output(s, vall_ref[:, lanes])
    ko_ref[0, 0, 0:keep - t, :] = ck[t:keep]
    ko_ref[0, 0, keep - t:keep, :] = kn
    vo_ref[0, 0, 0:keep - t, :] = cv[t:keep]
    vo_ref[0, 0, keep - t:keep, :] = vn
    for later in range(1, ko_ref.shape[0]):
        ko_ref[later] = jnp.zeros(ko_ref.shape[1:], F32)
        vo_ref[later] = jnp.zeros(vo_ref.shape[1:], F32)


def _attn_sample(z, cache_k, cache_v, bias_pairs, layer, row0, streams, t, k_buf, v_buf):
    depth, _, keep, _ = cache_k.shape
    blk0 = row0 // t
    ins = [z, z, z, cache_k, cache_v, bias_pairs]
    in_specs = [
        pl.BlockSpec((t, ATTN_WIDTH), lambda s: (blk0 + s, COL_Q)),
        pl.BlockSpec((t, ATTN_WIDTH), lambda s: (blk0 + s, COL_K)),
        pl.BlockSpec((t, ATTN_WIDTH), lambda s: (blk0 + s, COL_V)),
        pl.BlockSpec((1, 1, keep, ATTN_WIDTH), lambda s: (layer, s, 0, 0)),
        pl.BlockSpec((1, 1, keep, ATTN_WIDTH), lambda s: (layer, s, 0, 0)),
        pl.BlockSpec((HEAD_PAIRS, 2 * t, keep + t), lambda s: (0, 0, 0)),
    ]
    aliases = {}
    if k_buf is None:
        assert layer == 0
        buf_spec = pl.BlockSpec((depth, 1, keep, ATTN_WIDTH), lambda s: (0, s, 0, 0))
    else:
        aliases = {len(ins): 1, len(ins) + 1: 2}
        ins += [k_buf, v_buf]
        in_specs += [pl.BlockSpec(memory_space=pl.ANY), pl.BlockSpec(memory_space=pl.ANY)]
        buf_spec = pl.BlockSpec((1, 1, keep, ATTN_WIDTH), lambda s: (layer, s, 0, 0))
    buf = jax.ShapeDtypeStruct(cache_k.shape, F32)
    return pl.pallas_call(
        _attn_sample_body,
        out_shape=(jax.ShapeDtypeStruct((streams * t, ATTN_WIDTH), F32), buf, buf),
        grid=(streams,),
        in_specs=in_specs,
        out_specs=(
            pl.BlockSpec((t, ATTN_WIDTH), lambda s: (s, 0)),
            buf_spec,
            buf_spec,
        ),
        scratch_shapes=[pltpu.VMEM((keep + t, ATTN_WIDTH), BF16), pltpu.VMEM((keep + t, ATTN_WIDTH), BF16)],
        input_output_aliases=aliases,
        compiler_params=_params(("arbitrary",)),
        name="attn_sample",
    )(*ins)


def _merge(pooled, u, attn, gp, ga, x, wgrp_ref, scale_ref, wp_ref, wa_ref, wo_ref, g_ref, b_ref):
    pooled = pooled - u
    pool_y = jnp.dot(pooled.astype(BF16), wgrp_ref[...], preferred_element_type=F32) * scale_ref[...]
    mp = jnp.dot(pool_y.astype(BF16), wp_ref[...], preferred_element_type=F32)
    ma = jnp.dot(attn.astype(BF16), wa_ref[...], preferred_element_type=F32)
    m = jax.nn.sigmoid(gp) * mp + jax.nn.sigmoid(ga) * ma
    y = jnp.dot(m.astype(BF16), wo_ref[...], preferred_element_type=F32)
    return _layer_norm(DN_ALPHA * x + y, g_ref[...], b_ref[...])


def _window_means(read, pos, shape_out):
    outs = []
    for g, w in enumerate(POOL_WINDOWS):
        lanes = slice(g * POOL_GROUP, (g + 1) * POOL_GROUP)
        s = read(0, lanes)
        for back in range(1, w):
            s = s + read(back, lanes)
        outs.append((s / jnp.minimum(pos + 1, w).astype(F32)).reshape(shape_out))
    return jnp.concatenate(outs, axis=-1)


def _mix_body(n_prompt_tiles, tiles_per_seq, with_router, u_ref, up_ref, hist_ref, attn_p_ref, attn_s_ref,
              gp_ref, ga_ref, x_ref, wgrp_ref, scale_ref, wp_ref, wa_ref, wo_ref, g_ref, b_ref, *rest):
    if with_router:
        (wr_hi_ref, wr_lo_ref, br_ref, o_ref, mi_ref, mw_ref, cnt_ref,
         buf_ref, sbuf_ref, pooled_ref, attn_ref, carry_ref) = rest
    else:
        o_ref, buf_ref, sbuf_ref, pooled_ref, attn_ref = rest
    t = u_ref.shape[0]
    i = pl.program_id(0)
    u = u_ref[...]

    @pl.when(i < n_prompt_tiles)
    def _():
        tile = i % tiles_per_seq
        buf_ref[0:HIST_ROWS, :] = jnp.where(tile == 0, 0.0, up_ref[...])
        buf_ref[HIST_ROWS:HIST_ROWS + t, :] = u
        pos = tile * t + lax.broadcasted_iota(jnp.int32, (t, 1), 0)
        read = lambda back, lanes: buf_ref[HIST_ROWS - back:HIST_ROWS - back + t, lanes]
        pooled_ref[...] = _window_means(read, pos, (t, POOL_GROUP))
        attn_ref[...] = attn_p_ref[...]

    @pl.when(i == n_prompt_tiles)
    def _():
        streams, hrows, _ = hist_ref.shape
        ts = t // streams
        sbuf_ref[:, 0:hrows, :] = hist_ref[...]
        sbuf_ref[:, hrows:hrows + ts, :] = u.reshape(streams, ts, POOL_WIDTH)
        pos = PAST_LEN + lax.broadcasted_iota(jnp.int32, (1, ts, 1), 1)
        read = lambda back, lanes: sbuf_ref[:, hrows - back:hrows - back + ts, lanes]
        pooled_ref[...] = _window_means(read, pos, (t, POOL_GROUP))
        attn_ref[...] = attn_s_ref[...]

    x1 = _merge(pooled_ref[...], u, attn_ref[...], gp_ref[...], ga_ref[...], x_ref[...],
                wgrp_ref, scale_ref, wp_ref, wa_ref, wo_ref, g_ref, b_ref)
    o_ref[...] = x1
    if with_router:
        _route(x1, wr_hi_ref, wr_lo_ref, br_ref, mi_ref, mw_ref, cnt_ref, carry_ref)


def _mix(z, attn_p, attn_s, x, hist, weights, n_prompt, seq, router=None):
    n = x.shape[0]
    t = n - n_prompt
    n_prompt_tiles = n_prompt // t
    streams, hrows, _ = hist.shape
    hist_per_tile = t // HIST_ROWS
    last_p = n_prompt_tiles - 1
    zero = lambda i: (0, 0)
    rows = lambda width: pl.BlockSpec((t, width), lambda i: (i, 0))
    out_shape = [jax.ShapeDtypeStruct((n, D_MODEL), F32)]
    out_specs = [rows(D_MODEL)]
    router_specs, router_scratch = [], []
    if router is not None:
        router_specs = [pl.BlockSpec((D_MODEL, LANES), zero), pl.BlockSpec((D_MODEL, LANES), zero),
                        pl.BlockSpec((1, LANES), zero)]
        out_shape += [jax.ShapeDtypeStruct((n, LANES), jnp.int32), jax.ShapeDtypeStruct((n, LANES), F32),
                      jax.ShapeDtypeStruct((1, LANES), jnp.int32)]
        out_specs += [rows(LANES), rows(LANES), pl.BlockSpec((1, LANES), zero)]
        router_scratch = [pltpu.VMEM((1, LANES), F32)]
    out = pl.pallas_call(
        functools.partial(_mix_body, n_prompt_tiles, seq // t, router is not None),
        out_shape=out_shape,
        grid=(n_prompt_tiles + 1,),
        in_specs=[
            pl.BlockSpec((t, POOL_WIDTH), lambda i: (i, COL_U)),
            pl.BlockSpec((HIST_ROWS, POOL_WIDTH), lambda i: (jnp.maximum(i * hist_per_tile - 1, 0), COL_U)),
            pl.BlockSpec((streams, hrows, POOL_WIDTH), lambda i: (0, 0, 0)),
            pl.BlockSpec((t, ATTN_WIDTH), lambda i: (jnp.minimum(i, last_p), 0)),
            pl.BlockSpec((t, ATTN_WIDTH), zero),
            pl.BlockSpec((t, D_MODEL), lambda i: (i, 2)),
            pl.BlockSpec((t, D_MODEL), lambda i: (i, 3)),
            pl.BlockSpec((t, D_MODEL), lambda i: (i, 0)),
            pl.BlockSpec((POOL_WIDTH, POOL_WIDTH), zero),
            pl.BlockSpec((1, POOL_WIDTH), zero),
            pl.BlockSpec((POOL_WIDTH, D_MODEL), zero),
            pl.BlockSpec((ATTN_WIDTH, D_MODEL), zero),
            pl.BlockSpec((D_MODEL, D_MODEL), zero),
            pl.BlockSpec((1, D_MODEL), zero),
            pl.BlockSpec((1, D_MODEL), zero),
        ] + router_specs,
        out_specs=out_specs,
        scratch_shapes=[
            pltpu.VMEM((HIST_ROWS + t, POOL_WIDTH), F32),
            pltpu.VMEM((streams, hrows + t // streams, POOL_WIDTH), F32),
            pltpu.VMEM((t, POOL_WIDTH), F32),
            pltpu.VMEM((t, ATTN_WIDTH), F32),
        ] + router_scratch,
        compiler_params=_params(("arbitrary",)),
        name="mix",
    )(z, z, hist, attn_p, attn_s, z, z, x, *weights, *(router or ()))
    return out[0] if router is None else out


def _ffn_dense_body(x_ref, w1_ref, w3_ref, w2_ref, g_ref, b_ref, o_ref):
    x = x_ref[...]
    xb = x.astype(BF16)
    a = jnp.dot(xb, w1_ref[...], preferred_element_type=F32)
    c = jnp.dot(xb, w3_ref[...], preferred_element_type=F32)
    h = (jax.nn.silu(a) * c).astype(BF16)
    f = jnp.dot(h, w2_ref[...], preferred_element_type=F32)
    o_ref[...] = _layer_norm(DN_ALPHA * x + f, g_ref[...], b_ref[...])


def _ffn_dense(x, w1, w3, w2, g, b):
    n = x.shape[0]
    t = 256
    zero = lambda i: (0, 0)
    return pl.pallas_call(
        _ffn_dense_body,
        out_shape=jax.ShapeDtypeStruct((n, D_MODEL), F32),
        grid=(n // t,),
        in_specs=[
            pl.BlockSpec((t, D_MODEL), lambda i: (i, 0)),
            pl.BlockSpec((D_MODEL, D_FF), zero),
            pl.BlockSpec((D_MODEL, D_FF), zero),
            pl.BlockSpec((D_FF, D_MODEL), zero),
            pl.BlockSpec((1, D_MODEL), zero),
            pl.BlockSpec((1, D_MODEL), zero),
        ],
        out_specs=pl.BlockSpec((t, D_MODEL), lambda i: (i, 0)),
        compiler_params=_params(("arbitrary",)),
        name="ffn_dense",
    )(x, w1, w3, w2, g, b)


def _route(x, wr_hi_ref, wr_lo_ref, br_ref, mi_ref, mw_ref, cnt_ref, carry_ref):
    t = x.shape[0]

    @pl.when(pl.program_id(0) == 0)
    def _():
        carry_ref[...] = jnp.zeros_like(carry_ref)

    x_hi = x.astype(BF16)
    x_lo = (x - x_hi.astype(F32)).astype(BF16)
    logits = (jnp.dot(x_hi, wr_hi_ref[...], preferred_element_type=F32)
              + jnp.dot(x_lo, wr_hi_ref[...], preferred_element_type=F32)
              + jnp.dot(x_hi, wr_lo_ref[...], preferred_element_type=F32)) + br_ref[...]
    lane = lax.broadcasted_iota(jnp.int32, (t, LANES), 1)
    logits = jnp.where(lane < N_EXPERTS, logits, -jnp.inf)
    lane_f = lane.astype(F32)
    v0 = jnp.max(logits, axis=-1, keepdims=True)
    e0 = jnp.min(jnp.where(logits == v0, lane_f, float(LANES)), axis=-1, keepdims=True)
    rest = jnp.where(lane_f == e0, -jnp.inf, logits)
    v1 = jnp.max(rest, axis=-1, keepdims=True)
    e1 = jnp.min(jnp.where(rest == v1, lane_f, float(LANES)), axis=-1, keepdims=True)
    ex = jnp.exp(v1 - v0)
    w0 = 1.0 / (1.0 + ex)
    w1 = ex / (1.0 + ex)
    oh0 = (lane_f == e0).astype(F32)
    oh1 = (lane_f == e1).astype(F32)
    r_i = lax.broadcasted_iota(jnp.int32, (t, t), 0)
    c_i = lax.broadcasted_iota(jnp.int32, (t, t), 1)
    tri = (c_i < r_i).astype(BF16)
    pre0 = jnp.dot(tri, oh0.astype(BF16), preferred_element_type=F32)
    pre1 = jnp.dot(tri, oh1.astype(BF16), preferred_element_type=F32)
    cnt0 = jnp.sum(oh0, axis=0, keepdims=True)
    cnt1 = jnp.sum(oh1, axis=0, keepdims=True)
    carry = carry_ref[...]
    rank0 = jnp.sum(oh0 * (carry + pre0), axis=-1, keepdims=True)
    rank1 = jnp.sum(oh1 * (carry + cnt0 + pre1), axis=-1, keepdims=True)
    carry = carry + cnt0 + cnt1
    carry_ref[...] = carry
    cnt_ref[...] = carry.astype(jnp.int32)
    mi = jnp.where(lane == 0, e0, jnp.where(lane == 1, e1, 0.0))
    mi = jnp.where(lane == 2, rank0, jnp.where(lane == 3, rank1, mi))
    mi_ref[...] = mi.astype(jnp.int32)
    mw_ref[...] = jnp.where(lane == 0, w0, jnp.where(lane == 1, w1, 0.0))


def _router_operands(w_r, b_r):
    wr = jnp.zeros((D_MODEL, LANES), F32).at[:, :N_EXPERTS].set(w_r)
    br = jnp.zeros((1, LANES), F32).at[:, :N_EXPERTS].set(b_r[None, :])
    wr_hi = wr.astype(BF16)
    wr_lo = (wr - wr_hi.astype(F32)).astype(BF16)
    return wr_hi, wr_lo, br


def _dispatch_body(dest_ref, last_ref, nused_ref, x_ref, xs_ref, zero_ref, sem_ref):
    t = x_ref.shape[0]
    tm = zero_ref.shape[0]
    n_tiles = xs_ref.shape[0] // tm
    i = pl.program_id(0)

    @pl.when(i == 0)
    def _():
        zero_ref[...] = jnp.zeros_like(zero_ref)

        def fill(tile):
            cp = pltpu.make_async_copy(zero_ref, xs_ref.at[pl.ds(pl.multiple_of(tile * tm, tm), tm), :], sem_ref.at[1])
            cp.start()
            cp.wait()

        for e in range(N_EXPERTS):
            @pl.when(last_ref[e] >= 0)
            def _():
                fill(last_ref[e])

        def unused(tile, carry):
            fill(tile)
            return carry

        lax.fori_loop(nused_ref[0], n_tiles, unused, 0)

    def issue(r, carry):
        for slot in range(2):
            d = dest_ref[2 * r + slot]
            pltpu.make_async_copy(x_ref.at[pl.ds(r, 1), :], xs_ref.at[pl.ds(d, 1), :],
                                  sem_ref.at[0]).start(priority=slot)
        return carry

    lax.fori_loop(0, t, issue, 0, unroll=ISSUE_UNROLL)
    pltpu.make_async_copy(xs_ref.at[pl.ds(0, 2 * t), :], xs_ref.at[pl.ds(0, 2 * t), :], sem_ref.at[0]).wait()


def _dispatch(x, dest, last_tile, n_used, n_tiles):
    n = x.shape[0]
    t = 256
    tm = EXPERT_TILE
    return pl.pallas_call(
        _dispatch_body,
        out_shape=jax.ShapeDtypeStruct((n_tiles * tm, D_MODEL), F32),
        grid=(n // t,),
        in_specs=[
            pl.BlockSpec((2 * t,), lambda i: (i,), memory_space=pltpu.SMEM),
            pl.BlockSpec(memory_space=pltpu.SMEM),
            pl.BlockSpec(memory_space=pltpu.SMEM),
            pl.BlockSpec((t, D_MODEL), lambda i: (i, 0)),
        ],
        out_specs=pl.BlockSpec(memory_space=pl.ANY),
        scratch_shapes=[pltpu.VMEM((tm, D_MODEL), F32), pltpu.SemaphoreType.DMA((2,))],
        compiler_params=_params(("arbitrary",)),
        name="dispatch",
    )(dest, last_tile, n_used, x)


def _experts_body(te_ref, nused_ref, x_ref, w1_ref, w3_ref, w2_ref, o_ref):
    del te_ref
    i = pl.program_id(0)

    @pl.when(i < nused_ref[0])
    def _():
        xb = x_ref[...].astype(BF16)
        f = None
        for k in range(FF_SPLIT):
            cols = slice(k * (D_FF // FF_SPLIT), (k + 1) * (D_FF // FF_SPLIT))
            a = jnp.dot(xb, w1_ref[0, :, cols], preferred_element_type=F32)
            c = jnp.dot(xb, w3_ref[0, :, cols], preferred_element_type=F32)
            h = (jax.nn.silu(a) * c).astype(BF16)
            part = jnp.dot(h, w2_ref[0, cols, :], preferred_element_type=F32)
            f = part if f is None else f + part
        o_ref[...] = f

    @pl.when(i >= nused_ref[0])
    def _():
        o_ref[...] = jnp.zeros_like(o_ref)


def _experts(xs, tile_expert, n_used, w1, w3, w2):
    tm = EXPERT_TILE
    n_tiles = xs.shape[0] // tm

    def xmap(i, te, nu):
        return (jnp.minimum(i, nu[0] - 1), 0)

    def wmap(i, te, nu):
        return (te[jnp.minimum(i, nu[0] - 1)], 0, 0)

    return pl.pallas_call(
        _experts_body,
        out_shape=jax.ShapeDtypeStruct(xs.shape, F32),
        grid_spec=pltpu.PrefetchScalarGridSpec(
            num_scalar_prefetch=2,
            grid=(n_tiles,),
            in_specs=[
                pl.BlockSpec((tm, D_MODEL), xmap),
                pl.BlockSpec((1, D_MODEL, D_FF), wmap),
                pl.BlockSpec((1, D_MODEL, D_FF), wmap),
                pl.BlockSpec((1, D_FF, D_MODEL), wmap),
            ],
            out_specs=pl.BlockSpec((tm, D_MODEL), lambda i, te, nu: (i, 0)),
        ),
        compiler_params=_params(("arbitrary",), 60 * 1024 * 1024),
        name="experts",
    )(tile_expert, n_used, xs, w1, w3, w2)


def _combine_body(dest_ref, x_ref, mw_ref, g_ref, b_ref, ys_ref, o_ref, y0_ref, y1_ref, sem_ref):
    t = x_ref.shape[0]

    def issue(r, carry):
        pltpu.make_async_copy(ys_ref.at[pl.ds(dest_ref[2 * r], 1), :], y0_ref.at[pl.ds(r, 1), :],
                              sem_ref.at[0]).start(priority=0)
        pltpu.make_async_copy(ys_ref.at[pl.ds(dest_ref[2 * r + 1], 1), :], y1_ref.at[pl.ds(r, 1), :],
                              sem_ref.at[0]).start(priority=1)
        return carry

    lax.fori_loop(0, t, issue, 0, unroll=ISSUE_UNROLL)
    pltpu.make_async_copy(ys_ref.at[pl.ds(0, t), :], y0_ref, sem_ref.at[0]).wait()
    pltpu.make_async_copy(ys_ref.at[pl.ds(0, t), :], y1_ref, sem_ref.at[0]).wait()
    mw = mw_ref[...]
    f = mw[:, 0:1] * y0_ref[...] + mw[:, 1:2] * y1_ref[...]
    o_ref[...] = _layer_norm(DN_ALPHA * x_ref[...] + f, g_ref[...], b_ref[...])


def _combine(x, ys, dest, mw, g, b):
    n = x.shape[0]
    t = 256
    zero = lambda i: (0, 0)
    return pl.pallas_call(
        _combine_body,
        out_shape=jax.ShapeDtypeStruct((n, D_MODEL), F32),
        grid=(n // t,),
        in_specs=[
            pl.BlockSpec((2 * t,), lambda i: (i,), memory_space=pltpu.SMEM),
            pl.BlockSpec((t, D_MODEL), lambda i: (i, 0)),
            pl.BlockSpec((t, LANES), lambda i: (i, 0)),
            pl.BlockSpec((1, D_MODEL), zero),
            pl.BlockSpec((1, D_MODEL), zero),
            pl.BlockSpec(memory_space=pl.ANY),
        ],
        out_specs=pl.BlockSpec((t, D_MODEL), lambda i: (i, 0)),
        scratch_shapes=[pltpu.VMEM((t, D_MODEL), F32), pltpu.VMEM((t, D_MODEL), F32),
                        pltpu.SemaphoreType.DMA((1,))],
        compiler_params=_params(("arbitrary",)),
        name="combine",
    )(dest, x, mw, g, b, ys)


def _moe(x, mi, mw, cnt, w1, w3, w2, g, b):
    n = x.shape[0]
    tm = EXPERT_TILE
    n_tiles = (2 * n + N_EXPERTS * (tm - 1)) // tm
    counts = cnt[0, :N_EXPERTS]
    tiles = (counts + tm - 1) // tm
    tile_end = jnp.cumsum(tiles)
    start = (tile_end - tiles) * tm
    n_used = tile_end[-1:].astype(jnp.int32)
    tile_expert = jnp.sum((tile_end[None, :] <= jnp.arange(n_tiles)[:, None]).astype(jnp.int32), axis=1)
    tile_expert = jnp.minimum(tile_expert, N_EXPERTS - 1)
    last_tile = jnp.where(tiles > 0, tile_end - 1, -1).astype(jnp.int32)
    experts = mi[:, 0:2]
    group_start = sum(jnp.where(experts == e, start[e], 0) for e in range(N_EXPERTS))
    dest = (group_start + mi[:, 2:4]).astype(jnp.int32).reshape(2 * n)
    xs = _dispatch(x, dest, last_tile, n_used, n_tiles)
    ys = _experts(xs, tile_expert, n_used, w1, w3, w2)
    return _combine(x, ys, dest, mw, g, b)


def _block_diag(w):
    g, c, _ = w.shape
    out = jnp.zeros((g * c, g * c), w.dtype)
    for i in range(g):
        out = out.at[i * c:(i + 1) * c, i * c:(i + 1) * c].set(w[i])
    return out


def _bias_pairs(table, t_q, n_keys, offset):
    hi = t_q - 1 + offset
    span = t_q - 1 + n_keys
    cols = np.clip(hi - np.arange(span), -REL_CLIP, REL_CLIP) + REL_CLIP
    rev = table[:, cols].astype(F32)
    skew = jnp.tile(rev, (1, t_q + 1))[:, :t_q * (span + 1)].reshape(N_HEADS, t_q, span + 1)
    slab = skew[:, ::-1, :n_keys]
    return slab.reshape(HEAD_PAIRS, 2 * t_q, n_keys)


def kernel(x_prompt, x_sample, cache_k, cache_v, state_pool, w_in, b_in, w_pool_grp, pool_scale,
           rel_table, w_pool_br, w_attn_br, w_out, ln1_g, ln1_b, ln2_g, ln2_b,
           w1_dense, w3_dense, w2_dense, w_router, b_router, w1_exp, w3_exp, w2_exp):
    bp, tp, d = x_prompt.shape
    bs, ts, _ = x_sample.shape
    n_p, n_s = bp * tp, bs * ts
    n = n_p + n_s
    depth = w_in.shape[0]
    keep_s = cache_k.shape[2]
    keep_p = min(ATTN_REACH, tp)

    x = jnp.concatenate([x_prompt.reshape(n_p, d), x_sample.reshape(n_s, d)], axis=0)
    ck = cache_k.reshape(depth, bs, keep_s, ATTN_WIDTH)
    cv = cache_v.reshape(depth, bs, keep_s, ATTN_WIDTH)
    hist = jnp.pad(state_pool, ((0, 0), (0, 0), (HIST_ROWS - POOL_HIST, 0), (0, 0)))
    row = lambda v: v[None, :].astype(F32)

    k_buf = v_buf = None
    kp_new, vp_new, pp_new, ps_new = [], [], [], []
    for l in range(depth):
        wgrp = _block_diag(w_pool_grp[l]).astype(BF16)
        weights = (wgrp, row(pool_scale[l]), w_pool_br[l].astype(BF16), w_attn_br[l].astype(BF16),
                   w_out[l].astype(BF16), row(ln1_g[l]), row(ln1_b[l]))
        z = _in_proj(x, w_in[l].astype(BF16), row(b_in[l]))
        attn_p = _attn_prompt(z, _bias_pairs(rel_table[l], CHUNK, BAND, ATTN_REACH), bp, tp)
        attn_s, k_buf, v_buf = _attn_sample(z, ck, cv, _bias_pairs(rel_table[l], ts, keep_s + ts, keep_s),
                                            l, n_p, bs, ts, k_buf, v_buf)
        j = l // 2
        if l % 2 == 0:
            x1 = _mix(z, attn_p, attn_s, x, hist[l], weights, n_p, tp)
            x = _ffn_dense(x1, w1_dense[j].astype(BF16), w3_dense[j].astype(BF16), w2_dense[j].astype(BF16),
                           row(ln2_g[l]), row(ln2_b[l]))
        else:
            x1, mi, mw, cnt = _mix(z, attn_p, attn_s, x, hist[l], weights, n_p, tp,
                                   router=_router_operands(w_router[j], b_router[j]))
            x = _moe(x1, mi, mw, cnt, w1_exp[j].astype(BF16), w3_exp[j].astype(BF16),
                     w2_exp[j].astype(BF16), row(ln2_g[l]), row(ln2_b[l]))
        def tail(rows, col0, width, z=z):
            return jnp.stack([lax.slice(z, ((b + 1) * tp - rows, col0), ((b + 1) * tp, col0 + width))
                              for b in range(bp)])

        kp_new.append(tail(keep_p, COL_K * ATTN_WIDTH, ATTN_WIDTH).reshape(bp, keep_p, N_HEADS, HEAD_DIM))
        vp_new.append(tail(keep_p, COL_V * ATTN_WIDTH, ATTN_WIDTH).reshape(bp, keep_p, N_HEADS, HEAD_DIM))
        pp_new.append(tail(POOL_HIST, 0, POOL_WIDTH))
        us = z[n_p:, :POOL_WIDTH].reshape(bs, ts, POOL_WIDTH)
        ps_new.append(jnp.concatenate([state_pool[l], us], axis=1)[:, -POOL_HIST:])

    shape_s = (depth, bs, keep_s, N_HEADS, HEAD_DIM)
    return (x[:n_p].reshape(bp, tp, d), x[n_p:].reshape(bs, ts, d),
            jnp.stack(kp_new), jnp.stack(vp_new), jnp.stack(pp_new),
            k_buf.reshape(shape_s), v_buf.reshape(shape_s), jnp.stack(ps_new))
```

```python
import functools

import jax
import jax.numpy as jnp
import numpy as np
from jax import lax
from jax.experimental import pallas as pl
from jax.experimental.pallas import tpu as pltpu

F32 = jnp.float32
BF16 = jnp.bfloat16

D_MODEL = 1024
N_HEADS = 8
HEAD_DIM = 64
ATTN_WIDTH = N_HEADS * HEAD_DIM
CHUNK = 64
LEFT_CHUNKS = 8
BAND = (LEFT_CHUNKS + 1) * CHUNK
ATTN_REACH = LEFT_CHUNKS * CHUNK
REL_CLIP = 256
ATTN_SCALE = HEAD_DIM ** -0.5
POOL_WIDTH = 512
POOL_WINDOWS = (2, 4, 8, 16)
POOL_GROUP = POOL_WIDTH // len(POOL_WINDOWS)
POOL_HIST = max(POOL_WINDOWS) - 1
HIST_ROWS = POOL_HIST + 1
D_FF = 2816
N_EXPERTS = 8
PAST_LEN = 4096
DEPTH = 2
DN_ALPHA = (2 * DEPTH) ** 0.25
LN_EPS = 1e-5
NEG_INF = -1e30
IN_WIDTH = POOL_WIDTH + 3 * ATTN_WIDTH + 2 * D_MODEL
HEAD_PAIRS = N_HEADS // 2
PAIR_W = 2 * HEAD_DIM
LANES = 128

COL_U, COL_Q, COL_K, COL_V = 0, 1, 2, 3

EXPERT_TILE = 512
FF_SPLIT = 2
ISSUE_UNROLL = 8
VMEM_LIMIT = 56 * 1024 * 1024


def _pick(n, candidates):
    for c in candidates:
        if n % c == 0:
            return c
    raise ValueError(f"no tile in {candidates} divides {n}")


def _params(sem, vmem=None):
    return pltpu.CompilerParams(dimension_semantics=sem, vmem_limit_bytes=vmem or VMEM_LIMIT)


def _layer_norm(r, g, b):
    mu = jnp.mean(r, axis=-1, keepdims=True)
    c = r - mu
    var = jnp.mean(c * c, axis=-1, keepdims=True)
    return c * lax.rsqrt(var + LN_EPS) * g + b


def _inproj_body(x_ref, w_ref, b_ref, z_ref):
    x = x_ref[...].astype(BF16)
    z_ref[...] = jnp.dot(x, w_ref[...], preferred_element_type=F32) + b_ref[...]


def _in_proj(x, w_bf, b):
    n = x.shape[0]
    tm = _pick(n, (1280, 640, 256))
    tn = 1024
    return pl.pallas_call(
        _inproj_body,
        out_shape=jax.ShapeDtypeStruct((n, IN_WIDTH), F32),
        grid=(n // tm, IN_WIDTH // tn),
        in_specs=[
            pl.BlockSpec((tm, D_MODEL), lambda i, j: (i, 0)),
            pl.BlockSpec((D_MODEL, tn), lambda i, j: (0, j)),
            pl.BlockSpec((1, tn), lambda i, j: (0, j)),
        ],
        out_specs=pl.BlockSpec((tm, tn), lambda i, j: (i, j)),
        compiler_params=_params(("arbitrary", "arbitrary")),
        name="in_proj",
    )(x, w_bf, b)


def _pair_scores(q_pair, k_pair, bias):
    lane = lax.broadcasted_iota(jnp.int32, q_pair.shape, 1)
    qs = q_pair * ATTN_SCALE
    q2 = jnp.concatenate([jnp.where(lane < HEAD_DIM, qs, 0.0), jnp.where(lane >= HEAD_DIM, qs, 0.0)], axis=0)
    s = lax.dot_general(q2.astype(BF16), k_pair, (((1,), (1,)), ((), ())), preferred_element_type=F32)
    return s + bias


def _pair_output(s, v_pair):
    rows = s.shape[0] // 2
    m = jnp.max(s, axis=-1, keepdims=True)
    e = jnp.exp(s - m)
    l = jnp.sum(e, axis=-1, keepdims=True)
    o2 = jnp.dot(e.astype(BF16), v_pair, preferred_element_type=F32) / l
    lane = lax.broadcasted_iota(jnp.int32, (rows, PAIR_W), 1)
    return jnp.where(lane < HEAD_DIM, o2[:rows], o2[rows:])


def _attn_prompt_body(q_ref, kp_ref, kc_ref, vp_ref, vc_ref, bias_ref, o_ref, kext_ref, vext_ref):
    blk = q_ref.shape[0]
    j = pl.program_id(1)
    kext_ref[0:blk, :] = kp_ref[...].astype(BF16)
    kext_ref[blk:2 * blk, :] = kc_ref[...].astype(BF16)
    vext_ref[0:blk, :] = vp_ref[...].astype(BF16)
    vext_ref[blk:2 * blk, :] = vc_ref[...].astype(BF16)
    def chunk(c, carry):
        q0 = pl.multiple_of(c * CHUNK, CHUNK)
        slab = jnp.where(j == 0, c + 1, 0)
        pairs = [slice(hp * PAIR_W, (hp + 1) * PAIR_W) for hp in range(HEAD_PAIRS)]
        s = jnp.concatenate([_pair_scores(q_ref[pl.ds(q0, CHUNK), lanes], kext_ref[pl.ds(q0, BAND), lanes],
                                          bias_ref[slab, hp]) for hp, lanes in enumerate(pairs)], axis=0)
        m = jnp.max(s, axis=-1, keepdims=True)
        e = jnp.exp(s - m)
        inv = 1.0 / jnp.sum(e, axis=-1, keepdims=True)
        p = e.astype(BF16)
        lane = lax.broadcasted_iota(jnp.int32, (CHUNK, PAIR_W), 1)
        for hp, lanes in enumerate(pairs):
            rows = slice(hp * 2 * CHUNK, (hp + 1) * 2 * CHUNK)
            o2 = jnp.dot(p[rows], vext_ref[pl.ds(q0, BAND), lanes], preferred_element_type=F32) * inv[rows]
            o_ref[pl.ds(q0, CHUNK), lanes] = jnp.where(lane < HEAD_DIM, o2[:CHUNK], o2[CHUNK:])
        return carry

    lax.fori_loop(0, blk // CHUNK, chunk, 0, unroll=4)


def _attn_prompt(z, bias_pairs, batch, seq):
    blk = ATTN_REACH
    per_seq = seq // blk
    rows = batch * seq

    def cur(col):
        return lambda b, j: (b * per_seq + j, col)

    def prev(col):
        return lambda b, j: (b * per_seq + jnp.maximum(j - 1, 0), col)

    return pl.pallas_call(
        _attn_prompt_body,
        out_shape=jax.ShapeDtypeStruct((rows, ATTN_WIDTH), F32),
        grid=(batch, per_seq),
        in_specs=[
            pl.BlockSpec((blk, ATTN_WIDTH), cur(COL_Q)),
            pl.BlockSpec((blk, ATTN_WIDTH), prev(COL_K)),
            pl.BlockSpec((blk, ATTN_WIDTH), cur(COL_K)),
            pl.BlockSpec((blk, ATTN_WIDTH), prev(COL_V)),
            pl.BlockSpec((blk, ATTN_WIDTH), cur(COL_V)),
            pl.BlockSpec((1 + LEFT_CHUNKS, HEAD_PAIRS, 2 * CHUNK, BAND), lambda b, j: (0, 0, 0, 0)),
        ],
        out_specs=pl.BlockSpec((blk, ATTN_WIDTH), lambda b, j: (b * per_seq + j, 0)),
        scratch_shapes=[pltpu.VMEM((2 * blk, ATTN_WIDTH), BF16), pltpu.VMEM((2 * blk, ATTN_WIDTH), BF16)],
        compiler_params=_params(("arbitrary", "arbitrary")),
        name="attn_prompt",
    )(z, z, z, z, z, bias_pairs)


def _attn_sample_body(q_ref, kn_ref, vn_ref, ck_ref, cv_ref, bias_ref, *rest):
    o_ref, ko_ref, vo_ref, kall_ref, vall_ref = rest[-5:]
    keep = ck_ref.shape[2]
    t = q_ref.shape[0]
    ck = ck_ref[0, 0]
    cv = cv_ref[0, 0]
    kn = kn_ref[...]
    vn = vn_ref[...]
    kall_ref[0:keep, :] = ck.astype(BF16)
    kall_ref[keep:keep + t, :] = kn.astype(BF16)
    vall_ref[0:keep, :] = cv.astype(BF16)
    vall_ref[keep:keep + t, :] = vn.astype(BF16)
    for hp in range(HEAD_PAIRS):
        lanes = slice(hp * PAIR_W, (hp + 1) * PAIR_W)
        s = _pair_scores(q_ref[:, lanes], kall_ref[:, lanes], bias_ref[hp])
        o_ref[:, lanes] = _pair_output(s, vall_ref[:, lanes])
    ko_ref[0, 0, 0:keep - t, :] = ck[t:keep]
    ko_ref[0, 0, keep - t:keep, :] = kn
    vo_ref[0, 0, 0:keep - t, :] = cv[t:keep]
    vo_ref[0, 0, keep - t:keep, :] = vn
    for later in range(1, ko_ref.shape[0]):
        ko_ref[later] = jnp.zeros(ko_ref.shape[1:], F32)
        vo_ref[later] = jnp.zeros(vo_ref.shape[1:], F32)


def _attn_sample(z, cache_k, cache_v, bias_pairs, layer, row0, streams, t, k_buf, v_buf):
    depth, _, keep, _ = cache_k.shape
    blk0 = row0 // t
    ins = [z, z, z, cache_k, cache_v, bias_pairs]
    in_specs = [
        pl.BlockSpec((t, ATTN_WIDTH), lambda s: (blk0 + s, COL_Q)),
        pl.BlockSpec((t, ATTN_WIDTH), lambda s: (blk0 + s, COL_K)),
        pl.BlockSpec((t, ATTN_WIDTH), lambda s: (blk0 + s, COL_V)),
        pl.BlockSpec((1, 1, keep, ATTN_WIDTH), lambda s: (layer, s, 0, 0)),
        pl.BlockSpec((1, 1, keep, ATTN_WIDTH), lambda s: (layer, s, 0, 0)),
        pl.BlockSpec((HEAD_PAIRS, 2 * t, keep + t), lambda s: (0, 0, 0)),
    ]
    aliases = {}
    if k_buf is None:
        assert layer == 0
        buf_spec = pl.BlockSpec((depth, 1, keep, ATTN_WIDTH), lambda s: (0, s, 0, 0))
    else:
        aliases = {len(ins): 1, len(ins) + 1: 2}
        ins += [k_buf, v_buf]
        in_specs += [pl.BlockSpec(memory_space=pl.ANY), pl.BlockSpec(memory_space=pl.ANY)]
        buf_spec = pl.BlockSpec((1, 1, keep, ATTN_WIDTH), lambda s: (layer, s, 0, 0))
    buf = jax.ShapeDtypeStruct(cache_k.shape, F32)
    return pl.pallas_call(
        _attn_sample_body,
        out_shape=(jax.ShapeDtypeStruct((streams * t, ATTN_WIDTH), F32), buf, buf),
        grid=(streams,),
        in_specs=in_specs,
        out_specs=(
            pl.BlockSpec((t, ATTN_WIDTH), lambda s: (s, 0)),
            buf_spec,
            buf_spec,
        ),
        scratch_shapes=[pltpu.VMEM((keep + t, ATTN_WIDTH), BF16), pltpu.VMEM((keep + t, ATTN_WIDTH), BF16)],
        input_output_aliases=aliases,
        compiler_params=_params(("arbitrary",)),
        name="attn_sample",
    )(*ins)


def _merge(pooled, u, attn, gp, ga, x, wgrp_ref, scale_ref, wp_ref, wa_ref, wo_ref, g_ref, b_ref):
    pooled = pooled - u
    pool_y = jnp.dot(pooled.astype(BF16), wgrp_ref[...], preferred_element_type=F32) * scale_ref[...]
    mp = jnp.dot(pool_y.astype(BF16), wp_ref[...], preferred_element_type=F32)
    ma = jnp.dot(attn.astype(BF16), wa_ref[...], preferred_element_type=F32)
    m = jax.nn.sigmoid(gp) * mp + jax.nn.sigmoid(ga) * ma
    y = jnp.dot(m.astype(BF16), wo_ref[...], preferred_element_type=F32)
    return _layer_norm(DN_ALPHA * x + y, g_ref[...], b_ref[...])


def _window_means(read, pos, shape_out):
    outs = []
    for g, w in enumerate(POOL_WINDOWS):
        lanes = slice(g * POOL_GROUP, (g + 1) * POOL_GROUP)
        s = read(0, lanes)
        for back in range(1, w):
            s = s + read(back, lanes)
        outs.append((s / jnp.minimum(pos + 1, w).astype(F32)).reshape(shape_out))
    return jnp.concatenate(outs, axis=-1)


def _mix_body(n_prompt_tiles, tiles_per_seq, with_router, u_ref, up_ref, hist_ref, attn_p_ref, attn_s_ref,
              gp_ref, ga_ref, x_ref, wgrp_ref, scale_ref, wp_ref, wa_ref, wo_ref, g_ref, b_ref, *rest):
    if with_router:
        (wr_hi_ref, wr_lo_ref, br_ref, o_ref, mi_ref, mw_ref, cnt_ref,
         buf_ref, sbuf_ref, pooled_ref, attn_ref, carry_ref) = rest
    else:
        o_ref, buf_ref, sbuf_ref, pooled_ref, attn_ref = rest
    t = u_ref.shape[0]
    i = pl.program_id(0)
    u = u_ref[...]

    @pl.when(i < n_prompt_tiles)
    def _():
        tile = i % tiles_per_seq
        buf_ref[0:HIST_ROWS, :] = jnp.where(tile == 0, 0.0, up_ref[...])
        buf_ref[HIST_ROWS:HIST_ROWS + t, :] = u
        pos = tile * t + lax.broadcasted_iota(jnp.int32, (t, 1), 0)
        read = lambda back, lanes: buf_ref[HIST_ROWS - back:HIST_ROWS - back + t, lanes]
        pooled_ref[...] = _window_means(read, pos, (t, POOL_GROUP))
        attn_ref[...] = attn_p_ref[...]

    @pl.when(i == n_prompt_tiles)
    def _():
        streams, hrows, _ = hist_ref.shape
        ts = t // streams
        sbuf_ref[:, 0:hrows, :] = hist_ref[...]
        sbuf_ref[:, hrows:hrows + ts, :] = u.reshape(streams, ts, POOL_WIDTH)
        pos = PAST_LEN + lax.broadcasted_iota(jnp.int32, (1, ts, 1), 1)
        read = lambda back, lanes: sbuf_ref[:, hrows - back:hrows - back + ts, lanes]
        pooled_ref[...] = _window_means(read, pos, (t, POOL_GROUP))
        attn_ref[...] = attn_s_ref[...]

    x1 = _merge(pooled_ref[...], u, attn_ref[...], gp_ref[...], ga_ref[...], x_ref[...],
                wgrp_ref, scale_ref, wp_ref, wa_ref, wo_ref, g_ref, b_ref)
    o_ref[...] = x1
    if with_router:
        _route(x1, wr_hi_ref, wr_lo_ref, br_ref, mi_ref, mw_ref, cnt_ref, carry_ref)


def _mix(z, attn_p, attn_s, x, hist, weights, n_prompt, seq, router=None):
    n = x.shape[0]
    t = n - n_prompt
    n_prompt_tiles = n_prompt // t
    streams, hrows, _ = hist.shape
    hist_per_tile = t // HIST_ROWS
    last_p = n_prompt_tiles - 1
    zero = lambda i: (0, 0)
    rows = lambda width: pl.BlockSpec((t, width), lambda i: (i, 0))
    out_shape = [jax.ShapeDtypeStruct((n, D_MODEL), F32)]
    out_specs = [rows(D_MODEL)]
    router_specs, router_scratch = [], []
    if router is not None:
        router_specs = [pl.BlockSpec((D_MODEL, LANES), zero), pl.BlockSpec((D_MODEL, LANES), zero),
                        pl.BlockSpec((1, LANES), zero)]
        out_shape += [jax.ShapeDtypeStruct((n, LANES), jnp.int32), jax.ShapeDtypeStruct((n, LANES), F32),
                      jax.ShapeDtypeStruct((1, LANES), jnp.int32)]
        out_specs += [rows(LANES), rows(LANES), pl.BlockSpec((1, LANES), zero)]
        router_scratch = [pltpu.VMEM((1, LANES), F32)]
    out = pl.pallas_call(
        functools.partial(_mix_body, n_prompt_tiles, seq // t, router is not None),
        out_shape=out_shape,
        grid=(n_prompt_tiles + 1,),
        in_specs=[
            pl.BlockSpec((t, POOL_WIDTH), lambda i: (i, COL_U)),
            pl.BlockSpec((HIST_ROWS, POOL_WIDTH), lambda i: (jnp.maximum(i * hist_per_tile - 1, 0), COL_U)),
            pl.BlockSpec((streams, hrows, POOL_WIDTH), lambda i: (0, 0, 0)),
            pl.BlockSpec((t, ATTN_WIDTH), lambda i: (jnp.minimum(i, last_p), 0)),
            pl.BlockSpec((t, ATTN_WIDTH), zero),
            pl.BlockSpec((t, D_MODEL), lambda i: (i, 2)),
            pl.BlockSpec((t, D_MODEL), lambda i: (i, 3)),
            pl.BlockSpec((t, D_MODEL), lambda i: (i, 0)),
            pl.BlockSpec((POOL_WIDTH, POOL_WIDTH), zero),
            pl.BlockSpec((1, POOL_WIDTH), zero),
            pl.BlockSpec((POOL_WIDTH, D_MODEL), zero),
            pl.BlockSpec((ATTN_WIDTH, D_MODEL), zero),
            pl.BlockSpec((D_MODEL, D_MODEL), zero),
            pl.BlockSpec((1, D_MODEL), zero),
            pl.BlockSpec((1, D_MODEL), zero),
        ] + router_specs,
        out_specs=out_specs,
        scratch_shapes=[
            pltpu.VMEM((HIST_ROWS + t, POOL_WIDTH), F32),
            pltpu.VMEM((streams, hrows + t // streams, POOL_WIDTH), F32),
            pltpu.VMEM((t, POOL_WIDTH), F32),
            pltpu.VMEM((t, ATTN_WIDTH), F32),
        ] + router_scratch,
        compiler_params=_params(("arbitrary",)),
        name="mix",
    )(z, z, hist, attn_p, attn_s, z, z, x, *weights, *(router or ()))
    return out[0] if router is None else out


def _ffn_dense_body(x_ref, w1_ref, w3_ref, w2_ref, g_ref, b_ref, o_ref):
    x = x_ref[...]
    xb = x.astype(BF16)
    a = jnp.dot(xb, w1_ref[...], preferred_element_type=F32)
    c = jnp.dot(xb, w3_ref[...], preferred_element_type=F32)
    h = (jax.nn.silu(a) * c).astype(BF16)
    f = jnp.dot(h, w2_ref[...], preferred_element_type=F32)
    o_ref[...] = _layer_norm(DN_ALPHA * x + f, g_ref[...], b_ref[...])


def _ffn_dense(x, w1, w3, w2, g, b):
    n = x.shape[0]
    t = 256
    zero = lambda i: (0, 0)
    return pl.pallas_call(
        _ffn_dense_body,
        out_shape=jax.ShapeDtypeStruct((n, D_MODEL), F32),
        grid=(n // t,),
        in_specs=[
            pl.BlockSpec((t, D_MODEL), lambda i: (i, 0)),
            pl.BlockSpec((D_MODEL, D_FF), zero),
            pl.BlockSpec((D_MODEL, D_FF), zero),
            pl.BlockSpec((D_FF, D_MODEL), zero),
            pl.BlockSpec((1, D_MODEL), zero),
            pl.BlockSpec((1, D_MODEL), zero),
        ],
        out_specs=pl.BlockSpec((t, D_MODEL), lambda i: (i, 0)),
        compiler_params=_params(("arbitrary",)),
        name="ffn_dense",
    )(x, w1, w3, w2, g, b)


def _route(x, wr_hi_ref, wr_lo_ref, br_ref, mi_ref, mw_ref, cnt_ref, carry_ref):
    t = x.shape[0]

    @pl.when(pl.program_id(0) == 0)
    def _():
        carry_ref[...] = jnp.zeros_like(carry_ref)

    x_hi = x.astype(BF16)
    x_lo = (x - x_hi.astype(F32)).astype(BF16)
    logits = (jnp.dot(x_hi, wr_hi_ref[...], preferred_element_type=F32)
              + jnp.dot(x_lo, wr_hi_ref[...], preferred_element_type=F32)
              + jnp.dot(x_hi, wr_lo_ref[...], preferred_element_type=F32)) + br_ref[...]
    lane = lax.broadcasted_iota(jnp.int32, (t, LANES), 1)
    logits = jnp.where(lane < N_EXPERTS, logits, -jnp.inf)
    lane_f = lane.astype(F32)
    v0 = jnp.max(logits, axis=-1, keepdims=True)
    e0 = jnp.min(jnp.where(logits == v0, lane_f, float(LANES)), axis=-1, keepdims=True)
    rest = jnp.where(lane_f == e0, -jnp.inf, logits)
    v1 = jnp.max(rest, axis=-1, keepdims=True)
    e1 = jnp.min(jnp.where(rest == v1, lane_f, float(LANES)), axis=-1, keepdims=True)
    ex = jnp.exp(v1 - v0)
    w0 = 1.0 / (1.0 + ex)
    w1 = ex / (1.0 + ex)
    oh0 = (lane_f == e0).astype(F32)
    oh1 = (lane_f == e1).astype(F32)
    r_i = lax.broadcasted_iota(jnp.int32, (t, t), 0)
    c_i = lax.broadcasted_iota(jnp.int32, (t, t), 1)
    tri = (c_i < r_i).astype(BF16)
    pre0 = jnp.dot(tri, oh0.astype(BF16), preferred_element_type=F32)
    pre1 = jnp.dot(tri, oh1.astype(BF16), preferred_element_type=F32)
    cnt0 = jnp.sum(oh0, axis=0, keepdims=True)
    cnt1 = jnp.sum(oh1, axis=0, keepdims=True)
    carry = carry_ref[...]
    rank0 = jnp.sum(oh0 * (carry + pre0), axis=-1, keepdims=True)
    rank1 = jnp.sum(oh1 * (carry + cnt0 + pre1), axis=-1, keepdims=True)
    carry = carry + cnt0 + cnt1
    carry_ref[...] = carry
    cnt_ref[...] = carry.astype(jnp.int32)
    mi = jnp.where(lane == 0, e0, jnp.where(lane == 1, e1, 0.0))
    mi = jnp.where(lane == 2, rank0, jnp.where(lane == 3, rank1, mi))
    mi_ref[...] = mi.astype(jnp.int32)
    mw_ref[...] = jnp.where(lane == 0, w0, jnp.where(lane == 1, w1, 0.0))


def _router_operands(w_r, b_r):
    wr = jnp.zeros((D_MODEL, LANES), F32).at[:, :N_EXPERTS].set(w_r)
    br = jnp.zeros((1, LANES), F32).at[:, :N_EXPERTS].set(b_r[None, :])
    wr_hi = wr.astype(BF16)
    wr_lo = (wr - wr_hi.astype(F32)).astype(BF16)
    return wr_hi, wr_lo, br


def _dispatch_body(dest_ref, last_ref, nused_ref, x_ref, xs_ref, zero_ref, sem_ref):
    t = x_ref.shape[0]
    tm = zero_ref.shape[0]
    n_tiles = xs_ref.shape[0] // tm
    i = pl.program_id(0)

    @pl.when(i == 0)
    def _():
        zero_ref[...] = jnp.zeros_like(zero_ref)

        def fill(tile):
            cp = pltpu.make_async_copy(zero_ref, xs_ref.at[pl.ds(pl.multiple_of(tile * tm, tm), tm), :], sem_ref.at[1])
            cp.start()
            cp.wait()

        for e in range(N_EXPERTS):
            @pl.when(last_ref[e] >= 0)
            def _():
                fill(last_ref[e])

        def unused(tile, carry):
            fill(tile)
            return carry

        lax.fori_loop(nused_ref[0], n_tiles, unused, 0)

    def issue(r, carry):
        for slot in range(2):
            d = dest_ref[2 * r + slot]
            pltpu.make_async_copy(x_ref.at[pl.ds(r, 1), :], xs_ref.at[pl.ds(d, 1), :],
                                  sem_ref.at[0]).start(priority=slot)
        return carry

    lax.fori_loop(0, t, issue, 0, unroll=ISSUE_UNROLL)
    pltpu.make_async_copy(xs_ref.at[pl.ds(0, 2 * t), :], xs_ref.at[pl.ds(0, 2 * t), :], sem_ref.at[0]).wait()


def _dispatch(x, dest, last_tile, n_used, n_tiles):
    n = x.shape[0]
    t = 256
    tm = EXPERT_TILE
    return pl.pallas_call(
        _dispatch_body,
        out_shape=jax.ShapeDtypeStruct((n_tiles * tm, D_MODEL), F32),
        grid=(n // t,),
        in_specs=[
            pl.BlockSpec((2 * t,), lambda i: (i,), memory_space=pltpu.SMEM),
            pl.BlockSpec(memory_space=pltpu.SMEM),
            pl.BlockSpec(memory_space=pltpu.SMEM),
            pl.BlockSpec((t, D_MODEL), lambda i: (i, 0)),
        ],
        out_specs=pl.BlockSpec(memory_space=pl.ANY),
        scratch_shapes=[pltpu.VMEM((tm, D_MODEL), F32), pltpu.SemaphoreType.DMA((2,))],
        compiler_params=_params(("arbitrary",)),
        name="dispatch",
    )(dest, last_tile, n_used, x)


def _experts_body(te_ref, nused_ref, x_ref, w1_ref, w3_ref, w2_ref, o_ref):
    del te_ref
    i = pl.program_id(0)

    @pl.when(i < nused_ref[0])
    def _():
        xb = x_ref[...].astype(BF16)
        f = None
        for k in range(FF_SPLIT):
            cols = slice(k * (D_FF // FF_SPLIT), (k + 1) * (D_FF // FF_SPLIT))
            a = jnp.dot(xb, w1_ref[0, :, cols], preferred_element_type=F32)
            c = jnp.dot(xb, w3_ref[0, :, cols], preferred_element_type=F32)
            h = (jax.nn.silu(a) * c).astype(BF16)
            part = jnp.dot(h, w2_ref[0, cols, :], preferred_element_type=F32)
            f = part if f is None else f + part
        o_ref[...] = f

    @pl.when(i >= nused_ref[0])
    def _():
        o_ref[...] = jnp.zeros_like(o_ref)


def _experts(xs, tile_expert, n_used, w1, w3, w2):
    tm = EXPERT_TILE
    n_tiles = xs.shape[0] // tm

    def xmap(i, te, nu):
        return (jnp.minimum(i, nu[0] - 1), 0)

    def wmap(i, te, nu):
        return (te[jnp.minimum(i, nu[0] - 1)], 0, 0)

    return pl.pallas_call(
        _experts_body,
        out_shape=jax.ShapeDtypeStruct(xs.shape, F32),
        grid_spec=pltpu.PrefetchScalarGridSpec(
            num_scalar_prefetch=2,
            grid=(n_tiles,),
            in_specs=[
                pl.BlockSpec((tm, D_MODEL), xmap),
                pl.BlockSpec((1, D_MODEL, D_FF), wmap),
                pl.BlockSpec((1, D_MODEL, D_FF), wmap),
                pl.BlockSpec((1, D_FF, D_MODEL), wmap),
            ],
            out_specs=pl.BlockSpec((tm, D_MODEL), lambda i, te, nu: (i, 0)),
        ),
        compiler_params=_params(("arbitrary",), 60 * 1024 * 1024),
        name="experts",
    )(tile_expert, n_used, xs, w1, w3, w2)


def _combine_body(dest_ref, x_ref, mw_ref, g_ref, b_ref, ys_ref, o_ref, y0_ref, y1_ref, sem_ref):
    t = x_ref.shape[0]

    def issue(r, carry):
        pltpu.make_async_copy(ys_ref.at[pl.ds(dest_ref[2 * r], 1), :], y0_ref.at[pl.ds(r, 1), :],
                              sem_ref.at[0]).start(priority=0)
        pltpu.make_async_copy(ys_ref.at[pl.ds(dest_ref[2 * r + 1], 1), :], y1_ref.at[pl.ds(r, 1), :],
                              sem_ref.at[0]).start(priority=1)
        return carry

    lax.fori_loop(0, t, issue, 0, unroll=ISSUE_UNROLL)
    pltpu.make_async_copy(ys_ref.at[pl.ds(0, t), :], y0_ref, sem_ref.at[0]).wait()
    pltpu.make_async_copy(ys_ref.at[pl.ds(0, t), :], y1_ref, sem_ref.at[0]).wait()
    mw = mw_ref[...]
    f = mw[:, 0:1] * y0_ref[...] + mw[:, 1:2] * y1_ref[...]
    o_ref[...] = _layer_norm(DN_ALPHA * x_ref[...] + f, g_ref[...], b_ref[...])


def _combine(x, ys, dest, mw, g, b):
    n = x.shape[0]
    t = 256
    zero = lambda i: (0, 0)
    return pl.pallas_call(
        _combine_body,
        out_shape=jax.ShapeDtypeStruct((n, D_MODEL), F32),
        grid=(n // t,),
        in_specs=[
            pl.BlockSpec((2 * t,), lambda i: (i,), memory_space=pltpu.SMEM),
            pl.BlockSpec((t, D_MODEL), lambda i: (i, 0)),
            pl.BlockSpec((t, LANES), lambda i: (i, 0)),
            pl.BlockSpec((1, D_MODEL), zero),
            pl.BlockSpec((1, D_MODEL), zero),
            pl.BlockSpec(memory_space=pl.ANY),
        ],
        out_specs=pl.BlockSpec((t, D_MODEL), lambda i: (i, 0)),
        scratch_shapes=[pltpu.VMEM((t, D_MODEL), F32), pltpu.VMEM((t, D_MODEL), F32),
                        pltpu.SemaphoreType.DMA((1,))],
        compiler_params=_params(("arbitrary",)),
        name="combine",
    )(dest, x, mw, g, b, ys)


def _moe(x, mi, mw, cnt, w1, w3, w2, g, b):
    n = x.shape[0]
    tm = EXPERT_TILE
    n_tiles = (2 * n + N_EXPERTS * (tm - 1)) // tm
    counts = cnt[0, :N_EXPERTS]
    tiles = (counts + tm - 1) // tm
    tile_end = jnp.cumsum(tiles)
    start = (tile_end - tiles) * tm
    n_used = tile_end[-1:].astype(jnp.int32)
    tile_expert = jnp.sum((tile_end[None, :] <= jnp.arange(n_tiles)[:, None]).astype(jnp.int32), axis=1)
    tile_expert = jnp.minimum(tile_expert, N_EXPERTS - 1)
    last_tile = jnp.where(tiles > 0, tile_end - 1, -1).astype(jnp.int32)
    experts = mi[:, 0:2]
    group_start = sum(jnp.where(experts == e, start[e], 0) for e in range(N_EXPERTS))
    dest = (group_start + mi[:, 2:4]).astype(jnp.int32).reshape(2 * n)
    xs = _dispatch(x, dest, last_tile, n_used, n_tiles)
    ys = _experts(xs, tile_expert, n_used, w1, w3, w2)
    return _combine(x, ys, dest, mw, g, b)


def _block_diag(w):
    g, c, _ = w.shape
    out = jnp.zeros((g * c, g * c), w.dtype)
    for i in range(g):
        out = out.at[i * c:(i + 1) * c, i * c:(i + 1) * c].set(w[i])
    return out


def _bias_pairs(table, t_q, n_keys, offset):
    hi = t_q - 1 + offset
    span = t_q - 1 + n_keys
    cols = np.clip(hi - np.arange(span), -REL_CLIP, REL_CLIP) + REL_CLIP
    rev = table[:, cols].astype(F32)
    skew = jnp.tile(rev, (1, t_q + 1))[:, :t_q * (span + 1)].reshape(N_HEADS, t_q, span + 1)
    slab = skew[:, ::-1, :n_keys]
    return slab.reshape(HEAD_PAIRS, 2 * t_q, n_keys)


def _prompt_bias_slabs(table):
    bias = _bias_pairs(table, CHUNK, BAND, ATTN_REACH)
    col = np.arange(BAND)[None, :]
    first_valid = np.concatenate([[0], (LEFT_CHUNKS - np.arange(LEFT_CHUNKS)) * CHUNK])[:, None]
    valid = jnp.asarray(col >= first_valid)
    return jnp.where(valid[:, None, None, :], bias[None], NEG_INF)


def kernel(x_prompt, x_sample, cache_k, cache_v, state_pool, w_in, b_in, w_pool_grp, pool_scale,
           rel_table, w_pool_br, w_attn_br, w_out, ln1_g, ln1_b, ln2_g, ln2_b,
           w1_dense, w3_dense, w2_dense, w_router, b_router, w1_exp, w3_exp, w2_exp):
    bp, tp, d = x_prompt.shape
    bs, ts, _ = x_sample.shape
    n_p, n_s = bp * tp, bs * ts
    n = n_p + n_s
    depth = w_in.shape[0]
    keep_s = cache_k.shape[2]
    keep_p = min(ATTN_REACH, tp)

    x = jnp.concatenate([x_prompt.reshape(n_p, d), x_sample.reshape(n_s, d)], axis=0)
    ck = cache_k.reshape(depth, bs, keep_s, ATTN_WIDTH)
    cv = cache_v.reshape(depth, bs, keep_s, ATTN_WIDTH)
    hist = jnp.pad(state_pool, ((0, 0), (0, 0), (HIST_ROWS - POOL_HIST, 0), (0, 0)))
    row = lambda v: v[None, :].astype(F32)

    k_buf = v_buf = None
    kp_new, vp_new, pp_new, ps_new = [], [], [], []
    for l in range(depth):
        wgrp = _block_diag(w_pool_grp[l]).astype(BF16)
        weights = (wgrp, row(pool_scale[l]), w_pool_br[l].astype(BF16), w_attn_br[l].astype(BF16),
                   w_out[l].astype(BF16), row(ln1_g[l]), row(ln1_b[l]))
        z = _in_proj(x, w_in[l].astype(BF16), row(b_in[l]))
        attn_p = _attn_prompt(z, _prompt_bias_slabs(rel_table[l]), bp, tp)
        attn_s, k_buf, v_buf = _attn_sample(z, ck, cv, _bias_pairs(rel_table[l], ts, keep_s + ts, keep_s),
                                            l, n_p, bs, ts, k_buf, v_buf)
        j = l // 2
        if l % 2 == 0:
            x1 = _mix(z, attn_p, attn_s, x, hist[l], weights, n_p, tp)
            x = _ffn_dense(x1, w1_dense[j].astype(BF16), w3_dense[j].astype(BF16), w2_dense[j].astype(BF16),
                           row(ln2_g[l]), row(ln2_b[l]))
        else:
            x1, mi, mw, cnt = _mix(z, attn_p, attn_s, x, hist[l], weights, n_p, tp,
                                   router=_router_operands(w_router[j], b_router[j]))
            x = _moe(x1, mi, mw, cnt, w1_exp[j].astype(BF16), w3_exp[j].astype(BF16),
                     w2_exp[j].astype(BF16), row(ln2_g[l]), row(ln2_b[l]))
        def tail(rows, col0, width, z=z):
            return jnp.stack([lax.slice(z, ((b + 1) * tp - rows, col0), ((b + 1) * tp, col0 + width))
                              for b in range(bp)])

        kp_new.append(tail(keep_p, COL_K * ATTN_WIDTH, ATTN_WIDTH).reshape(bp, keep_p, N_HEADS, HEAD_DIM))
        vp_new.append(tail(keep_p, COL_V * ATTN_WIDTH, ATTN_WIDTH).reshape(bp, keep_p, N_HEADS, HEAD_DIM))
        pp_new.append(tail(POOL_HIST, 0, POOL_WIDTH))
        us = z[n_p:, :POOL_WIDTH].reshape(bs, ts, POOL_WIDTH)
        ps_new.append(jnp.concatenate([state_pool[l], us], axis=1)[:, -POOL_HIST:])

    shape_s = (depth, bs, keep_s, N_HEADS, HEAD_DIM)
    return (x[:n_p].reshape(bp, tp, d), x[n_p:].reshape(bs, ts, d),
            jnp.stack(kp_new), jnp.stack(vp_new), jnp.stack(pp_new),
            k_buf.reshape(shape_s), v_buf.reshape(shape_s), jnp.stack(ps_new))
```

```python
import functools

import jax
import jax.numpy as jnp
import numpy as np
from jax import lax
from jax.experimental import pallas as pl
from jax.experimental.pallas import tpu as pltpu

F32 = jnp.float32
BF16 = jnp.bfloat16

D_MODEL = 1024
N_HEADS = 8
HEAD_DIM = 64
ATTN_WIDTH = N_HEADS * HEAD_DIM
CHUNK = 64
LEFT_CHUNKS = 8
BAND = (LEFT_CHUNKS + 1) * CHUNK
ATTN_REACH = LEFT_CHUNKS * CHUNK
REL_CLIP = 256
ATTN_SCALE = HEAD_DIM ** -0.5
POOL_WIDTH = 512
POOL_WINDOWS = (2, 4, 8, 16)
POOL_GROUP = POOL_WIDTH // len(POOL_WINDOWS)
POOL_HIST = max(POOL_WINDOWS) - 1
HIST_ROWS = POOL_HIST + 1
D_FF = 2816
N_EXPERTS = 8
PAST_LEN = 4096
DEPTH = 2
DN_ALPHA = (2 * DEPTH) ** 0.25
LN_EPS = 1e-5
NEG_INF = -1e30
IN_WIDTH = POOL_WIDTH + 3 * ATTN_WIDTH + 2 * D_MODEL
HEAD_PAIRS = N_HEADS // 2
PAIR_W = 2 * HEAD_DIM
LANES = 128

Z32_WIDTH = POOL_WIDTH + 2 * ATTN_WIDTH
Z16_WIDTH = 2 * D_MODEL + ATTN_WIDTH
COL_U, COL_K, COL_V = 0, 1, 2
COL_GP, COL_GA = 0, 1
COL_Q = 2 * D_MODEL // ATTN_WIDTH

EXPERT_TILE = 512
FF_SPLIT = 2
ISSUE_UNROLL = 8
VMEM_LIMIT = 56 * 1024 * 1024


def _pick(n, candidates):
    for c in candidates:
        if n % c == 0:
            return c
    raise ValueError(f"no tile in {candidates} divides {n}")


def _params(sem, vmem=None):
    return pltpu.CompilerParams(dimension_semantics=sem, vmem_limit_bytes=vmem or VMEM_LIMIT)


def _layer_norm(r, g, b):
    mu = jnp.mean(r, axis=-1, keepdims=True)
    c = r - mu
    var = jnp.mean(c * c, axis=-1, keepdims=True)
    return c * lax.rsqrt(var + LN_EPS) * g + b


def _row_sources(x, t):
    if not isinstance(x, tuple):
        return [x], [pl.BlockSpec((t, x.shape[1]), lambda i: (i, 0))], (lambda refs, i: refs[0][...])
    first, last = x
    assert last.shape[0] == t and first.shape[0] % t == 0
    n_first = first.shape[0] // t
    specs = [pl.BlockSpec((t, first.shape[1]), lambda i: (jnp.minimum(i, n_first - 1), 0)),
             pl.BlockSpec((t, last.shape[1]), lambda i: (0, 0))]
    return [first, last], specs, (lambda refs, i: jnp.where(i < n_first, refs[0][...], refs[1][...]))


def _inproj_body(n_src, select, *refs):
    w_ref, b_ref, z32_ref, z16_ref = refs[n_src:]
    x = select(refs[:n_src], pl.program_id(0)).astype(BF16)
    z = jnp.dot(x, w_ref[...], preferred_element_type=F32) + b_ref[...]
    z32_ref[...] = z[:, :Z32_WIDTH]
    z16_ref[...] = z[:, Z32_WIDTH:].astype(BF16)


def _in_proj(x, w_bf, b, t):
    srcs, specs, select = _row_sources(x, t)
    n = sum(s.shape[0] for s in srcs)
    zero = lambda i: (0, 0)
    return pl.pallas_call(
        functools.partial(_inproj_body, len(srcs), select),
        out_shape=(jax.ShapeDtypeStruct((n, Z32_WIDTH), F32), jax.ShapeDtypeStruct((n, Z16_WIDTH), BF16)),
        grid=(n // t,),
        in_specs=specs + [pl.BlockSpec((D_MODEL, IN_WIDTH), zero), pl.BlockSpec((1, IN_WIDTH), zero)],
        out_specs=(pl.BlockSpec((t, Z32_WIDTH), lambda i: (i, 0)), pl.BlockSpec((t, Z16_WIDTH), lambda i: (i, 0))),
        compiler_params=_params(("arbitrary",)),
        name="in_proj",
    )(*srcs, w_bf, b)


def _reorder_in_columns(w):
    o = [0, POOL_WIDTH, POOL_WIDTH + ATTN_WIDTH, POOL_WIDTH + 2 * ATTN_WIDTH, POOL_WIDTH + 3 * ATTN_WIDTH, IN_WIDTH]
    u, q, k, v, g = (w[..., o[i]:o[i + 1]] for i in range(5))
    return jnp.concatenate([u, k, v, g, q], axis=-1)


def _pair_scores(q_pair, k_pair, bias):
    lane = lax.broadcasted_iota(jnp.int32, q_pair.shape, 1)
    qs = q_pair.astype(F32) * ATTN_SCALE
    q2 = jnp.concatenate([jnp.where(lane < HEAD_DIM, qs, 0.0), jnp.where(lane >= HEAD_DIM, qs, 0.0)], axis=0)
    s = lax.dot_general(q2.astype(BF16), k_pair, (((1,), (1,)), ((), ())), preferred_element_type=F32)
    return s + bias


def _pair_output(s, v_pair):
    rows = s.shape[0] // 2
    m = jnp.max(s, axis=-1, keepdims=True)
    e = jnp.exp(s - m)
    l = jnp.sum(e, axis=-1, keepdims=True)
    o2 = jnp.dot(e.astype(BF16), v_pair, preferred_element_type=F32) / l
    lane = lax.broadcasted_iota(jnp.int32, (rows, PAIR_W), 1)
    return jnp.where(lane < HEAD_DIM, o2[:rows], o2[rows:])


def _attn_prompt_body(q_ref, kp_ref, kc_ref, vp_ref, vc_ref, bias_ref, o_ref, kext_ref, vext_ref):
    blk = q_ref.shape[0]
    j = pl.program_id(1)
    kext_ref[0:blk, :] = kp_ref[...].astype(BF16)
    kext_ref[blk:2 * blk, :] = kc_ref[...].astype(BF16)
    vext_ref[0:blk, :] = vp_ref[...].astype(BF16)
    vext_ref[blk:2 * blk, :] = vc_ref[...].astype(BF16)
    def chunk(c, carry):
        q0 = pl.multiple_of(c * CHUNK, CHUNK)
        slab = jnp.where(j == 0, c + 1, 0)
        pairs = [slice(hp * PAIR_W, (hp + 1) * PAIR_W) for hp in range(HEAD_PAIRS)]
        s = jnp.concatenate([_pair_scores(q_ref[pl.ds(q0, CHUNK), lanes], kext_ref[pl.ds(q0, BAND), lanes],
                                          bias_ref[slab, hp]) for hp, lanes in enumerate(pairs)], axis=0)
        m = jnp.max(s, axis=-1, keepdims=True)
        e = jnp.exp(s - m)
        inv = 1.0 / jnp.sum(e, axis=-1, keepdims=True)
        p = e.astype(BF16)
        lane = lax.broadcasted_iota(jnp.int32, (CHUNK, PAIR_W), 1)
        for hp, lanes in enumerate(pairs):
            rows = slice(hp * 2 * CHUNK, (hp + 1) * 2 * CHUNK)
            o2 = jnp.dot(p[rows], vext_ref[pl.ds(q0, BAND), lanes], preferred_element_type=F32) * inv[rows]
            o_ref[pl.ds(q0, CHUNK), lanes] = jnp.where(lane < HEAD_DIM, o2[:CHUNK], o2[CHUNK:])
        return carry

    lax.fori_loop(0, blk // CHUNK, chunk, 0, unroll=4)


def _attn_prompt(z32, z16, bias_pairs, batch, seq):
    blk = ATTN_REACH
    per_seq = seq // blk
    rows = batch * seq

    def cur(col):
        return lambda b, j: (b * per_seq + j, col)

    def prev(col):
        return lambda b, j: (b * per_seq + jnp.maximum(j - 1, 0), col)

    return pl.pallas_call(
        _attn_prompt_body,
        out_shape=jax.ShapeDtypeStruct((rows, ATTN_WIDTH), F32),
        grid=(batch, per_seq),
        in_specs=[
            pl.BlockSpec((blk, ATTN_WIDTH), cur(COL_Q)),
            pl.BlockSpec((blk, ATTN_WIDTH), prev(COL_K)),
            pl.BlockSpec((blk, ATTN_WIDTH), cur(COL_K)),
            pl.BlockSpec((blk, ATTN_WIDTH), prev(COL_V)),
            pl.BlockSpec((blk, ATTN_WIDTH), cur(COL_V)),
            pl.BlockSpec((1 + LEFT_CHUNKS, HEAD_PAIRS, 2 * CHUNK, BAND), lambda b, j: (0, 0, 0, 0)),
        ],
        out_specs=pl.BlockSpec((blk, ATTN_WIDTH), lambda b, j: (b * per_seq + j, 0)),
        scratch_shapes=[pltpu.VMEM((2 * blk, ATTN_WIDTH), BF16), pltpu.VMEM((2 * blk, ATTN_WIDTH), BF16)],
        compiler_params=_params(("arbitrary", "arbitrary")),
        name="attn_prompt",
    )(z16, z32, z32, z32, z32, bias_pairs)


def _attn_sample_body(q_ref, kn_ref, vn_ref, ck_ref, cv_ref, bias_ref, *rest):
    o_ref, ko_ref, vo_ref, kall_ref, vall_ref = rest[-5:]
    keep = ck_ref.shape[2]
    t = q_ref.shape[0]
    ck = ck_ref[0, 0]
    cv = cv_ref[0, 0]
    kn = kn_ref[...]
    vn = vn_ref[...]
    kall_ref[0:keep, :] = ck.astype(BF16)
    kall_ref[keep:keep + t, :] = kn.astype(BF16)
    vall_ref[0:keep, :] = cv.astype(BF16)
    vall_ref[keep:keep + t, :] = vn.astype(BF16)
    for hp in range(HEAD_PAIRS):
        lanes = slice(hp * PAIR_W, (hp + 1) * PAIR_W)
        s = _pair_scores(q_ref[:, lanes], kall_ref[:, lanes], bias_ref[hp])
        o_ref[:, lanes] = _pair_output(s, vall_ref[:, lanes])
    ko_ref[0, 0, 0:keep - t, :] = ck[t:keep]
    ko_ref[0, 0, keep - t:keep, :] = kn
    vo_ref[0, 0, 0:keep - t, :] = cv[t:keep]
    vo_ref[0, 0, keep - t:keep, :] = vn
    for later in range(1, ko_ref.shape[0]):
        ko_ref[later] = jnp.zeros(ko_ref.shape[1:], F32)
        vo_ref[later] = jnp.zeros(vo_ref.shape[1:], F32)


def _attn_sample(z32, z16, cache_k, cache_v, bias_pairs, layer, row0, streams, t, k_buf, v_buf):
    depth, _, keep, _ = cache_k.shape
    blk0 = row0 // t
    ins = [z16, z32, z32, cache_k, cache_v, bias_pairs]
    in_specs = [
        pl.BlockSpec((t, ATTN_WIDTH), lambda s: (blk0 + s, COL_Q)),
        pl.BlockSpec((t, ATTN_WIDTH), lambda s: (blk0 + s, COL_K)),
        pl.BlockSpec((t, ATTN_WIDTH), lambda s: (blk0 + s, COL_V)),
        pl.BlockSpec((1, 1, keep, ATTN_WIDTH), lambda s: (layer, s, 0, 0)),
        pl.BlockSpec((1, 1, keep, ATTN_WIDTH), lambda s: (layer, s, 0, 0)),
        pl.BlockSpec((HEAD_PAIRS, 2 * t, keep + t), lambda s: (0, 0, 0)),
    ]
    aliases = {}
    if k_buf is None:
        assert layer == 0
        buf_spec = pl.BlockSpec((depth, 1, keep, ATTN_WIDTH), lambda s: (0, s, 0, 0))
    else:
        aliases = {len(ins): 1, len(ins) + 1: 2}
        ins += [k_buf, v_buf]
        in_specs += [pl.BlockSpec(memory_space=pl.ANY), pl.BlockSpec(memory_space=pl.ANY)]
        buf_spec = pl.BlockSpec((1, 1, keep, ATTN_WIDTH), lambda s: (layer, s, 0, 0))
    buf = jax.ShapeDtypeStruct(cache_k.shape, F32)
    return pl.pallas_call(
        _attn_sample_body,
        out_shape=(jax.ShapeDtypeStruct((streams * t, ATTN_WIDTH), F32), buf, buf),
        grid=(streams,),
        in_specs=in_specs,
        out_specs=(
            pl.BlockSpec((t, ATTN_WIDTH), lambda s: (s, 0)),
            buf_spec,
            buf_spec,
        ),
        scratch_shapes=[pltpu.VMEM((keep + t, ATTN_WIDTH), BF16), pltpu.VMEM((keep + t, ATTN_WIDTH), BF16)],
        input_output_aliases=aliases,
        compiler_params=_params(("arbitrary",)),
        name="attn_sample",
    )(*ins)


def _merge(pooled, u, attn, gp, ga, x, wgrp_ref, scale_ref, wp_ref, wa_ref, wo_ref, g_ref, b_ref):
    pooled = pooled - u
    pool_y = jnp.dot(pooled.astype(BF16), wgrp_ref[...], preferred_element_type=F32) * scale_ref[...]
    mp = jnp.dot(pool_y.astype(BF16), wp_ref[...], preferred_element_type=F32)
    ma = jnp.dot(attn.astype(BF16), wa_ref[...], preferred_element_type=F32)
    m = jax.nn.sigmoid(gp) * mp + jax.nn.sigmoid(ga) * ma
    y = jnp.dot(m.astype(BF16), wo_ref[...], preferred_element_type=F32)
    return _layer_norm(DN_ALPHA * x + y, g_ref[...], b_ref[...])


def _window_means(read, pos, shape_out):
    outs = []
    for g, w in enumerate(POOL_WINDOWS):
        lanes = slice(g * POOL_GROUP, (g + 1) * POOL_GROUP)
        s = read(0, lanes)
        for back in range(1, w):
            s = s + read(back, lanes)
        outs.append((s / jnp.minimum(pos + 1, w).astype(F32)).reshape(shape_out))
    return jnp.concatenate(outs, axis=-1)


def _mix_body(n_prompt_tiles, tiles_per_seq, with_router, n_src, select,
              u_ref, up_ref, hist_ref, attn_p_ref, attn_s_ref, gp_ref, ga_ref, *rest):
    x_refs, rest = rest[:n_src], rest[n_src:]
    (wgrp_ref, scale_ref, wp_ref, wa_ref, wo_ref, g_ref, b_ref), rest = rest[:7], rest[7:]
    if with_router:
        (wr_hi_ref, wr_lo_ref, br_ref, o_ref, mi_ref, mw_ref, cnt_ref,
         buf_ref, sbuf_ref, pooled_ref, attn_ref, carry_ref) = rest
    else:
        o_ref, buf_ref, sbuf_ref, pooled_ref, attn_ref = rest
    t = u_ref.shape[0]
    i = pl.program_id(0)
    u = u_ref[...]

    @pl.when(i < n_prompt_tiles)
    def _():
        tile = i % tiles_per_seq
        buf_ref[0:HIST_ROWS, :] = jnp.where(tile == 0, 0.0, up_ref[...])
        buf_ref[HIST_ROWS:HIST_ROWS + t, :] = u
        pos = tile * t + lax.broadcasted_iota(jnp.int32, (t, 1), 0)
        read = lambda back, lanes: buf_ref[HIST_ROWS - back:HIST_ROWS - back + t, lanes]
        pooled_ref[...] = _window_means(read, pos, (t, POOL_GROUP))
        attn_ref[...] = attn_p_ref[...]

    @pl.when(i == n_prompt_tiles)
    def _():
        streams, hrows, _ = hist_ref.shape
        ts = t // streams
        sbuf_ref[:, 0:hrows, :] = hist_ref[...]
        sbuf_ref[:, hrows:hrows + ts, :] = u.reshape(streams, ts, POOL_WIDTH)
        pos = PAST_LEN + lax.broadcasted_iota(jnp.int32, (1, ts, 1), 1)
        read = lambda back, lanes: sbuf_ref[:, hrows - back:hrows - back + ts, lanes]
        pooled_ref[...] = _window_means(read, pos, (t, POOL_GROUP))
        attn_ref[...] = attn_s_ref[...]

    x1 = _merge(pooled_ref[...], u, attn_ref[...], gp_ref[...].astype(F32), ga_ref[...].astype(F32),
                select(x_refs, i), wgrp_ref, scale_ref, wp_ref, wa_ref, wo_ref, g_ref, b_ref)
    o_ref[...] = x1
    if with_router:
        _route(x1, wr_hi_ref, wr_lo_ref, br_ref, mi_ref, mw_ref, cnt_ref, carry_ref)


def _mix(z32, z16, attn_p, attn_s, x, hist, weights, n_prompt, seq, router=None):
    n = z32.shape[0]
    t = n - n_prompt
    x_srcs, x_specs, select = _row_sources(x, t)
    n_prompt_tiles = n_prompt // t
    streams, hrows, _ = hist.shape
    hist_per_tile = t // HIST_ROWS
    last_p = n_prompt_tiles - 1
    zero = lambda i: (0, 0)
    rows = lambda width: pl.BlockSpec((t, width), lambda i: (i, 0))
    out_shape = [jax.ShapeDtypeStruct((n, D_MODEL), F32)]
    out_specs = [rows(D_MODEL)]
    router_specs, router_scratch = [], []
    if router is not None:
        router_specs = [pl.BlockSpec((D_MODEL, LANES), zero), pl.BlockSpec((D_MODEL, LANES), zero),
                        pl.BlockSpec((1, LANES), zero)]
        out_shape += [jax.ShapeDtypeStruct((n, LANES), jnp.int32), jax.ShapeDtypeStruct((n, LANES), F32),
                      jax.ShapeDtypeStruct((1, LANES), jnp.int32)]
        out_specs += [rows(LANES), rows(LANES), pl.BlockSpec((1, LANES), zero)]
        router_scratch = [pltpu.VMEM((1, LANES), F32)]
    out = pl.pallas_call(
        functools.partial(_mix_body, n_prompt_tiles, seq // t, router is not None, len(x_srcs), select),
        out_shape=out_shape,
        grid=(n_prompt_tiles + 1,),
        in_specs=[
            pl.BlockSpec((t, POOL_WIDTH), lambda i: (i, COL_U)),
            pl.BlockSpec((HIST_ROWS, POOL_WIDTH), lambda i: (jnp.maximum(i * hist_per_tile - 1, 0), COL_U)),
            pl.BlockSpec((streams, hrows, POOL_WIDTH), lambda i: (0, 0, 0)),
            pl.BlockSpec((t, ATTN_WIDTH), lambda i: (jnp.minimum(i, last_p), 0)),
            pl.BlockSpec((t, ATTN_WIDTH), zero),
            pl.BlockSpec((t, D_MODEL), lambda i: (i, COL_GP)),
            pl.BlockSpec((t, D_MODEL), lambda i: (i, COL_GA)),
        ] + x_specs + [
            pl.BlockSpec((POOL_WIDTH, POOL_WIDTH), zero),
            pl.BlockSpec((1, POOL_WIDTH), zero),
            pl.BlockSpec((POOL_WIDTH, D_MODEL), zero),
            pl.BlockSpec((ATTN_WIDTH, D_MODEL), zero),
            pl.BlockSpec((D_MODEL, D_MODEL), zero),
            pl.BlockSpec((1, D_MODEL), zero),
            pl.BlockSpec((1, D_MODEL), zero),
        ] + router_specs,
        out_specs=out_specs,
        scratch_shapes=[
            pltpu.VMEM((HIST_ROWS + t, POOL_WIDTH), F32),
            pltpu.VMEM((streams, hrows + t // streams, POOL_WIDTH), F32),
            pltpu.VMEM((t, POOL_WIDTH), F32),
            pltpu.VMEM((t, ATTN_WIDTH), F32),
        ] + router_scratch,
        compiler_params=_params(("arbitrary",)),
        name="mix",
    )(z32, z32, hist, attn_p, attn_s, z16, z16, *x_srcs, *weights, *(router or ()))
    return out[0] if router is None else out


def _ffn_dense_body(x_ref, w1_ref, w3_ref, w2_ref, g_ref, b_ref, o_ref):
    x = x_ref[...]
    xb = x.astype(BF16)
    a = jnp.dot(xb, w1_ref[...], preferred_element_type=F32)
    c = jnp.dot(xb, w3_ref[...], preferred_element_type=F32)
    h = (jax.nn.silu(a) * c).astype(BF16)
    f = jnp.dot(h, w2_ref[...], preferred_element_type=F32)
    o_ref[...] = _layer_norm(DN_ALPHA * x + f, g_ref[...], b_ref[...])


def _ffn_dense(x, w1, w3, w2, g, b):
    n = x.shape[0]
    t = 256
    zero = lambda i: (0, 0)
    return pl.pallas_call(
        _ffn_dense_body,
        out_shape=jax.ShapeDtypeStruct((n, D_MODEL), F32),
        grid=(n // t,),
        in_specs=[
            pl.BlockSpec((t, D_MODEL), lambda i: (i, 0)),
            pl.BlockSpec((D_MODEL, D_FF), zero),
            pl.BlockSpec((D_MODEL, D_FF), zero),
            pl.BlockSpec((D_FF, D_MODEL), zero),
            pl.BlockSpec((1, D_MODEL), zero),
            pl.BlockSpec((1, D_MODEL), zero),
        ],
        out_specs=pl.BlockSpec((t, D_MODEL), lambda i: (i, 0)),
        compiler_params=_params(("arbitrary",)),
        name="ffn_dense",
    )(x, w1, w3, w2, g, b)


def _route(x, wr_hi_ref, wr_lo_ref, br_ref, mi_ref, mw_ref, cnt_ref, carry_ref):
    t = x.shape[0]

    @pl.when(pl.program_id(0) == 0)
    def _():
        carry_ref[...] = jnp.zeros_like(carry_ref)

    x_hi = x.astype(BF16)
    x_lo = (x - x_hi.astype(F32)).astype(BF16)
    logits = (jnp.dot(x_hi, wr_hi_ref[...], preferred_element_type=F32)
              + jnp.dot(x_lo, wr_hi_ref[...], preferred_element_type=F32)
              + jnp.dot(x_hi, wr_lo_ref[...], preferred_element_type=F32)) + br_ref[...]
    lane = lax.broadcasted_iota(jnp.int32, (t, LANES), 1)
    logits = jnp.where(lane < N_EXPERTS, logits, -jnp.inf)
    lane_f = lane.astype(F32)
    v0 = jnp.max(logits, axis=-1, keepdims=True)
    e0 = jnp.min(jnp.where(logits == v0, lane_f, float(LANES)), axis=-1, keepdims=True)
    rest = jnp.where(lane_f == e0, -jnp.inf, logits)
    v1 = jnp.max(rest, axis=-1, keepdims=True)
    e1 = jnp.min(jnp.where(rest == v1, lane_f, float(LANES)), axis=-1, keepdims=True)
    ex = jnp.exp(v1 - v0)
    w0 = 1.0 / (1.0 + ex)
    w1 = ex / (1.0 + ex)
    oh0 = (lane_f == e0).astype(F32)
    oh1 = (lane_f == e1).astype(F32)
    r_i = lax.broadcasted_iota(jnp.int32, (t, t), 0)
    c_i = lax.broadcasted_iota(jnp.int32, (t, t), 1)
    tri = (c_i < r_i).astype(BF16)
    pre0 = jnp.dot(tri, oh0.astype(BF16), preferred_element_type=F32)
    pre1 = jnp.dot(tri, oh1.astype(BF16), preferred_element_type=F32)
    cnt0 = jnp.sum(oh0, axis=0, keepdims=True)
    cnt1 = jnp.sum(oh1, axis=0, keepdims=True)
    carry = carry_ref[...]
    rank0 = jnp.sum(oh0 * (carry + pre0), axis=-1, keepdims=True)
    rank1 = jnp.sum(oh1 * (carry + cnt0 + pre1), axis=-1, keepdims=True)
    carry = carry + cnt0 + cnt1
    carry_ref[...] = carry
    cnt_ref[...] = carry.astype(jnp.int32)
    mi = jnp.where(lane == 0, e0, jnp.where(lane == 1, e1, 0.0))
    mi = jnp.where(lane == 2, rank0, jnp.where(lane == 3, rank1, mi))
    mi_ref[...] = mi.astype(jnp.int32)
    mw_ref[...] = jnp.where(lane == 0, w0, jnp.where(lane == 1, w1, 0.0))


def _router_operands(w_r, b_r):
    wr = jnp.zeros((D_MODEL, LANES), F32).at[:, :N_EXPERTS].set(w_r)
    br = jnp.zeros((1, LANES), F32).at[:, :N_EXPERTS].set(b_r[None, :])
    wr_hi = wr.astype(BF16)
    wr_lo = (wr - wr_hi.astype(F32)).astype(BF16)
    return wr_hi, wr_lo, br


def _dispatch_body(dest_ref, last_ref, nused_ref, x_ref, xs_ref, zero_ref, sem_ref):
    t = x_ref.shape[0]
    tm = zero_ref.shape[0]
    n_tiles = xs_ref.shape[0] // tm
    i = pl.program_id(0)

    @pl.when(i == 0)
    def _():
        zero_ref[...] = jnp.zeros_like(zero_ref)

        def fill(tile):
            cp = pltpu.make_async_copy(zero_ref, xs_ref.at[pl.ds(pl.multiple_of(tile * tm, tm), tm), :], sem_ref.at[1])
            cp.start()
            cp.wait()

        for e in range(N_EXPERTS):
            @pl.when(last_ref[e] >= 0)
            def _():
                fill(last_ref[e])

        def unused(tile, carry):
            fill(tile)
            return carry

        lax.fori_loop(nused_ref[0], n_tiles, unused, 0)

    def issue(r, carry):
        for slot in range(2):
            d = dest_ref[2 * r + slot]
            pltpu.make_async_copy(x_ref.at[pl.ds(r, 1), :], xs_ref.at[pl.ds(d, 1), :],
                                  sem_ref.at[0]).start(priority=slot)
        return carry

    lax.fori_loop(0, t, issue, 0, unroll=ISSUE_UNROLL)
    pltpu.make_async_copy(xs_ref.at[pl.ds(0, 2 * t), :], xs_ref.at[pl.ds(0, 2 * t), :], sem_ref.at[0]).wait()


def _dispatch(x, dest, last_tile, n_used, n_tiles):
    n = x.shape[0]
    t = 256
    tm = EXPERT_TILE
    return pl.pallas_call(
        _dispatch_body,
        out_shape=jax.ShapeDtypeStruct((n_tiles * tm, D_MODEL), F32),
        grid=(n // t,),
        in_specs=[
            pl.BlockSpec((2 * t,), lambda i: (i,), memory_space=pltpu.SMEM),
            pl.BlockSpec(memory_space=pltpu.SMEM),
            pl.BlockSpec(memory_space=pltpu.SMEM),
            pl.BlockSpec((t, D_MODEL), lambda i: (i, 0)),
        ],
        out_specs=pl.BlockSpec(memory_space=pl.ANY),
        scratch_shapes=[pltpu.VMEM((tm, D_MODEL), F32), pltpu.SemaphoreType.DMA((2,))],
        compiler_params=_params(("arbitrary",)),
        name="dispatch",
    )(dest, last_tile, n_used, x)


def _experts_body(te_ref, nused_ref, x_ref, w1_ref, w3_ref, w2_ref, o_ref):
    del te_ref
    i = pl.program_id(0)

    @pl.when(i < nused_ref[0])
    def _():
        xb = x_ref[...].astype(BF16)
        f = None
        for k in range(FF_SPLIT):
            cols = slice(k * (D_FF // FF_SPLIT), (k + 1) * (D_FF // FF_SPLIT))
            a = jnp.dot(xb, w1_ref[0, :, cols], preferred_element_type=F32)
            c = jnp.dot(xb, w3_ref[0, :, cols], preferred_element_type=F32)
            h = (jax.nn.silu(a) * c).astype(BF16)
            part = jnp.dot(h, w2_ref[0, cols, :], preferred_element_type=F32)
            f = part if f is None else f + part
        o_ref[...] = f

    @pl.when(i >= nused_ref[0])
    def _():
        o_ref[...] = jnp.zeros_like(o_ref)


def _experts(xs, tile_expert, n_used, w1, w3, w2):
    tm = EXPERT_TILE
    n_tiles = xs.shape[0] // tm

    def xmap(i, te, nu):
        return (jnp.minimum(i, nu[0] - 1), 0)

    def wmap(i, te, nu):
        return (te[jnp.minimum(i, nu[0] - 1)], 0, 0)

    return pl.pallas_call(
        _experts_body,
        out_shape=jax.ShapeDtypeStruct(xs.shape, F32),
        grid_spec=pltpu.PrefetchScalarGridSpec(
            num_scalar_prefetch=2,
            grid=(n_tiles,),
            in_specs=[
                pl.BlockSpec((tm, D_MODEL), xmap),
                pl.BlockSpec((1, D_MODEL, D_FF), wmap),
                pl.BlockSpec((1, D_MODEL, D_FF), wmap),
                pl.BlockSpec((1, D_FF, D_MODEL), wmap),
            ],
            out_specs=pl.BlockSpec((tm, D_MODEL), lambda i, te, nu: (i, 0)),
        ),
        compiler_params=_params(("arbitrary",), 60 * 1024 * 1024),
        name="experts",
    )(tile_expert, n_used, xs, w1, w3, w2)


def _combine_body(n_first, dest_ref, x_ref, mw_ref, g_ref, b_ref, ys_ref, *rest):
    out_refs, (y0_ref, y1_ref, sem_ref) = rest[:-3], rest[-3:]
    t = x_ref.shape[0]

    def issue(r, carry):
        pltpu.make_async_copy(ys_ref.at[pl.ds(dest_ref[2 * r], 1), :], y0_ref.at[pl.ds(r, 1), :],
                              sem_ref.at[0]).start(priority=0)
        pltpu.make_async_copy(ys_ref.at[pl.ds(dest_ref[2 * r + 1], 1), :], y1_ref.at[pl.ds(r, 1), :],
                              sem_ref.at[0]).start(priority=1)
        return carry

    lax.fori_loop(0, t, issue, 0, unroll=ISSUE_UNROLL)
    pltpu.make_async_copy(ys_ref.at[pl.ds(0, t), :], y0_ref, sem_ref.at[0]).wait()
    pltpu.make_async_copy(ys_ref.at[pl.ds(0, t), :], y1_ref, sem_ref.at[0]).wait()
    mw = mw_ref[...]
    f = mw[:, 0:1] * y0_ref[...] + mw[:, 1:2] * y1_ref[...]
    out = _layer_norm(DN_ALPHA * x_ref[...] + f, g_ref[...], b_ref[...])
    if len(out_refs) == 1:
        out_refs[0][...] = out
    else:
        i = pl.program_id(0)

        @pl.when(i < n_first)
        def _():
            out_refs[0][...] = out

        @pl.when(i == n_first)
        def _():
            out_refs[1][...] = out


def _combine(x, ys, dest, mw, g, b, t, split):
    n = x.shape[0]
    zero = lambda i: (0, 0)
    n_first = n // t - 1
    if split:
        out_shape = (jax.ShapeDtypeStruct((n - t, D_MODEL), F32), jax.ShapeDtypeStruct((t, D_MODEL), F32))
        out_specs = (pl.BlockSpec((t, D_MODEL), lambda i: (jnp.minimum(i, n_first - 1), 0)),
                     pl.BlockSpec((t, D_MODEL), zero))
    else:
        out_shape = jax.ShapeDtypeStruct((n, D_MODEL), F32)
        out_specs = pl.BlockSpec((t, D_MODEL), lambda i: (i, 0))
    return pl.pallas_call(
        functools.partial(_combine_body, n_first),
        out_shape=out_shape,
        grid=(n // t,),
        in_specs=[
            pl.BlockSpec((2 * t,), lambda i: (i,), memory_space=pltpu.SMEM),
            pl.BlockSpec((t, D_MODEL), lambda i: (i, 0)),
            pl.BlockSpec((t, LANES), lambda i: (i, 0)),
            pl.BlockSpec((1, D_MODEL), zero),
            pl.BlockSpec((1, D_MODEL), zero),
            pl.BlockSpec(memory_space=pl.ANY),
        ],
        out_specs=out_specs,
        scratch_shapes=[pltpu.VMEM((t, D_MODEL), F32), pltpu.VMEM((t, D_MODEL), F32),
                        pltpu.SemaphoreType.DMA((1,))],
        compiler_params=_params(("arbitrary",)),
        name="combine",
    )(dest, x, mw, g, b, ys)


def _moe(x, mi, mw, cnt, w1, w3, w2, g, b, t, split):
    n = x.shape[0]
    tm = EXPERT_TILE
    n_tiles = (2 * n + N_EXPERTS * (tm - 1)) // tm
    counts = cnt[0, :N_EXPERTS]
    tiles = (counts + tm - 1) // tm
    tile_end = jnp.cumsum(tiles)
    start = (tile_end - tiles) * tm
    n_used = tile_end[-1:].astype(jnp.int32)
    tile_expert = jnp.sum((tile_end[None, :] <= jnp.arange(n_tiles)[:, None]).astype(jnp.int32), axis=1)
    tile_expert = jnp.minimum(tile_expert, N_EXPERTS - 1)
    last_tile = jnp.where(tiles > 0, tile_end - 1, -1).astype(jnp.int32)
    experts = mi[:, 0:2]
    group_start = sum(jnp.where(experts == e, start[e], 0) for e in range(N_EXPERTS))
    dest = (group_start + mi[:, 2:4]).astype(jnp.int32).reshape(2 * n)
    xs = _dispatch(x, dest, last_tile, n_used, n_tiles)
    ys = _experts(xs, tile_expert, n_used, w1, w3, w2)
    return _combine(x, ys, dest, mw, g, b, t, split)


def _block_diag(w):
    g, c, _ = w.shape
    out = jnp.zeros((g * c, g * c), w.dtype)
    for i in range(g):
        out = out.at[i * c:(i + 1) * c, i * c:(i + 1) * c].set(w[i])
    return out


def _bias_pairs(table, t_q, n_keys, offset):
    hi = t_q - 1 + offset
    span = t_q - 1 + n_keys
    cols = np.clip(hi - np.arange(span), -REL_CLIP, REL_CLIP) + REL_CLIP
    rev = table[:, cols].astype(F32)
    skew = jnp.tile(rev, (1, t_q + 1))[:, :t_q * (span + 1)].reshape(N_HEADS, t_q, span + 1)
    slab = skew[:, ::-1, :n_keys]
    return slab.reshape(HEAD_PAIRS, 2 * t_q, n_keys)


def _prompt_bias_slabs(table):
    bias = _bias_pairs(table, CHUNK, BAND, ATTN_REACH)
    col = np.arange(BAND)[None, :]
    first_valid = np.concatenate([[0], (LEFT_CHUNKS - np.arange(LEFT_CHUNKS)) * CHUNK])[:, None]
    valid = jnp.asarray(col >= first_valid)
    return jnp.where(valid[:, None, None, :], bias[None], NEG_INF)


def kernel(x_prompt, x_sample, cache_k, cache_v, state_pool, w_in, b_in, w_pool_grp, pool_scale,
           rel_table, w_pool_br, w_attn_br, w_out, ln1_g, ln1_b, ln2_g, ln2_b,
           w1_dense, w3_dense, w2_dense, w_router, b_router, w1_exp, w3_exp, w2_exp):
    bp, tp, d = x_prompt.shape
    bs, ts, _ = x_sample.shape
    n_p, n_s = bp * tp, bs * ts
    n = n_p + n_s
    depth = w_in.shape[0]
    keep_s = cache_k.shape[2]
    keep_p = min(ATTN_REACH, tp)

    x = (x_prompt.reshape(n_p, d), x_sample.reshape(n_s, d))
    ck = cache_k.reshape(depth, bs, keep_s, ATTN_WIDTH)
    cv = cache_v.reshape(depth, bs, keep_s, ATTN_WIDTH)
    hist = jnp.pad(state_pool, ((0, 0), (0, 0), (HIST_ROWS - POOL_HIST, 0), (0, 0)))
    row = lambda v: v[None, :].astype(F32)

    k_buf = v_buf = None
    kp_new, vp_new, pp_new, ps_new = [], [], [], []
    for l in range(depth):
        wgrp = _block_diag(w_pool_grp[l]).astype(BF16)
        weights = (wgrp, row(pool_scale[l]), w_pool_br[l].astype(BF16), w_attn_br[l].astype(BF16),
                   w_out[l].astype(BF16), row(ln1_g[l]), row(ln1_b[l]))
        z32, z16 = _in_proj(x, _reorder_in_columns(w_in[l]).astype(BF16), row(_reorder_in_columns(b_in[l])), n_s)
        attn_p = _attn_prompt(z32, z16, _prompt_bias_slabs(rel_table[l]), bp, tp)
        attn_s, k_buf, v_buf = _attn_sample(z32, z16, ck, cv, _bias_pairs(rel_table[l], ts, keep_s + ts, keep_s),
                                            l, n_p, bs, ts, k_buf, v_buf)
        j = l // 2
        last = l == depth - 1
        if l % 2 == 0:
            x1 = _mix(z32, z16, attn_p, attn_s, x, hist[l], weights, n_p, tp)
            x = _ffn_dense(x1, w1_dense[j].astype(BF16), w3_dense[j].astype(BF16), w2_dense[j].astype(BF16),
                           row(ln2_g[l]), row(ln2_b[l]))
            if last:
                x = (x[:n_p], x[n_p:])
        else:
            x1, mi, mw, cnt = _mix(z32, z16, attn_p, attn_s, x, hist[l], weights, n_p, tp,
                                   router=_router_operands(w_router[j], b_router[j]))
            x = _moe(x1, mi, mw, cnt, w1_exp[j].astype(BF16), w3_exp[j].astype(BF16),
                     w2_exp[j].astype(BF16), row(ln2_g[l]), row(ln2_b[l]), n_s, split=last)

        def tail(rows, col0, width, z32=z32):
            return jnp.stack([lax.slice(z32, ((b + 1) * tp - rows, col0), ((b + 1) * tp, col0 + width))
                              for b in range(bp)])

        kp_new.append(tail(keep_p, COL_K * ATTN_WIDTH, ATTN_WIDTH).reshape(bp, keep_p, N_HEADS, HEAD_DIM))
        vp_new.append(tail(keep_p, COL_V * ATTN_WIDTH, ATTN_WIDTH).reshape(bp, keep_p, N_HEADS, HEAD_DIM))
        pp_new.append(tail(POOL_HIST, COL_U * POOL_WIDTH, POOL_WIDTH))
        us = z32[n_p:, :POOL_WIDTH].reshape(bs, ts, POOL_WIDTH)
        ps_new.append(jnp.concatenate([state_pool[l], us], axis=1)[:, -POOL_HIST:])

    shape_s = (depth, bs, keep_s, N_HEADS, HEAD_DIM)
    return (x[0].reshape(bp, tp, d), x[1].reshape(bs, ts, d),
            jnp.stack(kp_new), jnp.stack(vp_new), jnp.stack(pp_new),
            k_buf.reshape(shape_s), v_buf.reshape(shape_s), jnp.stack(ps_new))
```

```python
import functools

import jax
import jax.numpy as jnp
import numpy as np
from jax import lax
from jax.experimental import pallas as pl
from jax.experimental.pallas import tpu as pltpu

F32 = jnp.float32
BF16 = jnp.bfloat16

D_MODEL = 1024
N_HEADS = 8
HEAD_DIM = 64
ATTN_WIDTH = N_HEADS * HEAD_DIM
CHUNK = 64
LEFT_CHUNKS = 8
BAND = (LEFT_CHUNKS + 1) * CHUNK
ATTN_REACH = LEFT_CHUNKS * CHUNK
REL_CLIP = 256
ATTN_SCALE = HEAD_DIM ** -0.5
POOL_WIDTH = 512
POOL_WINDOWS = (2, 4, 8, 16)
POOL_GROUP = POOL_WIDTH // len(POOL_WINDOWS)
POOL_HIST = max(POOL_WINDOWS) - 1
HIST_ROWS = POOL_HIST + 1
D_FF = 2816
N_EXPERTS = 8
PAST_LEN = 4096
DEPTH = 2
DN_ALPHA = (2 * DEPTH) ** 0.25
LN_EPS = 1e-5
NEG_INF = -1e30
IN_WIDTH = POOL_WIDTH + 3 * ATTN_WIDTH + 2 * D_MODEL
HEAD_PAIRS = N_HEADS // 2
PAIR_W = 2 * HEAD_DIM
LANES = 128

Z32_WIDTH = POOL_WIDTH + 2 * ATTN_WIDTH
Z16_WIDTH = 2 * D_MODEL + ATTN_WIDTH
COL_U, COL_K, COL_V = 0, 1, 2
COL_GP, COL_GA = 0, 1
COL_Q = 2 * D_MODEL // ATTN_WIDTH

EXPERT_TILE = 512
FF_SPLIT = 2
ISSUE_UNROLL = 8
VMEM_LIMIT = 56 * 1024 * 1024


def _pick(n, candidates):
    for c in candidates:
        if n % c == 0:
            return c
    raise ValueError(f"no tile in {candidates} divides {n}")


def _params(sem, vmem=None):
    return pltpu.CompilerParams(dimension_semantics=sem, vmem_limit_bytes=vmem or VMEM_LIMIT)


def _layer_norm(r, g, b):
    mu = jnp.mean(r, axis=-1, keepdims=True)
    c = r - mu
    var = jnp.mean(c * c, axis=-1, keepdims=True)
    return c * lax.rsqrt(var + LN_EPS) * g + b


def _row_sources(x, t):
    if not isinstance(x, tuple):
        return [x], [pl.BlockSpec((t, x.shape[1]), lambda i: (i, 0))], (lambda refs, i: refs[0][...])
    first, last = x
    assert last.shape[0] == t and first.shape[0] % t == 0
    n_first = first.shape[0] // t
    specs = [pl.BlockSpec((t, first.shape[1]), lambda i: (jnp.minimum(i, n_first - 1), 0)),
             pl.BlockSpec((t, last.shape[1]), lambda i: (0, 0))]
    return [first, last], specs, (lambda refs, i: jnp.where(i < n_first, refs[0][...], refs[1][...]))


def _inproj_body(n_src, select, *refs):
    w_ref, b_ref, z32_ref, z16_ref = refs[n_src:]
    x = select(refs[:n_src], pl.program_id(0)).astype(BF16)
    z = jnp.dot(x, w_ref[...], preferred_element_type=F32) + b_ref[...]
    z32_ref[...] = z[:, :Z32_WIDTH]
    z16_ref[...] = z[:, Z32_WIDTH:].astype(BF16)


def _in_proj(x, w_bf, b, t):
    srcs, specs, select = _row_sources(x, t)
    n = sum(s.shape[0] for s in srcs)
    zero = lambda i: (0, 0)
    return pl.pallas_call(
        functools.partial(_inproj_body, len(srcs), select),
        out_shape=(jax.ShapeDtypeStruct((n, Z32_WIDTH), F32), jax.ShapeDtypeStruct((n, Z16_WIDTH), BF16)),
        grid=(n // t,),
        in_specs=specs + [pl.BlockSpec((D_MODEL, IN_WIDTH), zero), pl.BlockSpec((1, IN_WIDTH), zero)],
        out_specs=(pl.BlockSpec((t, Z32_WIDTH), lambda i: (i, 0)), pl.BlockSpec((t, Z16_WIDTH), lambda i: (i, 0))),
        compiler_params=_params(("arbitrary",)),
        name="in_proj",
    )(*srcs, w_bf, b)


def _reorder_in_columns(w):
    o = [0, POOL_WIDTH, POOL_WIDTH + ATTN_WIDTH, POOL_WIDTH + 2 * ATTN_WIDTH, POOL_WIDTH + 3 * ATTN_WIDTH, IN_WIDTH]
    u, q, k, v, g = (w[..., o[i]:o[i + 1]] for i in range(5))
    return jnp.concatenate([u, k, v, g, q], axis=-1)


def _pair_scores(q_pair, k_pair, bias):
    lane = lax.broadcasted_iota(jnp.int32, q_pair.shape, 1)
    qs = q_pair.astype(F32) * ATTN_SCALE
    q2 = jnp.concatenate([jnp.where(lane < HEAD_DIM, qs, 0.0), jnp.where(lane >= HEAD_DIM, qs, 0.0)], axis=0)
    s = lax.dot_general(q2.astype(BF16), k_pair, (((1,), (1,)), ((), ())), preferred_element_type=F32)
    return s + bias


def _pair_output(s, v_pair):
    rows = s.shape[0] // 2
    m = jnp.max(s, axis=-1, keepdims=True)
    e = jnp.exp(s - m)
    l = jnp.sum(e, axis=-1, keepdims=True)
    o2 = jnp.dot(e.astype(BF16), v_pair, preferred_element_type=F32) / l
    lane = lax.broadcasted_iota(jnp.int32, (rows, PAIR_W), 1)
    return jnp.where(lane < HEAD_DIM, o2[:rows], o2[rows:])


def _attn_prompt_body(q_ref, kp_ref, kc_ref, vp_ref, vc_ref, bias_ref, o_ref, kt_ref, vt_ref, kext_ref, vext_ref):
    blk = q_ref.shape[0]
    j = pl.program_id(1)

    @pl.when(j == pl.num_programs(1) - 1)
    def _():
        kt_ref[0] = kc_ref[...]
        vt_ref[0] = vc_ref[...]

    kext_ref[0:blk, :] = kp_ref[...].astype(BF16)
    kext_ref[blk:2 * blk, :] = kc_ref[...].astype(BF16)
    vext_ref[0:blk, :] = vp_ref[...].astype(BF16)
    vext_ref[blk:2 * blk, :] = vc_ref[...].astype(BF16)
    def chunk(c, carry):
        q0 = pl.multiple_of(c * CHUNK, CHUNK)
        slab = jnp.where(j == 0, c + 1, 0)
        pairs = [slice(hp * PAIR_W, (hp + 1) * PAIR_W) for hp in range(HEAD_PAIRS)]
        s = jnp.concatenate([_pair_scores(q_ref[pl.ds(q0, CHUNK), lanes], kext_ref[pl.ds(q0, BAND), lanes],
                                          bias_ref[slab, hp]) for hp, lanes in enumerate(pairs)], axis=0)
        m = jnp.max(s, axis=-1, keepdims=True)
        e = jnp.exp(s - m)
        inv = 1.0 / jnp.sum(e, axis=-1, keepdims=True)
        p = e.astype(BF16)
        lane = lax.broadcasted_iota(jnp.int32, (CHUNK, PAIR_W), 1)
        for hp, lanes in enumerate(pairs):
            rows = slice(hp * 2 * CHUNK, (hp + 1) * 2 * CHUNK)
            o2 = jnp.dot(p[rows], vext_ref[pl.ds(q0, BAND), lanes], preferred_element_type=F32) * inv[rows]
            o_ref[pl.ds(q0, CHUNK), lanes] = jnp.where(lane < HEAD_DIM, o2[:CHUNK], o2[CHUNK:])
        return carry

    lax.fori_loop(0, blk // CHUNK, chunk, 0, unroll=4)


def _attn_prompt(z32, z16, bias_pairs, batch, seq):
    blk = ATTN_REACH
    per_seq = seq // blk
    rows = batch * seq

    def cur(col):
        return lambda b, j: (b * per_seq + j, col)

    def prev(col):
        return lambda b, j: (b * per_seq + jnp.maximum(j - 1, 0), col)

    tail = jax.ShapeDtypeStruct((batch, blk, ATTN_WIDTH), F32)
    tail_spec = pl.BlockSpec((1, blk, ATTN_WIDTH), lambda b, j: (b, 0, 0))
    return pl.pallas_call(
        _attn_prompt_body,
        out_shape=(jax.ShapeDtypeStruct((rows, ATTN_WIDTH), F32), tail, tail),
        grid=(batch, per_seq),
        in_specs=[
            pl.BlockSpec((blk, ATTN_WIDTH), cur(COL_Q)),
            pl.BlockSpec((blk, ATTN_WIDTH), prev(COL_K)),
            pl.BlockSpec((blk, ATTN_WIDTH), cur(COL_K)),
            pl.BlockSpec((blk, ATTN_WIDTH), prev(COL_V)),
            pl.BlockSpec((blk, ATTN_WIDTH), cur(COL_V)),
            pl.BlockSpec((1 + LEFT_CHUNKS, HEAD_PAIRS, 2 * CHUNK, BAND), lambda b, j: (0, 0, 0, 0)),
        ],
        out_specs=(pl.BlockSpec((blk, ATTN_WIDTH), lambda b, j: (b * per_seq + j, 0)), tail_spec, tail_spec),
        scratch_shapes=[pltpu.VMEM((2 * blk, ATTN_WIDTH), BF16), pltpu.VMEM((2 * blk, ATTN_WIDTH), BF16)],
        compiler_params=_params(("arbitrary", "arbitrary")),
        name="attn_prompt",
    )(z16, z32, z32, z32, z32, bias_pairs)


def _attn_sample_body(q_ref, kn_ref, vn_ref, ck_ref, cv_ref, bias_ref, *rest):
    o_ref, ko_ref, vo_ref, kall_ref, vall_ref = rest[-5:]
    keep = ck_ref.shape[2]
    t = q_ref.shape[0]
    ck = ck_ref[0, 0]
    cv = cv_ref[0, 0]
    kn = kn_ref[...]
    vn = vn_ref[...]
    kall_ref[0:keep, :] = ck.astype(BF16)
    kall_ref[keep:keep + t, :] = kn.astype(BF16)
    vall_ref[0:keep, :] = cv.astype(BF16)
    vall_ref[keep:keep + t, :] = vn.astype(BF16)
    for hp in range(HEAD_PAIRS):
        lanes = slice(hp * PAIR_W, (hp + 1) * PAIR_W)
        s = _pair_scores(q_ref[:, lanes], kall_ref[:, lanes], bias_ref[hp])
        o_ref[:, lanes] = _pair_output(s, vall_ref[:, lanes])
    ko_ref[0, 0, 0:keep - t, :] = ck[t:keep]
    ko_ref[0, 0, keep - t:keep, :] = kn
    vo_ref[0, 0, 0:keep - t, :] = cv[t:keep]
    vo_ref[0, 0, keep - t:keep, :] = vn
    for later in range(1, ko_ref.shape[0]):
        ko_ref[later] = jnp.zeros(ko_ref.shape[1:], F32)
        vo_ref[later] = jnp.zeros(vo_ref.shape[1:], F32)


def _attn_sample(z32, z16, cache_k, cache_v, bias_pairs, layer, row0, streams, t, k_buf, v_buf):
    depth, _, keep, _ = cache_k.shape
    blk0 = row0 // t
    ins = [z16, z32, z32, cache_k, cache_v, bias_pairs]
    in_specs = [
        pl.BlockSpec((t, ATTN_WIDTH), lambda s: (blk0 + s, COL_Q)),
        pl.BlockSpec((t, ATTN_WIDTH), lambda s: (blk0 + s, COL_K)),
        pl.BlockSpec((t, ATTN_WIDTH), lambda s: (blk0 + s, COL_V)),
        pl.BlockSpec((1, 1, keep, ATTN_WIDTH), lambda s: (layer, s, 0, 0)),
        pl.BlockSpec((1, 1, keep, ATTN_WIDTH), lambda s: (layer, s, 0, 0)),
        pl.BlockSpec((HEAD_PAIRS, 2 * t, keep + t), lambda s: (0, 0, 0)),
    ]
    aliases = {}
    if k_buf is None:
        assert layer == 0
        buf_spec = pl.BlockSpec((depth, 1, keep, ATTN_WIDTH), lambda s: (0, s, 0, 0))
    else:
        aliases = {len(ins): 1, len(ins) + 1: 2}
        ins += [k_buf, v_buf]
        in_specs += [pl.BlockSpec(memory_space=pl.ANY), pl.BlockSpec(memory_space=pl.ANY)]
        buf_spec = pl.BlockSpec((1, 1, keep, ATTN_WIDTH), lambda s: (layer, s, 0, 0))
    buf = jax.ShapeDtypeStruct(cache_k.shape, F32)
    return pl.pallas_call(
        _attn_sample_body,
        out_shape=(jax.ShapeDtypeStruct((streams * t, ATTN_WIDTH), F32), buf, buf),
        grid=(streams,),
        in_specs=in_specs,
        out_specs=(
            pl.BlockSpec((t, ATTN_WIDTH), lambda s: (s, 0)),
            buf_spec,
            buf_spec,
        ),
        scratch_shapes=[pltpu.VMEM((keep + t, ATTN_WIDTH), BF16), pltpu.VMEM((keep + t, ATTN_WIDTH), BF16)],
        input_output_aliases=aliases,
        compiler_params=_params(("arbitrary",)),
        name="attn_sample",
    )(*ins)


def _merge(pooled, u, attn, gp, ga, x, wgrp_ref, scale_ref, wp_ref, wa_ref, wo_ref, g_ref, b_ref):
    pooled = pooled - u
    pool_y = jnp.dot(pooled.astype(BF16), wgrp_ref[...], preferred_element_type=F32) * scale_ref[...]
    mp = jnp.dot(pool_y.astype(BF16), wp_ref[...], preferred_element_type=F32)
    ma = jnp.dot(attn.astype(BF16), wa_ref[...], preferred_element_type=F32)
    m = jax.nn.sigmoid(gp) * mp + jax.nn.sigmoid(ga) * ma
    y = jnp.dot(m.astype(BF16), wo_ref[...], preferred_element_type=F32)
    return _layer_norm(DN_ALPHA * x + y, g_ref[...], b_ref[...])


def _window_means(read, pos, shape_out):
    outs = []
    for g, w in enumerate(POOL_WINDOWS):
        lanes = slice(g * POOL_GROUP, (g + 1) * POOL_GROUP)
        s = read(0, lanes)
        for back in range(1, w):
            s = s + read(back, lanes)
        outs.append((s / jnp.minimum(pos + 1, w).astype(F32)).reshape(shape_out))
    return jnp.concatenate(outs, axis=-1)


def _mix_body(n_prompt_tiles, tiles_per_seq, with_router, n_src, select,
              u_ref, up_ref, hist_ref, attn_p_ref, attn_s_ref, gp_ref, ga_ref, *rest):
    x_refs, rest = rest[:n_src], rest[n_src:]
    (wgrp_ref, scale_ref, wp_ref, wa_ref, wo_ref, g_ref, b_ref), rest = rest[:7], rest[7:]
    if with_router:
        (wr_hi_ref, wr_lo_ref, br_ref, o_ref, mi_ref, mw_ref, cnt_ref,
         buf_ref, sbuf_ref, pooled_ref, attn_ref, carry_ref) = rest
    else:
        o_ref, buf_ref, sbuf_ref, pooled_ref, attn_ref = rest
    t = u_ref.shape[0]
    i = pl.program_id(0)
    u = u_ref[...]

    @pl.when(i < n_prompt_tiles)
    def _():
        tile = i % tiles_per_seq
        buf_ref[0:HIST_ROWS, :] = jnp.where(tile == 0, 0.0, up_ref[...])
        buf_ref[HIST_ROWS:HIST_ROWS + t, :] = u
        pos = tile * t + lax.broadcasted_iota(jnp.int32, (t, 1), 0)
        read = lambda back, lanes: buf_ref[HIST_ROWS - back:HIST_ROWS - back + t, lanes]
        pooled_ref[...] = _window_means(read, pos, (t, POOL_GROUP))
        attn_ref[...] = attn_p_ref[...]

    @pl.when(i == n_prompt_tiles)
    def _():
        streams, hrows, _ = hist_ref.shape
        ts = t // streams
        sbuf_ref[:, 0:hrows, :] = hist_ref[...]
        sbuf_ref[:, hrows:hrows + ts, :] = u.reshape(streams, ts, POOL_WIDTH)
        pos = PAST_LEN + lax.broadcasted_iota(jnp.int32, (1, ts, 1), 1)
        read = lambda back, lanes: sbuf_ref[:, hrows - back:hrows - back + ts, lanes]
        pooled_ref[...] = _window_means(read, pos, (t, POOL_GROUP))
        attn_ref[...] = attn_s_ref[...]

    x1 = _merge(pooled_ref[...], u, attn_ref[...], gp_ref[...].astype(F32), ga_ref[...].astype(F32),
                select(x_refs, i), wgrp_ref, scale_ref, wp_ref, wa_ref, wo_ref, g_ref, b_ref)
    o_ref[...] = x1
    if with_router:
        _route(x1, wr_hi_ref, wr_lo_ref, br_ref, mi_ref, mw_ref, cnt_ref, carry_ref)


def _mix(z32, z16, attn_p, attn_s, x, hist, weights, n_prompt, seq, router=None):
    n = z32.shape[0]
    t = n - n_prompt
    x_srcs, x_specs, select = _row_sources(x, t)
    n_prompt_tiles = n_prompt // t
    streams, hrows, _ = hist.shape
    hist_per_tile = t // HIST_ROWS
    last_p = n_prompt_tiles - 1
    zero = lambda i: (0, 0)
    rows = lambda width: pl.BlockSpec((t, width), lambda i: (i, 0))
    out_shape = [jax.ShapeDtypeStruct((n, D_MODEL), F32)]
    out_specs = [rows(D_MODEL)]
    router_specs, router_scratch = [], []
    if router is not None:
        router_specs = [pl.BlockSpec((D_MODEL, LANES), zero), pl.BlockSpec((D_MODEL, LANES), zero),
                        pl.BlockSpec((1, LANES), zero)]
        out_shape += [jax.ShapeDtypeStruct((n, LANES), jnp.int32), jax.ShapeDtypeStruct((n, LANES), F32),
                      jax.ShapeDtypeStruct((1, LANES), jnp.int32)]
        out_specs += [rows(LANES), rows(LANES), pl.BlockSpec((1, LANES), zero)]
        router_scratch = [pltpu.VMEM((1, LANES), F32)]
    out = pl.pallas_call(
        functools.partial(_mix_body, n_prompt_tiles, seq // t, router is not None, len(x_srcs), select),
        out_shape=out_shape,
        grid=(n_prompt_tiles + 1,),
        in_specs=[
            pl.BlockSpec((t, POOL_WIDTH), lambda i: (i, COL_U)),
            pl.BlockSpec((HIST_ROWS, POOL_WIDTH), lambda i: (jnp.maximum(i * hist_per_tile - 1, 0), COL_U)),
            pl.BlockSpec((streams, hrows, POOL_WIDTH), lambda i: (0, 0, 0)),
            pl.BlockSpec((t, ATTN_WIDTH), lambda i: (jnp.minimum(i, last_p), 0)),
            pl.BlockSpec((t, ATTN_WIDTH), zero),
            pl.BlockSpec((t, D_MODEL), lambda i: (i, COL_GP)),
            pl.BlockSpec((t, D_MODEL), lambda i: (i, COL_GA)),
        ] + x_specs + [
            pl.BlockSpec((POOL_WIDTH, POOL_WIDTH), zero),
            pl.BlockSpec((1, POOL_WIDTH), zero),
            pl.BlockSpec((POOL_WIDTH, D_MODEL), zero),
            pl.BlockSpec((ATTN_WIDTH, D_MODEL), zero),
            pl.BlockSpec((D_MODEL, D_MODEL), zero),
            pl.BlockSpec((1, D_MODEL), zero),
            pl.BlockSpec((1, D_MODEL), zero),
        ] + router_specs,
        out_specs=out_specs,
        scratch_shapes=[
            pltpu.VMEM((HIST_ROWS + t, POOL_WIDTH), F32),
            pltpu.VMEM((streams, hrows + t // streams, POOL_WIDTH), F32),
            pltpu.VMEM((t, POOL_WIDTH), F32),
            pltpu.VMEM((t, ATTN_WIDTH), F32),
        ] + router_scratch,
        compiler_params=_params(("arbitrary",)),
        name="mix",
    )(z32, z32, hist, attn_p, attn_s, z16, z16, *x_srcs, *weights, *(router or ()))
    return out[0] if router is None else out


def _ffn_dense_body(x_ref, w1_ref, w3_ref, w2_ref, g_ref, b_ref, o_ref):
    x = x_ref[...]
    xb = x.astype(BF16)
    a = jnp.dot(xb, w1_ref[...], preferred_element_type=F32)
    c = jnp.dot(xb, w3_ref[...], preferred_element_type=F32)
    h = (jax.nn.silu(a) * c).astype(BF16)
    f = jnp.dot(h, w2_ref[...], preferred_element_type=F32)
    o_ref[...] = _layer_norm(DN_ALPHA * x + f, g_ref[...], b_ref[...])


def _ffn_dense(x, w1, w3, w2, g, b):
    n = x.shape[0]
    t = 256
    zero = lambda i: (0, 0)
    return pl.pallas_call(
        _ffn_dense_body,
        out_shape=jax.ShapeDtypeStruct((n, D_MODEL), F32),
        grid=(n // t,),
        in_specs=[
            pl.BlockSpec((t, D_MODEL), lambda i: (i, 0)),
            pl.BlockSpec((D_MODEL, D_FF), zero),
            pl.BlockSpec((D_MODEL, D_FF), zero),
            pl.BlockSpec((D_FF, D_MODEL), zero),
            pl.BlockSpec((1, D_MODEL), zero),
            pl.BlockSpec((1, D_MODEL), zero),
        ],
        out_specs=pl.BlockSpec((t, D_MODEL), lambda i: (i, 0)),
        compiler_params=_params(("arbitrary",)),
        name="ffn_dense",
    )(x, w1, w3, w2, g, b)


def _route(x, wr_hi_ref, wr_lo_ref, br_ref, mi_ref, mw_ref, cnt_ref, carry_ref):
    t = x.shape[0]

    @pl.when(pl.program_id(0) == 0)
    def _():
        carry_ref[...] = jnp.zeros_like(carry_ref)

    x_hi = x.astype(BF16)
    x_lo = (x - x_hi.astype(F32)).astype(BF16)
    logits = (jnp.dot(x_hi, wr_hi_ref[...], preferred_element_type=F32)
              + jnp.dot(x_lo, wr_hi_ref[...], preferred_element_type=F32)
              + jnp.dot(x_hi, wr_lo_ref[...], preferred_element_type=F32)) + br_ref[...]
    lane = lax.broadcasted_iota(jnp.int32, (t, LANES), 1)
    logits = jnp.where(lane < N_EXPERTS, logits, -jnp.inf)
    lane_f = lane.astype(F32)
    v0 = jnp.max(logits, axis=-1, keepdims=True)
    e0 = jnp.min(jnp.where(logits == v0, lane_f, float(LANES)), axis=-1, keepdims=True)
    rest = jnp.where(lane_f == e0, -jnp.inf, logits)
    v1 = jnp.max(rest, axis=-1, keepdims=True)
    e1 = jnp.min(jnp.where(rest == v1, lane_f, float(LANES)), axis=-1, keepdims=True)
    ex = jnp.exp(v1 - v0)
    w0 = 1.0 / (1.0 + ex)
    w1 = ex / (1.0 + ex)
    oh0 = (lane_f == e0).astype(F32)
    oh1 = (lane_f == e1).astype(F32)
    r_i = lax.broadcasted_iota(jnp.int32, (t, t), 0)
    c_i = lax.broadcasted_iota(jnp.int32, (t, t), 1)
    tri = (c_i < r_i).astype(BF16)
    pre0 = jnp.dot(tri, oh0.astype(BF16), preferred_element_type=F32)
    pre1 = jnp.dot(tri, oh1.astype(BF16), preferred_element_type=F32)
    cnt0 = jnp.sum(oh0, axis=0, keepdims=True)
    cnt1 = jnp.sum(oh1, axis=0, keepdims=True)
    carry = carry_ref[...]
    rank0 = jnp.sum(oh0 * (carry + pre0), axis=-1, keepdims=True)
    rank1 = jnp.sum(oh1 * (carry + cnt0 + pre1), axis=-1, keepdims=True)
    carry = carry + cnt0 + cnt1
    carry_ref[...] = carry
    cnt_ref[...] = carry.astype(jnp.int32)
    mi = jnp.where(lane == 0, e0, jnp.where(lane == 1, e1, 0.0))
    mi = jnp.where(lane == 2, rank0, jnp.where(lane == 3, rank1, mi))
    mi_ref[...] = mi.astype(jnp.int32)
    mw_ref[...] = jnp.where(lane == 0, w0, jnp.where(lane == 1, w1, 0.0))


def _router_operands(w_r, b_r):
    wr = jnp.zeros((D_MODEL, LANES), F32).at[:, :N_EXPERTS].set(w_r)
    br = jnp.zeros((1, LANES), F32).at[:, :N_EXPERTS].set(b_r[None, :])
    wr_hi = wr.astype(BF16)
    wr_lo = (wr - wr_hi.astype(F32)).astype(BF16)
    return wr_hi, wr_lo, br


def _dispatch_body(dest_ref, last_ref, nused_ref, x_ref, xs_ref, zero_ref, sem_ref):
    t = dest_ref.shape[0] // 2
    tm = zero_ref.shape[0]
    n_tiles = xs_ref.shape[0] // tm
    i = pl.program_id(0)

    @pl.when(i == 0)
    def _():
        zero_ref[...] = jnp.zeros_like(zero_ref)

        def fill(tile):
            cp = pltpu.make_async_copy(zero_ref, xs_ref.at[pl.ds(pl.multiple_of(tile * tm, tm), tm), :], sem_ref.at[1])
            cp.start()
            cp.wait()

        for e in range(N_EXPERTS):
            @pl.when(last_ref[e] >= 0)
            def _():
                fill(last_ref[e])

        def unused(tile, carry):
            fill(tile)
            return carry

        lax.fori_loop(nused_ref[0], n_tiles, unused, 0)

    row0 = i * t

    def issue(r, carry):
        for slot in range(2):
            d = dest_ref[2 * r + slot]
            pltpu.make_async_copy(x_ref.at[pl.ds(row0 + r, 1), :], xs_ref.at[pl.ds(d, 1), :],
                                  sem_ref.at[0]).start(priority=slot)
        return carry

    lax.fori_loop(0, t, issue, 0, unroll=ISSUE_UNROLL)

    def wait_step():
        pltpu.make_async_copy(xs_ref.at[pl.ds(0, 2 * t), :], xs_ref.at[pl.ds(0, 2 * t), :], sem_ref.at[0]).wait()

    @pl.when(i > 0)
    def _():
        wait_step()

    @pl.when(i == pl.num_programs(0) - 1)
    def _():
        wait_step()


def _dispatch(x, dest, last_tile, n_used, n_tiles, t):
    n = x.shape[0]
    tm = EXPERT_TILE
    return pl.pallas_call(
        _dispatch_body,
        out_shape=jax.ShapeDtypeStruct((n_tiles * tm, D_MODEL), F32),
        grid=(n // t,),
        in_specs=[
            pl.BlockSpec((2 * t,), lambda i: (i,), memory_space=pltpu.SMEM),
            pl.BlockSpec(memory_space=pltpu.SMEM),
            pl.BlockSpec(memory_space=pltpu.SMEM),
            pl.BlockSpec(memory_space=pl.ANY),
        ],
        out_specs=pl.BlockSpec(memory_space=pl.ANY),
        scratch_shapes=[pltpu.VMEM((tm, D_MODEL), F32), pltpu.SemaphoreType.DMA((2,))],
        compiler_params=_params(("arbitrary",)),
        name="dispatch",
    )(dest, last_tile, n_used, x)


def _experts_body(te_ref, nused_ref, x_ref, w1_ref, w3_ref, w2_ref, o_ref):
    del te_ref
    i = pl.program_id(0)

    @pl.when(i < nused_ref[0])
    def _():
        xb = x_ref[...].astype(BF16)
        f = None
        for k in range(FF_SPLIT):
            cols = slice(k * (D_FF // FF_SPLIT), (k + 1) * (D_FF // FF_SPLIT))
            a = jnp.dot(xb, w1_ref[0, :, cols], preferred_element_type=F32)
            c = jnp.dot(xb, w3_ref[0, :, cols], preferred_element_type=F32)
            h = (jax.nn.silu(a) * c).astype(BF16)
            part = jnp.dot(h, w2_ref[0, cols, :], preferred_element_type=F32)
            f = part if f is None else f + part
        o_ref[...] = f

    @pl.when(i >= nused_ref[0])
    def _():
        o_ref[...] = jnp.zeros_like(o_ref)


def _experts(xs, tile_expert, n_used, w1, w3, w2):
    tm = EXPERT_TILE
    n_tiles = xs.shape[0] // tm

    def xmap(i, te, nu):
        return (jnp.minimum(i, nu[0] - 1), 0)

    def wmap(i, te, nu):
        return (te[jnp.minimum(i, nu[0] - 1)], 0, 0)

    return pl.pallas_call(
        _experts_body,
        out_shape=jax.ShapeDtypeStruct(xs.shape, F32),
        grid_spec=pltpu.PrefetchScalarGridSpec(
            num_scalar_prefetch=2,
            grid=(n_tiles,),
            in_specs=[
                pl.BlockSpec((tm, D_MODEL), xmap),
                pl.BlockSpec((1, D_MODEL, D_FF), wmap),
                pl.BlockSpec((1, D_MODEL, D_FF), wmap),
                pl.BlockSpec((1, D_FF, D_MODEL), wmap),
            ],
            out_specs=pl.BlockSpec((tm, D_MODEL), lambda i, te, nu: (i, 0)),
        ),
        compiler_params=_params(("arbitrary",), 60 * 1024 * 1024),
        name="experts",
    )(tile_expert, n_used, xs, w1, w3, w2)


def _combine_body(n_first, dest_ref, next_ref, x_ref, mw_ref, g_ref, b_ref, ys_ref, *rest):
    out_refs, (y0_ref, y1_ref, sem_ref) = rest[:-3], rest[-3:]
    t = x_ref.shape[0]
    i = pl.program_id(0)
    cur = i % 2

    def gather(idx_ref, buf):
        def issue(r, carry):
            pltpu.make_async_copy(ys_ref.at[pl.ds(idx_ref[2 * r], 1), :], y0_ref.at[buf, pl.ds(r, 1), :],
                                  sem_ref.at[buf]).start(priority=0)
            pltpu.make_async_copy(ys_ref.at[pl.ds(idx_ref[2 * r + 1], 1), :], y1_ref.at[buf, pl.ds(r, 1), :],
                                  sem_ref.at[buf]).start(priority=1)
            return carry

        lax.fori_loop(0, t, issue, 0, unroll=ISSUE_UNROLL)

    @pl.when(i == 0)
    def _():
        gather(dest_ref, 0)

    @pl.when(i + 1 < pl.num_programs(0))
    def _():
        gather(next_ref, 1 - cur)

    pltpu.make_async_copy(ys_ref.at[pl.ds(0, t), :], y0_ref.at[cur], sem_ref.at[cur]).wait()
    pltpu.make_async_copy(ys_ref.at[pl.ds(0, t), :], y1_ref.at[cur], sem_ref.at[cur]).wait()
    mw = mw_ref[...]
    f = mw[:, 0:1] * y0_ref[cur] + mw[:, 1:2] * y1_ref[cur]
    out = _layer_norm(DN_ALPHA * x_ref[...] + f, g_ref[...], b_ref[...])
    if len(out_refs) == 1:
        out_refs[0][...] = out
    else:
        @pl.when(i < n_first)
        def _():
            out_refs[0][...] = out

        @pl.when(i == n_first)
        def _():
            out_refs[1][...] = out


def _combine(x, ys, dest, mw, g, b, t, split):
    n = x.shape[0]
    zero = lambda i: (0, 0)
    n_first = n // t - 1
    last = n // t - 1
    if split:
        out_shape = (jax.ShapeDtypeStruct((n - t, D_MODEL), F32), jax.ShapeDtypeStruct((t, D_MODEL), F32))
        out_specs = (pl.BlockSpec((t, D_MODEL), lambda i: (jnp.minimum(i, n_first - 1), 0)),
                     pl.BlockSpec((t, D_MODEL), zero))
    else:
        out_shape = jax.ShapeDtypeStruct((n, D_MODEL), F32)
        out_specs = pl.BlockSpec((t, D_MODEL), lambda i: (i, 0))
    return pl.pallas_call(
        functools.partial(_combine_body, n_first),
        out_shape=out_shape,
        grid=(n // t,),
        in_specs=[
            pl.BlockSpec((2 * t,), lambda i: (i,), memory_space=pltpu.SMEM),
            pl.BlockSpec((2 * t,), lambda i: (jnp.minimum(i + 1, last),), memory_space=pltpu.SMEM),
            pl.BlockSpec((t, D_MODEL), lambda i: (i, 0)),
            pl.BlockSpec((t, LANES), lambda i: (i, 0)),
            pl.BlockSpec((1, D_MODEL), zero),
            pl.BlockSpec((1, D_MODEL), zero),
            pl.BlockSpec(memory_space=pl.ANY),
        ],
        out_specs=out_specs,
        scratch_shapes=[pltpu.VMEM((2, t, D_MODEL), F32), pltpu.VMEM((2, t, D_MODEL), F32),
                        pltpu.SemaphoreType.DMA((2,))],
        compiler_params=_params(("arbitrary",)),
        name="combine",
    )(dest, dest, x, mw, g, b, ys)


def _moe(x, mi, mw, cnt, w1, w3, w2, g, b, t, split):
    n = x.shape[0]
    tm = EXPERT_TILE
    n_tiles = (2 * n + N_EXPERTS * (tm - 1)) // tm
    counts = cnt[0, :N_EXPERTS]
    tiles = (counts + tm - 1) // tm
    tile_end = jnp.cumsum(tiles)
    start = (tile_end - tiles) * tm
    n_used = tile_end[-1:].astype(jnp.int32)
    tile_expert = jnp.sum((tile_end[None, :] <= jnp.arange(n_tiles)[:, None]).astype(jnp.int32), axis=1)
    tile_expert = jnp.minimum(tile_expert, N_EXPERTS - 1)
    last_tile = jnp.where(tiles > 0, tile_end - 1, -1).astype(jnp.int32)
    experts = mi[:, 0:2]
    group_start = sum(jnp.where(experts == e, start[e], 0) for e in range(N_EXPERTS))
    dest = (group_start + mi[:, 2:4]).astype(jnp.int32).reshape(2 * n)
    xs = _dispatch(x, dest, last_tile, n_used, n_tiles, t)
    ys = _experts(xs, tile_expert, n_used, w1, w3, w2)
    return _combine(x, ys, dest, mw, g, b, t, split)


def _block_diag(w):
    g, c, _ = w.shape
    out = jnp.zeros((g * c, g * c), w.dtype)
    for i in range(g):
        out = out.at[i * c:(i + 1) * c, i * c:(i + 1) * c].set(w[i])
    return out


def _bias_pairs(table, t_q, n_keys, offset):
    hi = t_q - 1 + offset
    span = t_q - 1 + n_keys
    cols = np.clip(hi - np.arange(span), -REL_CLIP, REL_CLIP) + REL_CLIP
    rev = table[:, cols].astype(F32)
    skew = jnp.tile(rev, (1, t_q + 1))[:, :t_q * (span + 1)].reshape(N_HEADS, t_q, span + 1)
    slab = skew[:, ::-1, :n_keys]
    return slab.reshape(HEAD_PAIRS, 2 * t_q, n_keys)


def _prompt_bias_slabs(table):
    bias = _bias_pairs(table, CHUNK, BAND, ATTN_REACH)
    col = np.arange(BAND)[None, :]
    first_valid = np.concatenate([[0], (LEFT_CHUNKS - np.arange(LEFT_CHUNKS)) * CHUNK])[:, None]
    valid = jnp.asarray(col >= first_valid)
    return jnp.where(valid[:, None, None, :], bias[None], NEG_INF)


def kernel(x_prompt, x_sample, cache_k, cache_v, state_pool, w_in, b_in, w_pool_grp, pool_scale,
           rel_table, w_pool_br, w_attn_br, w_out, ln1_g, ln1_b, ln2_g, ln2_b,
           w1_dense, w3_dense, w2_dense, w_router, b_router, w1_exp, w3_exp, w2_exp):
    bp, tp, d = x_prompt.shape
    bs, ts, _ = x_sample.shape
    n_p, n_s = bp * tp, bs * ts
    n = n_p + n_s
    depth = w_in.shape[0]
    keep_s = cache_k.shape[2]
    keep_p = min(ATTN_REACH, tp)

    x = (x_prompt.reshape(n_p, d), x_sample.reshape(n_s, d))
    ck = cache_k.reshape(depth, bs, keep_s, ATTN_WIDTH)
    cv = cache_v.reshape(depth, bs, keep_s, ATTN_WIDTH)
    hist = jnp.pad(state_pool, ((0, 0), (0, 0), (HIST_ROWS - POOL_HIST, 0), (0, 0)))
    row = lambda v: v[None, :].astype(F32)

    k_buf = v_buf = None
    kp_new, vp_new, pp_new, ps_new = [], [], [], []
    for l in range(depth):
        wgrp = _block_diag(w_pool_grp[l]).astype(BF16)
        weights = (wgrp, row(pool_scale[l]), w_pool_br[l].astype(BF16), w_attn_br[l].astype(BF16),
                   w_out[l].astype(BF16), row(ln1_g[l]), row(ln1_b[l]))
        z32, z16 = _in_proj(x, _reorder_in_columns(w_in[l]).astype(BF16), row(_reorder_in_columns(b_in[l])), n_s)
        attn_p, k_tail, v_tail = _attn_prompt(z32, z16, _prompt_bias_slabs(rel_table[l]), bp, tp)
        attn_s, k_buf, v_buf = _attn_sample(z32, z16, ck, cv, _bias_pairs(rel_table[l], ts, keep_s + ts, keep_s),
                                            l, n_p, bs, ts, k_buf, v_buf)
        j = l // 2
        last = l == depth - 1
        if l % 2 == 0:
            x1 = _mix(z32, z16, attn_p, attn_s, x, hist[l], weights, n_p, tp)
            x = _ffn_dense(x1, w1_dense[j].astype(BF16), w3_dense[j].astype(BF16), w2_dense[j].astype(BF16),
                           row(ln2_g[l]), row(ln2_b[l]))
            if last:
                x = (x[:n_p], x[n_p:])
        else:
            x1, mi, mw, cnt = _mix(z32, z16, attn_p, attn_s, x, hist[l], weights, n_p, tp,
                                   router=_router_operands(w_router[j], b_router[j]))
            x = _moe(x1, mi, mw, cnt, w1_exp[j].astype(BF16), w3_exp[j].astype(BF16),
                     w2_exp[j].astype(BF16), row(ln2_g[l]), row(ln2_b[l]), n_s, split=last)

        def tail(rows, col0, width, z32=z32):
            return jnp.stack([lax.slice(z32, ((b + 1) * tp - rows, col0), ((b + 1) * tp, col0 + width))
                              for b in range(bp)])

        kp_new.append(k_tail.reshape(bp, keep_p, N_HEADS, HEAD_DIM))
        vp_new.append(v_tail.reshape(bp, keep_p, N_HEADS, HEAD_DIM))
        pp_new.append(tail(POOL_HIST, COL_U * POOL_WIDTH, POOL_WIDTH))
        us = z32[n_p:, :POOL_WIDTH].reshape(bs, ts, POOL_WIDTH)
        ps_new.append(jnp.concatenate([state_pool[l], us], axis=1)[:, -POOL_HIST:])

    shape_s = (depth, bs, keep_s, N_HEADS, HEAD_DIM)
    return (x[0].reshape(bp, tp, d), x[1].reshape(bs, ts, d),
            jnp.stack(kp_new), jnp.stack(vp_new), jnp.stack(pp_new),
            k_buf.reshape(shape_s), v_buf.reshape(shape_s), jnp.stack(ps_new))
```

```python
import functools

import jax
import jax.numpy as jnp
import numpy as np
from jax import lax
from jax.experimental import pallas as pl
from jax.experimental.pallas import tpu as pltpu

F32 = jnp.float32
BF16 = jnp.bfloat16

D_MODEL = 1024
N_HEADS = 8
HEAD_DIM = 64
ATTN_WIDTH = N_HEADS * HEAD_DIM
CHUNK = 64
LEFT_CHUNKS = 8
BAND = (LEFT_CHUNKS + 1) * CHUNK
ATTN_REACH = LEFT_CHUNKS * CHUNK
REL_CLIP = 256
ATTN_SCALE = HEAD_DIM ** -0.5
POOL_WIDTH = 512
POOL_WINDOWS = (2, 4, 8, 16)
POOL_GROUP = POOL_WIDTH // len(POOL_WINDOWS)
POOL_HIST = max(POOL_WINDOWS) - 1
HIST_ROWS = POOL_HIST + 1
LEAD_ROWS = 8
D_FF = 2816
N_EXPERTS = 8
PAST_LEN = 4096
DEPTH = 2
DN_ALPHA = (2 * DEPTH) ** 0.25
LN_EPS = 1e-5
NEG_INF = -1e30
IN_WIDTH = POOL_WIDTH + 3 * ATTN_WIDTH + 2 * D_MODEL
HEAD_PAIRS = N_HEADS // 2
PAIR_W = 2 * HEAD_DIM
LANES = 128

Z32_WIDTH = POOL_WIDTH + 2 * ATTN_WIDTH
Z16_WIDTH = 2 * D_MODEL + ATTN_WIDTH
COL_U, COL_K, COL_V = 0, 1, 2
COL_GP, COL_GA = 0, 1
COL_Q = 2 * D_MODEL // ATTN_WIDTH

EXPERT_TILE = 512
FF_SPLIT = 2
ISSUE_UNROLL = 8
VMEM_LIMIT = 56 * 1024 * 1024


def _pick(n, candidates):
    for c in candidates:
        if n % c == 0:
            return c
    raise ValueError(f"no tile in {candidates} divides {n}")


def _params(sem, vmem=None):
    return pltpu.CompilerParams(dimension_semantics=sem, vmem_limit_bytes=vmem or VMEM_LIMIT)


def _layer_norm(r, g, b):
    mu = jnp.mean(r, axis=-1, keepdims=True)
    c = r - mu
    var = jnp.mean(c * c, axis=-1, keepdims=True)
    return c * lax.rsqrt(var + LN_EPS) * g + b


def _row_sources(x, t):
    if not isinstance(x, tuple):
        return [x], [pl.BlockSpec((t, x.shape[1]), lambda i: (i, 0))], (lambda refs, i: refs[0][...])
    first, last = x
    assert last.shape[0] == t and first.shape[0] % t == 0
    n_first = first.shape[0] // t
    specs = [pl.BlockSpec((t, first.shape[1]), lambda i: (jnp.minimum(i, n_first - 1), 0)),
             pl.BlockSpec((t, last.shape[1]), lambda i: (0, 0))]
    return [first, last], specs, (lambda refs, i: jnp.where(i < n_first, refs[0][...], refs[1][...]))


def _inproj_body(n_src, select, *refs):
    w_ref, b_ref, z32_ref, z16_ref = refs[n_src:]
    x = select(refs[:n_src], pl.program_id(0)).astype(BF16)
    z = jnp.dot(x, w_ref[...], preferred_element_type=F32) + b_ref[...]
    z32_ref[...] = z[:, :Z32_WIDTH]
    z16_ref[...] = z[:, Z32_WIDTH:].astype(BF16)


def _in_proj(x, w_bf, b, t):
    srcs, specs, select = _row_sources(x, t)
    n = sum(s.shape[0] for s in srcs)
    zero = lambda i: (0, 0)
    return pl.pallas_call(
        functools.partial(_inproj_body, len(srcs), select),
        out_shape=(jax.ShapeDtypeStruct((n, Z32_WIDTH), F32), jax.ShapeDtypeStruct((n, Z16_WIDTH), BF16)),
        grid=(n // t,),
        in_specs=specs + [pl.BlockSpec((D_MODEL, IN_WIDTH), zero), pl.BlockSpec((1, IN_WIDTH), zero)],
        out_specs=(pl.BlockSpec((t, Z32_WIDTH), lambda i: (i, 0)), pl.BlockSpec((t, Z16_WIDTH), lambda i: (i, 0))),
        compiler_params=_params(("arbitrary",)),
        name="in_proj",
    )(*srcs, w_bf, b)


def _reorder_in_columns(w):
    o = [0, POOL_WIDTH, POOL_WIDTH + ATTN_WIDTH, POOL_WIDTH + 2 * ATTN_WIDTH, POOL_WIDTH + 3 * ATTN_WIDTH, IN_WIDTH]
    u, q, k, v, g = (w[..., o[i]:o[i + 1]] for i in range(5))
    return jnp.concatenate([u, k, v, g, q], axis=-1)


def _pair_scores(q_pair, k_pair, bias):
    lane = lax.broadcasted_iota(jnp.int32, q_pair.shape, 1)
    qs = q_pair.astype(F32) * ATTN_SCALE
    q2 = jnp.concatenate([jnp.where(lane < HEAD_DIM, qs, 0.0), jnp.where(lane >= HEAD_DIM, qs, 0.0)], axis=0)
    s = lax.dot_general(q2.astype(BF16), k_pair, (((1,), (1,)), ((), ())), preferred_element_type=F32)
    return s + bias


def _pair_output(s, v_pair):
    rows = s.shape[0] // 2
    m = jnp.max(s, axis=-1, keepdims=True)
    e = jnp.exp(s - m)
    l = jnp.sum(e, axis=-1, keepdims=True)
    o2 = jnp.dot(e.astype(BF16), v_pair, preferred_element_type=F32) / l
    lane = lax.broadcasted_iota(jnp.int32, (rows, PAIR_W), 1)
    return jnp.where(lane < HEAD_DIM, o2[:rows], o2[rows:])


def _attn_prompt_body(q_ref, kp_ref, kc_ref, vp_ref, vc_ref, bias_ref, o_ref, kt_ref, vt_ref, kext_ref, vext_ref):
    blk = q_ref.shape[0]
    j = pl.program_id(1)

    @pl.when(j == pl.num_programs(1) - 1)
    def _():
        kt_ref[0] = kc_ref[...]
        vt_ref[0] = vc_ref[...]

    kext_ref[0:blk, :] = kp_ref[...].astype(BF16)
    kext_ref[blk:2 * blk, :] = kc_ref[...].astype(BF16)
    vext_ref[0:blk, :] = vp_ref[...].astype(BF16)
    vext_ref[blk:2 * blk, :] = vc_ref[...].astype(BF16)
    def chunk(c, carry):
        q0 = pl.multiple_of(c * CHUNK, CHUNK)
        slab = jnp.where(j == 0, c + 1, 0)
        pairs = [slice(hp * PAIR_W, (hp + 1) * PAIR_W) for hp in range(HEAD_PAIRS)]
        s = jnp.concatenate([_pair_scores(q_ref[pl.ds(q0, CHUNK), lanes], kext_ref[pl.ds(q0, BAND), lanes],
                                          bias_ref[slab, hp]) for hp, lanes in enumerate(pairs)], axis=0)
        m = jnp.max(s, axis=-1, keepdims=True)
        e = jnp.exp(s - m)
        inv = 1.0 / jnp.sum(e, axis=-1, keepdims=True)
        p = e.astype(BF16)
        lane = lax.broadcasted_iota(jnp.int32, (CHUNK, PAIR_W), 1)
        for hp, lanes in enumerate(pairs):
            rows = slice(hp * 2 * CHUNK, (hp + 1) * 2 * CHUNK)
            o2 = jnp.dot(p[rows], vext_ref[pl.ds(q0, BAND), lanes], preferred_element_type=F32) * inv[rows]
            o_ref[pl.ds(q0, CHUNK), lanes] = jnp.where(lane < HEAD_DIM, o2[:CHUNK], o2[CHUNK:])
        return carry

    lax.fori_loop(0, blk // CHUNK, chunk, 0, unroll=4)


def _attn_prompt(z32, z16, bias_pairs, batch, seq):
    blk = ATTN_REACH
    per_seq = seq // blk
    rows = batch * seq

    def cur(col):
        return lambda b, j: (b * per_seq + j, col)

    def prev(col):
        return lambda b, j: (b * per_seq + jnp.maximum(j - 1, 0), col)

    tail = jax.ShapeDtypeStruct((batch, blk, ATTN_WIDTH), F32)
    tail_spec = pl.BlockSpec((1, blk, ATTN_WIDTH), lambda b, j: (b, 0, 0))
    return pl.pallas_call(
        _attn_prompt_body,
        out_shape=(jax.ShapeDtypeStruct((rows, ATTN_WIDTH), F32), tail, tail),
        grid=(batch, per_seq),
        in_specs=[
            pl.BlockSpec((blk, ATTN_WIDTH), cur(COL_Q)),
            pl.BlockSpec((blk, ATTN_WIDTH), prev(COL_K)),
            pl.BlockSpec((blk, ATTN_WIDTH), cur(COL_K)),
            pl.BlockSpec((blk, ATTN_WIDTH), prev(COL_V)),
            pl.BlockSpec((blk, ATTN_WIDTH), cur(COL_V)),
            pl.BlockSpec((1 + LEFT_CHUNKS, HEAD_PAIRS, 2 * CHUNK, BAND), lambda b, j: (0, 0, 0, 0)),
        ],
        out_specs=(pl.BlockSpec((blk, ATTN_WIDTH), lambda b, j: (b * per_seq + j, 0)), tail_spec, tail_spec),
        scratch_shapes=[pltpu.VMEM((2 * blk, ATTN_WIDTH), BF16), pltpu.VMEM((2 * blk, ATTN_WIDTH), BF16)],
        compiler_params=_params(("arbitrary", "arbitrary")),
        name="attn_prompt",
    )(z16, z32, z32, z32, z32, bias_pairs)


def _attn_sample_body(q_ref, kn_ref, vn_ref, ck_ref, cv_ref, bias_ref, *rest):
    o_ref, ko_ref, vo_ref, kall_ref, vall_ref = rest[-5:]
    keep = ck_ref.shape[2]
    t = q_ref.shape[0]
    ck = ck_ref[0, 0]
    cv = cv_ref[0, 0]
    kn = kn_ref[...]
    vn = vn_ref[...]
    kall_ref[0:keep, :] = ck.astype(BF16)
    kall_ref[keep:keep + t, :] = kn.astype(BF16)
    vall_ref[0:keep, :] = cv.astype(BF16)
    vall_ref[keep:keep + t, :] = vn.astype(BF16)
    for hp in range(HEAD_PAIRS):
        lanes = slice(hp * PAIR_W, (hp + 1) * PAIR_W)
        s = _pair_scores(q_ref[:, lanes], kall_ref[:, lanes], bias_ref[hp])
        o_ref[:, lanes] = _pair_output(s, vall_ref[:, lanes])
    ko_ref[0, 0, 0:keep - t, :] = ck[t:keep]
    ko_ref[0, 0, keep - t:keep, :] = kn
    vo_ref[0, 0, 0:keep - t, :] = cv[t:keep]
    vo_ref[0, 0, keep - t:keep, :] = vn
    for later in range(1, ko_ref.shape[0]):
        ko_ref[later] = jnp.zeros(ko_ref.shape[1:], F32)
        vo_ref[later] = jnp.zeros(vo_ref.shape[1:], F32)


def _attn_sample(z32, z16, cache_k, cache_v, bias_pairs, layer, row0, streams, t, k_buf, v_buf):
    depth, _, keep, _ = cache_k.shape
    blk0 = row0 // t
    ins = [z16, z32, z32, cache_k, cache_v, bias_pairs]
    in_specs = [
        pl.BlockSpec((t, ATTN_WIDTH), lambda s: (blk0 + s, COL_Q)),
        pl.BlockSpec((t, ATTN_WIDTH), lambda s: (blk0 + s, COL_K)),
        pl.BlockSpec((t, ATTN_WIDTH), lambda s: (blk0 + s, COL_V)),
        pl.BlockSpec((1, 1, keep, ATTN_WIDTH), lambda s: (layer, s, 0, 0)),
        pl.BlockSpec((1, 1, keep, ATTN_WIDTH), lambda s: (layer, s, 0, 0)),
        pl.BlockSpec((HEAD_PAIRS, 2 * t, keep + t), lambda s: (0, 0, 0)),
    ]
    aliases = {}
    if k_buf is None:
        assert layer == 0
        buf_spec = pl.BlockSpec((depth, 1, keep, ATTN_WIDTH), lambda s: (0, s, 0, 0))
    else:
        aliases = {len(ins): 1, len(ins) + 1: 2}
        ins += [k_buf, v_buf]
        in_specs += [pl.BlockSpec(memory_space=pl.ANY), pl.BlockSpec(memory_space=pl.ANY)]
        buf_spec = pl.BlockSpec((1, 1, keep, ATTN_WIDTH), lambda s: (layer, s, 0, 0))
    buf = jax.ShapeDtypeStruct(cache_k.shape, F32)
    return pl.pallas_call(
        _attn_sample_body,
        out_shape=(jax.ShapeDtypeStruct((streams * t, ATTN_WIDTH), F32), buf, buf),
        grid=(streams,),
        in_specs=in_specs,
        out_specs=(
            pl.BlockSpec((t, ATTN_WIDTH), lambda s: (s, 0)),
            buf_spec,
            buf_spec,
        ),
        scratch_shapes=[pltpu.VMEM((keep + t, ATTN_WIDTH), BF16), pltpu.VMEM((keep + t, ATTN_WIDTH), BF16)],
        input_output_aliases=aliases,
        compiler_params=_params(("arbitrary",)),
        name="attn_sample",
    )(*ins)


def _merge(pooled, u, attn, gp, ga, x, wgrp_ref, scale_ref, wp_ref, wa_ref, wo_ref, g_ref, b_ref):
    pooled = pooled - u
    pool_y = jnp.dot(pooled.astype(BF16), wgrp_ref[...], preferred_element_type=F32) * scale_ref[...]
    mp = jnp.dot(pool_y.astype(BF16), wp_ref[...], preferred_element_type=F32)
    ma = jnp.dot(attn.astype(BF16), wa_ref[...], preferred_element_type=F32)
    m = jax.nn.sigmoid(gp) * mp + jax.nn.sigmoid(ga) * ma
    y = jnp.dot(m.astype(BF16), wo_ref[...], preferred_element_type=F32)
    return _layer_norm(DN_ALPHA * x + y, g_ref[...], b_ref[...])


def _window_means(read, pos, shape_out):
    outs = []
    for g, w in enumerate(POOL_WINDOWS):
        lanes = slice(g * POOL_GROUP, (g + 1) * POOL_GROUP)
        s = read(0, lanes)
        for back in range(1, w):
            s = s + read(back, lanes)
        outs.append((s / jnp.minimum(pos + 1, w).astype(F32)).reshape(shape_out))
    return jnp.concatenate(outs, axis=-1)


def _window_means_doubling(buf_ref, lvl_ref, pos, t):
    g = POOL_GROUP
    lo, hi = LEAD_ROWS, LEAD_ROWS + HIST_ROWS + t
    out0 = LEAD_ROWS + HIST_ROWS
    lvl_ref[0, lo:hi, :] = buf_ref[lo:hi, :] + buf_ref[lo - 1:hi - 1, :]
    lvl_ref[1, lo:hi, g:] = lvl_ref[0, lo:hi, g:] + lvl_ref[0, lo - 2:hi - 2, g:]
    lvl_ref[2, lo:hi, 2 * g:] = lvl_ref[1, lo:hi, 2 * g:] + lvl_ref[1, lo - 4:hi - 4, 2 * g:]
    s16 = lvl_ref[2, out0:out0 + t, 3 * g:] + lvl_ref[2, out0 - 8:out0 - 8 + t, 3 * g:]
    sums = [lvl_ref[k, out0:out0 + t, k * g:(k + 1) * g] for k in range(3)] + [s16]
    return jnp.concatenate([s / jnp.minimum(pos + 1, w).astype(F32) for s, w in zip(sums, POOL_WINDOWS)], axis=-1)


def _mix_body(n_prompt_tiles, tiles_per_seq, with_router, n_src, select,
              u_ref, up_ref, hist_ref, attn_p_ref, attn_s_ref, gp_ref, ga_ref, *rest):
    x_refs, rest = rest[:n_src], rest[n_src:]
    (wgrp_ref, scale_ref, wp_ref, wa_ref, wo_ref, g_ref, b_ref), rest = rest[:7], rest[7:]
    if with_router:
        (wr_hi_ref, wr_lo_ref, br_ref, o_ref, mi_ref, mw_ref, cnt_ref,
         buf_ref, lvl_ref, sbuf_ref, pooled_ref, attn_ref, carry_ref) = rest
    else:
        o_ref, buf_ref, lvl_ref, sbuf_ref, pooled_ref, attn_ref = rest
    t = u_ref.shape[0]
    i = pl.program_id(0)
    u = u_ref[...]

    @pl.when(i == 0)
    def _():
        buf_ref[0:LEAD_ROWS, :] = jnp.zeros((LEAD_ROWS, POOL_WIDTH), F32)
        lvl_ref[:, 0:LEAD_ROWS, :] = jnp.zeros((lvl_ref.shape[0], LEAD_ROWS, POOL_WIDTH), F32)

    @pl.when(i < n_prompt_tiles)
    def _():
        tile = i % tiles_per_seq
        h0 = LEAD_ROWS
        buf_ref[h0:h0 + HIST_ROWS, :] = jnp.where(tile == 0, 0.0, up_ref[...])
        buf_ref[h0 + HIST_ROWS:h0 + HIST_ROWS + t, :] = u
        pos = tile * t + lax.broadcasted_iota(jnp.int32, (t, 1), 0)
        pooled_ref[...] = _window_means_doubling(buf_ref, lvl_ref, pos, t)
        attn_ref[...] = attn_p_ref[...]

    @pl.when(i == n_prompt_tiles)
    def _():
        streams, hrows, _ = hist_ref.shape
        ts = t // streams
        sbuf_ref[:, 0:hrows, :] = hist_ref[...]
        sbuf_ref[:, hrows:hrows + ts, :] = u.reshape(streams, ts, POOL_WIDTH)
        pos = PAST_LEN + lax.broadcasted_iota(jnp.int32, (1, ts, 1), 1)
        read = lambda back, lanes: sbuf_ref[:, hrows - back:hrows - back + ts, lanes]
        pooled_ref[...] = _window_means(read, pos, (t, POOL_GROUP))
        attn_ref[...] = attn_s_ref[...]

    x1 = _merge(pooled_ref[...], u, attn_ref[...], gp_ref[...].astype(F32), ga_ref[...].astype(F32),
                select(x_refs, i), wgrp_ref, scale_ref, wp_ref, wa_ref, wo_ref, g_ref, b_ref)
    o_ref[...] = x1
    if with_router:
        _route(x1, wr_hi_ref, wr_lo_ref, br_ref, mi_ref, mw_ref, cnt_ref, carry_ref)


def _mix(z32, z16, attn_p, attn_s, x, hist, weights, n_prompt, seq, router=None):
    n = z32.shape[0]
    t = n - n_prompt
    x_srcs, x_specs, select = _row_sources(x, t)
    n_prompt_tiles = n_prompt // t
    streams, hrows, _ = hist.shape
    hist_per_tile = t // HIST_ROWS
    last_p = n_prompt_tiles - 1
    zero = lambda i: (0, 0)
    rows = lambda width: pl.BlockSpec((t, width), lambda i: (i, 0))
    out_shape = [jax.ShapeDtypeStruct((n, D_MODEL), F32)]
    out_specs = [rows(D_MODEL)]
    router_specs, router_scratch = [], []
    if router is not None:
        router_specs = [pl.BlockSpec((D_MODEL, LANES), zero), pl.BlockSpec((D_MODEL, LANES), zero),
                        pl.BlockSpec((1, LANES), zero)]
        out_shape += [jax.ShapeDtypeStruct((n, LANES), jnp.int32), jax.ShapeDtypeStruct((n, LANES), F32),
                      jax.ShapeDtypeStruct((1, LANES), jnp.int32)]
        out_specs += [rows(LANES), rows(LANES), pl.BlockSpec((1, LANES), zero)]
        router_scratch = [pltpu.VMEM((1, LANES), F32)]
    out = pl.pallas_call(
        functools.partial(_mix_body, n_prompt_tiles, seq // t, router is not None, len(x_srcs), select),
        out_shape=out_shape,
        grid=(n_prompt_tiles + 1,),
        in_specs=[
            pl.BlockSpec((t, POOL_WIDTH), lambda i: (i, COL_U)),
            pl.BlockSpec((HIST_ROWS, POOL_WIDTH), lambda i: (jnp.maximum(i * hist_per_tile - 1, 0), COL_U)),
            pl.BlockSpec((streams, hrows, POOL_WIDTH), lambda i: (0, 0, 0)),
            pl.BlockSpec((t, ATTN_WIDTH), lambda i: (jnp.minimum(i, last_p), 0)),
            pl.BlockSpec((t, ATTN_WIDTH), zero),
            pl.BlockSpec((t, D_MODEL), lambda i: (i, COL_GP)),
            pl.BlockSpec((t, D_MODEL), lambda i: (i, COL_GA)),
        ] + x_specs + [
            pl.BlockSpec((POOL_WIDTH, POOL_WIDTH), zero),
            pl.BlockSpec((1, POOL_WIDTH), zero),
            pl.BlockSpec((POOL_WIDTH, D_MODEL), zero),
            pl.BlockSpec((ATTN_WIDTH, D_MODEL), zero),
            pl.BlockSpec((D_MODEL, D_MODEL), zero),
            pl.BlockSpec((1, D_MODEL), zero),
            pl.BlockSpec((1, D_MODEL), zero),
        ] + router_specs,
        out_specs=out_specs,
        scratch_shapes=[
            pltpu.VMEM((LEAD_ROWS + HIST_ROWS + t, POOL_WIDTH), F32),
            pltpu.VMEM((3, LEAD_ROWS + HIST_ROWS + t, POOL_WIDTH), F32),
            pltpu.VMEM((streams, hrows + t // streams, POOL_WIDTH), F32),
            pltpu.VMEM((t, POOL_WIDTH), F32),
            pltpu.VMEM((t, ATTN_WIDTH), F32),
        ] + router_scratch,
        compiler_params=_params(("arbitrary",)),
        name="mix",
    )(z32, z32, hist, attn_p, attn_s, z16, z16, *x_srcs, *weights, *(router or ()))
    return out[0] if router is None else out


def _ffn_dense_body(x_ref, w1_ref, w3_ref, w2_ref, g_ref, b_ref, o_ref):
    x = x_ref[...]
    xb = x.astype(BF16)
    a = jnp.dot(xb, w1_ref[...], preferred_element_type=F32)
    c = jnp.dot(xb, w3_ref[...], preferred_element_type=F32)
    h = (jax.nn.silu(a) * c).astype(BF16)
    f = jnp.dot(h, w2_ref[...], preferred_element_type=F32)
    o_ref[...] = _layer_norm(DN_ALPHA * x + f, g_ref[...], b_ref[...])


def _ffn_dense(x, w1, w3, w2, g, b):
    n = x.shape[0]
    t = 256
    zero = lambda i: (0, 0)
    return pl.pallas_call(
        _ffn_dense_body,
        out_shape=jax.ShapeDtypeStruct((n, D_MODEL), F32),
        grid=(n // t,),
        in_specs=[
            pl.BlockSpec((t, D_MODEL), lambda i: (i, 0)),
            pl.BlockSpec((D_MODEL, D_FF), zero),
            pl.BlockSpec((D_MODEL, D_FF), zero),
            pl.BlockSpec((D_FF, D_MODEL), zero),
            pl.BlockSpec((1, D_MODEL), zero),
            pl.BlockSpec((1, D_MODEL), zero),
        ],
        out_specs=pl.BlockSpec((t, D_MODEL), lambda i: (i, 0)),
        compiler_params=_params(("arbitrary",)),
        name="ffn_dense",
    )(x, w1, w3, w2, g, b)


def _route(x, wr_hi_ref, wr_lo_ref, br_ref, mi_ref, mw_ref, cnt_ref, carry_ref):
    t = x.shape[0]

    @pl.when(pl.program_id(0) == 0)
    def _():
        carry_ref[...] = jnp.zeros_like(carry_ref)

    x_hi = x.astype(BF16)
    x_lo = (x - x_hi.astype(F32)).astype(BF16)
    logits = (jnp.dot(x_hi, wr_hi_ref[...], preferred_element_type=F32)
              + jnp.dot(x_lo, wr_hi_ref[...], preferred_element_type=F32)
              + jnp.dot(x_hi, wr_lo_ref[...], preferred_element_type=F32)) + br_ref[...]
    lane = lax.broadcasted_iota(jnp.int32, (t, LANES), 1)
    logits = jnp.where(lane < N_EXPERTS, logits, -jnp.inf)
    lane_f = lane.astype(F32)
    v0 = jnp.max(logits, axis=-1, keepdims=True)
    e0 = jnp.min(jnp.where(logits == v0, lane_f, float(LANES)), axis=-1, keepdims=True)
    rest = jnp.where(lane_f == e0, -jnp.inf, logits)
    v1 = jnp.max(rest, axis=-1, keepdims=True)
    e1 = jnp.min(jnp.where(rest == v1, lane_f, float(LANES)), axis=-1, keepdims=True)
    ex = jnp.exp(v1 - v0)
    w0 = 1.0 / (1.0 + ex)
    w1 = ex / (1.0 + ex)
    oh0 = (lane_f == e0).astype(F32)
    oh1 = (lane_f == e1).astype(F32)
    r_i = lax.broadcasted_iota(jnp.int32, (t, t), 0)
    c_i = lax.broadcasted_iota(jnp.int32, (t, t), 1)
    tri = (c_i < r_i).astype(BF16)
    pre0 = jnp.dot(tri, oh0.astype(BF16), preferred_element_type=F32)
    pre1 = jnp.dot(tri, oh1.astype(BF16), preferred_element_type=F32)
    cnt0 = jnp.sum(oh0, axis=0, keepdims=True)
    cnt1 = jnp.sum(oh1, axis=0, keepdims=True)
    carry = carry_ref[...]
    rank0 = jnp.sum(oh0 * (carry + pre0), axis=-1, keepdims=True)
    rank1 = jnp.sum(oh1 * (carry + cnt0 + pre1), axis=-1, keepdims=True)
    carry = carry + cnt0 + cnt1
    carry_ref[...] = carry
    cnt_ref[...] = carry.astype(jnp.int32)
    mi = jnp.where(lane == 0, e0, jnp.where(lane == 1, e1, 0.0))
    mi = jnp.where(lane == 2, rank0, jnp.where(lane == 3, rank1, mi))
    mi_ref[...] = mi.astype(jnp.int32)
    mw_ref[...] = jnp.where(lane == 0, w0, jnp.where(lane == 1, w1, 0.0))


def _router_operands(w_r, b_r):
    wr = jnp.zeros((D_MODEL, LANES), F32).at[:, :N_EXPERTS].set(w_r)
    br = jnp.zeros((1, LANES), F32).at[:, :N_EXPERTS].set(b_r[None, :])
    wr_hi = wr.astype(BF16)
    wr_lo = (wr - wr_hi.astype(F32)).astype(BF16)
    return wr_hi, wr_lo, br


def _dispatch_body(dest_ref, last_ref, nused_ref, x_ref, xs_ref, zero_ref, sem_ref):
    t = dest_ref.shape[0] // 2
    tm = zero_ref.shape[0]
    n_tiles = xs_ref.shape[0] // tm
    i = pl.program_id(0)

    @pl.when(i == 0)
    def _():
        zero_ref[...] = jnp.zeros_like(zero_ref)

        def fill(tile):
            cp = pltpu.make_async_copy(zero_ref, xs_ref.at[pl.ds(pl.multiple_of(tile * tm, tm), tm), :], sem_ref.at[1])
            cp.start()
            cp.wait()

        for e in range(N_EXPERTS):
            @pl.when(last_ref[e] >= 0)
            def _():
                fill(last_ref[e])

        def unused(tile, carry):
            fill(tile)
            return carry

        lax.fori_loop(nused_ref[0], n_tiles, unused, 0)

    def issue(r, carry):
        for slot in range(2):
            d = dest_ref[2 * r + slot]
            pltpu.make_async_copy(x_ref.at[pl.ds(r, 1), :], xs_ref.at[pl.ds(d, 1), :],
                                  sem_ref.at[0]).start(priority=slot)
        return carry

    lax.fori_loop(0, t, issue, 0, unroll=ISSUE_UNROLL)
    pltpu.make_async_copy(xs_ref.at[pl.ds(0, 2 * t), :], xs_ref.at[pl.ds(0, 2 * t), :], sem_ref.at[0]).wait()


def _dispatch(x, dest, last_tile, n_used, n_tiles, t):
    n = x.shape[0]
    tm = EXPERT_TILE
    return pl.pallas_call(
        _dispatch_body,
        out_shape=jax.ShapeDtypeStruct((n_tiles * tm, D_MODEL), F32),
        grid=(n // t,),
        in_specs=[
            pl.BlockSpec((2 * t,), lambda i: (i,), memory_space=pltpu.SMEM),
            pl.BlockSpec(memory_space=pltpu.SMEM),
            pl.BlockSpec(memory_space=pltpu.SMEM),
            pl.BlockSpec((t, D_MODEL), lambda i: (i, 0)),
        ],
        out_specs=pl.BlockSpec(memory_space=pl.ANY),
        scratch_shapes=[pltpu.VMEM((tm, D_MODEL), F32), pltpu.SemaphoreType.DMA((2,))],
        compiler_params=_params(("arbitrary",)),
        name="dispatch",
    )(dest, last_tile, n_used, x)


def _experts_body(te_ref, nused_ref, x_ref, w1_ref, w3_ref, w2_ref, o_ref):
    del te_ref
    i = pl.program_id(0)

    @pl.when(i < nused_ref[0])
    def _():
        xb = x_ref[...].astype(BF16)
        f = None
        for k in range(FF_SPLIT):
            cols = slice(k * (D_FF // FF_SPLIT), (k + 1) * (D_FF // FF_SPLIT))
            a = jnp.dot(xb, w1_ref[0, :, cols], preferred_element_type=F32)
            c = jnp.dot(xb, w3_ref[0, :, cols], preferred_element_type=F32)
            h = (jax.nn.silu(a) * c).astype(BF16)
            part = jnp.dot(h, w2_ref[0, cols, :], preferred_element_type=F32)
            f = part if f is None else f + part
        o_ref[...] = f

    @pl.when(i >= nused_ref[0])
    def _():
        o_ref[...] = jnp.zeros_like(o_ref)


def _experts(xs, tile_expert, n_used, w1, w3, w2):
    tm = EXPERT_TILE
    n_tiles = xs.shape[0] // tm

    def xmap(i, te, nu):
        return (jnp.minimum(i, nu[0] - 1), 0)

    def wmap(i, te, nu):
        return (te[jnp.minimum(i, nu[0] - 1)], 0, 0)

    return pl.pallas_call(
        _experts_body,
        out_shape=jax.ShapeDtypeStruct(xs.shape, F32),
        grid_spec=pltpu.PrefetchScalarGridSpec(
            num_scalar_prefetch=2,
            grid=(n_tiles,),
            in_specs=[
                pl.BlockSpec((tm, D_MODEL), xmap),
                pl.BlockSpec((1, D_MODEL, D_FF), wmap),
                pl.BlockSpec((1, D_MODEL, D_FF), wmap),
                pl.BlockSpec((1, D_FF, D_MODEL), wmap),
            ],
            out_specs=pl.BlockSpec((tm, D_MODEL), lambda i, te, nu: (i, 0)),
        ),
        compiler_params=_params(("arbitrary",), 60 * 1024 * 1024),
        name="experts",
    )(tile_expert, n_used, xs, w1, w3, w2)


def _combine_body(n_first, dest_ref, next_ref, x_ref, mw_ref, g_ref, b_ref, ys_ref, *rest):
    out_refs, (y0_ref, y1_ref, sem_ref) = rest[:-3], rest[-3:]
    t = x_ref.shape[0]
    i = pl.program_id(0)
    cur = i % 2

    def gather(idx_ref, buf):
        def issue(r, carry):
            pltpu.make_async_copy(ys_ref.at[pl.ds(idx_ref[2 * r], 1), :], y0_ref.at[buf, pl.ds(r, 1), :],
                                  sem_ref.at[buf]).start(priority=0)
            pltpu.make_async_copy(ys_ref.at[pl.ds(idx_ref[2 * r + 1], 1), :], y1_ref.at[buf, pl.ds(r, 1), :],
                                  sem_ref.at[buf]).start(priority=1)
            return carry

        lax.fori_loop(0, t, issue, 0, unroll=ISSUE_UNROLL)

    @pl.when(i == 0)
    def _():
        gather(dest_ref, 0)

    @pl.when(i + 1 < pl.num_programs(0))
    def _():
        gather(next_ref, 1 - cur)

    pltpu.make_async_copy(ys_ref.at[pl.ds(0, t), :], y0_ref.at[cur], sem_ref.at[cur]).wait()
    pltpu.make_async_copy(ys_ref.at[pl.ds(0, t), :], y1_ref.at[cur], sem_ref.at[cur]).wait()
    mw = mw_ref[...]
    f = mw[:, 0:1] * y0_ref[cur] + mw[:, 1:2] * y1_ref[cur]
    out = _layer_norm(DN_ALPHA * x_ref[...] + f, g_ref[...], b_ref[...])
    if len(out_refs) == 1:
        out_refs[0][...] = out
    else:
        @pl.when(i < n_first)
        def _():
            out_refs[0][...] = out

        @pl.when(i == n_first)
        def _():
            out_refs[1][...] = out


def _combine(x, ys, dest, mw, g, b, t, split):
    n = x.shape[0]
    zero = lambda i: (0, 0)
    n_first = n // t - 1
    last = n // t - 1
    if split:
        out_shape = (jax.ShapeDtypeStruct((n - t, D_MODEL), F32), jax.ShapeDtypeStruct((t, D_MODEL), F32))
        out_specs = (pl.BlockSpec((t, D_MODEL), lambda i: (jnp.minimum(i, n_first - 1), 0)),
                     pl.BlockSpec((t, D_MODEL), zero))
    else:
        out_shape = jax.ShapeDtypeStruct((n, D_MODEL), F32)
        out_specs = pl.BlockSpec((t, D_MODEL), lambda i: (i, 0))
    return pl.pallas_call(
        functools.partial(_combine_body, n_first),
        out_shape=out_shape,
        grid=(n // t,),
        in_specs=[
            pl.BlockSpec((2 * t,), lambda i: (i,), memory_space=pltpu.SMEM),
            pl.BlockSpec((2 * t,), lambda i: (jnp.minimum(i + 1, last),), memory_space=pltpu.SMEM),
            pl.BlockSpec((t, D_MODEL), lambda i: (i, 0)),
            pl.BlockSpec((t, LANES), lambda i: (i, 0)),
            pl.BlockSpec((1, D_MODEL), zero),
            pl.BlockSpec((1, D_MODEL), zero),
            pl.BlockSpec(memory_space=pl.ANY),
        ],
        out_specs=out_specs,
        scratch_shapes=[pltpu.VMEM((2, t, D_MODEL), F32), pltpu.VMEM((2, t, D_MODEL), F32),
                        pltpu.SemaphoreType.DMA((2,))],
        compiler_params=_params(("arbitrary",)),
        name="combine",
    )(dest, dest, x, mw, g, b, ys)


def _moe(x, mi, mw, cnt, w1, w3, w2, g, b, t, split):
    n = x.shape[0]
    tm = EXPERT_TILE
    n_tiles = (2 * n + N_EXPERTS * (tm - 1)) // tm
    counts = cnt[0, :N_EXPERTS]
    tiles = (counts + tm - 1) // tm
    tile_end = jnp.cumsum(tiles)
    start = (tile_end - tiles) * tm
    n_used = tile_end[-1:].astype(jnp.int32)
    tile_expert = jnp.sum((tile_end[None, :] <= jnp.arange(n_tiles)[:, None]).astype(jnp.int32), axis=1)
    tile_expert = jnp.minimum(tile_expert, N_EXPERTS - 1)
    last_tile = jnp.where(tiles > 0, tile_end - 1, -1).astype(jnp.int32)
    experts = mi[:, 0:2]
    group_start = sum(jnp.where(experts == e, start[e], 0) for e in range(N_EXPERTS))
    dest = (group_start + mi[:, 2:4]).astype(jnp.int32).reshape(2 * n)
    xs = _dispatch(x, dest, last_tile, n_used, n_tiles, t)
    ys = _experts(xs, tile_expert, n_used, w1, w3, w2)
    return _combine(x, ys, dest, mw, g, b, t, split)


def _block_diag(w):
    g, c, _ = w.shape
    out = jnp.zeros((g * c, g * c), w.dtype)
    for i in range(g):
        out = out.at[i * c:(i + 1) * c, i * c:(i + 1) * c].set(w[i])
    return out


def _bias_pairs(table, t_q, n_keys, offset):
    hi = t_q - 1 + offset
    span = t_q - 1 + n_keys
    cols = np.clip(hi - np.arange(span), -REL_CLIP, REL_CLIP) + REL_CLIP
    rev = table[:, cols].astype(F32)
    skew = jnp.tile(rev, (1, t_q + 1))[:, :t_q * (span + 1)].reshape(N_HEADS, t_q, span + 1)
    slab = skew[:, ::-1, :n_keys]
    return slab.reshape(HEAD_PAIRS, 2 * t_q, n_keys)


def _prompt_bias_slabs(table):
    bias = _bias_pairs(table, CHUNK, BAND, ATTN_REACH)
    col = np.arange(BAND)[None, :]
    first_valid = np.concatenate([[0], (LEFT_CHUNKS - np.arange(LEFT_CHUNKS)) * CHUNK])[:, None]
    valid = jnp.asarray(col >= first_valid)
    return jnp.where(valid[:, None, None, :], bias[None], NEG_INF)


def kernel(x_prompt, x_sample, cache_k, cache_v, state_pool, w_in, b_in, w_pool_grp, pool_scale,
           rel_table, w_pool_br, w_attn_br, w_out, ln1_g, ln1_b, ln2_g, ln2_b,
           w1_dense, w3_dense, w2_dense, w_router, b_router, w1_exp, w3_exp, w2_exp):
    bp, tp, d = x_prompt.shape
    bs, ts, _ = x_sample.shape
    n_p, n_s = bp * tp, bs * ts
    n = n_p + n_s
    depth = w_in.shape[0]
    keep_s = cache_k.shape[2]
    keep_p = min(ATTN_REACH, tp)

    x = (x_prompt.reshape(n_p, d), x_sample.reshape(n_s, d))
    ck = cache_k.reshape(depth, bs, keep_s, ATTN_WIDTH)
    cv = cache_v.reshape(depth, bs, keep_s, ATTN_WIDTH)
    hist = jnp.pad(state_pool, ((0, 0), (0, 0), (HIST_ROWS - POOL_HIST, 0), (0, 0)))
    row = lambda v: v[None, :].astype(F32)

    k_buf = v_buf = None
    kp_new, vp_new, pp_new, ps_new = [], [], [], []
    for l in range(depth):
        wgrp = _block_diag(w_pool_grp[l]).astype(BF16)
        weights = (wgrp, row(pool_scale[l]), w_pool_br[l].astype(BF16), w_attn_br[l].astype(BF16),
                   w_out[l].astype(BF16), row(ln1_g[l]), row(ln1_b[l]))
        z32, z16 = _in_proj(x, _reorder_in_columns(w_in[l]).astype(BF16), row(_reorder_in_columns(b_in[l])), n_s)
        attn_p, k_tail, v_tail = _attn_prompt(z32, z16, _prompt_bias_slabs(rel_table[l]), bp, tp)
        attn_s, k_buf, v_buf = _attn_sample(z32, z16, ck, cv, _bias_pairs(rel_table[l], ts, keep_s + ts, keep_s),
                                            l, n_p, bs, ts, k_buf, v_buf)
        j = l // 2
        last = l == depth - 1
        if l % 2 == 0:
            x1 = _mix(z32, z16, attn_p, attn_s, x, hist[l], weights, n_p, tp)
            x = _ffn_dense(x1, w1_dense[j].astype(BF16), w3_dense[j].astype(BF16), w2_dense[j].astype(BF16),
                           row(ln2_g[l]), row(ln2_b[l]))
            if last:
                x = (x[:n_p], x[n_p:])
        else:
            x1, mi, mw, cnt = _mix(z32, z16, attn_p, attn_s, x, hist[l], weights, n_p, tp,
                                   router=_router_operands(w_router[j], b_router[j]))
            x = _moe(x1, mi, mw, cnt, w1_exp[j].astype(BF16), w3_exp[j].astype(BF16),
                     w2_exp[j].astype(BF16), row(ln2_g[l]), row(ln2_b[l]), n_s, split=last)

        def tail(rows, col0, width, z32=z32):
            return jnp.stack([lax.slice(z32, ((b + 1) * tp - rows, col0), ((b + 1) * tp, col0 + width))
                              for b in range(bp)])

        kp_new.append(k_tail.reshape(bp, keep_p, N_HEADS, HEAD_DIM))
        vp_new.append(v_tail.reshape(bp, keep_p, N_HEADS, HEAD_DIM))
        pp_new.append(tail(POOL_HIST, COL_U * POOL_WIDTH, POOL_WIDTH))
        us = z32[n_p:, :POOL_WIDTH].reshape(bs, ts, POOL_WIDTH)
        ps_new.append(jnp.concatenate([state_pool[l], us], axis=1)[:, -POOL_HIST:])

    shape_s = (depth, bs, keep_s, N_HEADS, HEAD_DIM)
    return (x[0].reshape(bp, tp, d), x[1].reshape(bs, ts, d),
            jnp.stack(kp_new), jnp.stack(vp_new), jnp.stack(pp_new),
            k_buf.reshape(shape_s), v_buf.reshape(shape_s), jnp.stack(ps_new))
```

```python
import functools

import jax
import jax.numpy as jnp
import numpy as np
from jax import lax
from jax.experimental import pallas as pl
from jax.experimental.pallas import tpu as pltpu

F32 = jnp.float32
BF16 = jnp.bfloat16

D_MODEL = 1024
N_HEADS = 8
HEAD_DIM = 64
ATTN_WIDTH = N_HEADS * HEAD_DIM
CHUNK = 64
LEFT_CHUNKS = 8
BAND = (LEFT_CHUNKS + 1) * CHUNK
ATTN_REACH = LEFT_CHUNKS * CHUNK
REL_CLIP = 256
ATTN_SCALE = HEAD_DIM ** -0.5
POOL_WIDTH = 512
POOL_WINDOWS = (2, 4, 8, 16)
POOL_GROUP = POOL_WIDTH // len(POOL_WINDOWS)
POOL_HIST = max(POOL_WINDOWS) - 1
HIST_ROWS = POOL_HIST + 1
LEAD_ROWS = 8
D_FF = 2816
N_EXPERTS = 8
PAST_LEN = 4096
DEPTH = 2
DN_ALPHA = (2 * DEPTH) ** 0.25
LN_EPS = 1e-5
NEG_INF = -1e30
IN_WIDTH = POOL_WIDTH + 3 * ATTN_WIDTH + 2 * D_MODEL
HEAD_PAIRS = N_HEADS // 2
PAIR_W = 2 * HEAD_DIM
LANES = 128

Z32_WIDTH = POOL_WIDTH + 2 * ATTN_WIDTH
Z16_WIDTH = 2 * D_MODEL + ATTN_WIDTH
COL_U, COL_K, COL_V = 0, 1, 2
COL_GP, COL_GA = 0, 1
COL_Q = 2 * D_MODEL // ATTN_WIDTH

EXPERT_TILE = 512
FF_CHUNK = 256
FF_PIECES = D_FF // FF_CHUNK
ISSUE_UNROLL = 8
VMEM_LIMIT = 56 * 1024 * 1024
EXPERTS_VMEM_LIMIT = 62 * 1024 * 1024


def _pick(n, candidates):
    for c in candidates:
        if n % c == 0:
            return c
    raise ValueError(f"no tile in {candidates} divides {n}")


def _params(sem, vmem=None):
    return pltpu.CompilerParams(dimension_semantics=sem, vmem_limit_bytes=vmem or VMEM_LIMIT)


def _layer_norm(r, g, b):
    mu = jnp.mean(r, axis=-1, keepdims=True)
    c = r - mu
    var = jnp.mean(c * c, axis=-1, keepdims=True)
    return c * lax.rsqrt(var + LN_EPS) * g + b


def _row_sources(x, t):
    if not isinstance(x, tuple):
        return [x], [pl.BlockSpec((t, x.shape[1]), lambda i: (i, 0))], (lambda refs, i: refs[0][...])
    first, last = x
    assert last.shape[0] == t and first.shape[0] % t == 0
    n_first = first.shape[0] // t
    specs = [pl.BlockSpec((t, first.shape[1]), lambda i: (jnp.minimum(i, n_first - 1), 0)),
             pl.BlockSpec((t, last.shape[1]), lambda i: (0, 0))]
    return [first, last], specs, (lambda refs, i: jnp.where(i < n_first, refs[0][...], refs[1][...]))


def _inproj_body(n_src, select, *refs):
    w_ref, b_ref, z32_ref, z16_ref = refs[n_src:]
    x = select(refs[:n_src], pl.program_id(0)).astype(BF16)
    z = jnp.dot(x, w_ref[...], preferred_element_type=F32) + b_ref[...]
    z32_ref[...] = z[:, :Z32_WIDTH]
    z16_ref[...] = z[:, Z32_WIDTH:].astype(BF16)


def _in_proj(x, w_bf, b, t):
    srcs, specs, select = _row_sources(x, t)
    n = sum(s.shape[0] for s in srcs)
    zero = lambda i: (0, 0)
    return pl.pallas_call(
        functools.partial(_inproj_body, len(srcs), select),
        out_shape=(jax.ShapeDtypeStruct((n, Z32_WIDTH), F32), jax.ShapeDtypeStruct((n, Z16_WIDTH), BF16)),
        grid=(n // t,),
        in_specs=specs + [pl.BlockSpec((D_MODEL, IN_WIDTH), zero), pl.BlockSpec((1, IN_WIDTH), zero)],
        out_specs=(pl.BlockSpec((t, Z32_WIDTH), lambda i: (i, 0)), pl.BlockSpec((t, Z16_WIDTH), lambda i: (i, 0))),
        compiler_params=_params(("arbitrary",)),
        name="in_proj",
    )(*srcs, w_bf, b)


def _reorder_in_columns(w):
    o = [0, POOL_WIDTH, POOL_WIDTH + ATTN_WIDTH, POOL_WIDTH + 2 * ATTN_WIDTH, POOL_WIDTH + 3 * ATTN_WIDTH, IN_WIDTH]
    u, q, k, v, g = (w[..., o[i]:o[i + 1]] for i in range(5))
    return jnp.concatenate([u, k, v, g, q], axis=-1)


def _pair_scores(q_pair, k_pair, bias):
    lane = lax.broadcasted_iota(jnp.int32, q_pair.shape, 1)
    qs = q_pair.astype(F32) * ATTN_SCALE
    q2 = jnp.concatenate([jnp.where(lane < HEAD_DIM, qs, 0.0), jnp.where(lane >= HEAD_DIM, qs, 0.0)], axis=0)
    s = lax.dot_general(q2.astype(BF16), k_pair, (((1,), (1,)), ((), ())), preferred_element_type=F32)
    return s + bias


def _pair_output(s, v_pair):
    rows = s.shape[0] // 2
    m = jnp.max(s, axis=-1, keepdims=True)
    e = jnp.exp(s - m)
    l = jnp.sum(e, axis=-1, keepdims=True)
    o2 = jnp.dot(e.astype(BF16), v_pair, preferred_element_type=F32) / l
    lane = lax.broadcasted_iota(jnp.int32, (rows, PAIR_W), 1)
    return jnp.where(lane < HEAD_DIM, o2[:rows], o2[rows:])


def _attn_prompt_body(q_ref, kp_ref, kc_ref, vp_ref, vc_ref, bias_ref, o_ref, kt_ref, vt_ref, kext_ref, vext_ref):
    blk = q_ref.shape[0]
    j = pl.program_id(1)

    @pl.when(j == pl.num_programs(1) - 1)
    def _():
        kt_ref[0] = kc_ref[...]
        vt_ref[0] = vc_ref[...]

    kext_ref[0:blk, :] = kp_ref[...].astype(BF16)
    kext_ref[blk:2 * blk, :] = kc_ref[...].astype(BF16)
    vext_ref[0:blk, :] = vp_ref[...].astype(BF16)
    vext_ref[blk:2 * blk, :] = vc_ref[...].astype(BF16)
    def chunk(c, carry):
        q0 = pl.multiple_of(c * CHUNK, CHUNK)
        slab = jnp.where(j == 0, c + 1, 0)
        pairs = [slice(hp * PAIR_W, (hp + 1) * PAIR_W) for hp in range(HEAD_PAIRS)]
        s = jnp.concatenate([_pair_scores(q_ref[pl.ds(q0, CHUNK), lanes], kext_ref[pl.ds(q0, BAND), lanes],
                                          bias_ref[slab, hp]) for hp, lanes in enumerate(pairs)], axis=0)
        m = jnp.max(s, axis=-1, keepdims=True)
        e = jnp.exp(s - m)
        inv = 1.0 / jnp.sum(e, axis=-1, keepdims=True)
        p = e.astype(BF16)
        lane = lax.broadcasted_iota(jnp.int32, (CHUNK, PAIR_W), 1)
        for hp, lanes in enumerate(pairs):
            rows = slice(hp * 2 * CHUNK, (hp + 1) * 2 * CHUNK)
            o2 = jnp.dot(p[rows], vext_ref[pl.ds(q0, BAND), lanes], preferred_element_type=F32) * inv[rows]
            o_ref[pl.ds(q0, CHUNK), lanes] = jnp.where(lane < HEAD_DIM, o2[:CHUNK], o2[CHUNK:])
        return carry

    lax.fori_loop(0, blk // CHUNK, chunk, 0, unroll=4)


def _attn_prompt(z32, z16, bias_pairs, batch, seq):
    blk = ATTN_REACH
    per_seq = seq // blk
    rows = batch * seq

    def cur(col):
        return lambda b, j: (b * per_seq + j, col)

    def prev(col):
        return lambda b, j: (b * per_seq + jnp.maximum(j - 1, 0), col)

    tail = jax.ShapeDtypeStruct((batch, blk, ATTN_WIDTH), F32)
    tail_spec = pl.BlockSpec((1, blk, ATTN_WIDTH), lambda b, j: (b, 0, 0))
    return pl.pallas_call(
        _attn_prompt_body,
        out_shape=(jax.ShapeDtypeStruct((rows, ATTN_WIDTH), F32), tail, tail),
        grid=(batch, per_seq),
        in_specs=[
            pl.BlockSpec((blk, ATTN_WIDTH), cur(COL_Q)),
            pl.BlockSpec((blk, ATTN_WIDTH), prev(COL_K)),
            pl.BlockSpec((blk, ATTN_WIDTH), cur(COL_K)),
            pl.BlockSpec((blk, ATTN_WIDTH), prev(COL_V)),
            pl.BlockSpec((blk, ATTN_WIDTH), cur(COL_V)),
            pl.BlockSpec((1 + LEFT_CHUNKS, HEAD_PAIRS, 2 * CHUNK, BAND), lambda b, j: (0, 0, 0, 0)),
        ],
        out_specs=(pl.BlockSpec((blk, ATTN_WIDTH), lambda b, j: (b * per_seq + j, 0)), tail_spec, tail_spec),
        scratch_shapes=[pltpu.VMEM((2 * blk, ATTN_WIDTH), BF16), pltpu.VMEM((2 * blk, ATTN_WIDTH), BF16)],
        compiler_params=_params(("arbitrary", "arbitrary")),
        name="attn_prompt",
    )(z16, z32, z32, z32, z32, bias_pairs)


def _attn_sample_body(q_ref, kn_ref, vn_ref, ck_ref, cv_ref, bias_ref, *rest):
    o_ref, ko_ref, vo_ref, kall_ref, vall_ref = rest[-5:]
    keep = ck_ref.shape[2]
    t = q_ref.shape[0]
    ck = ck_ref[0, 0]
    cv = cv_ref[0, 0]
    kn = kn_ref[...]
    vn = vn_ref[...]
    kall_ref[0:keep, :] = ck.astype(BF16)
    kall_ref[keep:keep + t, :] = kn.astype(BF16)
    vall_ref[0:keep, :] = cv.astype(BF16)
    vall_ref[keep:keep + t, :] = vn.astype(BF16)
    for hp in range(HEAD_PAIRS):
        lanes = slice(hp * PAIR_W, (hp + 1) * PAIR_W)
        s = _pair_scores(q_ref[:, lanes], kall_ref[:, lanes], bias_ref[hp])
        o_ref[:, lanes] = _pair_output(s, vall_ref[:, lanes])
    ko_ref[0, 0, 0:keep - t, :] = ck[t:keep]
    ko_ref[0, 0, keep - t:keep, :] = kn
    vo_ref[0, 0, 0:keep - t, :] = cv[t:keep]
    vo_ref[0, 0, keep - t:keep, :] = vn
    for later in range(1, ko_ref.shape[0]):
        ko_ref[later] = jnp.zeros(ko_ref.shape[1:], F32)
        vo_ref[later] = jnp.zeros(vo_ref.shape[1:], F32)


def _attn_sample(z32, z16, cache_k, cache_v, bias_pairs, layer, row0, streams, t, k_buf, v_buf):
    depth, _, keep, _ = cache_k.shape
    blk0 = row0 // t
    ins = [z16, z32, z32, cache_k, cache_v, bias_pairs]
    in_specs = [
        pl.BlockSpec((t, ATTN_WIDTH), lambda s: (blk0 + s, COL_Q)),
        pl.BlockSpec((t, ATTN_WIDTH), lambda s: (blk0 + s, COL_K)),
        pl.BlockSpec((t, ATTN_WIDTH), lambda s: (blk0 + s, COL_V)),
        pl.BlockSpec((1, 1, keep, ATTN_WIDTH), lambda s: (layer, s, 0, 0)),
        pl.BlockSpec((1, 1, keep, ATTN_WIDTH), lambda s: (layer, s, 0, 0)),
        pl.BlockSpec((HEAD_PAIRS, 2 * t, keep + t), lambda s: (0, 0, 0)),
    ]
    aliases = {}
    if k_buf is None:
        assert layer == 0
        buf_spec = pl.BlockSpec((depth, 1, keep, ATTN_WIDTH), lambda s: (0, s, 0, 0))
    else:
        aliases = {len(ins): 1, len(ins) + 1: 2}
        ins += [k_buf, v_buf]
        in_specs += [pl.BlockSpec(memory_space=pl.ANY), pl.BlockSpec(memory_space=pl.ANY)]
        buf_spec = pl.BlockSpec((1, 1, keep, ATTN_WIDTH), lambda s: (layer, s, 0, 0))
    buf = jax.ShapeDtypeStruct(cache_k.shape, F32)
    return pl.pallas_call(
        _attn_sample_body,
        out_shape=(jax.ShapeDtypeStruct((streams * t, ATTN_WIDTH), F32), buf, buf),
        grid=(streams,),
        in_specs=in_specs,
        out_specs=(
            pl.BlockSpec((t, ATTN_WIDTH), lambda s: (s, 0)),
            buf_spec,
            buf_spec,
        ),
        scratch_shapes=[pltpu.VMEM((keep + t, ATTN_WIDTH), BF16), pltpu.VMEM((keep + t, ATTN_WIDTH), BF16)],
        input_output_aliases=aliases,
        compiler_params=_params(("arbitrary",)),
        name="attn_sample",
    )(*ins)


def _merge(pooled, u, attn, gp, ga, x, wgrp_ref, scale_ref, wp_ref, wa_ref, wo_ref, g_ref, b_ref):
    pooled = pooled - u
    pool_y = jnp.dot(pooled.astype(BF16), wgrp_ref[...], preferred_element_type=F32) * scale_ref[...]
    mp = jnp.dot(pool_y.astype(BF16), wp_ref[...], preferred_element_type=F32)
    ma = jnp.dot(attn.astype(BF16), wa_ref[...], preferred_element_type=F32)
    m = jax.nn.sigmoid(gp) * mp + jax.nn.sigmoid(ga) * ma
    y = jnp.dot(m.astype(BF16), wo_ref[...], preferred_element_type=F32)
    return _layer_norm(DN_ALPHA * x + y, g_ref[...], b_ref[...])


def _window_means(read, pos, shape_out):
    outs = []
    for g, w in enumerate(POOL_WINDOWS):
        lanes = slice(g * POOL_GROUP, (g + 1) * POOL_GROUP)
        s = read(0, lanes)
        for back in range(1, w):
            s = s + read(back, lanes)
        outs.append((s / jnp.minimum(pos + 1, w).astype(F32)).reshape(shape_out))
    return jnp.concatenate(outs, axis=-1)


def _window_means_doubling(buf_ref, lvl_ref, pos, t):
    g = POOL_GROUP
    lo, hi = LEAD_ROWS, LEAD_ROWS + HIST_ROWS + t
    out0 = LEAD_ROWS + HIST_ROWS
    lvl_ref[0, lo:hi, :] = buf_ref[lo:hi, :] + buf_ref[lo - 1:hi - 1, :]
    lvl_ref[1, lo:hi, g:] = lvl_ref[0, lo:hi, g:] + lvl_ref[0, lo - 2:hi - 2, g:]
    lvl_ref[2, lo:hi, 2 * g:] = lvl_ref[1, lo:hi, 2 * g:] + lvl_ref[1, lo - 4:hi - 4, 2 * g:]
    s16 = lvl_ref[2, out0:out0 + t, 3 * g:] + lvl_ref[2, out0 - 8:out0 - 8 + t, 3 * g:]
    sums = [lvl_ref[k, out0:out0 + t, k * g:(k + 1) * g] for k in range(3)] + [s16]
    return jnp.concatenate([s / jnp.minimum(pos + 1, w).astype(F32) for s, w in zip(sums, POOL_WINDOWS)], axis=-1)


def _mix_body(n_prompt_tiles, tiles_per_seq, with_router, n_src, select,
              u_ref, up_ref, hist_ref, attn_p_ref, attn_s_ref, gp_ref, ga_ref, *rest):
    x_refs, rest = rest[:n_src], rest[n_src:]
    (wgrp_ref, scale_ref, wp_ref, wa_ref, wo_ref, g_ref, b_ref), rest = rest[:7], rest[7:]
    if with_router:
        (wr_hi_ref, wr_lo_ref, br_ref, o_ref, mi_ref, mw_ref, cnt_ref,
         buf_ref, lvl_ref, sbuf_ref, pooled_ref, attn_ref, carry_ref) = rest
    else:
        o_ref, buf_ref, lvl_ref, sbuf_ref, pooled_ref, attn_ref = rest
    t = u_ref.shape[0]
    i = pl.program_id(0)
    u = u_ref[...]

    @pl.when(i == 0)
    def _():
        buf_ref[0:LEAD_ROWS, :] = jnp.zeros((LEAD_ROWS, POOL_WIDTH), F32)
        lvl_ref[:, 0:LEAD_ROWS, :] = jnp.zeros((lvl_ref.shape[0], LEAD_ROWS, POOL_WIDTH), F32)

    @pl.when(i < n_prompt_tiles)
    def _():
        tile = i % tiles_per_seq
        h0 = LEAD_ROWS
        buf_ref[h0:h0 + HIST_ROWS, :] = jnp.where(tile == 0, 0.0, up_ref[...])
        buf_ref[h0 + HIST_ROWS:h0 + HIST_ROWS + t, :] = u
        pos = tile * t + lax.broadcasted_iota(jnp.int32, (t, 1), 0)
        pooled_ref[...] = _window_means_doubling(buf_ref, lvl_ref, pos, t)
        attn_ref[...] = attn_p_ref[...]

    @pl.when(i == n_prompt_tiles)
    def _():
        streams, hrows, _ = hist_ref.shape
        ts = t // streams
        sbuf_ref[:, 0:hrows, :] = hist_ref[...]
        sbuf_ref[:, hrows:hrows + ts, :] = u.reshape(streams, ts, POOL_WIDTH)
        pos = PAST_LEN + lax.broadcasted_iota(jnp.int32, (1, ts, 1), 1)
        read = lambda back, lanes: sbuf_ref[:, hrows - back:hrows - back + ts, lanes]
        pooled_ref[...] = _window_means(read, pos, (t, POOL_GROUP))
        attn_ref[...] = attn_s_ref[...]

    x1 = _merge(pooled_ref[...], u, attn_ref[...], gp_ref[...].astype(F32), ga_ref[...].astype(F32),
                select(x_refs, i), wgrp_ref, scale_ref, wp_ref, wa_ref, wo_ref, g_ref, b_ref)
    o_ref[...] = x1
    if with_router:
        _route(x1, wr_hi_ref, wr_lo_ref, br_ref, mi_ref, mw_ref, cnt_ref, carry_ref)


def _mix(z32, z16, attn_p, attn_s, x, hist, weights, n_prompt, seq, router=None):
    n = z32.shape[0]
    t = n - n_prompt
    x_srcs, x_specs, select = _row_sources(x, t)
    n_prompt_tiles = n_prompt // t
    streams, hrows, _ = hist.shape
    hist_per_tile = t // HIST_ROWS
    last_p = n_prompt_tiles - 1
    zero = lambda i: (0, 0)
    rows = lambda width: pl.BlockSpec((t, width), lambda i: (i, 0))
    out_shape = [jax.ShapeDtypeStruct((n, D_MODEL), F32)]
    out_specs = [rows(D_MODEL)]
    router_specs, router_scratch = [], []
    if router is not None:
        router_specs = [pl.BlockSpec((D_MODEL, LANES), zero), pl.BlockSpec((D_MODEL, LANES), zero),
                        pl.BlockSpec((1, LANES), zero)]
        out_shape += [jax.ShapeDtypeStruct((n, LANES), jnp.int32), jax.ShapeDtypeStruct((n, LANES), F32),
                      jax.ShapeDtypeStruct((1, LANES), jnp.int32)]
        out_specs += [rows(LANES), rows(LANES), pl.BlockSpec((1, LANES), zero)]
        router_scratch = [pltpu.VMEM((1, LANES), F32)]
    out = pl.pallas_call(
        functools.partial(_mix_body, n_prompt_tiles, seq // t, router is not None, len(x_srcs), select),
        out_shape=out_shape,
        grid=(n_prompt_tiles + 1,),
        in_specs=[
            pl.BlockSpec((t, POOL_WIDTH), lambda i: (i, COL_U)),
            pl.BlockSpec((HIST_ROWS, POOL_WIDTH), lambda i: (jnp.maximum(i * hist_per_tile - 1, 0), COL_U)),
            pl.BlockSpec((streams, hrows, POOL_WIDTH), lambda i: (0, 0, 0)),
            pl.BlockSpec((t, ATTN_WIDTH), lambda i: (jnp.minimum(i, last_p), 0)),
            pl.BlockSpec((t, ATTN_WIDTH), zero),
            pl.BlockSpec((t, D_MODEL), lambda i: (i, COL_GP)),
            pl.BlockSpec((t, D_MODEL), lambda i: (i, COL_GA)),
        ] + x_specs + [
            pl.BlockSpec((POOL_WIDTH, POOL_WIDTH), zero),
            pl.BlockSpec((1, POOL_WIDTH), zero),
            pl.BlockSpec((POOL_WIDTH, D_MODEL), zero),
            pl.BlockSpec((ATTN_WIDTH, D_MODEL), zero),
            pl.BlockSpec((D_MODEL, D_MODEL), zero),
            pl.BlockSpec((1, D_MODEL), zero),
            pl.BlockSpec((1, D_MODEL), zero),
        ] + router_specs,
        out_specs=out_specs,
        scratch_shapes=[
            pltpu.VMEM((LEAD_ROWS + HIST_ROWS + t, POOL_WIDTH), F32),
            pltpu.VMEM((3, LEAD_ROWS + HIST_ROWS + t, POOL_WIDTH), F32),
            pltpu.VMEM((streams, hrows + t // streams, POOL_WIDTH), F32),
            pltpu.VMEM((t, POOL_WIDTH), F32),
            pltpu.VMEM((t, ATTN_WIDTH), F32),
        ] + router_scratch,
        compiler_params=_params(("arbitrary",)),
        name="mix",
    )(z32, z32, hist, attn_p, attn_s, z16, z16, *x_srcs, *weights, *(router or ()))
    return out[0] if router is None else out


def _ffn_dense_body(x_ref, w1_ref, w3_ref, w2_ref, g_ref, b_ref, o_ref):
    x = x_ref[...]
    xb = x.astype(BF16)
    a = jnp.dot(xb, w1_ref[...], preferred_element_type=F32)
    c = jnp.dot(xb, w3_ref[...], preferred_element_type=F32)
    h = (jax.nn.silu(a) * c).astype(BF16)
    f = jnp.dot(h, w2_ref[...], preferred_element_type=F32)
    o_ref[...] = _layer_norm(DN_ALPHA * x + f, g_ref[...], b_ref[...])


def _ffn_dense(x, w1, w3, w2, g, b):
    n = x.shape[0]
    t = 256
    zero = lambda i: (0, 0)
    return pl.pallas_call(
        _ffn_dense_body,
        out_shape=jax.ShapeDtypeStruct((n, D_MODEL), F32),
        grid=(n // t,),
        in_specs=[
            pl.BlockSpec((t, D_MODEL), lambda i: (i, 0)),
            pl.BlockSpec((D_MODEL, D_FF), zero),
            pl.BlockSpec((D_MODEL, D_FF), zero),
            pl.BlockSpec((D_FF, D_MODEL), zero),
            pl.BlockSpec((1, D_MODEL), zero),
            pl.BlockSpec((1, D_MODEL), zero),
        ],
        out_specs=pl.BlockSpec((t, D_MODEL), lambda i: (i, 0)),
        compiler_params=_params(("arbitrary",)),
        name="ffn_dense",
    )(x, w1, w3, w2, g, b)


def _route(x, wr_hi_ref, wr_lo_ref, br_ref, mi_ref, mw_ref, cnt_ref, carry_ref):
    t = x.shape[0]

    @pl.when(pl.program_id(0) == 0)
    def _():
        carry_ref[...] = jnp.zeros_like(carry_ref)

    x_hi = x.astype(BF16)
    x_lo = (x - x_hi.astype(F32)).astype(BF16)
    logits = (jnp.dot(x_hi, wr_hi_ref[...], preferred_element_type=F32)
              + jnp.dot(x_lo, wr_hi_ref[...], preferred_element_type=F32)
              + jnp.dot(x_hi, wr_lo_ref[...], preferred_element_type=F32)) + br_ref[...]
    lane = lax.broadcasted_iota(jnp.int32, (t, LANES), 1)
    logits = jnp.where(lane < N_EXPERTS, logits, -jnp.inf)
    lane_f = lane.astype(F32)
    v0 = jnp.max(logits, axis=-1, keepdims=True)
    e0 = jnp.min(jnp.where(logits == v0, lane_f, float(LANES)), axis=-1, keepdims=True)
    rest = jnp.where(lane_f == e0, -jnp.inf, logits)
    v1 = jnp.max(rest, axis=-1, keepdims=True)
    e1 = jnp.min(jnp.where(rest == v1, lane_f, float(LANES)), axis=-1, keepdims=True)
    ex = jnp.exp(v1 - v0)
    w0 = 1.0 / (1.0 + ex)
    w1 = ex / (1.0 + ex)
    oh0 = (lane_f == e0).astype(F32)
    oh1 = (lane_f == e1).astype(F32)
    r_i = lax.broadcasted_iota(jnp.int32, (t, t), 0)
    c_i = lax.broadcasted_iota(jnp.int32, (t, t), 1)
    tri = (c_i < r_i).astype(BF16)
    pre0 = jnp.dot(tri, oh0.astype(BF16), preferred_element_type=F32)
    pre1 = jnp.dot(tri, oh1.astype(BF16), preferred_element_type=F32)
    cnt0 = jnp.sum(oh0, axis=0, keepdims=True)
    cnt1 = jnp.sum(oh1, axis=0, keepdims=True)
    carry = carry_ref[...]
    rank0 = jnp.sum(oh0 * (carry + pre0), axis=-1, keepdims=True)
    rank1 = jnp.sum(oh1 * (carry + cnt0 + pre1), axis=-1, keepdims=True)
    carry = carry + cnt0 + cnt1
    carry_ref[...] = carry
    cnt_ref[...] = carry.astype(jnp.int32)
    mi = jnp.where(lane == 0, e0, jnp.where(lane == 1, e1, 0.0))
    mi = jnp.where(lane == 2, rank0, jnp.where(lane == 3, rank1, mi))
    mi_ref[...] = mi.astype(jnp.int32)
    mw_ref[...] = jnp.where(lane == 0, w0, jnp.where(lane == 1, w1, 0.0))


def _router_operands(w_r, b_r):
    wr = jnp.zeros((D_MODEL, LANES), F32).at[:, :N_EXPERTS].set(w_r)
    br = jnp.zeros((1, LANES), F32).at[:, :N_EXPERTS].set(b_r[None, :])
    wr_hi = wr.astype(BF16)
    wr_lo = (wr - wr_hi.astype(F32)).astype(BF16)
    return wr_hi, wr_lo, br


def _dispatch_body(dest_ref, last_ref, nused_ref, x_ref, xs_ref, zero_ref, sem_ref):
    t = dest_ref.shape[0] // 2
    tm = zero_ref.shape[0]
    n_tiles = xs_ref.shape[0] // tm
    i = pl.program_id(0)

    @pl.when(i == 0)
    def _():
        zero_ref[...] = jnp.zeros_like(zero_ref)

        def fill(tile):
            cp = pltpu.make_async_copy(zero_ref, xs_ref.at[pl.ds(pl.multiple_of(tile * tm, tm), tm), :], sem_ref.at[1])
            cp.start()
            cp.wait()

        for e in range(N_EXPERTS):
            @pl.when(last_ref[e] >= 0)
            def _():
                fill(last_ref[e])

        def unused(tile, carry):
            fill(tile)
            return carry

        lax.fori_loop(nused_ref[0], n_tiles, unused, 0)

    def issue(r, carry):
        for slot in range(2):
            d = dest_ref[2 * r + slot]
            pltpu.make_async_copy(x_ref.at[pl.ds(r, 1), :], xs_ref.at[pl.ds(d, 1), :],
                                  sem_ref.at[0]).start(priority=slot)
        return carry

    lax.fori_loop(0, t, issue, 0, unroll=ISSUE_UNROLL)
    pltpu.make_async_copy(xs_ref.at[pl.ds(0, 2 * t), :], xs_ref.at[pl.ds(0, 2 * t), :], sem_ref.at[0]).wait()


def _dispatch(x, dest, last_tile, n_used, n_tiles, t):
    n = x.shape[0]
    tm = EXPERT_TILE
    return pl.pallas_call(
        _dispatch_body,
        out_shape=jax.ShapeDtypeStruct((n_tiles * tm, D_MODEL), F32),
        grid=(n // t,),
        in_specs=[
            pl.BlockSpec((2 * t,), lambda i: (i,), memory_space=pltpu.SMEM),
            pl.BlockSpec(memory_space=pltpu.SMEM),
            pl.BlockSpec(memory_space=pltpu.SMEM),
            pl.BlockSpec((t, D_MODEL), lambda i: (i, 0)),
        ],
        out_specs=pl.BlockSpec(memory_space=pl.ANY),
        scratch_shapes=[pltpu.VMEM((tm, D_MODEL), F32), pltpu.SemaphoreType.DMA((2,))],
        compiler_params=_params(("arbitrary",)),
        name="dispatch",
    )(dest, last_tile, n_used, x)


def _experts_body(te_ref, first_ref, next_ref, slot_ref, nused_ref, x_ref, w1_hbm, w3_hbm, w2_hbm, o_ref,
                  wb1_ref, wb3_ref, wb2_ref, st1_ref, st3_ref, st2_ref, sem_ref):
    i = pl.program_id(0)

    def piece_copies(e, c):
        buf = c % 2
        cols = pl.ds(c * FF_CHUNK, FF_CHUNK)
        return (pltpu.make_async_copy(w1_hbm.at[e, :, cols], st1_ref.at[buf], sem_ref.at[buf, 0]),
                pltpu.make_async_copy(w3_hbm.at[e, :, cols], st3_ref.at[buf], sem_ref.at[buf, 1]),
                pltpu.make_async_copy(w2_hbm.at[e, cols, :], st2_ref.at[buf], sem_ref.at[buf, 2]))

    def start_piece(e, c):
        for cp in piece_copies(e, c):
            cp.start()

    def finish_piece(e, c, slot):
        for cp in piece_copies(e, c):
            cp.wait()
        cols = slice(c * FF_CHUNK, (c + 1) * FF_CHUNK)
        wb1_ref[slot, :, cols] = st1_ref[c % 2].astype(BF16)
        wb3_ref[slot, :, cols] = st3_ref[c % 2].astype(BF16)
        wb2_ref[slot, cols, :] = st2_ref[c % 2].astype(BF16)

    def ff_piece(xb, slot, c):
        cols = slice(c * FF_CHUNK, (c + 1) * FF_CHUNK)
        a = jnp.dot(xb, wb1_ref[slot, :, cols], preferred_element_type=F32)
        g = jnp.dot(xb, wb3_ref[slot, :, cols], preferred_element_type=F32)
        h = (jax.nn.silu(a) * g).astype(BF16)
        return jnp.dot(h, wb2_ref[slot, cols, :], preferred_element_type=F32)

    @pl.when(i == 0)
    def _():
        e, slot = te_ref[0], slot_ref[0]
        start_piece(e, 0)
        for c in range(FF_PIECES):
            if c + 1 < FF_PIECES:
                start_piece(e, c + 1)
            finish_piece(e, c, slot)

    live = i < nused_ref[0]
    prefetch = jnp.logical_and(live, jnp.logical_and(first_ref[i] == 1, next_ref[i] >= 0))

    @pl.when(jnp.logical_and(live, jnp.logical_not(prefetch)))
    def _():
        xb = x_ref[...].astype(BF16)
        slot = slot_ref[i]
        f = ff_piece(xb, slot, 0)
        for c in range(1, FF_PIECES):
            f = f + ff_piece(xb, slot, c)
        o_ref[...] = f

    @pl.when(prefetch)
    def _():
        xb = x_ref[...].astype(BF16)
        slot, e_next = slot_ref[i], next_ref[i]
        start_piece(e_next, 0)
        f = None
        for c in range(FF_PIECES):
            if c + 1 < FF_PIECES:
                start_piece(e_next, c + 1)
            part = ff_piece(xb, slot, c)
            f = part if f is None else f + part
            finish_piece(e_next, c, 1 - slot)
        o_ref[...] = f

    @pl.when(jnp.logical_not(live))
    def _():
        o_ref[...] = jnp.zeros_like(o_ref)


def _experts(xs, tile_expert, first, nxt, slot, n_used, w1, w3, w2):
    tm = EXPERT_TILE
    n_tiles = xs.shape[0] // tm
    hbm = pl.BlockSpec(memory_space=pl.ANY)
    return pl.pallas_call(
        _experts_body,
        out_shape=jax.ShapeDtypeStruct(xs.shape, F32),
        grid_spec=pltpu.PrefetchScalarGridSpec(
            num_scalar_prefetch=5,
            grid=(n_tiles,),
            in_specs=[pl.BlockSpec((tm, D_MODEL), lambda i, te, fi, nx, sl, nu: (jnp.minimum(i, nu[0] - 1), 0)),
                      hbm, hbm, hbm],
            out_specs=pl.BlockSpec((tm, D_MODEL), lambda i, te, fi, nx, sl, nu: (i, 0)),
            scratch_shapes=[
                pltpu.VMEM((2, D_MODEL, D_FF), BF16), pltpu.VMEM((2, D_MODEL, D_FF), BF16),
                pltpu.VMEM((2, D_FF, D_MODEL), BF16),
                pltpu.VMEM((2, D_MODEL, FF_CHUNK), F32), pltpu.VMEM((2, D_MODEL, FF_CHUNK), F32),
                pltpu.VMEM((2, FF_CHUNK, D_MODEL), F32),
                pltpu.SemaphoreType.DMA((2, 3)),
            ],
        ),
        compiler_params=_params(("arbitrary",), EXPERTS_VMEM_LIMIT),
        name="experts",
    )(tile_expert, first, nxt, slot, n_used, xs, w1, w3, w2)


def _combine_body(n_first, dest_ref, next_ref, x_ref, mw_ref, g_ref, b_ref, ys_ref, *rest):
    out_refs, (y0_ref, y1_ref, sem_ref) = rest[:-3], rest[-3:]
    t = x_ref.shape[0]
    i = pl.program_id(0)
    cur = i % 2

    def gather(idx_ref, buf):
        def issue(r, carry):
            pltpu.make_async_copy(ys_ref.at[pl.ds(idx_ref[2 * r], 1), :], y0_ref.at[buf, pl.ds(r, 1), :],
                                  sem_ref.at[buf]).start(priority=0)
            pltpu.make_async_copy(ys_ref.at[pl.ds(idx_ref[2 * r + 1], 1), :], y1_ref.at[buf, pl.ds(r, 1), :],
                                  sem_ref.at[buf]).start(priority=1)
            return carry

        lax.fori_loop(0, t, issue, 0, unroll=ISSUE_UNROLL)

    @pl.when(i == 0)
    def _():
        gather(dest_ref, 0)

    @pl.when(i + 1 < pl.num_programs(0))
    def _():
        gather(next_ref, 1 - cur)

    pltpu.make_async_copy(ys_ref.at[pl.ds(0, t), :], y0_ref.at[cur], sem_ref.at[cur]).wait()
    pltpu.make_async_copy(ys_ref.at[pl.ds(0, t), :], y1_ref.at[cur], sem_ref.at[cur]).wait()
    mw = mw_ref[...]
    f = mw[:, 0:1] * y0_ref[cur] + mw[:, 1:2] * y1_ref[cur]
    out = _layer_norm(DN_ALPHA * x_ref[...] + f, g_ref[...], b_ref[...])
    if len(out_refs) == 1:
        out_refs[0][...] = out
    else:
        @pl.when(i < n_first)
        def _():
            out_refs[0][...] = out

        @pl.when(i == n_first)
        def _():
            out_refs[1][...] = out


def _combine(x, ys, dest, mw, g, b, t, split):
    n = x.shape[0]
    zero = lambda i: (0, 0)
    n_first = n // t - 1
    last = n // t - 1
    if split:
        out_shape = (jax.ShapeDtypeStruct((n - t, D_MODEL), F32), jax.ShapeDtypeStruct((t, D_MODEL), F32))
        out_specs = (pl.BlockSpec((t, D_MODEL), lambda i: (jnp.minimum(i, n_first - 1), 0)),
                     pl.BlockSpec((t, D_MODEL), zero))
    else:
        out_shape = jax.ShapeDtypeStruct((n, D_MODEL), F32)
        out_specs = pl.BlockSpec((t, D_MODEL), lambda i: (i, 0))
    return pl.pallas_call(
        functools.partial(_combine_body, n_first),
        out_shape=out_shape,
        grid=(n // t,),
        in_specs=[
            pl.BlockSpec((2 * t,), lambda i: (i,), memory_space=pltpu.SMEM),
            pl.BlockSpec((2 * t,), lambda i: (jnp.minimum(i + 1, last),), memory_space=pltpu.SMEM),
            pl.BlockSpec((t, D_MODEL), lambda i: (i, 0)),
            pl.BlockSpec((t, LANES), lambda i: (i, 0)),
            pl.BlockSpec((1, D_MODEL), zero),
            pl.BlockSpec((1, D_MODEL), zero),
            pl.BlockSpec(memory_space=pl.ANY),
        ],
        out_specs=out_specs,
        scratch_shapes=[pltpu.VMEM((2, t, D_MODEL), F32), pltpu.VMEM((2, t, D_MODEL), F32),
                        pltpu.SemaphoreType.DMA((2,))],
        compiler_params=_params(("arbitrary",)),
        name="combine",
    )(dest, dest, x, mw, g, b, ys)


def _moe(x, mi, mw, cnt, w1, w3, w2, g, b, t, split):
    n = x.shape[0]
    tm = EXPERT_TILE
    n_tiles = (2 * n + N_EXPERTS * (tm - 1)) // tm
    counts = cnt[0, :N_EXPERTS]
    tiles = (counts + tm - 1) // tm
    tile_end = jnp.cumsum(tiles)
    start = (tile_end - tiles) * tm
    n_used = tile_end[-1:].astype(jnp.int32)
    tile_expert = jnp.sum((tile_end[None, :] <= jnp.arange(n_tiles)[:, None]).astype(jnp.int32), axis=1)
    tile_expert = jnp.minimum(tile_expert, N_EXPERTS - 1)
    last_tile = jnp.where(tiles > 0, tile_end - 1, -1).astype(jnp.int32)
    has = tiles > 0
    next_of = [jnp.int32(-1)] * N_EXPERTS
    for e in range(N_EXPERTS - 2, -1, -1):
        next_of[e] = jnp.where(has[e + 1], e + 1, next_of[e + 1])
    per_expert = jnp.stack([tile_end - tiles, jnp.stack(next_of), (jnp.cumsum(has) - 1) % 2]).astype(jnp.int32)
    onehot = (tile_expert[None, :, None] == jnp.arange(N_EXPERTS)[None, None, :]).astype(jnp.int32)
    first_tile, nxt, slot = jnp.sum(onehot * per_expert[:, None, :], axis=2)
    first = (first_tile == jnp.arange(n_tiles)).astype(jnp.int32)
    experts = mi[:, 0:2]
    group_start = sum(jnp.where(experts == e, start[e], 0) for e in range(N_EXPERTS))
    dest = (group_start + mi[:, 2:4]).astype(jnp.int32).reshape(2 * n)
    xs = _dispatch(x, dest, last_tile, n_used, n_tiles, t)
    ys = _experts(xs, tile_expert.astype(jnp.int32), first, nxt, slot, n_used, w1, w3, w2)
    return _combine(x, ys, dest, mw, g, b, t, split)


def _block_diag(w):
    g, c, _ = w.shape
    out = jnp.zeros((g * c, g * c), w.dtype)
    for i in range(g):
        out = out.at[i * c:(i + 1) * c, i * c:(i + 1) * c].set(w[i])
    return out


def _bias_pairs(table, t_q, n_keys, offset):
    hi = t_q - 1 + offset
    span = t_q - 1 + n_keys
    cols = np.clip(hi - np.arange(span), -REL_CLIP, REL_CLIP) + REL_CLIP
    rev = table[:, cols].astype(F32)
    skew = jnp.tile(rev, (1, t_q + 1))[:, :t_q * (span + 1)].reshape(N_HEADS, t_q, span + 1)
    slab = skew[:, ::-1, :n_keys]
    return slab.reshape(HEAD_PAIRS, 2 * t_q, n_keys)


def _prompt_bias_slabs(table):
    bias = _bias_pairs(table, CHUNK, BAND, ATTN_REACH)
    col = np.arange(BAND)[None, :]
    first_valid = np.concatenate([[0], (LEFT_CHUNKS - np.arange(LEFT_CHUNKS)) * CHUNK])[:, None]
    valid = jnp.asarray(col >= first_valid)
    return jnp.where(valid[:, None, None, :], bias[None], NEG_INF)


def kernel(x_prompt, x_sample, cache_k, cache_v, state_pool, w_in, b_in, w_pool_grp, pool_scale,
           rel_table, w_pool_br, w_attn_br, w_out, ln1_g, ln1_b, ln2_g, ln2_b,
           w1_dense, w3_dense, w2_dense, w_router, b_router, w1_exp, w3_exp, w2_exp):
    bp, tp, d = x_prompt.shape
    bs, ts, _ = x_sample.shape
    n_p, n_s = bp * tp, bs * ts
    n = n_p + n_s
    depth = w_in.shape[0]
    keep_s = cache_k.shape[2]
    keep_p = min(ATTN_REACH, tp)

    x = (x_prompt.reshape(n_p, d), x_sample.reshape(n_s, d))
    ck = cache_k.reshape(depth, bs, keep_s, ATTN_WIDTH)
    cv = cache_v.reshape(depth, bs, keep_s, ATTN_WIDTH)
    hist = jnp.pad(state_pool, ((0, 0), (0, 0), (HIST_ROWS - POOL_HIST, 0), (0, 0)))
    row = lambda v: v[None, :].astype(F32)

    k_buf = v_buf = None
    kp_new, vp_new, pp_new, ps_new = [], [], [], []
    for l in range(depth):
        wgrp = _block_diag(w_pool_grp[l]).astype(BF16)
        weights = (wgrp, row(pool_scale[l]), w_pool_br[l].astype(BF16), w_attn_br[l].astype(BF16),
                   w_out[l].astype(BF16), row(ln1_g[l]), row(ln1_b[l]))
        z32, z16 = _in_proj(x, _reorder_in_columns(w_in[l]).astype(BF16), row(_reorder_in_columns(b_in[l])), n_s)
        attn_p, k_tail, v_tail = _attn_prompt(z32, z16, _prompt_bias_slabs(rel_table[l]), bp, tp)
        attn_s, k_buf, v_buf = _attn_sample(z32, z16, ck, cv, _bias_pairs(rel_table[l], ts, keep_s + ts, keep_s),
                                            l, n_p, bs, ts, k_buf, v_buf)
        j = l // 2
        last = l == depth - 1
        if l % 2 == 0:
            x1 = _mix(z32, z16, attn_p, attn_s, x, hist[l], weights, n_p, tp)
            x = _ffn_dense(x1, w1_dense[j].astype(BF16), w3_dense[j].astype(BF16), w2_dense[j].astype(BF16),
                           row(ln2_g[l]), row(ln2_b[l]))
            if last:
                x = (x[:n_p], x[n_p:])
        else:
            x1, mi, mw, cnt = _mix(z32, z16, attn_p, attn_s, x, hist[l], weights, n_p, tp,
                                   router=_router_operands(w_router[j], b_router[j]))
            x = _moe(x1, mi, mw, cnt, w1_exp[j], w3_exp[j], w2_exp[j], row(ln2_g[l]), row(ln2_b[l]),
                     n_s, split=last)

        def tail(rows, col0, width, z32=z32):
            return jnp.stack([lax.slice(z32, ((b + 1) * tp - rows, col0), ((b + 1) * tp, col0 + width))
                              for b in range(bp)])

        kp_new.append(k_tail.reshape(bp, keep_p, N_HEADS, HEAD_DIM))
        vp_new.append(v_tail.reshape(bp, keep_p, N_HEADS, HEAD_DIM))
        pp_new.append(tail(POOL_HIST, COL_U * POOL_WIDTH, POOL_WIDTH))
        us = z32[n_p:, :POOL_WIDTH].reshape(bs, ts, POOL_WIDTH)
        ps_new.append(jnp.concatenate([state_pool[l], us], axis=1)[:, -POOL_HIST:])

    shape_s = (depth, bs, keep_s, N_HEADS, HEAD_DIM)
    return (x[0].reshape(bp, tp, d), x[1].reshape(bs, ts, d),
            jnp.stack(kp_new), jnp.stack(vp_new), jnp.stack(pp_new),
            k_buf.reshape(shape_s), v_buf.reshape(shape_s), jnp.stack(ps_new))
```

```python
import functools

import jax
import jax.numpy as jnp
import numpy as np
from jax import lax
from jax.experimental import pallas as pl
from jax.experimental.pallas import tpu as pltpu

F32 = jnp.float32
BF16 = jnp.bfloat16

D_MODEL = 1024
N_HEADS = 8
HEAD_DIM = 64
ATTN_WIDTH = N_HEADS * HEAD_DIM
CHUNK = 64
LEFT_CHUNKS = 8
BAND = (LEFT_CHUNKS + 1) * CHUNK
ATTN_REACH = LEFT_CHUNKS * CHUNK
REL_CLIP = 256
ATTN_SCALE = HEAD_DIM ** -0.5
POOL_WIDTH = 512
POOL_WINDOWS = (2, 4, 8, 16)
POOL_GROUP = POOL_WIDTH // len(POOL_WINDOWS)
POOL_HIST = max(POOL_WINDOWS) - 1
HIST_ROWS = POOL_HIST + 1
LEAD_ROWS = 8
D_FF = 2816
N_EXPERTS = 8
PAST_LEN = 4096
DEPTH = 2
DN_ALPHA = (2 * DEPTH) ** 0.25
LN_EPS = 1e-5
NEG_INF = -1e30
IN_WIDTH = POOL_WIDTH + 3 * ATTN_WIDTH + 2 * D_MODEL
HEAD_PAIRS = N_HEADS // 2
PAIR_W = 2 * HEAD_DIM
LANES = 128
ROUTE_ROWS = 16

Z32_WIDTH = POOL_WIDTH + 2 * ATTN_WIDTH
Z16_WIDTH = 2 * D_MODEL + ATTN_WIDTH
COL_U, COL_K, COL_V = 0, 1, 2
COL_GP, COL_GA = 0, 1
COL_Q = 2 * D_MODEL // ATTN_WIDTH

EXPERT_TILE = 512
FF_CHUNK = 256
FF_PIECES = D_FF // FF_CHUNK
ISSUE_UNROLL = 8
VMEM_LIMIT = 56 * 1024 * 1024
EXPERTS_VMEM_LIMIT = 62 * 1024 * 1024


def _pick(n, candidates):
    for c in candidates:
        if n % c == 0:
            return c
    raise ValueError(f"no tile in {candidates} divides {n}")


def _params(sem, vmem=None):
    return pltpu.CompilerParams(dimension_semantics=sem, vmem_limit_bytes=vmem or VMEM_LIMIT)


def _layer_norm(r, g, b):
    mu = jnp.mean(r, axis=-1, keepdims=True)
    c = r - mu
    var = jnp.mean(c * c, axis=-1, keepdims=True)
    return c * lax.rsqrt(var + LN_EPS) * g + b


def _row_sources(x, t):
    if not isinstance(x, tuple):
        return [x], [pl.BlockSpec((t, x.shape[1]), lambda i: (i, 0))], (lambda refs, i: refs[0][...])
    first, last = x
    assert last.shape[0] == t and first.shape[0] % t == 0
    n_first = first.shape[0] // t
    specs = [pl.BlockSpec((t, first.shape[1]), lambda i: (jnp.minimum(i, n_first - 1), 0)),
             pl.BlockSpec((t, last.shape[1]), lambda i: (0, 0))]
    return [first, last], specs, (lambda refs, i: jnp.where(i < n_first, refs[0][...], refs[1][...]))


def _inproj_body(n_src, select, *refs):
    w_ref, b_ref, z32_ref, z16_ref = refs[n_src:]
    x = select(refs[:n_src], pl.program_id(0)).astype(BF16)
    z = jnp.dot(x, w_ref[...], preferred_element_type=F32) + b_ref[...]
    z32_ref[...] = z[:, :Z32_WIDTH]
    z16_ref[...] = z[:, Z32_WIDTH:].astype(BF16)


def _in_proj(x, w_bf, b, t):
    srcs, specs, select = _row_sources(x, t)
    n = sum(s.shape[0] for s in srcs)
    zero = lambda i: (0, 0)
    return pl.pallas_call(
        functools.partial(_inproj_body, len(srcs), select),
        out_shape=(jax.ShapeDtypeStruct((n, Z32_WIDTH), F32), jax.ShapeDtypeStruct((n, Z16_WIDTH), BF16)),
        grid=(n // t,),
        in_specs=specs + [pl.BlockSpec((D_MODEL, IN_WIDTH), zero), pl.BlockSpec((1, IN_WIDTH), zero)],
        out_specs=(pl.BlockSpec((t, Z32_WIDTH), lambda i: (i, 0)), pl.BlockSpec((t, Z16_WIDTH), lambda i: (i, 0))),
        compiler_params=_params(("arbitrary",)),
        name="in_proj",
    )(*srcs, w_bf, b)


def _reorder_in_columns(w):
    o = [0, POOL_WIDTH, POOL_WIDTH + ATTN_WIDTH, POOL_WIDTH + 2 * ATTN_WIDTH, POOL_WIDTH + 3 * ATTN_WIDTH, IN_WIDTH]
    u, q, k, v, g = (w[..., o[i]:o[i + 1]] for i in range(5))
    return jnp.concatenate([u, k, v, g, q], axis=-1)


def _pair_scores(q_pair, k_pair, bias):
    lane = lax.broadcasted_iota(jnp.int32, q_pair.shape, 1)
    qs = q_pair.astype(F32) * ATTN_SCALE
    q2 = jnp.concatenate([jnp.where(lane < HEAD_DIM, qs, 0.0), jnp.where(lane >= HEAD_DIM, qs, 0.0)], axis=0)
    s = lax.dot_general(q2.astype(BF16), k_pair, (((1,), (1,)), ((), ())), preferred_element_type=F32)
    return s + bias


def _pair_output(s, v_pair):
    rows = s.shape[0] // 2
    m = jnp.max(s, axis=-1, keepdims=True)
    e = jnp.exp(s - m)
    l = jnp.sum(e, axis=-1, keepdims=True)
    o2 = jnp.dot(e.astype(BF16), v_pair, preferred_element_type=F32) / l
    lane = lax.broadcasted_iota(jnp.int32, (rows, PAIR_W), 1)
    return jnp.where(lane < HEAD_DIM, o2[:rows], o2[rows:])


def _attn_prompt_body(q_ref, kp_ref, kc_ref, vp_ref, vc_ref, bias_ref, o_ref, kt_ref, vt_ref, kext_ref, vext_ref):
    blk = q_ref.shape[0]
    j = pl.program_id(1)

    @pl.when(j == pl.num_programs(1) - 1)
    def _():
        kt_ref[0] = kc_ref[...]
        vt_ref[0] = vc_ref[...]

    kext_ref[0:blk, :] = kp_ref[...].astype(BF16)
    kext_ref[blk:2 * blk, :] = kc_ref[...].astype(BF16)
    vext_ref[0:blk, :] = vp_ref[...].astype(BF16)
    vext_ref[blk:2 * blk, :] = vc_ref[...].astype(BF16)
    def chunk(c, carry):
        q0 = pl.multiple_of(c * CHUNK, CHUNK)
        slab = jnp.where(j == 0, c + 1, 0)
        pairs = [slice(hp * PAIR_W, (hp + 1) * PAIR_W) for hp in range(HEAD_PAIRS)]
        s = jnp.concatenate([_pair_scores(q_ref[pl.ds(q0, CHUNK), lanes], kext_ref[pl.ds(q0, BAND), lanes],
                                          bias_ref[slab, hp]) for hp, lanes in enumerate(pairs)], axis=0)
        m = jnp.max(s, axis=-1, keepdims=True)
        e = jnp.exp(s - m)
        inv = 1.0 / jnp.sum(e, axis=-1, keepdims=True)
        p = e.astype(BF16)
        lane = lax.broadcasted_iota(jnp.int32, (CHUNK, PAIR_W), 1)
        for hp, lanes in enumerate(pairs):
            rows = slice(hp * 2 * CHUNK, (hp + 1) * 2 * CHUNK)
            o2 = jnp.dot(p[rows], vext_ref[pl.ds(q0, BAND), lanes], preferred_element_type=F32) * inv[rows]
            o_ref[pl.ds(q0, CHUNK), lanes] = jnp.where(lane < HEAD_DIM, o2[:CHUNK], o2[CHUNK:])
        return carry

    lax.fori_loop(0, blk // CHUNK, chunk, 0, unroll=4)


def _attn_prompt(z32, z16, bias_pairs, batch, seq):
    blk = ATTN_REACH
    per_seq = seq // blk
    rows = batch * seq

    def cur(col):
        return lambda b, j: (b * per_seq + j, col)

    def prev(col):
        return lambda b, j: (b * per_seq + jnp.maximum(j - 1, 0), col)

    tail = jax.ShapeDtypeStruct((batch, blk, ATTN_WIDTH), F32)
    tail_spec = pl.BlockSpec((1, blk, ATTN_WIDTH), lambda b, j: (b, 0, 0))
    return pl.pallas_call(
        _attn_prompt_body,
        out_shape=(jax.ShapeDtypeStruct((rows, ATTN_WIDTH), F32), tail, tail),
        grid=(batch, per_seq),
        in_specs=[
            pl.BlockSpec((blk, ATTN_WIDTH), cur(COL_Q)),
            pl.BlockSpec((blk, ATTN_WIDTH), prev(COL_K)),
            pl.BlockSpec((blk, ATTN_WIDTH), cur(COL_K)),
            pl.BlockSpec((blk, ATTN_WIDTH), prev(COL_V)),
            pl.BlockSpec((blk, ATTN_WIDTH), cur(COL_V)),
            pl.BlockSpec((1 + LEFT_CHUNKS, HEAD_PAIRS, 2 * CHUNK, BAND), lambda b, j: (0, 0, 0, 0)),
        ],
        out_specs=(pl.BlockSpec((blk, ATTN_WIDTH), lambda b, j: (b * per_seq + j, 0)), tail_spec, tail_spec),
        scratch_shapes=[pltpu.VMEM((2 * blk, ATTN_WIDTH), BF16), pltpu.VMEM((2 * blk, ATTN_WIDTH), BF16)],
        compiler_params=_params(("arbitrary", "arbitrary")),
        name="attn_prompt",
    )(z16, z32, z32, z32, z32, bias_pairs)


def _attn_sample_body(q_ref, kn_ref, vn_ref, ck_ref, cv_ref, bias_ref, *rest):
    o_ref, ko_ref, vo_ref, kall_ref, vall_ref = rest[-5:]
    keep = ck_ref.shape[2]
    t = q_ref.shape[0]
    ck = ck_ref[0, 0]
    cv = cv_ref[0, 0]
    kn = kn_ref[...]
    vn = vn_ref[...]
    kall_ref[0:keep, :] = ck.astype(BF16)
    kall_ref[keep:keep + t, :] = kn.astype(BF16)
    vall_ref[0:keep, :] = cv.astype(BF16)
    vall_ref[keep:keep + t, :] = vn.astype(BF16)
    for hp in range(HEAD_PAIRS):
        lanes = slice(hp * PAIR_W, (hp + 1) * PAIR_W)
        s = _pair_scores(q_ref[:, lanes], kall_ref[:, lanes], bias_ref[hp])
        o_ref[:, lanes] = _pair_output(s, vall_ref[:, lanes])
    ko_ref[0, 0, 0:keep - t, :] = ck[t:keep]
    ko_ref[0, 0, keep - t:keep, :] = kn
    vo_ref[0, 0, 0:keep - t, :] = cv[t:keep]
    vo_ref[0, 0, keep - t:keep, :] = vn
    for later in range(1, ko_ref.shape[0]):
        ko_ref[later] = jnp.zeros(ko_ref.shape[1:], F32)
        vo_ref[later] = jnp.zeros(vo_ref.shape[1:], F32)


def _attn_sample(z32, z16, cache_k, cache_v, bias_pairs, layer, row0, streams, t, k_buf, v_buf):
    depth, _, keep, _ = cache_k.shape
    blk0 = row0 // t
    ins = [z16, z32, z32, cache_k, cache_v, bias_pairs]
    in_specs = [
        pl.BlockSpec((t, ATTN_WIDTH), lambda s: (blk0 + s, COL_Q)),
        pl.BlockSpec((t, ATTN_WIDTH), lambda s: (blk0 + s, COL_K)),
        pl.BlockSpec((t, ATTN_WIDTH), lambda s: (blk0 + s, COL_V)),
        pl.BlockSpec((1, 1, keep, ATTN_WIDTH), lambda s: (layer, s, 0, 0)),
        pl.BlockSpec((1, 1, keep, ATTN_WIDTH), lambda s: (layer, s, 0, 0)),
        pl.BlockSpec((HEAD_PAIRS, 2 * t, keep + t), lambda s: (0, 0, 0)),
    ]
    aliases = {}
    if k_buf is None:
        assert layer == 0
        buf_spec = pl.BlockSpec((depth, 1, keep, ATTN_WIDTH), lambda s: (0, s, 0, 0))
    else:
        aliases = {len(ins): 1, len(ins) + 1: 2}
        ins += [k_buf, v_buf]
        in_specs += [pl.BlockSpec(memory_space=pl.ANY), pl.BlockSpec(memory_space=pl.ANY)]
        buf_spec = pl.BlockSpec((1, 1, keep, ATTN_WIDTH), lambda s: (layer, s, 0, 0))
    buf = jax.ShapeDtypeStruct(cache_k.shape, F32)
    return pl.pallas_call(
        _attn_sample_body,
        out_shape=(jax.ShapeDtypeStruct((streams * t, ATTN_WIDTH), F32), buf, buf),
        grid=(streams,),
        in_specs=in_specs,
        out_specs=(
            pl.BlockSpec((t, ATTN_WIDTH), lambda s: (s, 0)),
            buf_spec,
            buf_spec,
        ),
        scratch_shapes=[pltpu.VMEM((keep + t, ATTN_WIDTH), BF16), pltpu.VMEM((keep + t, ATTN_WIDTH), BF16)],
        input_output_aliases=aliases,
        compiler_params=_params(("arbitrary",)),
        name="attn_sample",
    )(*ins)


def _merge(pooled, u, attn, gp, ga, x, wgrp_ref, scale_ref, wp_ref, wa_ref, wo_ref, g_ref, b_ref):
    pooled = pooled - u
    pool_y = jnp.dot(pooled.astype(BF16), wgrp_ref[...], preferred_element_type=F32) * scale_ref[...]
    mp = jnp.dot(pool_y.astype(BF16), wp_ref[...], preferred_element_type=F32)
    ma = jnp.dot(attn.astype(BF16), wa_ref[...], preferred_element_type=F32)
    m = jax.nn.sigmoid(gp) * mp + jax.nn.sigmoid(ga) * ma
    y = jnp.dot(m.astype(BF16), wo_ref[...], preferred_element_type=F32)
    return _layer_norm(DN_ALPHA * x + y, g_ref[...], b_ref[...])


def _window_means(read, pos, shape_out):
    outs = []
    for g, w in enumerate(POOL_WINDOWS):
        lanes = slice(g * POOL_GROUP, (g + 1) * POOL_GROUP)
        s = read(0, lanes)
        for back in range(1, w):
            s = s + read(back, lanes)
        outs.append((s / jnp.minimum(pos + 1, w).astype(F32)).reshape(shape_out))
    return jnp.concatenate(outs, axis=-1)


def _window_means_doubling(buf_ref, lvl_ref, pos, t):
    g = POOL_GROUP
    lo, hi = LEAD_ROWS, LEAD_ROWS + HIST_ROWS + t
    out0 = LEAD_ROWS + HIST_ROWS
    lvl_ref[0, lo:hi, :] = buf_ref[lo:hi, :] + buf_ref[lo - 1:hi - 1, :]
    lvl_ref[1, lo:hi, g:] = lvl_ref[0, lo:hi, g:] + lvl_ref[0, lo - 2:hi - 2, g:]
    lvl_ref[2, lo:hi, 2 * g:] = lvl_ref[1, lo:hi, 2 * g:] + lvl_ref[1, lo - 4:hi - 4, 2 * g:]
    s16 = lvl_ref[2, out0:out0 + t, 3 * g:] + lvl_ref[2, out0 - 8:out0 - 8 + t, 3 * g:]
    sums = [lvl_ref[k, out0:out0 + t, k * g:(k + 1) * g] for k in range(3)] + [s16]
    return jnp.concatenate([s / jnp.minimum(pos + 1, w).astype(F32) for s, w in zip(sums, POOL_WINDOWS)], axis=-1)


def _mix_body(n_prompt_tiles, tiles_per_seq, with_router, n_src, select,
              u_ref, up_ref, hist_ref, attn_p_ref, attn_s_ref, gp_ref, ga_ref, *rest):
    x_refs, rest = rest[:n_src], rest[n_src:]
    (wgrp_ref, scale_ref, wp_ref, wa_ref, wo_ref, g_ref, b_ref), rest = rest[:7], rest[7:]
    if with_router:
        (wr_hi_ref, wr_lo_ref, br_ref, o_ref, tail_ref, mi_ref, mw_ref, cnt_ref,
         buf_ref, lvl_ref, sbuf_ref, pooled_ref, attn_ref, carry_ref) = rest
    else:
        o_ref, tail_ref, buf_ref, lvl_ref, sbuf_ref, pooled_ref, attn_ref = rest
    t = u_ref.shape[0]
    i = pl.program_id(0)
    u = u_ref[...]

    @pl.when(i == 0)
    def _():
        buf_ref[0:LEAD_ROWS, :] = jnp.zeros((LEAD_ROWS, POOL_WIDTH), F32)
        lvl_ref[:, 0:LEAD_ROWS, :] = jnp.zeros((lvl_ref.shape[0], LEAD_ROWS, POOL_WIDTH), F32)

    @pl.when(i < n_prompt_tiles)
    def _():
        tile = i % tiles_per_seq
        h0 = LEAD_ROWS
        buf_ref[h0:h0 + HIST_ROWS, :] = jnp.where(tile == 0, 0.0, up_ref[...])
        buf_ref[h0 + HIST_ROWS:h0 + HIST_ROWS + t, :] = u
        pos = tile * t + lax.broadcasted_iota(jnp.int32, (t, 1), 0)
        pooled_ref[...] = _window_means_doubling(buf_ref, lvl_ref, pos, t)
        attn_ref[...] = attn_p_ref[...]

        @pl.when(tile == tiles_per_seq - 1)
        def _():
            tail_ref[0] = u[t - HIST_ROWS:, :]

    @pl.when(i == n_prompt_tiles)
    def _():
        streams, hrows, _ = hist_ref.shape
        ts = t // streams
        sbuf_ref[:, 0:hrows, :] = hist_ref[...]
        sbuf_ref[:, hrows:hrows + ts, :] = u.reshape(streams, ts, POOL_WIDTH)
        pos = PAST_LEN + lax.broadcasted_iota(jnp.int32, (1, ts, 1), 1)
        read = lambda back, lanes: sbuf_ref[:, hrows - back:hrows - back + ts, lanes]
        pooled_ref[...] = _window_means(read, pos, (t, POOL_GROUP))
        attn_ref[...] = attn_s_ref[...]

    x1 = _merge(pooled_ref[...], u, attn_ref[...], gp_ref[...].astype(F32), ga_ref[...].astype(F32),
                select(x_refs, i), wgrp_ref, scale_ref, wp_ref, wa_ref, wo_ref, g_ref, b_ref)
    o_ref[...] = x1
    if with_router:
        _route(x1, wr_hi_ref, wr_lo_ref, br_ref, mi_ref, mw_ref, cnt_ref, carry_ref)


def _mix(z32, z16, attn_p, attn_s, x, hist, weights, n_prompt, seq, router=None):
    n = z32.shape[0]
    t = n - n_prompt
    x_srcs, x_specs, select = _row_sources(x, t)
    n_prompt_tiles = n_prompt // t
    streams, hrows, _ = hist.shape
    hist_per_tile = t // HIST_ROWS
    last_p = n_prompt_tiles - 1
    zero = lambda i: (0, 0)
    rows = lambda width: pl.BlockSpec((t, width), lambda i: (i, 0))
    tiles_per_seq = seq // t
    out_shape = [jax.ShapeDtypeStruct((n, D_MODEL), F32),
                 jax.ShapeDtypeStruct((n_prompt // seq, HIST_ROWS, POOL_WIDTH), F32)]
    out_specs = [rows(D_MODEL),
                 pl.BlockSpec((1, HIST_ROWS, POOL_WIDTH), lambda i: (jnp.minimum(i, last_p) // tiles_per_seq, 0, 0))]
    router_specs, router_scratch = [], []
    if router is not None:
        router_specs = [pl.BlockSpec((ROUTE_ROWS, D_MODEL), zero), pl.BlockSpec((ROUTE_ROWS, D_MODEL), zero),
                        pl.BlockSpec((ROUTE_ROWS, 1), zero)]
        per_tile = lambda dtype: jax.ShapeDtypeStruct((n // t, 8, t), dtype)
        out_shape += [per_tile(jnp.int32), per_tile(F32), jax.ShapeDtypeStruct((ROUTE_ROWS, LANES), jnp.int32)]
        out_specs += [pl.BlockSpec((1, 8, t), lambda i: (i, 0, 0)), pl.BlockSpec((1, 8, t), lambda i: (i, 0, 0)),
                      pl.BlockSpec((ROUTE_ROWS, LANES), zero)]
        router_scratch = [pltpu.VMEM((ROUTE_ROWS, LANES), F32)]
    out = pl.pallas_call(
        functools.partial(_mix_body, n_prompt_tiles, seq // t, router is not None, len(x_srcs), select),
        out_shape=out_shape,
        grid=(n_prompt_tiles + 1,),
        in_specs=[
            pl.BlockSpec((t, POOL_WIDTH), lambda i: (i, COL_U)),
            pl.BlockSpec((HIST_ROWS, POOL_WIDTH), lambda i: (jnp.maximum(i * hist_per_tile - 1, 0), COL_U)),
            pl.BlockSpec((streams, hrows, POOL_WIDTH), lambda i: (0, 0, 0)),
            pl.BlockSpec((t, ATTN_WIDTH), lambda i: (jnp.minimum(i, last_p), 0)),
            pl.BlockSpec((t, ATTN_WIDTH), zero),
            pl.BlockSpec((t, D_MODEL), lambda i: (i, COL_GP)),
            pl.BlockSpec((t, D_MODEL), lambda i: (i, COL_GA)),
        ] + x_specs + [
            pl.BlockSpec((POOL_WIDTH, POOL_WIDTH), zero),
            pl.BlockSpec((1, POOL_WIDTH), zero),
            pl.BlockSpec((POOL_WIDTH, D_MODEL), zero),
            pl.BlockSpec((ATTN_WIDTH, D_MODEL), zero),
            pl.BlockSpec((D_MODEL, D_MODEL), zero),
            pl.BlockSpec((1, D_MODEL), zero),
            pl.BlockSpec((1, D_MODEL), zero),
        ] + router_specs,
        out_specs=out_specs,
        scratch_shapes=[
            pltpu.VMEM((LEAD_ROWS + HIST_ROWS + t, POOL_WIDTH), F32),
            pltpu.VMEM((3, LEAD_ROWS + HIST_ROWS + t, POOL_WIDTH), F32),
            pltpu.VMEM((streams, hrows + t // streams, POOL_WIDTH), F32),
            pltpu.VMEM((t, POOL_WIDTH), F32),
            pltpu.VMEM((t, ATTN_WIDTH), F32),
        ] + router_scratch,
        compiler_params=_params(("arbitrary",)),
        name="mix",
    )(z32, z32, hist, attn_p, attn_s, z16, z16, *x_srcs, *weights, *(router or ()))
    return out


def _ffn_dense_body(x_ref, w1_ref, w3_ref, w2_ref, g_ref, b_ref, o_ref):
    x = x_ref[...]
    xb = x.astype(BF16)
    a = jnp.dot(xb, w1_ref[...], preferred_element_type=F32)
    c = jnp.dot(xb, w3_ref[...], preferred_element_type=F32)
    h = (jax.nn.silu(a) * c).astype(BF16)
    f = jnp.dot(h, w2_ref[...], preferred_element_type=F32)
    o_ref[...] = _layer_norm(DN_ALPHA * x + f, g_ref[...], b_ref[...])


def _ffn_dense(x, w1, w3, w2, g, b):
    n = x.shape[0]
    t = 256
    zero = lambda i: (0, 0)
    return pl.pallas_call(
        _ffn_dense_body,
        out_shape=jax.ShapeDtypeStruct((n, D_MODEL), F32),
        grid=(n // t,),
        in_specs=[
            pl.BlockSpec((t, D_MODEL), lambda i: (i, 0)),
            pl.BlockSpec((D_MODEL, D_FF), zero),
            pl.BlockSpec((D_MODEL, D_FF), zero),
            pl.BlockSpec((D_FF, D_MODEL), zero),
            pl.BlockSpec((1, D_MODEL), zero),
            pl.BlockSpec((1, D_MODEL), zero),
        ],
        out_specs=pl.BlockSpec((t, D_MODEL), lambda i: (i, 0)),
        compiler_params=_params(("arbitrary",)),
        name="ffn_dense",
    )(x, w1, w3, w2, g, b)


def _route(x, wr_hi_ref, wr_lo_ref, br_ref, mi_ref, mw_ref, cnt_ref, carry_ref):
    t = x.shape[0]

    @pl.when(pl.program_id(0) == 0)
    def _():
        carry_ref[...] = jnp.zeros_like(carry_ref)

    x_hi = x.astype(BF16)
    x_lo = (x - x_hi.astype(F32)).astype(BF16)
    nt = (((1,), (1,)), ((), ()))
    logits = (lax.dot_general(wr_hi_ref[...], x_hi, nt, preferred_element_type=F32)
              + lax.dot_general(wr_hi_ref[...], x_lo, nt, preferred_element_type=F32)
              + lax.dot_general(wr_lo_ref[...], x_hi, nt, preferred_element_type=F32)) + br_ref[...]
    row = lax.broadcasted_iota(jnp.int32, (ROUTE_ROWS, t), 0)
    row_f = row.astype(F32)
    logits = jnp.where(row < N_EXPERTS, logits, -jnp.inf)
    v0 = jnp.max(logits, axis=0, keepdims=True)
    e0 = jnp.min(jnp.where(logits == v0, row_f, float(ROUTE_ROWS)), axis=0, keepdims=True)
    rest = jnp.where(row_f == e0, -jnp.inf, logits)
    v1 = jnp.max(rest, axis=0, keepdims=True)
    e1 = jnp.min(jnp.where(rest == v1, row_f, float(ROUTE_ROWS)), axis=0, keepdims=True)
    ex = jnp.exp(v1 - v0)
    w0 = 1.0 / (1.0 + ex)
    w1 = ex / (1.0 + ex)
    oh0 = (row_f == e0).astype(F32)
    oh1 = (row_f == e1).astype(F32)
    before = (lax.broadcasted_iota(jnp.int32, (t, t), 0) < lax.broadcasted_iota(jnp.int32, (t, t), 1)).astype(BF16)
    pre0 = jnp.dot(oh0.astype(BF16), before, preferred_element_type=F32)
    pre1 = jnp.dot(oh1.astype(BF16), before, preferred_element_type=F32)
    cnt0 = jnp.sum(oh0, axis=1, keepdims=True)
    cnt1 = jnp.sum(oh1, axis=1, keepdims=True)
    carry = carry_ref[:, 0:1]
    rank0 = jnp.sum(oh0 * (carry + pre0), axis=0, keepdims=True)
    rank1 = jnp.sum(oh1 * (carry + cnt0 + pre1), axis=0, keepdims=True)
    carry = jnp.broadcast_to(carry + cnt0 + cnt1, carry_ref.shape)
    carry_ref[...] = carry
    cnt_ref[...] = carry.astype(jnp.int32)
    r8 = lax.broadcasted_iota(jnp.int32, (8, t), 0)
    mi = jnp.where(r8 == 0, e0, jnp.where(r8 == 1, e1, jnp.where(r8 == 2, rank0, jnp.where(r8 == 3, rank1, 0.0))))
    mi_ref[0] = mi.astype(jnp.int32)
    mw_ref[0] = jnp.where(r8 == 0, w0, jnp.where(r8 == 1, w1, 0.0))


def _router_operands(w_r, b_r):
    wr = jnp.zeros((ROUTE_ROWS, D_MODEL), F32).at[:N_EXPERTS].set(w_r.T)
    br = jnp.zeros((ROUTE_ROWS, 1), F32).at[:N_EXPERTS, 0].set(b_r)
    wr_hi = wr.astype(BF16)
    wr_lo = (wr - wr_hi.astype(F32)).astype(BF16)
    return wr_hi, wr_lo, br


def _dispatch_body(dest_ref, last_ref, nused_ref, x_ref, xs_ref, zero_ref, sem_ref):
    t = dest_ref.shape[0] // 2
    tm = zero_ref.shape[0]
    n_tiles = xs_ref.shape[0] // tm
    i = pl.program_id(0)

    @pl.when(i == 0)
    def _():
        zero_ref[...] = jnp.zeros_like(zero_ref)

        def fill(tile):
            cp = pltpu.make_async_copy(zero_ref, xs_ref.at[pl.ds(pl.multiple_of(tile * tm, tm), tm), :], sem_ref.at[1])
            cp.start()
            cp.wait()

        for e in range(N_EXPERTS):
            @pl.when(last_ref[e] >= 0)
            def _():
                fill(last_ref[e])

        def unused(tile, carry):
            fill(tile)
            return carry

        lax.fori_loop(nused_ref[0], n_tiles, unused, 0)

    def issue(r, carry):
        for slot in range(2):
            d = dest_ref[slot * t + r]
            pltpu.make_async_copy(x_ref.at[pl.ds(r, 1), :], xs_ref.at[pl.ds(d, 1), :],
                                  sem_ref.at[0]).start(priority=slot)
        return carry

    lax.fori_loop(0, t, issue, 0, unroll=ISSUE_UNROLL)
    pltpu.make_async_copy(xs_ref.at[pl.ds(0, 2 * t), :], xs_ref.at[pl.ds(0, 2 * t), :], sem_ref.at[0]).wait()


def _dispatch(x, dest, last_tile, n_used, n_tiles, t):
    n = x.shape[0]
    tm = EXPERT_TILE
    return pl.pallas_call(
        _dispatch_body,
        out_shape=jax.ShapeDtypeStruct((n_tiles * tm, D_MODEL), F32),
        grid=(n // t,),
        in_specs=[
            pl.BlockSpec((2 * t,), lambda i: (i,), memory_space=pltpu.SMEM),
            pl.BlockSpec(memory_space=pltpu.SMEM),
            pl.BlockSpec(memory_space=pltpu.SMEM),
            pl.BlockSpec((t, D_MODEL), lambda i: (i, 0)),
        ],
        out_specs=pl.BlockSpec(memory_space=pl.ANY),
        scratch_shapes=[pltpu.VMEM((tm, D_MODEL), F32), pltpu.SemaphoreType.DMA((2,))],
        compiler_params=_params(("arbitrary",)),
        name="dispatch",
    )(dest, last_tile, n_used, x)


def _experts_body(te_ref, first_ref, next_ref, slot_ref, nused_ref, x_ref, w1_hbm, w3_hbm, w2_hbm, o_ref,
                  wb1_ref, wb3_ref, wb2_ref, st1_ref, st3_ref, st2_ref, sem_ref):
    i = pl.program_id(0)

    def piece_copies(e, c):
        buf = c % 2
        cols = pl.ds(c * FF_CHUNK, FF_CHUNK)
        return (pltpu.make_async_copy(w1_hbm.at[e, :, cols], st1_ref.at[buf], sem_ref.at[buf, 0]),
                pltpu.make_async_copy(w3_hbm.at[e, :, cols], st3_ref.at[buf], sem_ref.at[buf, 1]),
                pltpu.make_async_copy(w2_hbm.at[e, cols, :], st2_ref.at[buf], sem_ref.at[buf, 2]))

    def start_piece(e, c):
        for cp in piece_copies(e, c):
            cp.start()

    def finish_piece(e, c, slot):
        for cp in piece_copies(e, c):
            cp.wait()
        cols = slice(c * FF_CHUNK, (c + 1) * FF_CHUNK)
        wb1_ref[slot, :, cols] = st1_ref[c % 2].astype(BF16)
        wb3_ref[slot, :, cols] = st3_ref[c % 2].astype(BF16)
        wb2_ref[slot, cols, :] = st2_ref[c % 2].astype(BF16)

    def ff_piece(xb, slot, c):
        cols = slice(c * FF_CHUNK, (c + 1) * FF_CHUNK)
        a = jnp.dot(xb, wb1_ref[slot, :, cols], preferred_element_type=F32)
        g = jnp.dot(xb, wb3_ref[slot, :, cols], preferred_element_type=F32)
        h = (jax.nn.silu(a) * g).astype(BF16)
        return jnp.dot(h, wb2_ref[slot, cols, :], preferred_element_type=F32)

    @pl.when(i == 0)
    def _():
        e, slot = te_ref[0], slot_ref[0]
        start_piece(e, 0)
        for c in range(FF_PIECES):
            if c + 1 < FF_PIECES:
                start_piece(e, c + 1)
            finish_piece(e, c, slot)

    live = i < nused_ref[0]
    prefetch = jnp.logical_and(live, jnp.logical_and(first_ref[i] == 1, next_ref[i] >= 0))

    @pl.when(jnp.logical_and(live, jnp.logical_not(prefetch)))
    def _():
        xb = x_ref[...].astype(BF16)
        slot = slot_ref[i]
        f = ff_piece(xb, slot, 0)
        for c in range(1, FF_PIECES):
            f = f + ff_piece(xb, slot, c)
        o_ref[...] = f

    @pl.when(prefetch)
    def _():
        xb = x_ref[...].astype(BF16)
        slot, e_next = slot_ref[i], next_ref[i]
        start_piece(e_next, 0)
        f = None
        for c in range(FF_PIECES):
            if c + 1 < FF_PIECES:
                start_piece(e_next, c + 1)
            part = ff_piece(xb, slot, c)
            f = part if f is None else f + part
            finish_piece(e_next, c, 1 - slot)
        o_ref[...] = f

    @pl.when(jnp.logical_not(live))
    def _():
        o_ref[...] = jnp.zeros_like(o_ref)


def _experts(xs, tile_expert, first, nxt, slot, n_used, w1, w3, w2):
    tm = EXPERT_TILE
    n_tiles = xs.shape[0] // tm
    hbm = pl.BlockSpec(memory_space=pl.ANY)
    return pl.pallas_call(
        _experts_body,
        out_shape=jax.ShapeDtypeStruct(xs.shape, F32),
        grid_spec=pltpu.PrefetchScalarGridSpec(
            num_scalar_prefetch=5,
            grid=(n_tiles,),
            in_specs=[pl.BlockSpec((tm, D_MODEL), lambda i, te, fi, nx, sl, nu: (jnp.minimum(i, nu[0] - 1), 0)),
                      hbm, hbm, hbm],
            out_specs=pl.BlockSpec((tm, D_MODEL), lambda i, te, fi, nx, sl, nu: (i, 0)),
            scratch_shapes=[
                pltpu.VMEM((2, D_MODEL, D_FF), BF16), pltpu.VMEM((2, D_MODEL, D_FF), BF16),
                pltpu.VMEM((2, D_FF, D_MODEL), BF16),
                pltpu.VMEM((2, D_MODEL, FF_CHUNK), F32), pltpu.VMEM((2, D_MODEL, FF_CHUNK), F32),
                pltpu.VMEM((2, FF_CHUNK, D_MODEL), F32),
                pltpu.SemaphoreType.DMA((2, 3)),
            ],
        ),
        compiler_params=_params(("arbitrary",), EXPERTS_VMEM_LIMIT),
        name="experts",
    )(tile_expert, first, nxt, slot, n_used, xs, w1, w3, w2)


def _combine_body(n_first, dest_ref, next_ref, x_ref, mw_ref, g_ref, b_ref, ys_ref, *rest):
    out_refs, (y0_ref, y1_ref, sem_ref) = rest[:-3], rest[-3:]
    t = x_ref.shape[0]
    i = pl.program_id(0)
    cur = i % 2

    def gather(idx_ref, buf):
        def issue(r, carry):
            pltpu.make_async_copy(ys_ref.at[pl.ds(idx_ref[r], 1), :], y0_ref.at[buf, pl.ds(r, 1), :],
                                  sem_ref.at[buf]).start(priority=0)
            pltpu.make_async_copy(ys_ref.at[pl.ds(idx_ref[t + r], 1), :], y1_ref.at[buf, pl.ds(r, 1), :],
                                  sem_ref.at[buf]).start(priority=1)
            return carry

        lax.fori_loop(0, t, issue, 0, unroll=ISSUE_UNROLL)

    @pl.when(i == 0)
    def _():
        gather(dest_ref, 0)

    @pl.when(i + 1 < pl.num_programs(0))
    def _():
        gather(next_ref, 1 - cur)

    pltpu.make_async_copy(ys_ref.at[pl.ds(0, t), :], y0_ref.at[cur], sem_ref.at[cur]).wait()
    pltpu.make_async_copy(ys_ref.at[pl.ds(0, t), :], y1_ref.at[cur], sem_ref.at[cur]).wait()
    mw = mw_ref[...]
    f = mw[:, 0:1] * y0_ref[cur] + mw[:, 1:2] * y1_ref[cur]
    out = _layer_norm(DN_ALPHA * x_ref[...] + f, g_ref[...], b_ref[...])
    if len(out_refs) == 1:
        out_refs[0][...] = out
    else:
        @pl.when(i < n_first)
        def _():
            out_refs[0][...] = out

        @pl.when(i == n_first)
        def _():
            out_refs[1][...] = out


def _combine(x, ys, dest, mw, g, b, t, split):
    n = x.shape[0]
    zero = lambda i: (0, 0)
    n_first = n // t - 1
    last = n // t - 1
    if split:
        out_shape = (jax.ShapeDtypeStruct((n - t, D_MODEL), F32), jax.ShapeDtypeStruct((t, D_MODEL), F32))
        out_specs = (pl.BlockSpec((t, D_MODEL), lambda i: (jnp.minimum(i, n_first - 1), 0)),
                     pl.BlockSpec((t, D_MODEL), zero))
    else:
        out_shape = jax.ShapeDtypeStruct((n, D_MODEL), F32)
        out_specs = pl.BlockSpec((t, D_MODEL), lambda i: (i, 0))
    return pl.pallas_call(
        functools.partial(_combine_body, n_first),
        out_shape=out_shape,
        grid=(n // t,),
        in_specs=[
            pl.BlockSpec((2 * t,), lambda i: (i,), memory_space=pltpu.SMEM),
            pl.BlockSpec((2 * t,), lambda i: (jnp.minimum(i + 1, last),), memory_space=pltpu.SMEM),
            pl.BlockSpec((t, D_MODEL), lambda i: (i, 0)),
            pl.BlockSpec((t, 2), lambda i: (i, 0)),
            pl.BlockSpec((1, D_MODEL), zero),
            pl.BlockSpec((1, D_MODEL), zero),
            pl.BlockSpec(memory_space=pl.ANY),
        ],
        out_specs=out_specs,
        scratch_shapes=[pltpu.VMEM((2, t, D_MODEL), F32), pltpu.VMEM((2, t, D_MODEL), F32),
                        pltpu.SemaphoreType.DMA((2,))],
        compiler_params=_params(("arbitrary",)),
        name="combine",
    )(dest, dest, x, mw, g, b, ys)


def _moe(x, mi, mw, cnt, w1, w3, w2, g, b, t, split):
    n = x.shape[0]
    tm = EXPERT_TILE
    n_tiles = (2 * n + N_EXPERTS * (tm - 1)) // tm
    counts = cnt[:N_EXPERTS, 0]
    tiles = (counts + tm - 1) // tm
    tile_end = jnp.cumsum(tiles)
    start = (tile_end - tiles) * tm
    n_used = tile_end[-1:].astype(jnp.int32)
    tile_expert = jnp.sum((tile_end[None, :] <= jnp.arange(n_tiles)[:, None]).astype(jnp.int32), axis=1)
    tile_expert = jnp.minimum(tile_expert, N_EXPERTS - 1)
    last_tile = jnp.where(tiles > 0, tile_end - 1, -1).astype(jnp.int32)
    has = tiles > 0
    next_of = [jnp.int32(-1)] * N_EXPERTS
    for e in range(N_EXPERTS - 2, -1, -1):
        next_of[e] = jnp.where(has[e + 1], e + 1, next_of[e + 1])
    per_expert = jnp.stack([tile_end - tiles, jnp.stack(next_of), (jnp.cumsum(has) - 1) % 2]).astype(jnp.int32)
    onehot = (tile_expert[None, :, None] == jnp.arange(N_EXPERTS)[None, None, :]).astype(jnp.int32)
    first_tile, nxt, slot = jnp.sum(onehot * per_expert[:, None, :], axis=2)
    first = (first_tile == jnp.arange(n_tiles)).astype(jnp.int32)
    experts = mi[:, 0:2, :]
    group_start = sum(jnp.where(experts == e, start[e], 0) for e in range(N_EXPERTS))
    dest = (group_start + mi[:, 2:4, :]).astype(jnp.int32).reshape(2 * n)
    mw = jnp.swapaxes(mw[:, 0:2, :], 1, 2).reshape(n, 2)
    xs = _dispatch(x, dest, last_tile, n_used, n_tiles, t)
    ys = _experts(xs, tile_expert.astype(jnp.int32), first, nxt, slot, n_used, w1, w3, w2)
    return _combine(x, ys, dest, mw, g, b, t, split)


def _block_diag(w):
    layers, g, c, _ = w.shape
    out = jnp.zeros((layers, g * c, g * c), w.dtype)
    for i in range(g):
        out = out.at[:, i * c:(i + 1) * c, i * c:(i + 1) * c].set(w[:, i])
    return out


def _bias_pairs(table, t_q, n_keys, offset):
    layers = table.shape[0]
    hi = t_q - 1 + offset
    span = t_q - 1 + n_keys
    cols = np.clip(hi - np.arange(span), -REL_CLIP, REL_CLIP) + REL_CLIP
    rev = table.reshape(layers * N_HEADS, -1)[:, cols].astype(F32)
    skew = jnp.tile(rev, (1, t_q + 1))[:, :t_q * (span + 1)].reshape(layers * N_HEADS, t_q, span + 1)
    slab = skew[:, ::-1, :n_keys]
    return slab.reshape(layers, HEAD_PAIRS, 2 * t_q, n_keys)


def _prompt_bias_slabs(table):
    bias = _bias_pairs(table, CHUNK, BAND, ATTN_REACH)
    col = np.arange(BAND)[None, :]
    first_valid = np.concatenate([[0], (LEFT_CHUNKS - np.arange(LEFT_CHUNKS)) * CHUNK])[:, None]
    valid = jnp.asarray(col >= first_valid)
    return jnp.where(valid[None, :, None, None, :], bias[:, None], NEG_INF)


def kernel(x_prompt, x_sample, cache_k, cache_v, state_pool, w_in, b_in, w_pool_grp, pool_scale,
           rel_table, w_pool_br, w_attn_br, w_out, ln1_g, ln1_b, ln2_g, ln2_b,
           w1_dense, w3_dense, w2_dense, w_router, b_router, w1_exp, w3_exp, w2_exp):
    bp, tp, d = x_prompt.shape
    bs, ts, _ = x_sample.shape
    n_p, n_s = bp * tp, bs * ts
    n = n_p + n_s
    depth = w_in.shape[0]
    keep_s = cache_k.shape[2]
    keep_p = min(ATTN_REACH, tp)

    x = (x_prompt.reshape(n_p, d), x_sample.reshape(n_s, d))
    ck = cache_k.reshape(depth, bs, keep_s, ATTN_WIDTH)
    cv = cache_v.reshape(depth, bs, keep_s, ATTN_WIDTH)
    hist = jnp.pad(state_pool, ((0, 0), (0, 0), (HIST_ROWS - POOL_HIST, 0), (0, 0)))
    assert tp % ATTN_REACH == 0 and keep_p == ATTN_REACH and ts >= POOL_HIST

    rows = lambda v: v[:, None, :].astype(F32)
    w_in_bf, b_in_r = _reorder_in_columns(w_in).astype(BF16), rows(_reorder_in_columns(b_in))
    wgrp, wp_bf, wa_bf, wo_bf = (a.astype(BF16) for a in (_block_diag(w_pool_grp), w_pool_br, w_attn_br, w_out))
    scale_r, g1, b1, g2, b2 = (rows(a) for a in (pool_scale, ln1_g, ln1_b, ln2_g, ln2_b))
    dense_bf = [a.astype(BF16) for a in (w1_dense, w3_dense, w2_dense)]
    bias_p = _prompt_bias_slabs(rel_table)
    bias_s = _bias_pairs(rel_table, ts, keep_s + ts, keep_s)

    k_buf = v_buf = None
    kp_new, vp_new, pp_new, ps_new = [], [], [], []
    for l in range(depth):
        weights = (wgrp[l], scale_r[l], wp_bf[l], wa_bf[l], wo_bf[l], g1[l], b1[l])
        z32, z16 = _in_proj(x, w_in_bf[l], b_in_r[l], n_s)
        attn_p, k_tail, v_tail = _attn_prompt(z32, z16, bias_p[l], bp, tp)
        attn_s, k_buf, v_buf = _attn_sample(z32, z16, ck, cv, bias_s[l], l, n_p, bs, ts, k_buf, v_buf)
        j = l // 2
        last = l == depth - 1
        if l % 2 == 0:
            x1, u_tail = _mix(z32, z16, attn_p, attn_s, x, hist[l], weights, n_p, tp)
            x = _ffn_dense(x1, dense_bf[0][j], dense_bf[1][j], dense_bf[2][j], g2[l], b2[l])
            if last:
                x = (x[:n_p], x[n_p:])
        else:
            x1, u_tail, mi, mw, cnt = _mix(z32, z16, attn_p, attn_s, x, hist[l], weights, n_p, tp,
                                           router=_router_operands(w_router[j], b_router[j]))
            x = _moe(x1, mi, mw, cnt, w1_exp[j], w3_exp[j], w2_exp[j], g2[l], b2[l], n_s, split=last)

        kp_new.append(k_tail.reshape(bp, keep_p, N_HEADS, HEAD_DIM))
        vp_new.append(v_tail.reshape(bp, keep_p, N_HEADS, HEAD_DIM))
        pp_new.append(u_tail[:, HIST_ROWS - POOL_HIST:])
        ps_new.append(z32[n_p:, :POOL_WIDTH].reshape(bs, ts, POOL_WIDTH)[:, ts - POOL_HIST:])

    shape_s = (depth, bs, keep_s, N_HEADS, HEAD_DIM)
    return (x[0].reshape(bp, tp, d), x[1].reshape(bs, ts, d),
            jnp.stack(kp_new), jnp.stack(vp_new), jnp.stack(pp_new),
            k_buf.reshape(shape_s), v_buf.reshape(shape_s), jnp.stack(ps_new))
```

```python
import functools

import jax
import jax.numpy as jnp
import numpy as np
from jax import lax
from jax.experimental import pallas as pl
from jax.experimental.pallas import tpu as pltpu

F32 = jnp.float32
BF16 = jnp.bfloat16

D_MODEL = 1024
N_HEADS = 8
HEAD_DIM = 64
ATTN_WIDTH = N_HEADS * HEAD_DIM
CHUNK = 64
LEFT_CHUNKS = 8
BAND = (LEFT_CHUNKS + 1) * CHUNK
ATTN_REACH = LEFT_CHUNKS * CHUNK
REL_CLIP = 256
ATTN_SCALE = HEAD_DIM ** -0.5
POOL_WIDTH = 512
POOL_WINDOWS = (2, 4, 8, 16)
POOL_GROUP = POOL_WIDTH // len(POOL_WINDOWS)
POOL_HIST = max(POOL_WINDOWS) - 1
HIST_ROWS = POOL_HIST + 1
LEAD_ROWS = 8
D_FF = 2816
N_EXPERTS = 8
PAST_LEN = 4096
DEPTH = 2
DN_ALPHA = (2 * DEPTH) ** 0.25
LN_EPS = 1e-5
NEG_INF = -1e30
IN_WIDTH = POOL_WIDTH + 3 * ATTN_WIDTH + 2 * D_MODEL
HEAD_PAIRS = N_HEADS // 2
PAIR_W = 2 * HEAD_DIM
LANES = 128
ROUTE_ROWS = 16

Z32_WIDTH = POOL_WIDTH + 2 * ATTN_WIDTH
Z16_WIDTH = 2 * D_MODEL + 3 * ATTN_WIDTH
COL_U, COL_K, COL_V = 0, 1, 2
COL_GP, COL_GA = 0, 1
COL_Q = 2 * D_MODEL // ATTN_WIDTH
COL_K16, COL_V16 = COL_Q + 1, COL_Q + 2

EXPERT_TILE = 512
FF_CHUNK = 256
FF_PIECES = D_FF // FF_CHUNK
ISSUE_UNROLL = 8
VMEM_LIMIT = 56 * 1024 * 1024
EXPERTS_VMEM_LIMIT = 62 * 1024 * 1024


def _pick(n, candidates):
    for c in candidates:
        if n % c == 0:
            return c
    raise ValueError(f"no tile in {candidates} divides {n}")


def _params(sem, vmem=None):
    return pltpu.CompilerParams(dimension_semantics=sem, vmem_limit_bytes=vmem or VMEM_LIMIT)


def _layer_norm(r, g, b):
    mu = jnp.mean(r, axis=-1, keepdims=True)
    c = r - mu
    var = jnp.mean(c * c, axis=-1, keepdims=True)
    return c * lax.rsqrt(var + LN_EPS) * g + b


def _row_sources(x, t):
    if not isinstance(x, tuple):
        return [x], [pl.BlockSpec((t, x.shape[1]), lambda i: (i, 0))], (lambda refs, i: refs[0][...])
    first, last = x
    assert last.shape[0] == t and first.shape[0] % t == 0
    n_first = first.shape[0] // t
    specs = [pl.BlockSpec((t, first.shape[1]), lambda i: (jnp.minimum(i, n_first - 1), 0)),
             pl.BlockSpec((t, last.shape[1]), lambda i: (0, 0))]
    return [first, last], specs, (lambda refs, i: jnp.where(i < n_first, refs[0][...], refs[1][...]))


def _inproj_body(n_src, select, *refs):
    w_ref, b_ref, z32_ref, z16_ref = refs[n_src:]
    x = select(refs[:n_src], pl.program_id(0)).astype(BF16)
    z = jnp.dot(x, w_ref[...], preferred_element_type=F32) + b_ref[...]
    z32_ref[...] = z[:, :Z32_WIDTH]
    z16_ref[...] = jnp.concatenate([z[:, Z32_WIDTH:], z[:, POOL_WIDTH:Z32_WIDTH]], axis=-1).astype(BF16)


def _in_proj(x, w_bf, b, t):
    srcs, specs, select = _row_sources(x, t)
    n = sum(s.shape[0] for s in srcs)
    zero = lambda i: (0, 0)
    return pl.pallas_call(
        functools.partial(_inproj_body, len(srcs), select),
        out_shape=(jax.ShapeDtypeStruct((n, Z32_WIDTH), F32), jax.ShapeDtypeStruct((n, Z16_WIDTH), BF16)),
        grid=(n // t,),
        in_specs=specs + [pl.BlockSpec((D_MODEL, IN_WIDTH), zero), pl.BlockSpec((1, IN_WIDTH), zero)],
        out_specs=(pl.BlockSpec((t, Z32_WIDTH), lambda i: (i, 0)), pl.BlockSpec((t, Z16_WIDTH), lambda i: (i, 0))),
        compiler_params=_params(("arbitrary",)),
        name="in_proj",
    )(*srcs, w_bf, b)


def _reorder_in_columns(w):
    o = [0, POOL_WIDTH, POOL_WIDTH + ATTN_WIDTH, POOL_WIDTH + 2 * ATTN_WIDTH, POOL_WIDTH + 3 * ATTN_WIDTH, IN_WIDTH]
    u, q, k, v, g = (w[..., o[i]:o[i + 1]] for i in range(5))
    return jnp.concatenate([u, k, v, g, q], axis=-1)


def _pair_scores(q_pair, k_pair, bias):
    lane = lax.broadcasted_iota(jnp.int32, q_pair.shape, 1)
    qs = q_pair.astype(F32) * ATTN_SCALE
    q2 = jnp.concatenate([jnp.where(lane < HEAD_DIM, qs, 0.0), jnp.where(lane >= HEAD_DIM, qs, 0.0)], axis=0)
    s = lax.dot_general(q2.astype(BF16), k_pair, (((1,), (1,)), ((), ())), preferred_element_type=F32)
    return s + bias


def _pair_output(s, v_pair):
    rows = s.shape[0] // 2
    m = jnp.max(s, axis=-1, keepdims=True)
    e = jnp.exp(s - m)
    l = jnp.sum(e, axis=-1, keepdims=True)
    o2 = jnp.dot(e.astype(BF16), v_pair, preferred_element_type=F32) / l
    lane = lax.broadcasted_iota(jnp.int32, (rows, PAIR_W), 1)
    return jnp.where(lane < HEAD_DIM, o2[:rows], o2[rows:])


def _attn_prompt_body(q_ref, kp_ref, kc_ref, vp_ref, vc_ref, kl_ref, vl_ref, bias_ref,
                      o_ref, kt_ref, vt_ref, kext_ref, vext_ref):
    blk = q_ref.shape[0]
    j = pl.program_id(1)

    @pl.when(j == pl.num_programs(1) - 1)
    def _():
        kt_ref[0] = kl_ref[...]
        vt_ref[0] = vl_ref[...]

    kext_ref[0:blk, :] = kp_ref[...]
    kext_ref[blk:2 * blk, :] = kc_ref[...]
    vext_ref[0:blk, :] = vp_ref[...]
    vext_ref[blk:2 * blk, :] = vc_ref[...]
    def chunk(c, carry):
        q0 = pl.multiple_of(c * CHUNK, CHUNK)
        slab = jnp.where(j == 0, c + 1, 0)
        pairs = [slice(hp * PAIR_W, (hp + 1) * PAIR_W) for hp in range(HEAD_PAIRS)]
        s = jnp.concatenate([_pair_scores(q_ref[pl.ds(q0, CHUNK), lanes], kext_ref[pl.ds(q0, BAND), lanes],
                                          bias_ref[slab, hp]) for hp, lanes in enumerate(pairs)], axis=0)
        m = jnp.max(s, axis=-1, keepdims=True)
        e = jnp.exp(s - m)
        inv = 1.0 / jnp.sum(e, axis=-1, keepdims=True)
        p = e.astype(BF16)
        lane = lax.broadcasted_iota(jnp.int32, (CHUNK, PAIR_W), 1)
        for hp, lanes in enumerate(pairs):
            rows = slice(hp * 2 * CHUNK, (hp + 1) * 2 * CHUNK)
            o2 = jnp.dot(p[rows], vext_ref[pl.ds(q0, BAND), lanes], preferred_element_type=F32) * inv[rows]
            o_ref[pl.ds(q0, CHUNK), lanes] = jnp.where(lane < HEAD_DIM, o2[:CHUNK], o2[CHUNK:])
        return carry

    lax.fori_loop(0, blk // CHUNK, chunk, 0, unroll=8)


def _attn_prompt(z32, z16, bias_pairs, batch, seq):
    blk = ATTN_REACH
    per_seq = seq // blk
    rows = batch * seq

    def cur(col):
        return lambda b, j: (b * per_seq + j, col)

    def prev(col):
        return lambda b, j: (b * per_seq + jnp.maximum(j - 1, 0), col)

    def last(col):
        return lambda b, j: (b * per_seq + per_seq - 1, col)

    tail = jax.ShapeDtypeStruct((batch, blk, ATTN_WIDTH), F32)
    tail_spec = pl.BlockSpec((1, blk, ATTN_WIDTH), lambda b, j: (b, 0, 0))
    return pl.pallas_call(
        _attn_prompt_body,
        out_shape=(jax.ShapeDtypeStruct((rows, ATTN_WIDTH), F32), tail, tail),
        grid=(batch, per_seq),
        in_specs=[
            pl.BlockSpec((blk, ATTN_WIDTH), cur(COL_Q)),
            pl.BlockSpec((blk, ATTN_WIDTH), prev(COL_K16)),
            pl.BlockSpec((blk, ATTN_WIDTH), cur(COL_K16)),
            pl.BlockSpec((blk, ATTN_WIDTH), prev(COL_V16)),
            pl.BlockSpec((blk, ATTN_WIDTH), cur(COL_V16)),
            pl.BlockSpec((blk, ATTN_WIDTH), last(COL_K)),
            pl.BlockSpec((blk, ATTN_WIDTH), last(COL_V)),
            pl.BlockSpec((1 + LEFT_CHUNKS, HEAD_PAIRS, 2 * CHUNK, BAND), lambda b, j: (0, 0, 0, 0)),
        ],
        out_specs=(pl.BlockSpec((blk, ATTN_WIDTH), lambda b, j: (b * per_seq + j, 0)), tail_spec, tail_spec),
        scratch_shapes=[pltpu.VMEM((2 * blk, ATTN_WIDTH), BF16), pltpu.VMEM((2 * blk, ATTN_WIDTH), BF16)],
        compiler_params=_params(("arbitrary", "arbitrary")),
        name="attn_prompt",
    )(z16, z16, z16, z16, z16, z32, z32, bias_pairs)


def _attn_sample_body(q_ref, kn_ref, vn_ref, ck_ref, cv_ref, bias_ref, *rest):
    o_ref, ko_ref, vo_ref, kall_ref, vall_ref = rest[-5:]
    keep = ck_ref.shape[2]
    t = q_ref.shape[0]
    ck = ck_ref[0, 0]
    cv = cv_ref[0, 0]
    kn = kn_ref[...]
    vn = vn_ref[...]
    kall_ref[0:keep, :] = ck.astype(BF16)
    kall_ref[keep:keep + t, :] = kn.astype(BF16)
    vall_ref[0:keep, :] = cv.astype(BF16)
    vall_ref[keep:keep + t, :] = vn.astype(BF16)
    for hp in range(HEAD_PAIRS):
        lanes = slice(hp * PAIR_W, (hp + 1) * PAIR_W)
        s = _pair_scores(q_ref[:, lanes], kall_ref[:, lanes], bias_ref[hp])
        o_ref[:, lanes] = _pair_output(s, vall_ref[:, lanes])
    ko_ref[0, 0, 0:keep - t, :] = ck[t:keep]
    ko_ref[0, 0, keep - t:keep, :] = kn
    vo_ref[0, 0, 0:keep - t, :] = cv[t:keep]
    vo_ref[0, 0, keep - t:keep, :] = vn
    for later in range(1, ko_ref.shape[0]):
        ko_ref[later] = jnp.zeros(ko_ref.shape[1:], F32)
        vo_ref[later] = jnp.zeros(vo_ref.shape[1:], F32)


def _attn_sample(z32, z16, cache_k, cache_v, bias_pairs, layer, row0, streams, t, k_buf, v_buf):
    depth, _, keep, _ = cache_k.shape
    blk0 = row0 // t
    ins = [z16, z32, z32, cache_k, cache_v, bias_pairs]
    in_specs = [
        pl.BlockSpec((t, ATTN_WIDTH), lambda s: (blk0 + s, COL_Q)),
        pl.BlockSpec((t, ATTN_WIDTH), lambda s: (blk0 + s, COL_K)),
        pl.BlockSpec((t, ATTN_WIDTH), lambda s: (blk0 + s, COL_V)),
        pl.BlockSpec((1, 1, keep, ATTN_WIDTH), lambda s: (layer, s, 0, 0)),
        pl.BlockSpec((1, 1, keep, ATTN_WIDTH), lambda s: (layer, s, 0, 0)),
        pl.BlockSpec((HEAD_PAIRS, 2 * t, keep + t), lambda s: (0, 0, 0)),
    ]
    aliases = {}
    if k_buf is None:
        assert layer == 0
        buf_spec = pl.BlockSpec((depth, 1, keep, ATTN_WIDTH), lambda s: (0, s, 0, 0))
    else:
        aliases = {len(ins): 1, len(ins) + 1: 2}
        ins += [k_buf, v_buf]
        in_specs += [pl.BlockSpec(memory_space=pl.ANY), pl.BlockSpec(memory_space=pl.ANY)]
        buf_spec = pl.BlockSpec((1, 1, keep, ATTN_WIDTH), lambda s: (layer, s, 0, 0))
    buf = jax.ShapeDtypeStruct(cache_k.shape, F32)
    return pl.pallas_call(
        _attn_sample_body,
        out_shape=(jax.ShapeDtypeStruct((streams * t, ATTN_WIDTH), F32), buf, buf),
        grid=(streams,),
        in_specs=in_specs,
        out_specs=(
            pl.BlockSpec((t, ATTN_WIDTH), lambda s: (s, 0)),
            buf_spec,
            buf_spec,
        ),
        scratch_shapes=[pltpu.VMEM((keep + t, ATTN_WIDTH), BF16), pltpu.VMEM((keep + t, ATTN_WIDTH), BF16)],
        input_output_aliases=aliases,
        compiler_params=_params(("arbitrary",)),
        name="attn_sample",
    )(*ins)


def _merge(pooled, u, attn, gp, ga, x, wgrp_ref, scale_ref, wp_ref, wa_ref, wo_ref, g_ref, b_ref):
    pooled = pooled - u
    pool_y = jnp.dot(pooled.astype(BF16), wgrp_ref[...], preferred_element_type=F32) * scale_ref[...]
    mp = jnp.dot(pool_y.astype(BF16), wp_ref[...], preferred_element_type=F32)
    ma = jnp.dot(attn.astype(BF16), wa_ref[...], preferred_element_type=F32)
    m = jax.nn.sigmoid(gp) * mp + jax.nn.sigmoid(ga) * ma
    y = jnp.dot(m.astype(BF16), wo_ref[...], preferred_element_type=F32)
    return _layer_norm(DN_ALPHA * x + y, g_ref[...], b_ref[...])


def _window_means(read, pos, shape_out):
    outs = []
    for g, w in enumerate(POOL_WINDOWS):
        lanes = slice(g * POOL_GROUP, (g + 1) * POOL_GROUP)
        s = read(0, lanes)
        for back in range(1, w):
            s = s + read(back, lanes)
        outs.append((s / jnp.minimum(pos + 1, w).astype(F32)).reshape(shape_out))
    return jnp.concatenate(outs, axis=-1)


def _window_means_doubling(buf_ref, lvl_ref, pos, t):
    g = POOL_GROUP
    lo, hi = LEAD_ROWS, LEAD_ROWS + HIST_ROWS + t
    out0 = LEAD_ROWS + HIST_ROWS
    lvl_ref[0, lo:hi, :] = buf_ref[lo:hi, :] + buf_ref[lo - 1:hi - 1, :]
    lvl_ref[1, lo:hi, g:] = lvl_ref[0, lo:hi, g:] + lvl_ref[0, lo - 2:hi - 2, g:]
    lvl_ref[2, lo:hi, 2 * g:] = lvl_ref[1, lo:hi, 2 * g:] + lvl_ref[1, lo - 4:hi - 4, 2 * g:]
    s16 = lvl_ref[2, out0:out0 + t, 3 * g:] + lvl_ref[2, out0 - 8:out0 - 8 + t, 3 * g:]
    sums = [lvl_ref[k, out0:out0 + t, k * g:(k + 1) * g] for k in range(3)] + [s16]
    return jnp.concatenate([s / jnp.minimum(pos + 1, w).astype(F32) for s, w in zip(sums, POOL_WINDOWS)], axis=-1)


def _mix_body(n_prompt_tiles, tiles_per_seq, with_router, n_src, select,
              u_ref, up_ref, hist_ref, attn_p_ref, attn_s_ref, gp_ref, ga_ref, *rest):
    x_refs, rest = rest[:n_src], rest[n_src:]
    (wgrp_ref, scale_ref, wp_ref, wa_ref, wo_ref, g_ref, b_ref), rest = rest[:7], rest[7:]
    if with_router:
        (wr_hi_ref, wr_lo_ref, br_ref, o_ref, tail_ref, mi_ref, mw_ref, cnt_ref,
         buf_ref, lvl_ref, sbuf_ref, pooled_ref, attn_ref, carry_ref) = rest
    else:
        o_ref, tail_ref, buf_ref, lvl_ref, sbuf_ref, pooled_ref, attn_ref = rest
    t = u_ref.shape[0]
    i = pl.program_id(0)
    u = u_ref[...]

    @pl.when(i == 0)
    def _():
        buf_ref[0:LEAD_ROWS, :] = jnp.zeros((LEAD_ROWS, POOL_WIDTH), F32)
        lvl_ref[:, 0:LEAD_ROWS, :] = jnp.zeros((lvl_ref.shape[0], LEAD_ROWS, POOL_WIDTH), F32)

    @pl.when(i < n_prompt_tiles)
    def _():
        tile = i % tiles_per_seq
        h0 = LEAD_ROWS
        buf_ref[h0:h0 + HIST_ROWS, :] = jnp.where(tile == 0, 0.0, up_ref[...])
        buf_ref[h0 + HIST_ROWS:h0 + HIST_ROWS + t, :] = u
        pos = tile * t + lax.broadcasted_iota(jnp.int32, (t, 1), 0)
        pooled_ref[...] = _window_means_doubling(buf_ref, lvl_ref, pos, t)
        attn_ref[...] = attn_p_ref[...]

        @pl.when(tile == tiles_per_seq - 1)
        def _():
            tail_ref[0] = u[t - HIST_ROWS:, :]

    @pl.when(i == n_prompt_tiles)
    def _():
        streams, hrows, _ = hist_ref.shape
        ts = t // streams
        sbuf_ref[:, 0:hrows, :] = hist_ref[...]
        sbuf_ref[:, hrows:hrows + ts, :] = u.reshape(streams, ts, POOL_WIDTH)
        pos = PAST_LEN + lax.broadcasted_iota(jnp.int32, (1, ts, 1), 1)
        read = lambda back, lanes: sbuf_ref[:, hrows - back:hrows - back + ts, lanes]
        pooled_ref[...] = _window_means(read, pos, (t, POOL_GROUP))
        attn_ref[...] = attn_s_ref[...]

    x1 = _merge(pooled_ref[...], u, attn_ref[...], gp_ref[...].astype(F32), ga_ref[...].astype(F32),
                select(x_refs, i), wgrp_ref, scale_ref, wp_ref, wa_ref, wo_ref, g_ref, b_ref)
    o_ref[...] = x1
    if with_router:
        _route(x1, wr_hi_ref, wr_lo_ref, br_ref, mi_ref, mw_ref, cnt_ref, carry_ref)


def _mix(z32, z16, attn_p, attn_s, x, hist, weights, n_prompt, seq, router=None):
    n = z32.shape[0]
    t = n - n_prompt
    x_srcs, x_specs, select = _row_sources(x, t)
    n_prompt_tiles = n_prompt // t
    streams, hrows, _ = hist.shape
    hist_per_tile = t // HIST_ROWS
    last_p = n_prompt_tiles - 1
    zero = lambda i: (0, 0)
    rows = lambda width: pl.BlockSpec((t, width), lambda i: (i, 0))
    tiles_per_seq = seq // t
    out_shape = [jax.ShapeDtypeStruct((n, D_MODEL), F32),
                 jax.ShapeDtypeStruct((n_prompt // seq, HIST_ROWS, POOL_WIDTH), F32)]
    out_specs = [rows(D_MODEL),
                 pl.BlockSpec((1, HIST_ROWS, POOL_WIDTH), lambda i: (jnp.minimum(i, last_p) // tiles_per_seq, 0, 0))]
    router_specs, router_scratch = [], []
    if router is not None:
        router_specs = [pl.BlockSpec((ROUTE_ROWS, D_MODEL), zero), pl.BlockSpec((ROUTE_ROWS, D_MODEL), zero),
                        pl.BlockSpec((ROUTE_ROWS, 1), zero)]
        per_tile = lambda dtype: jax.ShapeDtypeStruct((n // t, 8, t), dtype)
        out_shape += [per_tile(jnp.int32), per_tile(F32), jax.ShapeDtypeStruct((ROUTE_ROWS, LANES), jnp.int32)]
        out_specs += [pl.BlockSpec((1, 8, t), lambda i: (i, 0, 0)), pl.BlockSpec((1, 8, t), lambda i: (i, 0, 0)),
                      pl.BlockSpec((ROUTE_ROWS, LANES), zero)]
        router_scratch = [pltpu.VMEM((ROUTE_ROWS, LANES), F32)]
    out = pl.pallas_call(
        functools.partial(_mix_body, n_prompt_tiles, seq // t, router is not None, len(x_srcs), select),
        out_shape=out_shape,
        grid=(n_prompt_tiles + 1,),
        in_specs=[
            pl.BlockSpec((t, POOL_WIDTH), lambda i: (i, COL_U)),
            pl.BlockSpec((HIST_ROWS, POOL_WIDTH), lambda i: (jnp.maximum(i * hist_per_tile - 1, 0), COL_U)),
            pl.BlockSpec((streams, hrows, POOL_WIDTH), lambda i: (0, 0, 0)),
            pl.BlockSpec((t, ATTN_WIDTH), lambda i: (jnp.minimum(i, last_p), 0)),
            pl.BlockSpec((t, ATTN_WIDTH), zero),
            pl.BlockSpec((t, D_MODEL), lambda i: (i, COL_GP)),
            pl.BlockSpec((t, D_MODEL), lambda i: (i, COL_GA)),
        ] + x_specs + [
            pl.BlockSpec((POOL_WIDTH, POOL_WIDTH), zero),
            pl.BlockSpec((1, POOL_WIDTH), zero),
            pl.BlockSpec((POOL_WIDTH, D_MODEL), zero),
            pl.BlockSpec((ATTN_WIDTH, D_MODEL), zero),
            pl.BlockSpec((D_MODEL, D_MODEL), zero),
            pl.BlockSpec((1, D_MODEL), zero),
            pl.BlockSpec((1, D_MODEL), zero),
        ] + router_specs,
        out_specs=out_specs,
        scratch_shapes=[
            pltpu.VMEM((LEAD_ROWS + HIST_ROWS + t, POOL_WIDTH), F32),
            pltpu.VMEM((3, LEAD_ROWS + HIST_ROWS + t, POOL_WIDTH), F32),
            pltpu.VMEM((streams, hrows + t // streams, POOL_WIDTH), F32),
            pltpu.VMEM((t, POOL_WIDTH), F32),
            pltpu.VMEM((t, ATTN_WIDTH), F32),
        ] + router_scratch,
        compiler_params=_params(("arbitrary",)),
        name="mix",
    )(z32, z32, hist, attn_p, attn_s, z16, z16, *x_srcs, *weights, *(router or ()))
    return out


def _ffn_dense_body(x_ref, w1_ref, w3_ref, w2_ref, g_ref, b_ref, o_ref):
    x = x_ref[...]
    xb = x.astype(BF16)
    a = jnp.dot(xb, w1_ref[...], preferred_element_type=F32)
    c = jnp.dot(xb, w3_ref[...], preferred_element_type=F32)
    h = (jax.nn.silu(a) * c).astype(BF16)
    f = jnp.dot(h, w2_ref[...], preferred_element_type=F32)
    o_ref[...] = _layer_norm(DN_ALPHA * x + f, g_ref[...], b_ref[...])


def _ffn_dense(x, w1, w3, w2, g, b):
    n = x.shape[0]
    t = 256
    zero = lambda i: (0, 0)
    return pl.pallas_call(
        _ffn_dense_body,
        out_shape=jax.ShapeDtypeStruct((n, D_MODEL), F32),
        grid=(n // t,),
        in_specs=[
            pl.BlockSpec((t, D_MODEL), lambda i: (i, 0)),
            pl.BlockSpec((D_MODEL, D_FF), zero),
            pl.BlockSpec((D_MODEL, D_FF), zero),
            pl.BlockSpec((D_FF, D_MODEL), zero),
            pl.BlockSpec((1, D_MODEL), zero),
            pl.BlockSpec((1, D_MODEL), zero),
        ],
        out_specs=pl.BlockSpec((t, D_MODEL), lambda i: (i, 0)),
        compiler_params=_params(("arbitrary",)),
        name="ffn_dense",
    )(x, w1, w3, w2, g, b)


def _route(x, wr_hi_ref, wr_lo_ref, br_ref, mi_ref, mw_ref, cnt_ref, carry_ref):
    t = x.shape[0]

    @pl.when(pl.program_id(0) == 0)
    def _():
        carry_ref[...] = jnp.zeros_like(carry_ref)

    x_hi = x.astype(BF16)
    x_lo = (x - x_hi.astype(F32)).astype(BF16)
    nt = (((1,), (1,)), ((), ()))
    logits = (lax.dot_general(wr_hi_ref[...], x_hi, nt, preferred_element_type=F32)
              + lax.dot_general(wr_hi_ref[...], x_lo, nt, preferred_element_type=F32)
              + lax.dot_general(wr_lo_ref[...], x_hi, nt, preferred_element_type=F32)) + br_ref[...]
    row = lax.broadcasted_iota(jnp.int32, (ROUTE_ROWS, t), 0)
    row_f = row.astype(F32)
    logits = jnp.where(row < N_EXPERTS, logits, -jnp.inf)
    v0 = jnp.max(logits, axis=0, keepdims=True)
    e0 = jnp.min(jnp.where(logits == v0, row_f, float(ROUTE_ROWS)), axis=0, keepdims=True)
    rest = jnp.where(row_f == e0, -jnp.inf, logits)
    v1 = jnp.max(rest, axis=0, keepdims=True)
    e1 = jnp.min(jnp.where(rest == v1, row_f, float(ROUTE_ROWS)), axis=0, keepdims=True)
    ex = jnp.exp(v1 - v0)
    w0 = 1.0 / (1.0 + ex)
    w1 = ex / (1.0 + ex)
    oh0 = (row_f == e0).astype(F32)
    oh1 = (row_f == e1).astype(F32)
    before = (lax.broadcasted_iota(jnp.int32, (t, t), 0) < lax.broadcasted_iota(jnp.int32, (t, t), 1)).astype(BF16)
    pre0 = jnp.dot(oh0.astype(BF16), before, preferred_element_type=F32)
    pre1 = jnp.dot(oh1.astype(BF16), before, preferred_element_type=F32)
    cnt0 = jnp.sum(oh0, axis=1, keepdims=True)
    cnt1 = jnp.sum(oh1, axis=1, keepdims=True)
    carry = carry_ref[:, 0:1]
    rank0 = jnp.sum(oh0 * (carry + pre0), axis=0, keepdims=True)
    rank1 = jnp.sum(oh1 * (carry + cnt0 + pre1), axis=0, keepdims=True)
    carry = jnp.broadcast_to(carry + cnt0 + cnt1, carry_ref.shape)
    carry_ref[...] = carry
    cnt_ref[...] = carry.astype(jnp.int32)
    r8 = lax.broadcasted_iota(jnp.int32, (8, t), 0)
    mi = jnp.where(r8 == 0, e0, jnp.where(r8 == 1, e1, jnp.where(r8 == 2, rank0, jnp.where(r8 == 3, rank1, 0.0))))
    mi_ref[0] = mi.astype(jnp.int32)
    mw_ref[0] = jnp.where(r8 == 0, w0, jnp.where(r8 == 1, w1, 0.0))


def _router_operands(w_r, b_r):
    wr = jnp.zeros((ROUTE_ROWS, D_MODEL), F32).at[:N_EXPERTS].set(w_r.T)
    br = jnp.zeros((ROUTE_ROWS, 1), F32).at[:N_EXPERTS, 0].set(b_r)
    wr_hi = wr.astype(BF16)
    wr_lo = (wr - wr_hi.astype(F32)).astype(BF16)
    return wr_hi, wr_lo, br


def _dispatch_body(dest_ref, last_ref, nused_ref, x_ref, xs_ref, zero_ref, sem_ref):
    t = dest_ref.shape[0] // 2
    tm = zero_ref.shape[0]
    n_tiles = xs_ref.shape[0] // tm
    i = pl.program_id(0)

    @pl.when(i == 0)
    def _():
        zero_ref[...] = jnp.zeros_like(zero_ref)

        def fill(tile):
            cp = pltpu.make_async_copy(zero_ref, xs_ref.at[pl.ds(pl.multiple_of(tile * tm, tm), tm), :], sem_ref.at[1])
            cp.start()
            cp.wait()

        for e in range(N_EXPERTS):
            @pl.when(last_ref[e] >= 0)
            def _():
                fill(last_ref[e])

        def unused(tile, carry):
            fill(tile)
            return carry

        lax.fori_loop(nused_ref[0], n_tiles, unused, 0)

    def issue(r, carry):
        for slot in range(2):
            d = dest_ref[slot * t + r]
            pltpu.make_async_copy(x_ref.at[pl.ds(r, 1), :], xs_ref.at[pl.ds(d, 1), :],
                                  sem_ref.at[0]).start(priority=slot)
        return carry

    lax.fori_loop(0, t, issue, 0, unroll=ISSUE_UNROLL)
    pltpu.make_async_copy(xs_ref.at[pl.ds(0, 2 * t), :], xs_ref.at[pl.ds(0, 2 * t), :], sem_ref.at[0]).wait()


def _dispatch(x, dest, last_tile, n_used, n_tiles, t):
    n = x.shape[0]
    tm = EXPERT_TILE
    return pl.pallas_call(
        _dispatch_body,
        out_shape=jax.ShapeDtypeStruct((n_tiles * tm, D_MODEL), F32),
        grid=(n // t,),
        in_specs=[
            pl.BlockSpec((2 * t,), lambda i: (i,), memory_space=pltpu.SMEM),
            pl.BlockSpec(memory_space=pltpu.SMEM),
            pl.BlockSpec(memory_space=pltpu.SMEM),
            pl.BlockSpec((t, D_MODEL), lambda i: (i, 0)),
        ],
        out_specs=pl.BlockSpec(memory_space=pl.ANY),
        scratch_shapes=[pltpu.VMEM((tm, D_MODEL), F32), pltpu.SemaphoreType.DMA((2,))],
        compiler_params=_params(("arbitrary",)),
        name="dispatch",
    )(dest, last_tile, n_used, x)


def _experts_body(te_ref, first_ref, next_ref, slot_ref, nused_ref, x_ref, w1_hbm, w3_hbm, w2_hbm, o_ref,
                  wb1_ref, wb3_ref, wb2_ref, st1_ref, st3_ref, st2_ref, sem_ref):
    i = pl.program_id(0)

    def piece_copies(e, c):
        buf = c % 2
        cols = pl.ds(c * FF_CHUNK, FF_CHUNK)
        return (pltpu.make_async_copy(w1_hbm.at[e, :, cols], st1_ref.at[buf], sem_ref.at[buf, 0]),
                pltpu.make_async_copy(w3_hbm.at[e, :, cols], st3_ref.at[buf], sem_ref.at[buf, 1]),
                pltpu.make_async_copy(w2_hbm.at[e, cols, :], st2_ref.at[buf], sem_ref.at[buf, 2]))

    def start_piece(e, c):
        for cp in piece_copies(e, c):
            cp.start()

    def finish_piece(e, c, slot):
        for cp in piece_copies(e, c):
            cp.wait()
        cols = slice(c * FF_CHUNK, (c + 1) * FF_CHUNK)
        wb1_ref[slot, :, cols] = st1_ref[c % 2].astype(BF16)
        wb3_ref[slot, :, cols] = st3_ref[c % 2].astype(BF16)
        wb2_ref[slot, cols, :] = st2_ref[c % 2].astype(BF16)

    def ff_piece(xb, slot, c):
        cols = slice(c * FF_CHUNK, (c + 1) * FF_CHUNK)
        a = jnp.dot(xb, wb1_ref[slot, :, cols], preferred_element_type=F32)
        g = jnp.dot(xb, wb3_ref[slot, :, cols], preferred_element_type=F32)
        h = (jax.nn.silu(a) * g).astype(BF16)
        return jnp.dot(h, wb2_ref[slot, cols, :], preferred_element_type=F32)

    @pl.when(i == 0)
    def _():
        e, slot = te_ref[0], slot_ref[0]
        start_piece(e, 0)
        for c in range(FF_PIECES):
            if c + 1 < FF_PIECES:
                start_piece(e, c + 1)
            finish_piece(e, c, slot)

    live = i < nused_ref[0]
    prefetch = jnp.logical_and(live, jnp.logical_and(first_ref[i] == 1, next_ref[i] >= 0))

    @pl.when(jnp.logical_and(live, jnp.logical_not(prefetch)))
    def _():
        xb = x_ref[...].astype(BF16)
        slot = slot_ref[i]
        f = ff_piece(xb, slot, 0)
        for c in range(1, FF_PIECES):
            f = f + ff_piece(xb, slot, c)
        o_ref[...] = f

    @pl.when(prefetch)
    def _():
        xb = x_ref[...].astype(BF16)
        slot, e_next = slot_ref[i], next_ref[i]
        start_piece(e_next, 0)
        f = None
        for c in range(FF_PIECES):
            if c + 1 < FF_PIECES:
                start_piece(e_next, c + 1)
            part = ff_piece(xb, slot, c)
            f = part if f is None else f + part
            finish_piece(e_next, c, 1 - slot)
        o_ref[...] = f

    @pl.when(jnp.logical_not(live))
    def _():
        o_ref[...] = jnp.zeros_like(o_ref)


def _experts(xs, tile_expert, first, nxt, slot, n_used, w1, w3, w2):
    tm = EXPERT_TILE
    n_tiles = xs.shape[0] // tm
    hbm = pl.BlockSpec(memory_space=pl.ANY)
    return pl.pallas_call(
        _experts_body,
        out_shape=jax.ShapeDtypeStruct(xs.shape, F32),
        grid_spec=pltpu.PrefetchScalarGridSpec(
            num_scalar_prefetch=5,
            grid=(n_tiles,),
            in_specs=[pl.BlockSpec((tm, D_MODEL), lambda i, te, fi, nx, sl, nu: (jnp.minimum(i, nu[0] - 1), 0)),
                      hbm, hbm, hbm],
            out_specs=pl.BlockSpec((tm, D_MODEL), lambda i, te, fi, nx, sl, nu: (i, 0)),
            scratch_shapes=[
                pltpu.VMEM((2, D_MODEL, D_FF), BF16), pltpu.VMEM((2, D_MODEL, D_FF), BF16),
                pltpu.VMEM((2, D_FF, D_MODEL), BF16),
                pltpu.VMEM((2, D_MODEL, FF_CHUNK), F32), pltpu.VMEM((2, D_MODEL, FF_CHUNK), F32),
                pltpu.VMEM((2, FF_CHUNK, D_MODEL), F32),
                pltpu.SemaphoreType.DMA((2, 3)),
            ],
        ),
        compiler_params=_params(("arbitrary",), EXPERTS_VMEM_LIMIT),
        name="experts",
    )(tile_expert, first, nxt, slot, n_used, xs, w1, w3, w2)


def _combine_body(n_first, dest_ref, next_ref, x_ref, mw_ref, g_ref, b_ref, ys_ref, *rest):
    out_refs, (y0_ref, y1_ref, sem_ref) = rest[:-3], rest[-3:]
    t = x_ref.shape[0]
    i = pl.program_id(0)
    cur = i % 2

    def gather(idx_ref, buf):
        def issue(r, carry):
            pltpu.make_async_copy(ys_ref.at[pl.ds(idx_ref[r], 1), :], y0_ref.at[buf, pl.ds(r, 1), :],
                                  sem_ref.at[buf]).start(priority=0)
            pltpu.make_async_copy(ys_ref.at[pl.ds(idx_ref[t + r], 1), :], y1_ref.at[buf, pl.ds(r, 1), :],
                                  sem_ref.at[buf]).start(priority=1)
            return carry

        lax.fori_loop(0, t, issue, 0, unroll=ISSUE_UNROLL)

    @pl.when(i == 0)
    def _():
        gather(dest_ref, 0)

    @pl.when(i + 1 < pl.num_programs(0))
    def _():
        gather(next_ref, 1 - cur)

    pltpu.make_async_copy(ys_ref.at[pl.ds(0, t), :], y0_ref.at[cur], sem_ref.at[cur]).wait()
    pltpu.make_async_copy(ys_ref.at[pl.ds(0, t), :], y1_ref.at[cur], sem_ref.at[cur]).wait()
    mw = mw_ref[...]
    f = mw[:, 0:1] * y0_ref[cur] + mw[:, 1:2] * y1_ref[cur]
    out = _layer_norm(DN_ALPHA * x_ref[...] + f, g_ref[...], b_ref[...])
    if len(out_refs) == 1:
        out_refs[0][...] = out
    else:
        @pl.when(i < n_first)
        def _():
            out_refs[0][...] = out

        @pl.when(i == n_first)
        def _():
            out_refs[1][...] = out


def _combine(x, ys, dest, mw, g, b, t, split):
    n = x.shape[0]
    zero = lambda i: (0, 0)
    n_first = n // t - 1
    last = n // t - 1
    if split:
        out_shape = (jax.ShapeDtypeStruct((n - t, D_MODEL), F32), jax.ShapeDtypeStruct((t, D_MODEL), F32))
        out_specs = (pl.BlockSpec((t, D_MODEL), lambda i: (jnp.minimum(i, n_first - 1), 0)),
                     pl.BlockSpec((t, D_MODEL), zero))
    else:
        out_shape = jax.ShapeDtypeStruct((n, D_MODEL), F32)
        out_specs = pl.BlockSpec((t, D_MODEL), lambda i: (i, 0))
    return pl.pallas_call(
        functools.partial(_combine_body, n_first),
        out_shape=out_shape,
        grid=(n // t,),
        in_specs=[
            pl.BlockSpec((2 * t,), lambda i: (i,), memory_space=pltpu.SMEM),
            pl.BlockSpec((2 * t,), lambda i: (jnp.minimum(i + 1, last),), memory_space=pltpu.SMEM),
            pl.BlockSpec((t, D_MODEL), lambda i: (i, 0)),
            pl.BlockSpec((t, 2), lambda i: (i, 0)),
            pl.BlockSpec((1, D_MODEL), zero),
            pl.BlockSpec((1, D_MODEL), zero),
            pl.BlockSpec(memory_space=pl.ANY),
        ],
        out_specs=out_specs,
        scratch_shapes=[pltpu.VMEM((2, t, D_MODEL), F32), pltpu.VMEM((2, t, D_MODEL), F32),
                        pltpu.SemaphoreType.DMA((2,))],
        compiler_params=_params(("arbitrary",)),
        name="combine",
    )(dest, dest, x, mw, g, b, ys)


def _moe(x, mi, mw, cnt, w1, w3, w2, g, b, t, split):
    n = x.shape[0]
    tm = EXPERT_TILE
    n_tiles = (2 * n + N_EXPERTS * (tm - 1)) // tm
    counts = cnt[:N_EXPERTS, 0]
    tiles = (counts + tm - 1) // tm
    tile_end = jnp.cumsum(tiles)
    start = (tile_end - tiles) * tm
    n_used = tile_end[-1:].astype(jnp.int32)
    tile_expert = jnp.sum((tile_end[None, :] <= jnp.arange(n_tiles)[:, None]).astype(jnp.int32), axis=1)
    tile_expert = jnp.minimum(tile_expert, N_EXPERTS - 1)
    last_tile = jnp.where(tiles > 0, tile_end - 1, -1).astype(jnp.int32)
    has = tiles > 0
    next_of = [jnp.int32(-1)] * N_EXPERTS
    for e in range(N_EXPERTS - 2, -1, -1):
        next_of[e] = jnp.where(has[e + 1], e + 1, next_of[e + 1])
    per_expert = jnp.stack([tile_end - tiles, jnp.stack(next_of), (jnp.cumsum(has) - 1) % 2]).astype(jnp.int32)
    onehot = (tile_expert[None, :, None] == jnp.arange(N_EXPERTS)[None, None, :]).astype(jnp.int32)
    first_tile, nxt, slot = jnp.sum(onehot * per_expert[:, None, :], axis=2)
    first = (first_tile == jnp.arange(n_tiles)).astype(jnp.int32)
    experts = mi[:, 0:2, :]
    group_start = sum(jnp.where(experts == e, start[e], 0) for e in range(N_EXPERTS))
    dest = (group_start + mi[:, 2:4, :]).astype(jnp.int32).reshape(2 * n)
    mw = jnp.swapaxes(mw[:, 0:2, :], 1, 2).reshape(n, 2)
    xs = _dispatch(x, dest, last_tile, n_used, n_tiles, t)
    ys = _experts(xs, tile_expert.astype(jnp.int32), first, nxt, slot, n_used, w1, w3, w2)
    return _combine(x, ys, dest, mw, g, b, t, split)


def _block_diag(w):
    layers, g, c, _ = w.shape
    out = jnp.zeros((layers, g * c, g * c), w.dtype)
    for i in range(g):
        out = out.at[:, i * c:(i + 1) * c, i * c:(i + 1) * c].set(w[:, i])
    return out


def _bias_pairs(table, t_q, n_keys, offset):
    layers = table.shape[0]
    hi = t_q - 1 + offset
    span = t_q - 1 + n_keys
    cols = np.clip(hi - np.arange(span), -REL_CLIP, REL_CLIP) + REL_CLIP
    rev = table.reshape(layers * N_HEADS, -1)[:, cols].astype(F32)
    skew = jnp.tile(rev, (1, t_q + 1))[:, :t_q * (span + 1)].reshape(layers * N_HEADS, t_q, span + 1)
    slab = skew[:, ::-1, :n_keys]
    return slab.reshape(layers, HEAD_PAIRS, 2 * t_q, n_keys)


def _prompt_bias_slabs(table):
    bias = _bias_pairs(table, CHUNK, BAND, ATTN_REACH)
    col = np.arange(BAND)[None, :]
    first_valid = np.concatenate([[0], (LEFT_CHUNKS - np.arange(LEFT_CHUNKS)) * CHUNK])[:, None]
    valid = jnp.asarray(col >= first_valid)
    return jnp.where(valid[None, :, None, None, :], bias[:, None], NEG_INF)


def kernel(x_prompt, x_sample, cache_k, cache_v, state_pool, w_in, b_in, w_pool_grp, pool_scale,
           rel_table, w_pool_br, w_attn_br, w_out, ln1_g, ln1_b, ln2_g, ln2_b,
           w1_dense, w3_dense, w2_dense, w_router, b_router, w1_exp, w3_exp, w2_exp):
    bp, tp, d = x_prompt.shape
    bs, ts, _ = x_sample.shape
    n_p, n_s = bp * tp, bs * ts
    n = n_p + n_s
    depth = w_in.shape[0]
    keep_s = cache_k.shape[2]
    keep_p = min(ATTN_REACH, tp)

    x = (x_prompt.reshape(n_p, d), x_sample.reshape(n_s, d))
    ck = cache_k.reshape(depth, bs, keep_s, ATTN_WIDTH)
    cv = cache_v.reshape(depth, bs, keep_s, ATTN_WIDTH)
    hist = jnp.pad(state_pool, ((0, 0), (0, 0), (HIST_ROWS - POOL_HIST, 0), (0, 0)))
    assert tp % ATTN_REACH == 0 and keep_p == ATTN_REACH and ts >= POOL_HIST

    rows = lambda v: v[:, None, :].astype(F32)
    w_in_bf, b_in_r = _reorder_in_columns(w_in).astype(BF16), rows(_reorder_in_columns(b_in))
    wgrp, wp_bf, wa_bf, wo_bf = (a.astype(BF16) for a in (_block_diag(w_pool_grp), w_pool_br, w_attn_br, w_out))
    scale_r, g1, b1, g2, b2 = (rows(a) for a in (pool_scale, ln1_g, ln1_b, ln2_g, ln2_b))
    dense_bf = [a.astype(BF16) for a in (w1_dense, w3_dense, w2_dense)]
    bias_p = _prompt_bias_slabs(rel_table)
    bias_s = _bias_pairs(rel_table, ts, keep_s + ts, keep_s)

    k_buf = v_buf = None
    kp_new, vp_new, pp_new, ps_new = [], [], [], []
    for l in range(depth):
        weights = (wgrp[l], scale_r[l], wp_bf[l], wa_bf[l], wo_bf[l], g1[l], b1[l])
        z32, z16 = _in_proj(x, w_in_bf[l], b_in_r[l], n_s)
        attn_p, k_tail, v_tail = _attn_prompt(z32, z16, bias_p[l], bp, tp)
        attn_s, k_buf, v_buf = _attn_sample(z32, z16, ck, cv, bias_s[l], l, n_p, bs, ts, k_buf, v_buf)
        j = l // 2
        last = l == depth - 1
        if l % 2 == 0:
            x1, u_tail = _mix(z32, z16, attn_p, attn_s, x, hist[l], weights, n_p, tp)
            x = _ffn_dense(x1, dense_bf[0][j], dense_bf[1][j], dense_bf[2][j], g2[l], b2[l])
            if last:
                x = (x[:n_p], x[n_p:])
        else:
            x1, u_tail, mi, mw, cnt = _mix(z32, z16, attn_p, attn_s, x, hist[l], weights, n_p, tp,
                                           router=_router_operands(w_router[j], b_router[j]))
            x = _moe(x1, mi, mw, cnt, w1_exp[j], w3_exp[j], w2_exp[j], g2[l], b2[l], n_s, split=last)

        kp_new.append(k_tail.reshape(bp, keep_p, N_HEADS, HEAD_DIM))
        vp_new.append(v_tail.reshape(bp, keep_p, N_HEADS, HEAD_DIM))
        pp_new.append(u_tail[:, HIST_ROWS - POOL_HIST:])
        ps_new.append(z32[n_p:, :POOL_WIDTH].reshape(bs, ts, POOL_WIDTH)[:, ts - POOL_HIST:])

    shape_s = (depth, bs, keep_s, N_HEADS, HEAD_DIM)
    return (x[0].reshape(bp, tp, d), x[1].reshape(bs, ts, d),
            jnp.stack(kp_new), jnp.stack(vp_new), jnp.stack(pp_new),
            k_buf.reshape(shape_s), v_buf.reshape(shape_s), jnp.stack(ps_new))
```

```python
import functools

import jax
import jax.numpy as jnp
import numpy as np
from jax import lax
from jax.experimental import pallas as pl
from jax.experimental.pallas import tpu as pltpu

F32 = jnp.float32
BF16 = jnp.bfloat16

D_MODEL = 1024
N_HEADS = 8
HEAD_DIM = 64
ATTN_WIDTH = N_HEADS * HEAD_DIM
CHUNK = 64
LEFT_CHUNKS = 8
BAND = (LEFT_CHUNKS + 1) * CHUNK
ATTN_REACH = LEFT_CHUNKS * CHUNK
REL_CLIP = 256
ATTN_SCALE = HEAD_DIM ** -0.5
POOL_WIDTH = 512
POOL_WINDOWS = (2, 4, 8, 16)
POOL_GROUP = POOL_WIDTH // len(POOL_WINDOWS)
POOL_HIST = max(POOL_WINDOWS) - 1
HIST_ROWS = POOL_HIST + 1
LEAD_ROWS = 8
D_FF = 2816
N_EXPERTS = 8
PAST_LEN = 4096
DEPTH = 2
DN_ALPHA = (2 * DEPTH) ** 0.25
LN_EPS = 1e-5
NEG_INF = -1e30
IN_WIDTH = POOL_WIDTH + 3 * ATTN_WIDTH + 2 * D_MODEL
HEAD_PAIRS = N_HEADS // 2
PAIR_W = 2 * HEAD_DIM
LANES = 128
ROUTE_ROWS = 16

Z32_WIDTH = POOL_WIDTH + 2 * ATTN_WIDTH
Z16_WIDTH = 2 * D_MODEL + 3 * ATTN_WIDTH
COL_U, COL_K, COL_V = 0, 1, 2
COL_GP, COL_GA = 0, 1
COL_Q = 2 * D_MODEL // ATTN_WIDTH
COL_K16, COL_V16 = COL_Q + 1, COL_Q + 2
IN_SPLITS = (0, POOL_WIDTH, POOL_WIDTH + ATTN_WIDTH, POOL_WIDTH + 3 * ATTN_WIDTH, IN_WIDTH)

EXPERT_TILE = 512
FF_CHUNK = 256
FF_PIECES = D_FF // FF_CHUNK
ISSUE_UNROLL = 8
VMEM_LIMIT = 56 * 1024 * 1024
EXPERTS_VMEM_LIMIT = 62 * 1024 * 1024


def _pick(n, candidates):
    for c in candidates:
        if n % c == 0:
            return c
    raise ValueError(f"no tile in {candidates} divides {n}")


def _params(sem, vmem=None):
    return pltpu.CompilerParams(dimension_semantics=sem, vmem_limit_bytes=vmem or VMEM_LIMIT)


def _layer_norm(r, g, b):
    mu = jnp.mean(r, axis=-1, keepdims=True)
    c = r - mu
    var = jnp.mean(c * c, axis=-1, keepdims=True)
    return c * lax.rsqrt(var + LN_EPS) * g + b


def _row_sources(x, t):
    if not isinstance(x, tuple):
        return [x], [pl.BlockSpec((t, x.shape[1]), lambda i: (i, 0))], (lambda refs, i: refs[0][...])
    first, last = x
    assert last.shape[0] == t and first.shape[0] % t == 0
    n_first = first.shape[0] // t
    specs = [pl.BlockSpec((t, first.shape[1]), lambda i: (jnp.minimum(i, n_first - 1), 0)),
             pl.BlockSpec((t, last.shape[1]), lambda i: (0, 0))]
    return [first, last], specs, (lambda refs, i: jnp.where(i < n_first, refs[0][...], refs[1][...]))


def _inproj_body(n_src, select, *refs):
    w_ref, b_ref, z32_ref, z16_ref = refs[n_src:]
    x = select(refs[:n_src], pl.program_id(0)).astype(BF16)
    z = jnp.dot(x, w_ref[...], preferred_element_type=F32) + b_ref[...]
    u, q, kv, gates = (z[:, a:b] for a, b in zip(IN_SPLITS[:-1], IN_SPLITS[1:]))
    z32_ref[...] = jnp.concatenate([u, kv], axis=-1)
    z16_ref[...] = jnp.concatenate([gates, q, kv], axis=-1).astype(BF16)


def _in_proj(x, w_bf, b, t):
    srcs, specs, select = _row_sources(x, t)
    n = sum(s.shape[0] for s in srcs)
    zero = lambda i: (0, 0)
    return pl.pallas_call(
        functools.partial(_inproj_body, len(srcs), select),
        out_shape=(jax.ShapeDtypeStruct((n, Z32_WIDTH), F32), jax.ShapeDtypeStruct((n, Z16_WIDTH), BF16)),
        grid=(n // t,),
        in_specs=specs + [pl.BlockSpec((D_MODEL, IN_WIDTH), zero), pl.BlockSpec((1, IN_WIDTH), zero)],
        out_specs=(pl.BlockSpec((t, Z32_WIDTH), lambda i: (i, 0)), pl.BlockSpec((t, Z16_WIDTH), lambda i: (i, 0))),
        compiler_params=_params(("arbitrary",)),
        name="in_proj",
    )(*srcs, w_bf, b)


def _pair_scores(q_pair, k_pair, bias):
    lane = lax.broadcasted_iota(jnp.int32, q_pair.shape, 1)
    qs = q_pair.astype(F32) * ATTN_SCALE
    q2 = jnp.concatenate([jnp.where(lane < HEAD_DIM, qs, 0.0), jnp.where(lane >= HEAD_DIM, qs, 0.0)], axis=0)
    s = lax.dot_general(q2.astype(BF16), k_pair, (((1,), (1,)), ((), ())), preferred_element_type=F32)
    return s + bias


def _pair_output(s, v_pair):
    rows = s.shape[0] // 2
    m = jnp.max(s, axis=-1, keepdims=True)
    e = jnp.exp(s - m)
    l = jnp.sum(e, axis=-1, keepdims=True)
    o2 = jnp.dot(e.astype(BF16), v_pair, preferred_element_type=F32) / l
    lane = lax.broadcasted_iota(jnp.int32, (rows, PAIR_W), 1)
    return jnp.where(lane < HEAD_DIM, o2[:rows], o2[rows:])


def _attn_prompt_body(q_ref, kp_ref, kc_ref, vp_ref, vc_ref, kl_ref, vl_ref, bias_ref,
                      o_ref, kt_ref, vt_ref, kext_ref, vext_ref):
    blk = q_ref.shape[0]
    j = pl.program_id(1)

    @pl.when(j == pl.num_programs(1) - 1)
    def _():
        kt_ref[0] = kl_ref[...]
        vt_ref[0] = vl_ref[...]

    kext_ref[0:blk, :] = kp_ref[...]
    kext_ref[blk:2 * blk, :] = kc_ref[...]
    vext_ref[0:blk, :] = vp_ref[...]
    vext_ref[blk:2 * blk, :] = vc_ref[...]
    def chunk(c, carry):
        q0 = pl.multiple_of(c * CHUNK, CHUNK)
        slab = jnp.where(j == 0, c + 1, 0)
        pairs = [slice(hp * PAIR_W, (hp + 1) * PAIR_W) for hp in range(HEAD_PAIRS)]
        s = jnp.concatenate([_pair_scores(q_ref[pl.ds(q0, CHUNK), lanes], kext_ref[pl.ds(q0, BAND), lanes],
                                          bias_ref[slab, hp]) for hp, lanes in enumerate(pairs)], axis=0)
        m = jnp.max(s, axis=-1, keepdims=True)
        e = jnp.exp(s - m)
        inv = 1.0 / jnp.sum(e, axis=-1, keepdims=True)
        p = e.astype(BF16)
        lane = lax.broadcasted_iota(jnp.int32, (CHUNK, PAIR_W), 1)
        for hp, lanes in enumerate(pairs):
            rows = slice(hp * 2 * CHUNK, (hp + 1) * 2 * CHUNK)
            o2 = jnp.dot(p[rows], vext_ref[pl.ds(q0, BAND), lanes], preferred_element_type=F32) * inv[rows]
            o_ref[pl.ds(q0, CHUNK), lanes] = jnp.where(lane < HEAD_DIM, o2[:CHUNK], o2[CHUNK:])
        return carry

    lax.fori_loop(0, blk // CHUNK, chunk, 0, unroll=8)


def _attn_prompt(z32, z16, bias_pairs, batch, seq):
    blk = ATTN_REACH
    per_seq = seq // blk
    rows = batch * seq

    def cur(col):
        return lambda b, j: (b * per_seq + j, col)

    def prev(col):
        return lambda b, j: (b * per_seq + jnp.maximum(j - 1, 0), col)

    def last(col):
        return lambda b, j: (b * per_seq + per_seq - 1, col)

    tail = jax.ShapeDtypeStruct((batch, blk, ATTN_WIDTH), F32)
    tail_spec = pl.BlockSpec((1, blk, ATTN_WIDTH), lambda b, j: (b, 0, 0))
    return pl.pallas_call(
        _attn_prompt_body,
        out_shape=(jax.ShapeDtypeStruct((rows, ATTN_WIDTH), F32), tail, tail),
        grid=(batch, per_seq),
        in_specs=[
            pl.BlockSpec((blk, ATTN_WIDTH), cur(COL_Q)),
            pl.BlockSpec((blk, ATTN_WIDTH), prev(COL_K16)),
            pl.BlockSpec((blk, ATTN_WIDTH), cur(COL_K16)),
            pl.BlockSpec((blk, ATTN_WIDTH), prev(COL_V16)),
            pl.BlockSpec((blk, ATTN_WIDTH), cur(COL_V16)),
            pl.BlockSpec((blk, ATTN_WIDTH), last(COL_K)),
            pl.BlockSpec((blk, ATTN_WIDTH), last(COL_V)),
            pl.BlockSpec((1 + LEFT_CHUNKS, HEAD_PAIRS, 2 * CHUNK, BAND), lambda b, j: (0, 0, 0, 0)),
        ],
        out_specs=(pl.BlockSpec((blk, ATTN_WIDTH), lambda b, j: (b * per_seq + j, 0)), tail_spec, tail_spec),
        scratch_shapes=[pltpu.VMEM((2 * blk, ATTN_WIDTH), BF16), pltpu.VMEM((2 * blk, ATTN_WIDTH), BF16)],
        compiler_params=_params(("arbitrary", "arbitrary")),
        name="attn_prompt",
    )(z16, z16, z16, z16, z16, z32, z32, bias_pairs)


def _attn_sample_body(q_ref, kn_ref, vn_ref, ck_ref, cv_ref, bias_ref, *rest):
    o_ref, ko_ref, vo_ref, kall_ref, vall_ref = rest[-5:]
    keep = ck_ref.shape[2]
    t = q_ref.shape[0]
    ck = ck_ref[0, 0]
    cv = cv_ref[0, 0]
    kn = kn_ref[...]
    vn = vn_ref[...]
    kall_ref[0:keep, :] = ck.astype(BF16)
    kall_ref[keep:keep + t, :] = kn.astype(BF16)
    vall_ref[0:keep, :] = cv.astype(BF16)
    vall_ref[keep:keep + t, :] = vn.astype(BF16)
    for hp in range(HEAD_PAIRS):
        lanes = slice(hp * PAIR_W, (hp + 1) * PAIR_W)
        s = _pair_scores(q_ref[:, lanes], kall_ref[:, lanes], bias_ref[hp])
        o_ref[:, lanes] = _pair_output(s, vall_ref[:, lanes])
    ko_ref[0, 0, 0:keep - t, :] = ck[t:keep]
    ko_ref[0, 0, keep - t:keep, :] = kn
    vo_ref[0, 0, 0:keep - t, :] = cv[t:keep]
    vo_ref[0, 0, keep - t:keep, :] = vn
    for later in range(1, ko_ref.shape[0]):
        ko_ref[later] = jnp.zeros(ko_ref.shape[1:], F32)
        vo_ref[later] = jnp.zeros(vo_ref.shape[1:], F32)


def _attn_sample(z32, z16, cache_k, cache_v, bias_pairs, layer, row0, streams, t, k_buf, v_buf):
    depth, _, keep, _ = cache_k.shape
    blk0 = row0 // t
    ins = [z16, z32, z32, cache_k, cache_v, bias_pairs]
    in_specs = [
        pl.BlockSpec((t, ATTN_WIDTH), lambda s: (blk0 + s, COL_Q)),
        pl.BlockSpec((t, ATTN_WIDTH), lambda s: (blk0 + s, COL_K)),
        pl.BlockSpec((t, ATTN_WIDTH), lambda s: (blk0 + s, COL_V)),
        pl.BlockSpec((1, 1, keep, ATTN_WIDTH), lambda s: (layer, s, 0, 0)),
        pl.BlockSpec((1, 1, keep, ATTN_WIDTH), lambda s: (layer, s, 0, 0)),
        pl.BlockSpec((HEAD_PAIRS, 2 * t, keep + t), lambda s: (0, 0, 0)),
    ]
    aliases = {}
    if k_buf is None:
        assert layer == 0
        buf_spec = pl.BlockSpec((depth, 1, keep, ATTN_WIDTH), lambda s: (0, s, 0, 0))
    else:
        aliases = {len(ins): 1, len(ins) + 1: 2}
        ins += [k_buf, v_buf]
        in_specs += [pl.BlockSpec(memory_space=pl.ANY), pl.BlockSpec(memory_space=pl.ANY)]
        buf_spec = pl.BlockSpec((1, 1, keep, ATTN_WIDTH), lambda s: (layer, s, 0, 0))
    buf = jax.ShapeDtypeStruct(cache_k.shape, F32)
    return pl.pallas_call(
        _attn_sample_body,
        out_shape=(jax.ShapeDtypeStruct((streams * t, ATTN_WIDTH), F32), buf, buf),
        grid=(streams,),
        in_specs=in_specs,
        out_specs=(
            pl.BlockSpec((t, ATTN_WIDTH), lambda s: (s, 0)),
            buf_spec,
            buf_spec,
        ),
        scratch_shapes=[pltpu.VMEM((keep + t, ATTN_WIDTH), BF16), pltpu.VMEM((keep + t, ATTN_WIDTH), BF16)],
        input_output_aliases=aliases,
        compiler_params=_params(("arbitrary",)),
        name="attn_sample",
    )(*ins)


def _merge(pooled, u, attn, gp, ga, x, wgrp_ref, scale_ref, wp_ref, wa_ref, wo_ref, g_ref, b_ref):
    pooled = pooled - u
    pool_y = jnp.dot(pooled.astype(BF16), wgrp_ref[...], preferred_element_type=F32) * scale_ref[...]
    mp = jnp.dot(pool_y.astype(BF16), wp_ref[...], preferred_element_type=F32)
    ma = jnp.dot(attn.astype(BF16), wa_ref[...], preferred_element_type=F32)
    m = jax.nn.sigmoid(gp) * mp + jax.nn.sigmoid(ga) * ma
    y = jnp.dot(m.astype(BF16), wo_ref[...], preferred_element_type=F32)
    return _layer_norm(DN_ALPHA * x + y, g_ref[...], b_ref[...])


def _window_means(read, pos, shape_out):
    outs = []
    for g, w in enumerate(POOL_WINDOWS):
        lanes = slice(g * POOL_GROUP, (g + 1) * POOL_GROUP)
        s = read(0, lanes)
        for back in range(1, w):
            s = s + read(back, lanes)
        outs.append((s / jnp.minimum(pos + 1, w).astype(F32)).reshape(shape_out))
    return jnp.concatenate(outs, axis=-1)


def _window_means_doubling(buf_ref, lvl_ref, pos, t):
    g = POOL_GROUP
    lo, hi = LEAD_ROWS, LEAD_ROWS + HIST_ROWS + t
    out0 = LEAD_ROWS + HIST_ROWS
    lvl_ref[0, lo:hi, :] = buf_ref[lo:hi, :] + buf_ref[lo - 1:hi - 1, :]
    lvl_ref[1, lo:hi, g:] = lvl_ref[0, lo:hi, g:] + lvl_ref[0, lo - 2:hi - 2, g:]
    lvl_ref[2, lo:hi, 2 * g:] = lvl_ref[1, lo:hi, 2 * g:] + lvl_ref[1, lo - 4:hi - 4, 2 * g:]
    s16 = lvl_ref[2, out0:out0 + t, 3 * g:] + lvl_ref[2, out0 - 8:out0 - 8 + t, 3 * g:]
    sums = [lvl_ref[k, out0:out0 + t, k * g:(k + 1) * g] for k in range(3)] + [s16]
    return jnp.concatenate([s / jnp.minimum(pos + 1, w).astype(F32) for s, w in zip(sums, POOL_WINDOWS)], axis=-1)


def _mix_body(n_prompt_tiles, tiles_per_seq, with_router, n_src, select,
              u_ref, up_ref, hist_ref, attn_p_ref, attn_s_ref, gp_ref, ga_ref, *rest):
    x_refs, rest = rest[:n_src], rest[n_src:]
    (wgrp_ref, scale_ref, wp_ref, wa_ref, wo_ref, g_ref, b_ref), rest = rest[:7], rest[7:]
    if with_router:
        (wr_hi_ref, wr_lo_ref, br_ref, o_ref, tail_ref, mi_ref, mw_ref, cnt_ref,
         buf_ref, lvl_ref, sbuf_ref, pooled_ref, attn_ref, carry_ref) = rest
    else:
        o_ref, tail_ref, buf_ref, lvl_ref, sbuf_ref, pooled_ref, attn_ref = rest
    t = u_ref.shape[0]
    i = pl.program_id(0)
    u = u_ref[...]

    @pl.when(i == 0)
    def _():
        buf_ref[0:LEAD_ROWS, :] = jnp.zeros((LEAD_ROWS, POOL_WIDTH), F32)
        lvl_ref[:, 0:LEAD_ROWS, :] = jnp.zeros((lvl_ref.shape[0], LEAD_ROWS, POOL_WIDTH), F32)

    @pl.when(i < n_prompt_tiles)
    def _():
        tile = i % tiles_per_seq
        h0 = LEAD_ROWS
        buf_ref[h0:h0 + HIST_ROWS, :] = jnp.where(tile == 0, 0.0, up_ref[...])
        buf_ref[h0 + HIST_ROWS:h0 + HIST_ROWS + t, :] = u
        pos = tile * t + lax.broadcasted_iota(jnp.int32, (t, 1), 0)
        pooled_ref[...] = _window_means_doubling(buf_ref, lvl_ref, pos, t)
        attn_ref[...] = attn_p_ref[...]

        @pl.when(tile == tiles_per_seq - 1)
        def _():
            tail_ref[0] = u[t - HIST_ROWS:, :]

    @pl.when(i == n_prompt_tiles)
    def _():
        streams, hrows, _ = hist_ref.shape
        ts = t // streams
        sbuf_ref[:, 0:hrows, :] = hist_ref[...]
        sbuf_ref[:, hrows:hrows + ts, :] = u.reshape(streams, ts, POOL_WIDTH)
        pos = PAST_LEN + lax.broadcasted_iota(jnp.int32, (1, ts, 1), 1)
        read = lambda back, lanes: sbuf_ref[:, hrows - back:hrows - back + ts, lanes]
        pooled_ref[...] = _window_means(read, pos, (t, POOL_GROUP))
        attn_ref[...] = attn_s_ref[...]

    x1 = _merge(pooled_ref[...], u, attn_ref[...], gp_ref[...].astype(F32), ga_ref[...].astype(F32),
                select(x_refs, i), wgrp_ref, scale_ref, wp_ref, wa_ref, wo_ref, g_ref, b_ref)
    o_ref[...] = x1
    if with_router:
        _route(x1, wr_hi_ref, wr_lo_ref, br_ref, mi_ref, mw_ref, cnt_ref, carry_ref)


def _mix(z32, z16, attn_p, attn_s, x, hist, weights, n_prompt, seq, router=None):
    n = z32.shape[0]
    t = n - n_prompt
    x_srcs, x_specs, select = _row_sources(x, t)
    n_prompt_tiles = n_prompt // t
    streams, hrows, _ = hist.shape
    hist_per_tile = t // HIST_ROWS
    last_p = n_prompt_tiles - 1
    zero = lambda i: (0, 0)
    rows = lambda width: pl.BlockSpec((t, width), lambda i: (i, 0))
    tiles_per_seq = seq // t
    out_shape = [jax.ShapeDtypeStruct((n, D_MODEL), F32),
                 jax.ShapeDtypeStruct((n_prompt // seq, HIST_ROWS, POOL_WIDTH), F32)]
    out_specs = [rows(D_MODEL),
                 pl.BlockSpec((1, HIST_ROWS, POOL_WIDTH), lambda i: (jnp.minimum(i, last_p) // tiles_per_seq, 0, 0))]
    router_specs, router_scratch = [], []
    if router is not None:
        router_specs = [pl.BlockSpec((ROUTE_ROWS, D_MODEL), zero), pl.BlockSpec((ROUTE_ROWS, D_MODEL), zero),
                        pl.BlockSpec((ROUTE_ROWS, 1), zero)]
        per_tile = lambda dtype: jax.ShapeDtypeStruct((n // t, 8, t), dtype)
        out_shape += [per_tile(jnp.int32), per_tile(F32), jax.ShapeDtypeStruct((ROUTE_ROWS, LANES), jnp.int32)]
        out_specs += [pl.BlockSpec((1, 8, t), lambda i: (i, 0, 0)), pl.BlockSpec((1, 8, t), lambda i: (i, 0, 0)),
                      pl.BlockSpec((ROUTE_ROWS, LANES), zero)]
        router_scratch = [pltpu.VMEM((ROUTE_ROWS, LANES), F32)]
    out = pl.pallas_call(
        functools.partial(_mix_body, n_prompt_tiles, seq // t, router is not None, len(x_srcs), select),
        out_shape=out_shape,
        grid=(n_prompt_tiles + 1,),
        in_specs=[
            pl.BlockSpec((t, POOL_WIDTH), lambda i: (i, COL_U)),
            pl.BlockSpec((HIST_ROWS, POOL_WIDTH), lambda i: (jnp.maximum(i * hist_per_tile - 1, 0), COL_U)),
            pl.BlockSpec((streams, hrows, POOL_WIDTH), lambda i: (0, 0, 0)),
            pl.BlockSpec((t, ATTN_WIDTH), lambda i: (jnp.minimum(i, last_p), 0)),
            pl.BlockSpec((t, ATTN_WIDTH), zero),
            pl.BlockSpec((t, D_MODEL), lambda i: (i, COL_GP)),
            pl.BlockSpec((t, D_MODEL), lambda i: (i, COL_GA)),
        ] + x_specs + [
            pl.BlockSpec((POOL_WIDTH, POOL_WIDTH), zero),
            pl.BlockSpec((1, POOL_WIDTH), zero),
            pl.BlockSpec((POOL_WIDTH, D_MODEL), zero),
            pl.BlockSpec((ATTN_WIDTH, D_MODEL), zero),
            pl.BlockSpec((D_MODEL, D_MODEL), zero),
            pl.BlockSpec((1, D_MODEL), zero),
            pl.BlockSpec((1, D_MODEL), zero),
        ] + router_specs,
        out_specs=out_specs,
        scratch_shapes=[
            pltpu.VMEM((LEAD_ROWS + HIST_ROWS + t, POOL_WIDTH), F32),
            pltpu.VMEM((3, LEAD_ROWS + HIST_ROWS + t, POOL_WIDTH), F32),
            pltpu.VMEM((streams, hrows + t // streams, POOL_WIDTH), F32),
            pltpu.VMEM((t, POOL_WIDTH), F32),
            pltpu.VMEM((t, ATTN_WIDTH), F32),
        ] + router_scratch,
        compiler_params=_params(("arbitrary",)),
        name="mix",
    )(z32, z32, hist, attn_p, attn_s, z16, z16, *x_srcs, *weights, *(router or ()))
    return out


def _ffn_dense_body(x_ref, w1_ref, w3_ref, w2_ref, g_ref, b_ref, o_ref):
    x = x_ref[...]
    xb = x.astype(BF16)
    a = jnp.dot(xb, w1_ref[...], preferred_element_type=F32)
    c = jnp.dot(xb, w3_ref[...], preferred_element_type=F32)
    h = (jax.nn.silu(a) * c).astype(BF16)
    f = jnp.dot(h, w2_ref[...], preferred_element_type=F32)
    o_ref[...] = _layer_norm(DN_ALPHA * x + f, g_ref[...], b_ref[...])


def _ffn_dense(x, w1, w3, w2, g, b):
    n = x.shape[0]
    t = 256
    zero = lambda i: (0, 0)
    return pl.pallas_call(
        _ffn_dense_body,
        out_shape=jax.ShapeDtypeStruct((n, D_MODEL), F32),
        grid=(n // t,),
        in_specs=[
            pl.BlockSpec((t, D_MODEL), lambda i: (i, 0)),
            pl.BlockSpec((D_MODEL, D_FF), zero),
            pl.BlockSpec((D_MODEL, D_FF), zero),
            pl.BlockSpec((D_FF, D_MODEL), zero),
            pl.BlockSpec((1, D_MODEL), zero),
            pl.BlockSpec((1, D_MODEL), zero),
        ],
        out_specs=pl.BlockSpec((t, D_MODEL), lambda i: (i, 0)),
        compiler_params=_params(("arbitrary",)),
        name="ffn_dense",
    )(x, w1, w3, w2, g, b)


def _route(x, wr_hi_ref, wr_lo_ref, br_ref, mi_ref, mw_ref, cnt_ref, carry_ref):
    t = x.shape[0]

    @pl.when(pl.program_id(0) == 0)
    def _():
        carry_ref[...] = jnp.zeros_like(carry_ref)

    x_hi = x.astype(BF16)
    x_lo = (x - x_hi.astype(F32)).astype(BF16)
    nt = (((1,), (1,)), ((), ()))
    logits = (lax.dot_general(wr_hi_ref[...], x_hi, nt, preferred_element_type=F32)
              + lax.dot_general(wr_hi_ref[...], x_lo, nt, preferred_element_type=F32)
              + lax.dot_general(wr_lo_ref[...], x_hi, nt, preferred_element_type=F32)) + br_ref[...]
    row = lax.broadcasted_iota(jnp.int32, (ROUTE_ROWS, t), 0)
    row_f = row.astype(F32)
    logits = jnp.where(row < N_EXPERTS, logits, -jnp.inf)
    v0 = jnp.max(logits, axis=0, keepdims=True)
    e0 = jnp.min(jnp.where(logits == v0, row_f, float(ROUTE_ROWS)), axis=0, keepdims=True)
    rest = jnp.where(row_f == e0, -jnp.inf, logits)
    v1 = jnp.max(rest, axis=0, keepdims=True)
    e1 = jnp.min(jnp.where(rest == v1, row_f, float(ROUTE_ROWS)), axis=0, keepdims=True)
    ex = jnp.exp(v1 - v0)
    w0 = 1.0 / (1.0 + ex)
    w1 = ex / (1.0 + ex)
    oh0 = (row_f == e0).astype(F32)
    oh1 = (row_f == e1).astype(F32)
    before = (lax.broadcasted_iota(jnp.int32, (t, t), 0) < lax.broadcasted_iota(jnp.int32, (t, t), 1)).astype(BF16)
    pre0 = jnp.dot(oh0.astype(BF16), before, preferred_element_type=F32)
    pre1 = jnp.dot(oh1.astype(BF16), before, preferred_element_type=F32)
    cnt0 = jnp.sum(oh0, axis=1, keepdims=True)
    cnt1 = jnp.sum(oh1, axis=1, keepdims=True)
    carry = carry_ref[:, 0:1]
    rank0 = jnp.sum(oh0 * (carry + pre0), axis=0, keepdims=True)
    rank1 = jnp.sum(oh1 * (carry + cnt0 + pre1), axis=0, keepdims=True)
    carry = jnp.broadcast_to(carry + cnt0 + cnt1, carry_ref.shape)
    carry_ref[...] = carry
    cnt_ref[...] = carry.astype(jnp.int32)
    r8 = lax.broadcasted_iota(jnp.int32, (8, t), 0)
    mi = jnp.where(r8 == 0, e0, jnp.where(r8 == 1, e1, jnp.where(r8 == 2, rank0, jnp.where(r8 == 3, rank1, 0.0))))
    mi_ref[0] = mi.astype(jnp.int32)
    mw_ref[0] = jnp.where(r8 == 0, w0, jnp.where(r8 == 1, w1, 0.0))


def _router_operands(w_r, b_r):
    wr = jnp.zeros((ROUTE_ROWS, D_MODEL), F32).at[:N_EXPERTS].set(w_r.T)
    br = jnp.zeros((ROUTE_ROWS, 1), F32).at[:N_EXPERTS, 0].set(b_r)
    wr_hi = wr.astype(BF16)
    wr_lo = (wr - wr_hi.astype(F32)).astype(BF16)
    return wr_hi, wr_lo, br


def _dispatch_body(dest_ref, last_ref, nused_ref, x_ref, xs_ref, zero_ref, sem_ref):
    t = dest_ref.shape[0] // 2
    tm = zero_ref.shape[0]
    n_tiles = xs_ref.shape[0] // tm
    i = pl.program_id(0)

    @pl.when(i == 0)
    def _():
        zero_ref[...] = jnp.zeros_like(zero_ref)

        def fill(tile):
            cp = pltpu.make_async_copy(zero_ref, xs_ref.at[pl.ds(pl.multiple_of(tile * tm, tm), tm), :], sem_ref.at[1])
            cp.start()
            cp.wait()

        for e in range(N_EXPERTS):
            @pl.when(last_ref[e] >= 0)
            def _():
                fill(last_ref[e])

        def unused(tile, carry):
            fill(tile)
            return carry

        lax.fori_loop(nused_ref[0], n_tiles, unused, 0)

    def issue(r, carry):
        for slot in range(2):
            d = dest_ref[slot * t + r]
            pltpu.make_async_copy(x_ref.at[pl.ds(r, 1), :], xs_ref.at[pl.ds(d, 1), :],
                                  sem_ref.at[0]).start(priority=slot)
        return carry

    lax.fori_loop(0, t, issue, 0, unroll=ISSUE_UNROLL)
    pltpu.make_async_copy(xs_ref.at[pl.ds(0, 2 * t), :], xs_ref.at[pl.ds(0, 2 * t), :], sem_ref.at[0]).wait()


def _dispatch(x, dest, last_tile, n_used, n_tiles, t):
    n = x.shape[0]
    tm = EXPERT_TILE
    return pl.pallas_call(
        _dispatch_body,
        out_shape=jax.ShapeDtypeStruct((n_tiles * tm, D_MODEL), F32),
        grid=(n // t,),
        in_specs=[
            pl.BlockSpec((2 * t,), lambda i: (i,), memory_space=pltpu.SMEM),
            pl.BlockSpec(memory_space=pltpu.SMEM),
            pl.BlockSpec(memory_space=pltpu.SMEM),
            pl.BlockSpec((t, D_MODEL), lambda i: (i, 0)),
        ],
        out_specs=pl.BlockSpec(memory_space=pl.ANY),
        scratch_shapes=[pltpu.VMEM((tm, D_MODEL), F32), pltpu.SemaphoreType.DMA((2,))],
        compiler_params=_params(("arbitrary",)),
        name="dispatch",
    )(dest, last_tile, n_used, x)


def _experts_body(te_ref, first_ref, next_ref, slot_ref, nused_ref, x_ref, w1_hbm, w3_hbm, w2_hbm, o_ref,
                  wb1_ref, wb3_ref, wb2_ref, st1_ref, st3_ref, st2_ref, sem_ref):
    i = pl.program_id(0)

    def piece_copies(e, c):
        buf = c % 2
        cols = pl.ds(c * FF_CHUNK, FF_CHUNK)
        return (pltpu.make_async_copy(w1_hbm.at[e, :, cols], st1_ref.at[buf], sem_ref.at[buf, 0]),
                pltpu.make_async_copy(w3_hbm.at[e, :, cols], st3_ref.at[buf], sem_ref.at[buf, 1]),
                pltpu.make_async_copy(w2_hbm.at[e, cols, :], st2_ref.at[buf], sem_ref.at[buf, 2]))

    def start_piece(e, c):
        for cp in piece_copies(e, c):
            cp.start()

    def finish_piece(e, c, slot):
        for cp in piece_copies(e, c):
            cp.wait()
        cols = slice(c * FF_CHUNK, (c + 1) * FF_CHUNK)
        wb1_ref[slot, :, cols] = st1_ref[c % 2].astype(BF16)
        wb3_ref[slot, :, cols] = st3_ref[c % 2].astype(BF16)
        wb2_ref[slot, cols, :] = st2_ref[c % 2].astype(BF16)

    def ff_piece(xb, slot, c):
        cols = slice(c * FF_CHUNK, (c + 1) * FF_CHUNK)
        a = jnp.dot(xb, wb1_ref[slot, :, cols], preferred_element_type=F32)
        g = jnp.dot(xb, wb3_ref[slot, :, cols], preferred_element_type=F32)
        h = (jax.nn.silu(a) * g).astype(BF16)
        return jnp.dot(h, wb2_ref[slot, cols, :], preferred_element_type=F32)

    @pl.when(i == 0)
    def _():
        e, slot = te_ref[0], slot_ref[0]
        start_piece(e, 0)
        for c in range(FF_PIECES):
            if c + 1 < FF_PIECES:
                start_piece(e, c + 1)
            finish_piece(e, c, slot)

    live = i < nused_ref[0]
    prefetch = jnp.logical_and(live, jnp.logical_and(first_ref[i] == 1, next_ref[i] >= 0))

    @pl.when(jnp.logical_and(live, jnp.logical_not(prefetch)))
    def _():
        xb = x_ref[...].astype(BF16)
        slot = slot_ref[i]
        f = ff_piece(xb, slot, 0)
        for c in range(1, FF_PIECES):
            f = f + ff_piece(xb, slot, c)
        o_ref[...] = f

    @pl.when(prefetch)
    def _():
        xb = x_ref[...].astype(BF16)
        slot, e_next = slot_ref[i], next_ref[i]
        start_piece(e_next, 0)
        f = None
        for c in range(FF_PIECES):
            if c + 1 < FF_PIECES:
                start_piece(e_next, c + 1)
            part = ff_piece(xb, slot, c)
            f = part if f is None else f + part
            finish_piece(e_next, c, 1 - slot)
        o_ref[...] = f

    @pl.when(jnp.logical_not(live))
    def _():
        o_ref[...] = jnp.zeros_like(o_ref)


def _experts(xs, tile_expert, first, nxt, slot, n_used, w1, w3, w2):
    tm = EXPERT_TILE
    n_tiles = xs.shape[0] // tm
    hbm = pl.BlockSpec(memory_space=pl.ANY)
    return pl.pallas_call(
        _experts_body,
        out_shape=jax.ShapeDtypeStruct(xs.shape, F32),
        grid_spec=pltpu.PrefetchScalarGridSpec(
            num_scalar_prefetch=5,
            grid=(n_tiles,),
            in_specs=[pl.BlockSpec((tm, D_MODEL), lambda i, te, fi, nx, sl, nu: (jnp.minimum(i, nu[0] - 1), 0)),
                      hbm, hbm, hbm],
            out_specs=pl.BlockSpec((tm, D_MODEL), lambda i, te, fi, nx, sl, nu: (i, 0)),
            scratch_shapes=[
                pltpu.VMEM((2, D_MODEL, D_FF), BF16), pltpu.VMEM((2, D_MODEL, D_FF), BF16),
                pltpu.VMEM((2, D_FF, D_MODEL), BF16),
                pltpu.VMEM((2, D_MODEL, FF_CHUNK), F32), pltpu.VMEM((2, D_MODEL, FF_CHUNK), F32),
                pltpu.VMEM((2, FF_CHUNK, D_MODEL), F32),
                pltpu.SemaphoreType.DMA((2, 3)),
            ],
        ),
        compiler_params=_params(("arbitrary",), EXPERTS_VMEM_LIMIT),
        name="experts",
    )(tile_expert, first, nxt, slot, n_used, xs, w1, w3, w2)


def _combine_body(n_first, dest_ref, next_ref, x_ref, mw_ref, g_ref, b_ref, ys_ref, *rest):
    out_refs, (y0_ref, y1_ref, sem_ref) = rest[:-3], rest[-3:]
    t = x_ref.shape[0]
    i = pl.program_id(0)
    cur = i % 2

    def gather(idx_ref, buf):
        def issue(r, carry):
            pltpu.make_async_copy(ys_ref.at[pl.ds(idx_ref[r], 1), :], y0_ref.at[buf, pl.ds(r, 1), :],
                                  sem_ref.at[buf]).start(priority=0)
            pltpu.make_async_copy(ys_ref.at[pl.ds(idx_ref[t + r], 1), :], y1_ref.at[buf, pl.ds(r, 1), :],
                                  sem_ref.at[buf]).start(priority=1)
            return carry

        lax.fori_loop(0, t, issue, 0, unroll=ISSUE_UNROLL)

    @pl.when(i == 0)
    def _():
        gather(dest_ref, 0)

    @pl.when(i + 1 < pl.num_programs(0))
    def _():
        gather(next_ref, 1 - cur)

    pltpu.make_async_copy(ys_ref.at[pl.ds(0, t), :], y0_ref.at[cur], sem_ref.at[cur]).wait()
    pltpu.make_async_copy(ys_ref.at[pl.ds(0, t), :], y1_ref.at[cur], sem_ref.at[cur]).wait()
    mw = mw_ref[...]
    f = mw[:, 0:1] * y0_ref[cur] + mw[:, 1:2] * y1_ref[cur]
    out = _layer_norm(DN_ALPHA * x_ref[...] + f, g_ref[...], b_ref[...])
    if len(out_refs) == 1:
        out_refs[0][...] = out
    else:
        @pl.when(i < n_first)
        def _():
            out_refs[0][...] = out

        @pl.when(i == n_first)
        def _():
            out_refs[1][...] = out


def _combine(x, ys, dest, mw, g, b, t, split):
    n = x.shape[0]
    zero = lambda i: (0, 0)
    n_first = n // t - 1
    last = n // t - 1
    if split:
        out_shape = (jax.ShapeDtypeStruct((n - t, D_MODEL), F32), jax.ShapeDtypeStruct((t, D_MODEL), F32))
        out_specs = (pl.BlockSpec((t, D_MODEL), lambda i: (jnp.minimum(i, n_first - 1), 0)),
                     pl.BlockSpec((t, D_MODEL), zero))
    else:
        out_shape = jax.ShapeDtypeStruct((n, D_MODEL), F32)
        out_specs = pl.BlockSpec((t, D_MODEL), lambda i: (i, 0))
    return pl.pallas_call(
        functools.partial(_combine_body, n_first),
        out_shape=out_shape,
        grid=(n // t,),
        in_specs=[
            pl.BlockSpec((2 * t,), lambda i: (i,), memory_space=pltpu.SMEM),
            pl.BlockSpec((2 * t,), lambda i: (jnp.minimum(i + 1, last),), memory_space=pltpu.SMEM),
            pl.BlockSpec((t, D_MODEL), lambda i: (i, 0)),
            pl.BlockSpec((t, 2), lambda i: (i, 0)),
            pl.BlockSpec((1, D_MODEL), zero),
            pl.BlockSpec((1, D_MODEL), zero),
            pl.BlockSpec(memory_space=pl.ANY),
        ],
        out_specs=out_specs,
        scratch_shapes=[pltpu.VMEM((2, t, D_MODEL), F32), pltpu.VMEM((2, t, D_MODEL), F32),
                        pltpu.SemaphoreType.DMA((2,))],
        compiler_params=_params(("arbitrary",)),
        name="combine",
    )(dest, dest, x, mw, g, b, ys)


def _moe(x, mi, mw, cnt, w1, w3, w2, g, b, t, split):
    n = x.shape[0]
    tm = EXPERT_TILE
    n_tiles = (2 * n + N_EXPERTS * (tm - 1)) // tm
    counts = cnt[:N_EXPERTS, 0]
    tiles = (counts + tm - 1) // tm
    tile_end = jnp.cumsum(tiles)
    start = (tile_end - tiles) * tm
    n_used = tile_end[-1:].astype(jnp.int32)
    tile_expert = jnp.sum((tile_end[None, :] <= jnp.arange(n_tiles)[:, None]).astype(jnp.int32), axis=1)
    tile_expert = jnp.minimum(tile_expert, N_EXPERTS - 1)
    last_tile = jnp.where(tiles > 0, tile_end - 1, -1).astype(jnp.int32)
    has = tiles > 0
    next_of = [jnp.int32(-1)] * N_EXPERTS
    for e in range(N_EXPERTS - 2, -1, -1):
        next_of[e] = jnp.where(has[e + 1], e + 1, next_of[e + 1])
    per_expert = jnp.stack([tile_end - tiles, jnp.stack(next_of), (jnp.cumsum(has) - 1) % 2]).astype(jnp.int32)
    onehot = (tile_expert[None, :, None] == jnp.arange(N_EXPERTS)[None, None, :]).astype(jnp.int32)
    first_tile, nxt, slot = jnp.sum(onehot * per_expert[:, None, :], axis=2)
    first = (first_tile == jnp.arange(n_tiles)).astype(jnp.int32)
    experts = mi[:, 0:2, :]
    group_start = sum(jnp.where(experts == e, start[e], 0) for e in range(N_EXPERTS))
    dest = (group_start + mi[:, 2:4, :]).astype(jnp.int32).reshape(2 * n)
    mw = jnp.swapaxes(mw[:, 0:2, :], 1, 2).reshape(n, 2)
    xs = _dispatch(x, dest, last_tile, n_used, n_tiles, t)
    ys = _experts(xs, tile_expert.astype(jnp.int32), first, nxt, slot, n_used, w1, w3, w2)
    return _combine(x, ys, dest, mw, g, b, t, split)


def _block_diag(w):
    layers, g, c, _ = w.shape
    out = jnp.zeros((layers, g * c, g * c), w.dtype)
    for i in range(g):
        out = out.at[:, i * c:(i + 1) * c, i * c:(i + 1) * c].set(w[:, i])
    return out


def _bias_pairs(table, t_q, n_keys, offset):
    layers = table.shape[0]
    hi = t_q - 1 + offset
    span = t_q - 1 + n_keys
    cols = np.clip(hi - np.arange(span), -REL_CLIP, REL_CLIP) + REL_CLIP
    rev = table.reshape(layers * N_HEADS, -1)[:, cols].astype(F32)
    skew = jnp.tile(rev, (1, t_q + 1))[:, :t_q * (span + 1)].reshape(layers * N_HEADS, t_q, span + 1)
    slab = skew[:, ::-1, :n_keys]
    return slab.reshape(layers, HEAD_PAIRS, 2 * t_q, n_keys)


def _prompt_bias_slabs(table):
    bias = _bias_pairs(table, CHUNK, BAND, ATTN_REACH)
    col = np.arange(BAND)[None, :]
    first_valid = np.concatenate([[0], (LEFT_CHUNKS - np.arange(LEFT_CHUNKS)) * CHUNK])[:, None]
    valid = jnp.asarray(col >= first_valid)
    return jnp.where(valid[None, :, None, None, :], bias[:, None], NEG_INF)


def kernel(x_prompt, x_sample, cache_k, cache_v, state_pool, w_in, b_in, w_pool_grp, pool_scale,
           rel_table, w_pool_br, w_attn_br, w_out, ln1_g, ln1_b, ln2_g, ln2_b,
           w1_dense, w3_dense, w2_dense, w_router, b_router, w1_exp, w3_exp, w2_exp):
    bp, tp, d = x_prompt.shape
    bs, ts, _ = x_sample.shape
    n_p, n_s = bp * tp, bs * ts
    n = n_p + n_s
    depth = w_in.shape[0]
    keep_s = cache_k.shape[2]
    keep_p = min(ATTN_REACH, tp)

    x = (x_prompt.reshape(n_p, d), x_sample.reshape(n_s, d))
    ck = cache_k.reshape(depth, bs, keep_s, ATTN_WIDTH)
    cv = cache_v.reshape(depth, bs, keep_s, ATTN_WIDTH)
    hist = jnp.pad(state_pool, ((0, 0), (0, 0), (HIST_ROWS - POOL_HIST, 0), (0, 0)))
    assert tp % ATTN_REACH == 0 and keep_p == ATTN_REACH and ts >= POOL_HIST

    rows = lambda v: v[:, None, :].astype(F32)
    w_in_bf, b_in_r = w_in.astype(BF16), rows(b_in)
    wgrp, wp_bf, wa_bf, wo_bf = (a.astype(BF16) for a in (_block_diag(w_pool_grp), w_pool_br, w_attn_br, w_out))
    scale_r, g1, b1, g2, b2 = (rows(a) for a in (pool_scale, ln1_g, ln1_b, ln2_g, ln2_b))
    dense_bf = [a.astype(BF16) for a in (w1_dense, w3_dense, w2_dense)]
    bias_p = _prompt_bias_slabs(rel_table)
    bias_s = _bias_pairs(rel_table, ts, keep_s + ts, keep_s)

    k_buf = v_buf = None
    kp_new, vp_new, pp_new, ps_new = [], [], [], []
    for l in range(depth):
        weights = (wgrp[l], scale_r[l], wp_bf[l], wa_bf[l], wo_bf[l], g1[l], b1[l])
        z32, z16 = _in_proj(x, w_in_bf[l], b_in_r[l], n_s)
        attn_p, k_tail, v_tail = _attn_prompt(z32, z16, bias_p[l], bp, tp)
        attn_s, k_buf, v_buf = _attn_sample(z32, z16, ck, cv, bias_s[l], l, n_p, bs, ts, k_buf, v_buf)
        j = l // 2
        last = l == depth - 1
        if l % 2 == 0:
            x1, u_tail = _mix(z32, z16, attn_p, attn_s, x, hist[l], weights, n_p, tp)
            x = _ffn_dense(x1, dense_bf[0][j], dense_bf[1][j], dense_bf[2][j], g2[l], b2[l])
            if last:
                x = (x[:n_p], x[n_p:])
        else:
            x1, u_tail, mi, mw, cnt = _mix(z32, z16, attn_p, attn_s, x, hist[l], weights, n_p, tp,
                                           router=_router_operands(w_router[j], b_router[j]))
            x = _moe(x1, mi, mw, cnt, w1_exp[j], w3_exp[j], w2_exp[j], g2[l], b2[l], n_s, split=last)

        kp_new.append(k_tail.reshape(bp, keep_p, N_HEADS, HEAD_DIM))
        vp_new.append(v_tail.reshape(bp, keep_p, N_HEADS, HEAD_DIM))
        pp_new.append(u_tail[:, HIST_ROWS - POOL_HIST:])
        ps_new.append(z32[n_p:, :POOL_WIDTH].reshape(bs, ts, POOL_WIDTH)[:, ts - POOL_HIST:])

    shape_s = (depth, bs, keep_s, N_HEADS, HEAD_DIM)
    return (x[0].reshape(bp, tp, d), x[1].reshape(bs, ts, d),
            jnp.stack(kp_new), jnp.stack(vp_new), jnp.stack(pp_new),
            k_buf.reshape(shape_s), v_buf.reshape(shape_s), jnp.stack(ps_new))
```

```python
import functools

import jax
import jax.numpy as jnp
import numpy as np
from jax import lax
from jax.experimental import pallas as pl
from jax.experimental.pallas import tpu as pltpu

F32 = jnp.float32
BF16 = jnp.bfloat16

D_MODEL = 1024
N_HEADS = 8
HEAD_DIM = 64
ATTN_WIDTH = N_HEADS * HEAD_DIM
CHUNK = 64
LEFT_CHUNKS = 8
BAND = (LEFT_CHUNKS + 1) * CHUNK
ATTN_REACH = LEFT_CHUNKS * CHUNK
REL_CLIP = 256
ATTN_SCALE = HEAD_DIM ** -0.5
POOL_WIDTH = 512
POOL_WINDOWS = (2, 4, 8, 16)
POOL_GROUP = POOL_WIDTH // len(POOL_WINDOWS)
POOL_HIST = max(POOL_WINDOWS) - 1
HIST_ROWS = POOL_HIST + 1
LEAD_ROWS = 8
D_FF = 2816
N_EXPERTS = 8
PAST_LEN = 4096
DEPTH = 2
DN_ALPHA = (2 * DEPTH) ** 0.25
LN_EPS = 1e-5
NEG_INF = -1e30
IN_WIDTH = POOL_WIDTH + 3 * ATTN_WIDTH + 2 * D_MODEL
HEAD_PAIRS = N_HEADS // 2
PAIR_W = 2 * HEAD_DIM
LANES = 128
ROUTE_ROWS = 16

Z32_WIDTH = POOL_WIDTH + 2 * ATTN_WIDTH
Z16_WIDTH = 2 * D_MODEL + 3 * ATTN_WIDTH
COL_U, COL_K, COL_V = 0, 1, 2
COL_GP, COL_GA = 0, 1
COL_Q = 2 * D_MODEL // ATTN_WIDTH
COL_K16, COL_V16 = COL_Q + 1, COL_Q + 2
IN_SPLITS = (0, POOL_WIDTH, POOL_WIDTH + ATTN_WIDTH, POOL_WIDTH + 3 * ATTN_WIDTH, IN_WIDTH)

EXPERT_TILE = 512
FF_CHUNK = 256
FF_PIECES = D_FF // FF_CHUNK
ISSUE_UNROLL = 8
VMEM_LIMIT = 56 * 1024 * 1024
EXPERTS_VMEM_LIMIT = 62 * 1024 * 1024


def _pick(n, candidates):
    for c in candidates:
        if n % c == 0:
            return c
    raise ValueError(f"no tile in {candidates} divides {n}")


def _params(sem, vmem=None):
    return pltpu.CompilerParams(dimension_semantics=sem, vmem_limit_bytes=vmem or VMEM_LIMIT)


def _layer_norm(r, g, b):
    mu = jnp.mean(r, axis=-1, keepdims=True)
    c = r - mu
    var = jnp.mean(c * c, axis=-1, keepdims=True)
    return c * lax.rsqrt(var + LN_EPS) * g + b


def _row_sources(x, t):
    if not isinstance(x, tuple):
        return [x], [pl.BlockSpec((t, x.shape[1]), lambda i: (i, 0))], (lambda refs, i: refs[0][...])
    first, last = x
    assert last.shape[0] == t and first.shape[0] % t == 0
    n_first = first.shape[0] // t
    specs = [pl.BlockSpec((t, first.shape[1]), lambda i: (jnp.minimum(i, n_first - 1), 0)),
             pl.BlockSpec((t, last.shape[1]), lambda i: (0, 0))]
    return [first, last], specs, (lambda refs, i: jnp.where(i < n_first, refs[0][...], refs[1][...]))


def _inproj_body(n_src, select, *refs):
    w_ref, b_ref, z32_ref, z16_ref = refs[n_src:]
    x = select(refs[:n_src], pl.program_id(0)).astype(BF16)
    z = jnp.dot(x, w_ref[...], preferred_element_type=F32) + b_ref[...]
    u, q, kv, gates = (z[:, a:b] for a, b in zip(IN_SPLITS[:-1], IN_SPLITS[1:]))
    z32_ref[...] = jnp.concatenate([u, kv], axis=-1)
    z16_ref[...] = jnp.concatenate([gates, q, kv], axis=-1).astype(BF16)


def _in_proj(x, w_bf, b, t):
    srcs, specs, select = _row_sources(x, t)
    n = sum(s.shape[0] for s in srcs)
    zero = lambda i: (0, 0)
    return pl.pallas_call(
        functools.partial(_inproj_body, len(srcs), select),
        out_shape=(jax.ShapeDtypeStruct((n, Z32_WIDTH), F32), jax.ShapeDtypeStruct((n, Z16_WIDTH), BF16)),
        grid=(n // t,),
        in_specs=specs + [pl.BlockSpec((D_MODEL, IN_WIDTH), zero), pl.BlockSpec((1, IN_WIDTH), zero)],
        out_specs=(pl.BlockSpec((t, Z32_WIDTH), lambda i: (i, 0)), pl.BlockSpec((t, Z16_WIDTH), lambda i: (i, 0))),
        compiler_params=_params(("arbitrary",)),
        name="in_proj",
    )(*srcs, w_bf, b)


def _pair_scores(q_pair, k_pair, bias):
    lane = lax.broadcasted_iota(jnp.int32, q_pair.shape, 1)
    qs = q_pair.astype(F32) * ATTN_SCALE
    q2 = jnp.concatenate([jnp.where(lane < HEAD_DIM, qs, 0.0), jnp.where(lane >= HEAD_DIM, qs, 0.0)], axis=0)
    s = lax.dot_general(q2.astype(BF16), k_pair, (((1,), (1,)), ((), ())), preferred_element_type=F32)
    return s + bias


def _pair_output(s, v_pair):
    rows = s.shape[0] // 2
    m = jnp.max(s, axis=-1, keepdims=True)
    e = jnp.exp(s - m)
    l = jnp.sum(e, axis=-1, keepdims=True)
    o2 = jnp.dot(e.astype(BF16), v_pair, preferred_element_type=F32) / l
    lane = lax.broadcasted_iota(jnp.int32, (rows, PAIR_W), 1)
    return jnp.where(lane < HEAD_DIM, o2[:rows], o2[rows:])


def _attn_prompt_body(q_ref, kp_ref, kc_ref, vp_ref, vc_ref, kl_ref, vl_ref, bias_ref,
                      o_ref, kt_ref, vt_ref, kext_ref, vext_ref):
    blk = q_ref.shape[0]
    j = pl.program_id(1)

    @pl.when(j == pl.num_programs(1) - 1)
    def _():
        kt_ref[0] = kl_ref[...]
        vt_ref[0] = vl_ref[...]

    kext_ref[0:blk, :] = kp_ref[...]
    kext_ref[blk:2 * blk, :] = kc_ref[...]
    vext_ref[0:blk, :] = vp_ref[...]
    vext_ref[blk:2 * blk, :] = vc_ref[...]
    def chunk(c, carry):
        q0 = pl.multiple_of(c * CHUNK, CHUNK)
        slab = jnp.where(j == 0, c + 1, 0)
        pairs = [slice(hp * PAIR_W, (hp + 1) * PAIR_W) for hp in range(HEAD_PAIRS)]
        s = jnp.concatenate([_pair_scores(q_ref[pl.ds(q0, CHUNK), lanes], kext_ref[pl.ds(q0, BAND), lanes],
                                          bias_ref[slab, hp]) for hp, lanes in enumerate(pairs)], axis=0)
        m = jnp.max(s, axis=-1, keepdims=True)
        e = jnp.exp(s - m)
        inv = 1.0 / jnp.sum(e, axis=-1, keepdims=True)
        p = e.astype(BF16)
        lane = lax.broadcasted_iota(jnp.int32, (CHUNK, PAIR_W), 1)
        for hp, lanes in enumerate(pairs):
            rows = slice(hp * 2 * CHUNK, (hp + 1) * 2 * CHUNK)
            o2 = jnp.dot(p[rows], vext_ref[pl.ds(q0, BAND), lanes], preferred_element_type=F32) * inv[rows]
            o_ref[pl.ds(q0, CHUNK), lanes] = jnp.where(lane < HEAD_DIM, o2[:CHUNK], o2[CHUNK:])
        return carry

    lax.fori_loop(0, blk // CHUNK, chunk, 0, unroll=8)


def _attn_prompt(z32, z16, bias_pairs, batch, seq):
    blk = ATTN_REACH
    per_seq = seq // blk
    rows = batch * seq

    def cur(col):
        return lambda b, j: (b * per_seq + j, col)

    def prev(col):
        return lambda b, j: (b * per_seq + jnp.maximum(j - 1, 0), col)

    def last(col):
        return lambda b, j: (b * per_seq + per_seq - 1, col)

    tail = jax.ShapeDtypeStruct((batch, blk, ATTN_WIDTH), F32)
    tail_spec = pl.BlockSpec((1, blk, ATTN_WIDTH), lambda b, j: (b, 0, 0))
    return pl.pallas_call(
        _attn_prompt_body,
        out_shape=(jax.ShapeDtypeStruct((rows, ATTN_WIDTH), F32), tail, tail),
        grid=(batch, per_seq),
        in_specs=[
            pl.BlockSpec((blk, ATTN_WIDTH), cur(COL_Q)),
            pl.BlockSpec((blk, ATTN_WIDTH), prev(COL_K16)),
            pl.BlockSpec((blk, ATTN_WIDTH), cur(COL_K16)),
            pl.BlockSpec((blk, ATTN_WIDTH), prev(COL_V16)),
            pl.BlockSpec((blk, ATTN_WIDTH), cur(COL_V16)),
            pl.BlockSpec((blk, ATTN_WIDTH), last(COL_K)),
            pl.BlockSpec((blk, ATTN_WIDTH), last(COL_V)),
            pl.BlockSpec((1 + LEFT_CHUNKS, HEAD_PAIRS, 2 * CHUNK, BAND), lambda b, j: (0, 0, 0, 0)),
        ],
        out_specs=(pl.BlockSpec((blk, ATTN_WIDTH), lambda b, j: (b * per_seq + j, 0)), tail_spec, tail_spec),
        scratch_shapes=[pltpu.VMEM((2 * blk, ATTN_WIDTH), BF16), pltpu.VMEM((2 * blk, ATTN_WIDTH), BF16)],
        compiler_params=_params(("arbitrary", "arbitrary")),
        name="attn_prompt",
    )(z16, z16, z16, z16, z16, z32, z32, bias_pairs)


def _attn_sample_body(q_ref, kn_ref, vn_ref, ck_ref, cv_ref, bias_ref, *rest):
    o_ref, ko_ref, vo_ref, kall_ref, vall_ref = rest[-5:]
    keep = ck_ref.shape[2]
    t = q_ref.shape[0]
    ck = ck_ref[0, 0]
    cv = cv_ref[0, 0]
    kn = kn_ref[...]
    vn = vn_ref[...]
    kall_ref[0:keep, :] = ck.astype(BF16)
    kall_ref[keep:keep + t, :] = kn.astype(BF16)
    vall_ref[0:keep, :] = cv.astype(BF16)
    vall_ref[keep:keep + t, :] = vn.astype(BF16)
    for hp in range(HEAD_PAIRS):
        lanes = slice(hp * PAIR_W, (hp + 1) * PAIR_W)
        s = _pair_scores(q_ref[:, lanes], kall_ref[:, lanes], bias_ref[hp])
        o_ref[:, lanes] = _pair_output(s, vall_ref[:, lanes])
    ko_ref[0, 0, 0:keep - t, :] = ck[t:keep]
    ko_ref[0, 0, keep - t:keep, :] = kn
    vo_ref[0, 0, 0:keep - t, :] = cv[t:keep]
    vo_ref[0, 0, keep - t:keep, :] = vn
    for later in range(1, ko_ref.shape[0]):
        ko_ref[later] = jnp.zeros(ko_ref.shape[1:], F32)
        vo_ref[later] = jnp.zeros(vo_ref.shape[1:], F32)


def _attn_sample(z32, z16, cache_k, cache_v, bias_pairs, layer, row0, streams, t, k_buf, v_buf):
    depth, _, keep, _ = cache_k.shape
    blk0 = row0 // t
    ins = [z16, z32, z32, cache_k, cache_v, bias_pairs]
    in_specs = [
        pl.BlockSpec((t, ATTN_WIDTH), lambda s: (blk0 + s, COL_Q)),
        pl.BlockSpec((t, ATTN_WIDTH), lambda s: (blk0 + s, COL_K)),
        pl.BlockSpec((t, ATTN_WIDTH), lambda s: (blk0 + s, COL_V)),
        pl.BlockSpec((1, 1, keep, ATTN_WIDTH), lambda s: (layer, s, 0, 0)),
        pl.BlockSpec((1, 1, keep, ATTN_WIDTH), lambda s: (layer, s, 0, 0)),
        pl.BlockSpec((HEAD_PAIRS, 2 * t, keep + t), lambda s: (0, 0, 0)),
    ]
    aliases = {}
    if k_buf is None:
        assert layer == 0
        buf_spec = pl.BlockSpec((depth, 1, keep, ATTN_WIDTH), lambda s: (0, s, 0, 0))
    else:
        aliases = {len(ins): 1, len(ins) + 1: 2}
        ins += [k_buf, v_buf]
        in_specs += [pl.BlockSpec(memory_space=pl.ANY), pl.BlockSpec(memory_space=pl.ANY)]
        buf_spec = pl.BlockSpec((1, 1, keep, ATTN_WIDTH), lambda s: (layer, s, 0, 0))
    buf = jax.ShapeDtypeStruct(cache_k.shape, F32)
    return pl.pallas_call(
        _attn_sample_body,
        out_shape=(jax.ShapeDtypeStruct((streams * t, ATTN_WIDTH), F32), buf, buf),
        grid=(streams,),
        in_specs=in_specs,
        out_specs=(
            pl.BlockSpec((t, ATTN_WIDTH), lambda s: (s, 0)),
            buf_spec,
            buf_spec,
        ),
        scratch_shapes=[pltpu.VMEM((keep + t, ATTN_WIDTH), BF16), pltpu.VMEM((keep + t, ATTN_WIDTH), BF16)],
        input_output_aliases=aliases,
        compiler_params=_params(("arbitrary",)),
        name="attn_sample",
    )(*ins)


def _merge(pooled, u, attn, gp, ga, x, wgrp_ref, scale_ref, wp_ref, wa_ref, wo_ref, g_ref, b_ref):
    pooled = pooled - u
    pool_y = jnp.dot(pooled.astype(BF16), wgrp_ref[...], preferred_element_type=F32) * scale_ref[...]
    mp = jnp.dot(pool_y.astype(BF16), wp_ref[...], preferred_element_type=F32)
    ma = jnp.dot(attn.astype(BF16), wa_ref[...], preferred_element_type=F32)
    m = jax.nn.sigmoid(gp) * mp + jax.nn.sigmoid(ga) * ma
    y = jnp.dot(m.astype(BF16), wo_ref[...], preferred_element_type=F32)
    return _layer_norm(DN_ALPHA * x + y, g_ref[...], b_ref[...])


def _window_means(read, pos, shape_out):
    outs = []
    for g, w in enumerate(POOL_WINDOWS):
        lanes = slice(g * POOL_GROUP, (g + 1) * POOL_GROUP)
        s = read(0, lanes)
        for back in range(1, w):
            s = s + read(back, lanes)
        outs.append((s / jnp.minimum(pos + 1, w).astype(F32)).reshape(shape_out))
    return jnp.concatenate(outs, axis=-1)


def _window_means_doubling(buf_ref, lvl_ref, pos, t):
    g = POOL_GROUP
    lo, hi = LEAD_ROWS, LEAD_ROWS + HIST_ROWS + t
    out0 = LEAD_ROWS + HIST_ROWS
    lvl_ref[0, lo:hi, :] = buf_ref[lo:hi, :] + buf_ref[lo - 1:hi - 1, :]
    lvl_ref[1, lo:hi, g:] = lvl_ref[0, lo:hi, g:] + lvl_ref[0, lo - 2:hi - 2, g:]
    lvl_ref[2, lo:hi, 2 * g:] = lvl_ref[1, lo:hi, 2 * g:] + lvl_ref[1, lo - 4:hi - 4, 2 * g:]
    s16 = lvl_ref[2, out0:out0 + t, 3 * g:] + lvl_ref[2, out0 - 8:out0 - 8 + t, 3 * g:]
    sums = [lvl_ref[k, out0:out0 + t, k * g:(k + 1) * g] for k in range(3)] + [s16]
    return jnp.concatenate([s / jnp.minimum(pos + 1, w).astype(F32) for s, w in zip(sums, POOL_WINDOWS)], axis=-1)


def _mix_body(n_prompt_tiles, tiles_per_seq, with_router, n_src, select,
              u_ref, up_ref, hist_ref, attn_p_ref, attn_s_ref, gp_ref, ga_ref, *rest):
    x_refs, rest = rest[:n_src], rest[n_src:]
    (wgrp_ref, scale_ref, wp_ref, wa_ref, wo_ref, g_ref, b_ref), rest = rest[:7], rest[7:]
    if with_router:
        (wr_hi_ref, wr_lo_ref, br_ref, o_ref, tail_ref, mi_ref, mw_ref, cnt_ref,
         buf_ref, lvl_ref, sbuf_ref, pooled_ref, attn_ref, carry_ref) = rest
    else:
        o_ref, tail_ref, buf_ref, lvl_ref, sbuf_ref, pooled_ref, attn_ref = rest
    t = u_ref.shape[0]
    i = pl.program_id(0)
    u = u_ref[...]

    @pl.when(i == 0)
    def _():
        buf_ref[0:LEAD_ROWS, :] = jnp.zeros((LEAD_ROWS, POOL_WIDTH), F32)
        lvl_ref[:, 0:LEAD_ROWS, :] = jnp.zeros((lvl_ref.shape[0], LEAD_ROWS, POOL_WIDTH), F32)

    @pl.when(i < n_prompt_tiles)
    def _():
        tile = i % tiles_per_seq
        h0 = LEAD_ROWS
        buf_ref[h0:h0 + HIST_ROWS, :] = jnp.where(tile == 0, 0.0, up_ref[...])
        buf_ref[h0 + HIST_ROWS:h0 + HIST_ROWS + t, :] = u
        pos = tile * t + lax.broadcasted_iota(jnp.int32, (t, 1), 0)
        pooled_ref[...] = _window_means_doubling(buf_ref, lvl_ref, pos, t)
        attn_ref[...] = attn_p_ref[...]

        @pl.when(tile == tiles_per_seq - 1)
        def _():
            tail_ref[0] = u[t - HIST_ROWS:, :]

    @pl.when(i == n_prompt_tiles)
    def _():
        streams, hrows, _ = hist_ref.shape
        ts = t // streams
        sbuf_ref[:, 0:hrows, :] = hist_ref[...]
        sbuf_ref[:, hrows:hrows + ts, :] = u.reshape(streams, ts, POOL_WIDTH)
        pos = PAST_LEN + lax.broadcasted_iota(jnp.int32, (1, ts, 1), 1)
        read = lambda back, lanes: sbuf_ref[:, hrows - back:hrows - back + ts, lanes]
        pooled_ref[...] = _window_means(read, pos, (t, POOL_GROUP))
        attn_ref[...] = attn_s_ref[...]

    x1 = _merge(pooled_ref[...], u, attn_ref[...], gp_ref[...].astype(F32), ga_ref[...].astype(F32),
                select(x_refs, i), wgrp_ref, scale_ref, wp_ref, wa_ref, wo_ref, g_ref, b_ref)
    o_ref[...] = x1
    if with_router:
        _route(x1, wr_hi_ref, wr_lo_ref, br_ref, mi_ref, mw_ref, cnt_ref, carry_ref)


def _mix(z32, z16, attn_p, attn_s, x, hist, weights, n_prompt, seq, router=None):
    n = z32.shape[0]
    t = n - n_prompt
    x_srcs, x_specs, select = _row_sources(x, t)
    n_prompt_tiles = n_prompt // t
    streams, hrows, _ = hist.shape
    hist_per_tile = t // HIST_ROWS
    last_p = n_prompt_tiles - 1
    zero = lambda i: (0, 0)
    rows = lambda width: pl.BlockSpec((t, width), lambda i: (i, 0))
    tiles_per_seq = seq // t
    out_shape = [jax.ShapeDtypeStruct((n, D_MODEL), F32),
                 jax.ShapeDtypeStruct((n_prompt // seq, HIST_ROWS, POOL_WIDTH), F32)]
    out_specs = [rows(D_MODEL),
                 pl.BlockSpec((1, HIST_ROWS, POOL_WIDTH), lambda i: (jnp.minimum(i, last_p) // tiles_per_seq, 0, 0))]
    router_specs, router_scratch = [], []
    if router is not None:
        router_specs = [pl.BlockSpec((ROUTE_ROWS, D_MODEL), zero), pl.BlockSpec((ROUTE_ROWS, D_MODEL), zero),
                        pl.BlockSpec((ROUTE_ROWS, 1), zero)]
        per_tile = lambda dtype: jax.ShapeDtypeStruct((n // t, 8, t), dtype)
        out_shape += [per_tile(jnp.int32), per_tile(F32), jax.ShapeDtypeStruct((ROUTE_ROWS, LANES), jnp.int32)]
        out_specs += [pl.BlockSpec((1, 8, t), lambda i: (i, 0, 0)), pl.BlockSpec((1, 8, t), lambda i: (i, 0, 0)),
                      pl.BlockSpec((ROUTE_ROWS, LANES), zero)]
        router_scratch = [pltpu.VMEM((ROUTE_ROWS, LANES), F32)]
    out = pl.pallas_call(
        functools.partial(_mix_body, n_prompt_tiles, seq // t, router is not None, len(x_srcs), select),
        out_shape=out_shape,
        grid=(n_prompt_tiles + 1,),
        in_specs=[
            pl.BlockSpec((t, POOL_WIDTH), lambda i: (i, COL_U)),
            pl.BlockSpec((HIST_ROWS, POOL_WIDTH), lambda i: (jnp.maximum(i * hist_per_tile - 1, 0), COL_U)),
            pl.BlockSpec((streams, hrows, POOL_WIDTH), lambda i: (0, 0, 0)),
            pl.BlockSpec((t, ATTN_WIDTH), lambda i: (jnp.minimum(i, last_p), 0)),
            pl.BlockSpec((t, ATTN_WIDTH), zero),
            pl.BlockSpec((t, D_MODEL), lambda i: (i, COL_GP)),
            pl.BlockSpec((t, D_MODEL), lambda i: (i, COL_GA)),
        ] + x_specs + [
            pl.BlockSpec((POOL_WIDTH, POOL_WIDTH), zero),
            pl.BlockSpec((1, POOL_WIDTH), zero),
            pl.BlockSpec((POOL_WIDTH, D_MODEL), zero),
            pl.BlockSpec((ATTN_WIDTH, D_MODEL), zero),
            pl.BlockSpec((D_MODEL, D_MODEL), zero),
            pl.BlockSpec((1, D_MODEL), zero),
            pl.BlockSpec((1, D_MODEL), zero),
        ] + router_specs,
        out_specs=out_specs,
        scratch_shapes=[
            pltpu.VMEM((LEAD_ROWS + HIST_ROWS + t, POOL_WIDTH), F32),
            pltpu.VMEM((3, LEAD_ROWS + HIST_ROWS + t, POOL_WIDTH), F32),
            pltpu.VMEM((streams, hrows + t // streams, POOL_WIDTH), F32),
            pltpu.VMEM((t, POOL_WIDTH), F32),
            pltpu.VMEM((t, ATTN_WIDTH), F32),
        ] + router_scratch,
        compiler_params=_params(("arbitrary",)),
        name="mix",
    )(z32, z32, hist, attn_p, attn_s, z16, z16, *x_srcs, *weights, *(router or ()))
    return out


def _ffn_dense_body(x_ref, w1_ref, w3_ref, w2_ref, g_ref, b_ref, o_ref):
    x = x_ref[...]
    xb = x.astype(BF16)
    a = jnp.dot(xb, w1_ref[...], preferred_element_type=F32)
    c = jnp.dot(xb, w3_ref[...], preferred_element_type=F32)
    h = (jax.nn.silu(a) * c).astype(BF16)
    f = jnp.dot(h, w2_ref[...], preferred_element_type=F32)
    o_ref[...] = _layer_norm(DN_ALPHA * x + f, g_ref[...], b_ref[...])


def _ffn_dense(x, w1, w3, w2, g, b):
    n = x.shape[0]
    t = 256
    zero = lambda i: (0, 0)
    return pl.pallas_call(
        _ffn_dense_body,
        out_shape=jax.ShapeDtypeStruct((n, D_MODEL), F32),
        grid=(n // t,),
        in_specs=[
            pl.BlockSpec((t, D_MODEL), lambda i: (i, 0)),
            pl.BlockSpec((D_MODEL, D_FF), zero),
            pl.BlockSpec((D_MODEL, D_FF), zero),
            pl.BlockSpec((D_FF, D_MODEL), zero),
            pl.BlockSpec((1, D_MODEL), zero),
            pl.BlockSpec((1, D_MODEL), zero),
        ],
        out_specs=pl.BlockSpec((t, D_MODEL), lambda i: (i, 0)),
        compiler_params=_params(("arbitrary",)),
        name="ffn_dense",
    )(x, w1, w3, w2, g, b)


def _route(x, wr_hi_ref, wr_lo_ref, br_ref, mi_ref, mw_ref, cnt_ref, carry_ref):
    t = x.shape[0]

    @pl.when(pl.program_id(0) == 0)
    def _():
        carry_ref[...] = jnp.zeros_like(carry_ref)

    x_hi = x.astype(BF16)
    x_lo = (x - x_hi.astype(F32)).astype(BF16)
    nt = (((1,), (1,)), ((), ()))
    logits = (lax.dot_general(wr_hi_ref[...], x_hi, nt, preferred_element_type=F32)
              + lax.dot_general(wr_hi_ref[...], x_lo, nt, preferred_element_type=F32)
              + lax.dot_general(wr_lo_ref[...], x_hi, nt, preferred_element_type=F32)) + br_ref[...]
    row = lax.broadcasted_iota(jnp.int32, (ROUTE_ROWS, t), 0)
    row_f = row.astype(F32)
    logits = jnp.where(row < N_EXPERTS, logits, -jnp.inf)
    v0 = jnp.max(logits, axis=0, keepdims=True)
    e0 = jnp.min(jnp.where(logits == v0, row_f, float(ROUTE_ROWS)), axis=0, keepdims=True)
    rest = jnp.where(row_f == e0, -jnp.inf, logits)
    v1 = jnp.max(rest, axis=0, keepdims=True)
    e1 = jnp.min(jnp.where(rest == v1, row_f, float(ROUTE_ROWS)), axis=0, keepdims=True)
    ex = jnp.exp(v1 - v0)
    w0 = 1.0 / (1.0 + ex)
    w1 = ex / (1.0 + ex)
    oh0 = (row_f == e0).astype(F32)
    oh1 = (row_f == e1).astype(F32)
    before = (lax.broadcasted_iota(jnp.int32, (t, t), 0) < lax.broadcasted_iota(jnp.int32, (t, t), 1)).astype(BF16)
    pre0 = jnp.dot(oh0.astype(BF16), before, preferred_element_type=F32)
    pre1 = jnp.dot(oh1.astype(BF16), before, preferred_element_type=F32)
    cnt0 = jnp.sum(oh0, axis=1, keepdims=True)
    cnt1 = jnp.sum(oh1, axis=1, keepdims=True)
    carry = carry_ref[:, 0:1]
    rank0 = jnp.sum(oh0 * (carry + pre0), axis=0, keepdims=True)
    rank1 = jnp.sum(oh1 * (carry + cnt0 + pre1), axis=0, keepdims=True)
    carry = jnp.broadcast_to(carry + cnt0 + cnt1, carry_ref.shape)
    carry_ref[...] = carry
    cnt_ref[...] = carry.astype(jnp.int32)
    r8 = lax.broadcasted_iota(jnp.int32, (8, t), 0)
    mi = jnp.where(r8 == 0, e0, jnp.where(r8 == 1, e1, jnp.where(r8 == 2, rank0, jnp.where(r8 == 3, rank1, 0.0))))
    mi_ref[0] = mi.astype(jnp.int32)
    mw_ref[0] = jnp.where(r8 == 0, w0, jnp.where(r8 == 1, w1, 0.0))


def _router_operands(w_r, b_r):
    wr = jnp.zeros((ROUTE_ROWS, D_MODEL), F32).at[:N_EXPERTS].set(w_r.T)
    br = jnp.zeros((ROUTE_ROWS, 1), F32).at[:N_EXPERTS, 0].set(b_r)
    wr_hi = wr.astype(BF16)
    wr_lo = (wr - wr_hi.astype(F32)).astype(BF16)
    return wr_hi, wr_lo, br


def _dispatch_body(dest_ref, last_ref, nused_ref, x_ref, xs_ref, zero_ref, rows_ref, sem_ref, row_sem_ref):
    t = dest_ref.shape[0] // 2
    tm = zero_ref.shape[0]
    n_tiles = xs_ref.shape[0] // tm
    i = pl.program_id(0)

    @pl.when(i == 0)
    def _():
        zero_ref[...] = jnp.zeros_like(zero_ref)

        def fill(tile):
            cp = pltpu.make_async_copy(zero_ref, xs_ref.at[pl.ds(pl.multiple_of(tile * tm, tm), tm), :], sem_ref.at[0])
            cp.start()
            cp.wait()

        for e in range(N_EXPERTS):
            @pl.when(last_ref[e] >= 0)
            def _():
                fill(last_ref[e])

        def unused(tile, carry):
            fill(tile)
            return carry

        lax.fori_loop(nused_ref[0], n_tiles, unused, 0)

    cur = i % 2
    rows_ref[cur] = x_ref[...]

    def issue(r, carry):
        for slot in range(2):
            d = dest_ref[slot * t + r]
            pltpu.make_async_copy(rows_ref.at[cur, pl.ds(r, 1), :], xs_ref.at[pl.ds(d, 1), :],
                                  row_sem_ref.at[cur]).start(priority=slot)
        return carry

    lax.fori_loop(0, t, issue, 0, unroll=ISSUE_UNROLL)

    def wait_rows(buf):
        pltpu.make_async_copy(xs_ref.at[pl.ds(0, 2 * t), :], xs_ref.at[pl.ds(0, 2 * t), :], row_sem_ref.at[buf]).wait()

    @pl.when(i > 0)
    def _():
        wait_rows(1 - cur)

    @pl.when(i == pl.num_programs(0) - 1)
    def _():
        wait_rows(cur)


def _dispatch(x, dest, last_tile, n_used, n_tiles, t):
    n = x.shape[0]
    tm = EXPERT_TILE
    return pl.pallas_call(
        _dispatch_body,
        out_shape=jax.ShapeDtypeStruct((n_tiles * tm, D_MODEL), F32),
        grid=(n // t,),
        in_specs=[
            pl.BlockSpec((2 * t,), lambda i: (i,), memory_space=pltpu.SMEM),
            pl.BlockSpec(memory_space=pltpu.SMEM),
            pl.BlockSpec(memory_space=pltpu.SMEM),
            pl.BlockSpec((t, D_MODEL), lambda i: (i, 0)),
        ],
        out_specs=pl.BlockSpec(memory_space=pl.ANY),
        scratch_shapes=[pltpu.VMEM((tm, D_MODEL), F32), pltpu.VMEM((2, t, D_MODEL), F32),
                        pltpu.SemaphoreType.DMA((1,)), pltpu.SemaphoreType.DMA((2,))],
        compiler_params=_params(("arbitrary",)),
        name="dispatch",
    )(dest, last_tile, n_used, x)


def _experts_body(te_ref, first_ref, next_ref, slot_ref, nused_ref, x_ref, w1_hbm, w3_hbm, w2_hbm, o_ref,
                  wb1_ref, wb3_ref, wb2_ref, st1_ref, st3_ref, st2_ref, sem_ref):
    i = pl.program_id(0)

    def piece_copies(e, c):
        buf = c % 2
        cols = pl.ds(c * FF_CHUNK, FF_CHUNK)
        return (pltpu.make_async_copy(w1_hbm.at[e, :, cols], st1_ref.at[buf], sem_ref.at[buf, 0]),
                pltpu.make_async_copy(w3_hbm.at[e, :, cols], st3_ref.at[buf], sem_ref.at[buf, 1]),
                pltpu.make_async_copy(w2_hbm.at[e, cols, :], st2_ref.at[buf], sem_ref.at[buf, 2]))

    def start_piece(e, c):
        for cp in piece_copies(e, c):
            cp.start()

    def finish_piece(e, c, slot):
        for cp in piece_copies(e, c):
            cp.wait()
        cols = slice(c * FF_CHUNK, (c + 1) * FF_CHUNK)
        wb1_ref[slot, :, cols] = st1_ref[c % 2].astype(BF16)
        wb3_ref[slot, :, cols] = st3_ref[c % 2].astype(BF16)
        wb2_ref[slot, cols, :] = st2_ref[c % 2].astype(BF16)

    def ff_piece(xb, slot, c):
        cols = slice(c * FF_CHUNK, (c + 1) * FF_CHUNK)
        a = jnp.dot(xb, wb1_ref[slot, :, cols], preferred_element_type=F32)
        g = jnp.dot(xb, wb3_ref[slot, :, cols], preferred_element_type=F32)
        h = (jax.nn.silu(a) * g).astype(BF16)
        return jnp.dot(h, wb2_ref[slot, cols, :], preferred_element_type=F32)

    @pl.when(i == 0)
    def _():
        e, slot = te_ref[0], slot_ref[0]
        start_piece(e, 0)
        for c in range(FF_PIECES):
            if c + 1 < FF_PIECES:
                start_piece(e, c + 1)
            finish_piece(e, c, slot)

    live = i < nused_ref[0]
    prefetch = jnp.logical_and(live, jnp.logical_and(first_ref[i] == 1, next_ref[i] >= 0))

    @pl.when(jnp.logical_and(live, jnp.logical_not(prefetch)))
    def _():
        xb = x_ref[...].astype(BF16)
        slot = slot_ref[i]
        f = ff_piece(xb, slot, 0)
        for c in range(1, FF_PIECES):
            f = f + ff_piece(xb, slot, c)
        o_ref[...] = f

    @pl.when(prefetch)
    def _():
        xb = x_ref[...].astype(BF16)
        slot, e_next = slot_ref[i], next_ref[i]
        start_piece(e_next, 0)
        f = None
        for c in range(FF_PIECES):
            if c + 1 < FF_PIECES:
                start_piece(e_next, c + 1)
            part = ff_piece(xb, slot, c)
            f = part if f is None else f + part
            finish_piece(e_next, c, 1 - slot)
        o_ref[...] = f

    @pl.when(jnp.logical_not(live))
    def _():
        o_ref[...] = jnp.zeros_like(o_ref)


def _experts(xs, tile_expert, first, nxt, slot, n_used, w1, w3, w2):
    tm = EXPERT_TILE
    n_tiles = xs.shape[0] // tm
    hbm = pl.BlockSpec(memory_space=pl.ANY)
    return pl.pallas_call(
        _experts_body,
        out_shape=jax.ShapeDtypeStruct(xs.shape, F32),
        grid_spec=pltpu.PrefetchScalarGridSpec(
            num_scalar_prefetch=5,
            grid=(n_tiles,),
            in_specs=[pl.BlockSpec((tm, D_MODEL), lambda i, te, fi, nx, sl, nu: (jnp.minimum(i, nu[0] - 1), 0)),
                      hbm, hbm, hbm],
            out_specs=pl.BlockSpec((tm, D_MODEL), lambda i, te, fi, nx, sl, nu: (i, 0)),
            scratch_shapes=[
                pltpu.VMEM((2, D_MODEL, D_FF), BF16), pltpu.VMEM((2, D_MODEL, D_FF), BF16),
                pltpu.VMEM((2, D_FF, D_MODEL), BF16),
                pltpu.VMEM((2, D_MODEL, FF_CHUNK), F32), pltpu.VMEM((2, D_MODEL, FF_CHUNK), F32),
                pltpu.VMEM((2, FF_CHUNK, D_MODEL), F32),
                pltpu.SemaphoreType.DMA((2, 3)),
            ],
        ),
        compiler_params=_params(("arbitrary",), EXPERTS_VMEM_LIMIT),
        name="experts",
    )(tile_expert, first, nxt, slot, n_used, xs, w1, w3, w2)


def _combine_body(n_first, dest_ref, next_ref, x_ref, mw_ref, g_ref, b_ref, ys_ref, *rest):
    out_refs, (y0_ref, y1_ref, sem_ref) = rest[:-3], rest[-3:]
    t = x_ref.shape[0]
    i = pl.program_id(0)
    cur = i % 2

    def gather(idx_ref, buf):
        def issue(r, carry):
            pltpu.make_async_copy(ys_ref.at[pl.ds(idx_ref[r], 1), :], y0_ref.at[buf, pl.ds(r, 1), :],
                                  sem_ref.at[buf]).start(priority=0)
            pltpu.make_async_copy(ys_ref.at[pl.ds(idx_ref[t + r], 1), :], y1_ref.at[buf, pl.ds(r, 1), :],
                                  sem_ref.at[buf]).start(priority=1)
            return carry

        lax.fori_loop(0, t, issue, 0, unroll=ISSUE_UNROLL)

    @pl.when(i == 0)
    def _():
        gather(dest_ref, 0)

    @pl.when(i + 1 < pl.num_programs(0))
    def _():
        gather(next_ref, 1 - cur)

    pltpu.make_async_copy(ys_ref.at[pl.ds(0, t), :], y0_ref.at[cur], sem_ref.at[cur]).wait()
    pltpu.make_async_copy(ys_ref.at[pl.ds(0, t), :], y1_ref.at[cur], sem_ref.at[cur]).wait()
    mw = mw_ref[...]
    f = mw[:, 0:1] * y0_ref[cur] + mw[:, 1:2] * y1_ref[cur]
    out = _layer_norm(DN_ALPHA * x_ref[...] + f, g_ref[...], b_ref[...])
    if len(out_refs) == 1:
        out_refs[0][...] = out
    else:
        @pl.when(i < n_first)
        def _():
            out_refs[0][...] = out

        @pl.when(i == n_first)
        def _():
            out_refs[1][...] = out


def _combine(x, ys, dest, mw, g, b, t, split):
    n = x.shape[0]
    zero = lambda i: (0, 0)
    n_first = n // t - 1
    last = n // t - 1
    if split:
        out_shape = (jax.ShapeDtypeStruct((n - t, D_MODEL), F32), jax.ShapeDtypeStruct((t, D_MODEL), F32))
        out_specs = (pl.BlockSpec((t, D_MODEL), lambda i: (jnp.minimum(i, n_first - 1), 0)),
                     pl.BlockSpec((t, D_MODEL), zero))
    else:
        out_shape = jax.ShapeDtypeStruct((n, D_MODEL), F32)
        out_specs = pl.BlockSpec((t, D_MODEL), lambda i: (i, 0))
    return pl.pallas_call(
        functools.partial(_combine_body, n_first),
        out_shape=out_shape,
        grid=(n // t,),
        in_specs=[
            pl.BlockSpec((2 * t,), lambda i: (i,), memory_space=pltpu.SMEM),
            pl.BlockSpec((2 * t,), lambda i: (jnp.minimum(i + 1, last),), memory_space=pltpu.SMEM),
            pl.BlockSpec((t, D_MODEL), lambda i: (i, 0)),
            pl.BlockSpec((t, 2), lambda i: (i, 0)),
            pl.BlockSpec((1, D_MODEL), zero),
            pl.BlockSpec((1, D_MODEL), zero),
            pl.BlockSpec(memory_space=pl.ANY),
        ],
        out_specs=out_specs,
        scratch_shapes=[pltpu.VMEM((2, t, D_MODEL), F32), pltpu.VMEM((2, t, D_MODEL), F32),
                        pltpu.SemaphoreType.DMA((2,))],
        compiler_params=_params(("arbitrary",)),
        name="combine",
    )(dest, dest, x, mw, g, b, ys)


def _moe(x, mi, mw, cnt, w1, w3, w2, g, b, t, split):
    n = x.shape[0]
    tm = EXPERT_TILE
    n_tiles = (2 * n + N_EXPERTS * (tm - 1)) // tm
    counts = cnt[:N_EXPERTS, 0]
    tiles = (counts + tm - 1) // tm
    tile_end = jnp.cumsum(tiles)
    start = (tile_end - tiles) * tm
    n_used = tile_end[-1:].astype(jnp.int32)
    tile_expert = jnp.sum((tile_end[None, :] <= jnp.arange(n_tiles)[:, None]).astype(jnp.int32), axis=1)
    tile_expert = jnp.minimum(tile_expert, N_EXPERTS - 1)
    last_tile = jnp.where(tiles > 0, tile_end - 1, -1).astype(jnp.int32)
    has = tiles > 0
    next_of = [jnp.int32(-1)] * N_EXPERTS
    for e in range(N_EXPERTS - 2, -1, -1):
        next_of[e] = jnp.where(has[e + 1], e + 1, next_of[e + 1])
    per_expert = jnp.stack([tile_end - tiles, jnp.stack(next_of), (jnp.cumsum(has) - 1) % 2]).astype(jnp.int32)
    onehot = (tile_expert[None, :, None] == jnp.arange(N_EXPERTS)[None, None, :]).astype(jnp.int32)
    first_tile, nxt, slot = jnp.sum(onehot * per_expert[:, None, :], axis=2)
    first = (first_tile == jnp.arange(n_tiles)).astype(jnp.int32)
    experts = mi[:, 0:2, :]
    group_start = sum(jnp.where(experts == e, start[e], 0) for e in range(N_EXPERTS))
    dest = (group_start + mi[:, 2:4, :]).astype(jnp.int32).reshape(2 * n)
    mw = jnp.swapaxes(mw[:, 0:2, :], 1, 2).reshape(n, 2)
    xs = _dispatch(x, dest, last_tile, n_used, n_tiles, t)
    ys = _experts(xs, tile_expert.astype(jnp.int32), first, nxt, slot, n_used, w1, w3, w2)
    return _combine(x, ys, dest, mw, g, b, t, split)


def _block_diag(w):
    layers, g, c, _ = w.shape
    out = jnp.zeros((layers, g * c, g * c), w.dtype)
    for i in range(g):
        out = out.at[:, i * c:(i + 1) * c, i * c:(i + 1) * c].set(w[:, i])
    return out


def _bias_pairs(table, t_q, n_keys, offset):
    layers = table.shape[0]
    hi = t_q - 1 + offset
    span = t_q - 1 + n_keys
    cols = np.clip(hi - np.arange(span), -REL_CLIP, REL_CLIP) + REL_CLIP
    rev = table.reshape(layers * N_HEADS, -1)[:, cols].astype(F32)
    skew = jnp.tile(rev, (1, t_q + 1))[:, :t_q * (span + 1)].reshape(layers * N_HEADS, t_q, span + 1)
    slab = skew[:, ::-1, :n_keys]
    return slab.reshape(layers, HEAD_PAIRS, 2 * t_q, n_keys)


def _prompt_bias_slabs(table):
    bias = _bias_pairs(table, CHUNK, BAND, ATTN_REACH)
    col = np.arange(BAND)[None, :]
    first_valid = np.concatenate([[0], (LEFT_CHUNKS - np.arange(LEFT_CHUNKS)) * CHUNK])[:, None]
    valid = jnp.asarray(col >= first_valid)
    return jnp.where(valid[None, :, None, None, :], bias[:, None], NEG_INF)


def kernel(x_prompt, x_sample, cache_k, cache_v, state_pool, w_in, b_in, w_pool_grp, pool_scale,
           rel_table, w_pool_br, w_attn_br, w_out, ln1_g, ln1_b, ln2_g, ln2_b,
           w1_dense, w3_dense, w2_dense, w_router, b_router, w1_exp, w3_exp, w2_exp):
    bp, tp, d = x_prompt.shape
    bs, ts, _ = x_sample.shape
    n_p, n_s = bp * tp, bs * ts
    n = n_p + n_s
    depth = w_in.shape[0]
    keep_s = cache_k.shape[2]
    keep_p = min(ATTN_REACH, tp)

    x = (x_prompt.reshape(n_p, d), x_sample.reshape(n_s, d))
    ck = cache_k.reshape(depth, bs, keep_s, ATTN_WIDTH)
    cv = cache_v.reshape(depth, bs, keep_s, ATTN_WIDTH)
    hist = jnp.pad(state_pool, ((0, 0), (0, 0), (HIST_ROWS - POOL_HIST, 0), (0, 0)))
    assert tp % ATTN_REACH == 0 and keep_p == ATTN_REACH and ts >= POOL_HIST

    rows = lambda v: v[:, None, :].astype(F32)
    w_in_bf, b_in_r = w_in.astype(BF16), rows(b_in)
    wgrp, wp_bf, wa_bf, wo_bf = (a.astype(BF16) for a in (_block_diag(w_pool_grp), w_pool_br, w_attn_br, w_out))
    scale_r, g1, b1, g2, b2 = (rows(a) for a in (pool_scale, ln1_g, ln1_b, ln2_g, ln2_b))
    dense_bf = [a.astype(BF16) for a in (w1_dense, w3_dense, w2_dense)]
    bias_p = _prompt_bias_slabs(rel_table)
    bias_s = _bias_pairs(rel_table, ts, keep_s + ts, keep_s)

    k_buf = v_buf = None
    kp_new, vp_new, pp_new, ps_new = [], [], [], []
    for l in range(depth):
        weights = (wgrp[l], scale_r[l], wp_bf[l], wa_bf[l], wo_bf[l], g1[l], b1[l])
        z32, z16 = _in_proj(x, w_in_bf[l], b_in_r[l], n_s)
        attn_p, k_tail, v_tail = _attn_prompt(z32, z16, bias_p[l], bp, tp)
        attn_s, k_buf, v_buf = _attn_sample(z32, z16, ck, cv, bias_s[l], l, n_p, bs, ts, k_buf, v_buf)
        j = l // 2
        last = l == depth - 1
        if l % 2 == 0:
            x1, u_tail = _mix(z32, z16, attn_p, attn_s, x, hist[l], weights, n_p, tp)
            x = _ffn_dense(x1, dense_bf[0][j], dense_bf[1][j], dense_bf[2][j], g2[l], b2[l])
            if last:
                x = (x[:n_p], x[n_p:])
        else:
            x1, u_tail, mi, mw, cnt = _mix(z32, z16, attn_p, attn_s, x, hist[l], weights, n_p, tp,
                                           router=_router_operands(w_router[j], b_router[j]))
            x = _moe(x1, mi, mw, cnt, w1_exp[j], w3_exp[j], w2_exp[j], g2[l], b2[l], n_s, split=last)

        kp_new.append(k_tail.reshape(bp, keep_p, N_HEADS, HEAD_DIM))
        vp_new.append(v_tail.reshape(bp, keep_p, N_HEADS, HEAD_DIM))
        pp_new.append(u_tail[:, HIST_ROWS - POOL_HIST:])
        ps_new.append(z32[n_p:, :POOL_WIDTH].reshape(bs, ts, POOL_WIDTH)[:, ts - POOL_HIST:])

    shape_s = (depth, bs, keep_s, N_HEADS, HEAD_DIM)
    return (x[0].reshape(bp, tp, d), x[1].reshape(bs, ts, d),
            jnp.stack(kp_new), jnp.stack(vp_new), jnp.stack(pp_new),
            k_buf.reshape(shape_s), v_buf.reshape(shape_s), jnp.stack(ps_new))
```

```python
import functools

import jax
import jax.numpy as jnp
import numpy as np
from jax import lax
from jax.experimental import pallas as pl
from jax.experimental.pallas import tpu as pltpu

F32 = jnp.float32
BF16 = jnp.bfloat16

D_MODEL = 1024
N_HEADS = 8
HEAD_DIM = 64
ATTN_WIDTH = N_HEADS * HEAD_DIM
CHUNK = 64
LEFT_CHUNKS = 8
BAND = (LEFT_CHUNKS + 1) * CHUNK
ATTN_REACH = LEFT_CHUNKS * CHUNK
REL_CLIP = 256
ATTN_SCALE = HEAD_DIM ** -0.5
POOL_WIDTH = 512
POOL_WINDOWS = (2, 4, 8, 16)
POOL_GROUP = POOL_WIDTH // len(POOL_WINDOWS)
POOL_HIST = max(POOL_WINDOWS) - 1
HIST_ROWS = POOL_HIST + 1
LEAD_ROWS = 8
D_FF = 2816
N_EXPERTS = 8
PAST_LEN = 4096
DEPTH = 2
DN_ALPHA = (2 * DEPTH) ** 0.25
LN_EPS = 1e-5
NEG_INF = -1e30
IN_WIDTH = POOL_WIDTH + 3 * ATTN_WIDTH + 2 * D_MODEL
HEAD_PAIRS = N_HEADS // 2
PAIR_W = 2 * HEAD_DIM
LANES = 128
ROUTE_ROWS = 16

Z32_WIDTH = POOL_WIDTH + 2 * ATTN_WIDTH
Z16_WIDTH = 2 * D_MODEL + 3 * ATTN_WIDTH
COL_U, COL_K, COL_V = 0, 1, 2
COL_GP, COL_GA = 0, 1
COL_Q = 2 * D_MODEL // ATTN_WIDTH
COL_K16, COL_V16 = COL_Q + 1, COL_Q + 2
IN_SPLITS = (0, POOL_WIDTH, POOL_WIDTH + ATTN_WIDTH, POOL_WIDTH + 3 * ATTN_WIDTH, IN_WIDTH)

EXPERT_TILE = 512
FF_CHUNK = 256
FF_PIECES = D_FF // FF_CHUNK
ISSUE_UNROLL = 8
DENSE_SUBTILES = 4
VMEM_LIMIT = 56 * 1024 * 1024
EXPERTS_VMEM_LIMIT = 62 * 1024 * 1024


def _pick(n, candidates):
    for c in candidates:
        if n % c == 0:
            return c
    raise ValueError(f"no tile in {candidates} divides {n}")


def _params(sem, vmem=None):
    return pltpu.CompilerParams(dimension_semantics=sem, vmem_limit_bytes=vmem or VMEM_LIMIT)


def _layer_norm(r, g, b):
    mu = jnp.mean(r, axis=-1, keepdims=True)
    c = r - mu
    var = jnp.mean(c * c, axis=-1, keepdims=True)
    return c * lax.rsqrt(var + LN_EPS) * g + b


def _row_sources(x, t):
    if not isinstance(x, tuple):
        return [x], [pl.BlockSpec((t, x.shape[1]), lambda i: (i, 0))], (lambda refs, i: refs[0][...])
    first, last = x
    assert last.shape[0] == t and first.shape[0] % t == 0
    n_first = first.shape[0] // t
    specs = [pl.BlockSpec((t, first.shape[1]), lambda i: (jnp.minimum(i, n_first - 1), 0)),
             pl.BlockSpec((t, last.shape[1]), lambda i: (0, 0))]
    return [first, last], specs, (lambda refs, i: jnp.where(i < n_first, refs[0][...], refs[1][...]))


def _inproj_body(n_src, select, *refs):
    w_ref, b_ref, z32_ref, z16_ref = refs[n_src:]
    x = select(refs[:n_src], pl.program_id(0)).astype(BF16)
    z = jnp.dot(x, w_ref[...], preferred_element_type=F32) + b_ref[...]
    u, q, kv, gates = (z[:, a:b] for a, b in zip(IN_SPLITS[:-1], IN_SPLITS[1:]))
    z32_ref[...] = jnp.concatenate([u, kv], axis=-1)
    z16_ref[...] = jnp.concatenate([gates, q, kv], axis=-1).astype(BF16)


def _in_proj(x, w_bf, b, t):
    srcs, specs, select = _row_sources(x, t)
    n = sum(s.shape[0] for s in srcs)
    zero = lambda i: (0, 0)
    return pl.pallas_call(
        functools.partial(_inproj_body, len(srcs), select),
        out_shape=(jax.ShapeDtypeStruct((n, Z32_WIDTH), F32), jax.ShapeDtypeStruct((n, Z16_WIDTH), BF16)),
        grid=(n // t,),
        in_specs=specs + [pl.BlockSpec((D_MODEL, IN_WIDTH), zero), pl.BlockSpec((1, IN_WIDTH), zero)],
        out_specs=(pl.BlockSpec((t, Z32_WIDTH), lambda i: (i, 0)), pl.BlockSpec((t, Z16_WIDTH), lambda i: (i, 0))),
        compiler_params=_params(("arbitrary",)),
        name="in_proj",
    )(*srcs, w_bf, b)


def _pair_scores(q_pair, k_pair, bias):
    lane = lax.broadcasted_iota(jnp.int32, q_pair.shape, 1)
    qs = q_pair.astype(F32) * ATTN_SCALE
    q2 = jnp.concatenate([jnp.where(lane < HEAD_DIM, qs, 0.0), jnp.where(lane >= HEAD_DIM, qs, 0.0)], axis=0)
    s = lax.dot_general(q2.astype(BF16), k_pair, (((1,), (1,)), ((), ())), preferred_element_type=F32)
    return s + bias


def _pair_output(s, v_pair):
    rows = s.shape[0] // 2
    m = jnp.max(s, axis=-1, keepdims=True)
    e = jnp.exp(s - m)
    l = jnp.sum(e, axis=-1, keepdims=True)
    o2 = jnp.dot(e.astype(BF16), v_pair, preferred_element_type=F32) / l
    lane = lax.broadcasted_iota(jnp.int32, (rows, PAIR_W), 1)
    return jnp.where(lane < HEAD_DIM, o2[:rows], o2[rows:])


def _attn_prompt_body(q_ref, kp_ref, kc_ref, vp_ref, vc_ref, kl_ref, vl_ref, bias_ref,
                      o_ref, kt_ref, vt_ref, kext_ref, vext_ref):
    blk = q_ref.shape[0]
    j = pl.program_id(1)

    @pl.when(j == pl.num_programs(1) - 1)
    def _():
        kt_ref[0] = kl_ref[...]
        vt_ref[0] = vl_ref[...]

    kext_ref[0:blk, :] = kp_ref[...]
    kext_ref[blk:2 * blk, :] = kc_ref[...]
    vext_ref[0:blk, :] = vp_ref[...]
    vext_ref[blk:2 * blk, :] = vc_ref[...]
    def chunk(c, carry):
        q0 = pl.multiple_of(c * CHUNK, CHUNK)
        slab = jnp.where(j == 0, c + 1, 0)
        pairs = [slice(hp * PAIR_W, (hp + 1) * PAIR_W) for hp in range(HEAD_PAIRS)]
        s = jnp.concatenate([_pair_scores(q_ref[pl.ds(q0, CHUNK), lanes], kext_ref[pl.ds(q0, BAND), lanes],
                                          bias_ref[slab, hp]) for hp, lanes in enumerate(pairs)], axis=0)
        m = jnp.max(s, axis=-1, keepdims=True)
        e = jnp.exp(s - m)
        inv = 1.0 / jnp.sum(e, axis=-1, keepdims=True)
        p = e.astype(BF16)
        lane = lax.broadcasted_iota(jnp.int32, (CHUNK, PAIR_W), 1)
        for hp, lanes in enumerate(pairs):
            rows = slice(hp * 2 * CHUNK, (hp + 1) * 2 * CHUNK)
            o2 = jnp.dot(p[rows], vext_ref[pl.ds(q0, BAND), lanes], preferred_element_type=F32) * inv[rows]
            o_ref[pl.ds(q0, CHUNK), lanes] = jnp.where(lane < HEAD_DIM, o2[:CHUNK], o2[CHUNK:])
        return carry

    lax.fori_loop(0, blk // CHUNK, chunk, 0, unroll=8)


def _attn_prompt(z32, z16, bias_pairs, batch, seq):
    blk = ATTN_REACH
    per_seq = seq // blk
    rows = batch * seq

    def cur(col):
        return lambda b, j: (b * per_seq + j, col)

    def prev(col):
        return lambda b, j: (b * per_seq + jnp.maximum(j - 1, 0), col)

    def last(col):
        return lambda b, j: (b * per_seq + per_seq - 1, col)

    tail = jax.ShapeDtypeStruct((batch, blk, ATTN_WIDTH), F32)
    tail_spec = pl.BlockSpec((1, blk, ATTN_WIDTH), lambda b, j: (b, 0, 0))
    return pl.pallas_call(
        _attn_prompt_body,
        out_shape=(jax.ShapeDtypeStruct((rows, ATTN_WIDTH), F32), tail, tail),
        grid=(batch, per_seq),
        in_specs=[
            pl.BlockSpec((blk, ATTN_WIDTH), cur(COL_Q)),
            pl.BlockSpec((blk, ATTN_WIDTH), prev(COL_K16)),
            pl.BlockSpec((blk, ATTN_WIDTH), cur(COL_K16)),
            pl.BlockSpec((blk, ATTN_WIDTH), prev(COL_V16)),
            pl.BlockSpec((blk, ATTN_WIDTH), cur(COL_V16)),
            pl.BlockSpec((blk, ATTN_WIDTH), last(COL_K)),
            pl.BlockSpec((blk, ATTN_WIDTH), last(COL_V)),
            pl.BlockSpec((1 + LEFT_CHUNKS, HEAD_PAIRS, 2 * CHUNK, BAND), lambda b, j: (0, 0, 0, 0)),
        ],
        out_specs=(pl.BlockSpec((blk, ATTN_WIDTH), lambda b, j: (b * per_seq + j, 0)), tail_spec, tail_spec),
        scratch_shapes=[pltpu.VMEM((2 * blk, ATTN_WIDTH), BF16), pltpu.VMEM((2 * blk, ATTN_WIDTH), BF16)],
        compiler_params=_params(("arbitrary", "arbitrary")),
        name="attn_prompt",
    )(z16, z16, z16, z16, z16, z32, z32, bias_pairs)


def _attn_sample_body(q_ref, kn_ref, vn_ref, ck_ref, cv_ref, bias_ref, *rest):
    o_ref, ko_ref, vo_ref, kall_ref, vall_ref = rest[-5:]
    keep = ck_ref.shape[2]
    t = q_ref.shape[0]
    ck = ck_ref[0, 0]
    cv = cv_ref[0, 0]
    kn = kn_ref[...]
    vn = vn_ref[...]
    kall_ref[0:keep, :] = ck.astype(BF16)
    kall_ref[keep:keep + t, :] = kn.astype(BF16)
    vall_ref[0:keep, :] = cv.astype(BF16)
    vall_ref[keep:keep + t, :] = vn.astype(BF16)
    for hp in range(HEAD_PAIRS):
        lanes = slice(hp * PAIR_W, (hp + 1) * PAIR_W)
        s = _pair_scores(q_ref[:, lanes], kall_ref[:, lanes], bias_ref[hp])
        o_ref[:, lanes] = _pair_output(s, vall_ref[:, lanes])
    ko_ref[0, 0, 0:keep - t, :] = ck[t:keep]
    ko_ref[0, 0, keep - t:keep, :] = kn
    vo_ref[0, 0, 0:keep - t, :] = cv[t:keep]
    vo_ref[0, 0, keep - t:keep, :] = vn
    for later in range(1, ko_ref.shape[0]):
        ko_ref[later] = jnp.zeros(ko_ref.shape[1:], F32)
        vo_ref[later] = jnp.zeros(vo_ref.shape[1:], F32)


def _attn_sample(z32, z16, cache_k, cache_v, bias_pairs, layer, row0, streams, t, k_buf, v_buf):
    depth, _, keep, _ = cache_k.shape
    blk0 = row0 // t
    ins = [z16, z32, z32, cache_k, cache_v, bias_pairs]
    in_specs = [
        pl.BlockSpec((t, ATTN_WIDTH), lambda s: (blk0 + s, COL_Q)),
        pl.BlockSpec((t, ATTN_WIDTH), lambda s: (blk0 + s, COL_K)),
        pl.BlockSpec((t, ATTN_WIDTH), lambda s: (blk0 + s, COL_V)),
        pl.BlockSpec((1, 1, keep, ATTN_WIDTH), lambda s: (layer, s, 0, 0)),
        pl.BlockSpec((1, 1, keep, ATTN_WIDTH), lambda s: (layer, s, 0, 0)),
        pl.BlockSpec((HEAD_PAIRS, 2 * t, keep + t), lambda s: (0, 0, 0)),
    ]
    aliases = {}
    if k_buf is None:
        assert layer == 0
        buf_spec = pl.BlockSpec((depth, 1, keep, ATTN_WIDTH), lambda s: (0, s, 0, 0))
    else:
        aliases = {len(ins): 1, len(ins) + 1: 2}
        ins += [k_buf, v_buf]
        in_specs += [pl.BlockSpec(memory_space=pl.ANY), pl.BlockSpec(memory_space=pl.ANY)]
        buf_spec = pl.BlockSpec((1, 1, keep, ATTN_WIDTH), lambda s: (layer, s, 0, 0))
    buf = jax.ShapeDtypeStruct(cache_k.shape, F32)
    return pl.pallas_call(
        _attn_sample_body,
        out_shape=(jax.ShapeDtypeStruct((streams * t, ATTN_WIDTH), F32), buf, buf),
        grid=(streams,),
        in_specs=in_specs,
        out_specs=(
            pl.BlockSpec((t, ATTN_WIDTH), lambda s: (s, 0)),
            buf_spec,
            buf_spec,
        ),
        scratch_shapes=[pltpu.VMEM((keep + t, ATTN_WIDTH), BF16), pltpu.VMEM((keep + t, ATTN_WIDTH), BF16)],
        input_output_aliases=aliases,
        compiler_params=_params(("arbitrary",)),
        name="attn_sample",
    )(*ins)


def _merge(pooled, u, attn, gp, ga, x, wgrp_ref, scale_ref, wp_ref, wa_ref, wo_ref, g_ref, b_ref):
    pooled = pooled - u
    pool_y = jnp.dot(pooled.astype(BF16), wgrp_ref[...], preferred_element_type=F32) * scale_ref[...]
    mp = jnp.dot(pool_y.astype(BF16), wp_ref[...], preferred_element_type=F32)
    ma = jnp.dot(attn.astype(BF16), wa_ref[...], preferred_element_type=F32)
    m = jax.nn.sigmoid(gp) * mp + jax.nn.sigmoid(ga) * ma
    y = jnp.dot(m.astype(BF16), wo_ref[...], preferred_element_type=F32)
    return _layer_norm(DN_ALPHA * x + y, g_ref[...], b_ref[...])


def _window_means(read, pos, shape_out):
    outs = []
    for g, w in enumerate(POOL_WINDOWS):
        lanes = slice(g * POOL_GROUP, (g + 1) * POOL_GROUP)
        s = read(0, lanes)
        for back in range(1, w):
            s = s + read(back, lanes)
        outs.append((s / jnp.minimum(pos + 1, w).astype(F32)).reshape(shape_out))
    return jnp.concatenate(outs, axis=-1)


def _window_means_doubling(buf_ref, lvl_ref, pos, t):
    g = POOL_GROUP
    lo, hi = LEAD_ROWS, LEAD_ROWS + HIST_ROWS + t
    out0 = LEAD_ROWS + HIST_ROWS
    lvl_ref[0, lo:hi, :] = buf_ref[lo:hi, :] + buf_ref[lo - 1:hi - 1, :]
    lvl_ref[1, lo:hi, g:] = lvl_ref[0, lo:hi, g:] + lvl_ref[0, lo - 2:hi - 2, g:]
    lvl_ref[2, lo:hi, 2 * g:] = lvl_ref[1, lo:hi, 2 * g:] + lvl_ref[1, lo - 4:hi - 4, 2 * g:]
    s16 = lvl_ref[2, out0:out0 + t, 3 * g:] + lvl_ref[2, out0 - 8:out0 - 8 + t, 3 * g:]
    sums = [lvl_ref[k, out0:out0 + t, k * g:(k + 1) * g] for k in range(3)] + [s16]
    return jnp.concatenate([s / jnp.minimum(pos + 1, w).astype(F32) for s, w in zip(sums, POOL_WINDOWS)], axis=-1)


def _mix_body(n_prompt_tiles, tiles_per_seq, with_router, n_src, select,
              u_ref, up_ref, hist_ref, attn_p_ref, attn_s_ref, gp_ref, ga_ref, *rest):
    x_refs, rest = rest[:n_src], rest[n_src:]
    (wgrp_ref, scale_ref, wp_ref, wa_ref, wo_ref, g_ref, b_ref), rest = rest[:7], rest[7:]
    if with_router:
        (wr_hi_ref, wr_lo_ref, br_ref, o_ref, tail_ref, mi_ref, mw_ref, cnt_ref,
         buf_ref, lvl_ref, sbuf_ref, pooled_ref, attn_ref, carry_ref) = rest
    else:
        o_ref, tail_ref, buf_ref, lvl_ref, sbuf_ref, pooled_ref, attn_ref = rest
    t = u_ref.shape[0]
    i = pl.program_id(0)
    u = u_ref[...]

    @pl.when(i == 0)
    def _():
        buf_ref[0:LEAD_ROWS, :] = jnp.zeros((LEAD_ROWS, POOL_WIDTH), F32)
        lvl_ref[:, 0:LEAD_ROWS, :] = jnp.zeros((lvl_ref.shape[0], LEAD_ROWS, POOL_WIDTH), F32)

    @pl.when(i < n_prompt_tiles)
    def _():
        tile = i % tiles_per_seq
        h0 = LEAD_ROWS
        buf_ref[h0:h0 + HIST_ROWS, :] = jnp.where(tile == 0, 0.0, up_ref[...])
        buf_ref[h0 + HIST_ROWS:h0 + HIST_ROWS + t, :] = u
        pos = tile * t + lax.broadcasted_iota(jnp.int32, (t, 1), 0)
        pooled_ref[...] = _window_means_doubling(buf_ref, lvl_ref, pos, t)
        attn_ref[...] = attn_p_ref[...]

        @pl.when(tile == tiles_per_seq - 1)
        def _():
            tail_ref[0] = u[t - HIST_ROWS:, :]

    @pl.when(i == n_prompt_tiles)
    def _():
        streams, hrows, _ = hist_ref.shape
        ts = t // streams
        sbuf_ref[:, 0:hrows, :] = hist_ref[...]
        sbuf_ref[:, hrows:hrows + ts, :] = u.reshape(streams, ts, POOL_WIDTH)
        pos = PAST_LEN + lax.broadcasted_iota(jnp.int32, (1, ts, 1), 1)
        read = lambda back, lanes: sbuf_ref[:, hrows - back:hrows - back + ts, lanes]
        pooled_ref[...] = _window_means(read, pos, (t, POOL_GROUP))
        attn_ref[...] = attn_s_ref[...]

    x1 = _merge(pooled_ref[...], u, attn_ref[...], gp_ref[...].astype(F32), ga_ref[...].astype(F32),
                select(x_refs, i), wgrp_ref, scale_ref, wp_ref, wa_ref, wo_ref, g_ref, b_ref)
    o_ref[...] = x1
    if with_router:
        _route(x1, wr_hi_ref, wr_lo_ref, br_ref, mi_ref, mw_ref, cnt_ref, carry_ref)


def _mix(z32, z16, attn_p, attn_s, x, hist, weights, n_prompt, seq, router=None):
    n = z32.shape[0]
    t = n - n_prompt
    x_srcs, x_specs, select = _row_sources(x, t)
    n_prompt_tiles = n_prompt // t
    streams, hrows, _ = hist.shape
    hist_per_tile = t // HIST_ROWS
    last_p = n_prompt_tiles - 1
    zero = lambda i: (0, 0)
    rows = lambda width: pl.BlockSpec((t, width), lambda i: (i, 0))
    tiles_per_seq = seq // t
    out_shape = [jax.ShapeDtypeStruct((n, D_MODEL), F32),
                 jax.ShapeDtypeStruct((n_prompt // seq, HIST_ROWS, POOL_WIDTH), F32)]
    out_specs = [rows(D_MODEL),
                 pl.BlockSpec((1, HIST_ROWS, POOL_WIDTH), lambda i: (jnp.minimum(i, last_p) // tiles_per_seq, 0, 0))]
    router_specs, router_scratch = [], []
    if router is not None:
        router_specs = [pl.BlockSpec((ROUTE_ROWS, D_MODEL), zero), pl.BlockSpec((ROUTE_ROWS, D_MODEL), zero),
                        pl.BlockSpec((ROUTE_ROWS, 1), zero)]
        per_tile = lambda dtype: jax.ShapeDtypeStruct((n // t, 8, t), dtype)
        out_shape += [per_tile(jnp.int32), per_tile(F32), jax.ShapeDtypeStruct((ROUTE_ROWS, LANES), jnp.int32)]
        out_specs += [pl.BlockSpec((1, 8, t), lambda i: (i, 0, 0)), pl.BlockSpec((1, 8, t), lambda i: (i, 0, 0)),
                      pl.BlockSpec((ROUTE_ROWS, LANES), zero)]
        router_scratch = [pltpu.VMEM((ROUTE_ROWS, LANES), F32)]
    out = pl.pallas_call(
        functools.partial(_mix_body, n_prompt_tiles, seq // t, router is not None, len(x_srcs), select),
        out_shape=out_shape,
        grid=(n_prompt_tiles + 1,),
        in_specs=[
            pl.BlockSpec((t, POOL_WIDTH), lambda i: (i, COL_U)),
            pl.BlockSpec((HIST_ROWS, POOL_WIDTH), lambda i: (jnp.maximum(i * hist_per_tile - 1, 0), COL_U)),
            pl.BlockSpec((streams, hrows, POOL_WIDTH), lambda i: (0, 0, 0)),
            pl.BlockSpec((t, ATTN_WIDTH), lambda i: (jnp.minimum(i, last_p), 0)),
            pl.BlockSpec((t, ATTN_WIDTH), zero),
            pl.BlockSpec((t, D_MODEL), lambda i: (i, COL_GP)),
            pl.BlockSpec((t, D_MODEL), lambda i: (i, COL_GA)),
        ] + x_specs + [
            pl.BlockSpec((POOL_WIDTH, POOL_WIDTH), zero),
            pl.BlockSpec((1, POOL_WIDTH), zero),
            pl.BlockSpec((POOL_WIDTH, D_MODEL), zero),
            pl.BlockSpec((ATTN_WIDTH, D_MODEL), zero),
            pl.BlockSpec((D_MODEL, D_MODEL), zero),
            pl.BlockSpec((1, D_MODEL), zero),
            pl.BlockSpec((1, D_MODEL), zero),
        ] + router_specs,
        out_specs=out_specs,
        scratch_shapes=[
            pltpu.VMEM((LEAD_ROWS + HIST_ROWS + t, POOL_WIDTH), F32),
            pltpu.VMEM((3, LEAD_ROWS + HIST_ROWS + t, POOL_WIDTH), F32),
            pltpu.VMEM((streams, hrows + t // streams, POOL_WIDTH), F32),
            pltpu.VMEM((t, POOL_WIDTH), F32),
            pltpu.VMEM((t, ATTN_WIDTH), F32),
        ] + router_scratch,
        compiler_params=_params(("arbitrary",)),
        name="mix",
    )(z32, z32, hist, attn_p, attn_s, z16, z16, *x_srcs, *weights, *(router or ()))
    return out


def _ffn_dense_body(x_ref, w1_ref, w3_ref, w2_ref, g_ref, b_ref, o_ref):
    sub = x_ref.shape[0] // DENSE_SUBTILES
    for k in range(DENSE_SUBTILES):
        rows = slice(k * sub, (k + 1) * sub)
        x = x_ref[rows, :]
        xb = x.astype(BF16)
        a = jnp.dot(xb, w1_ref[...], preferred_element_type=F32)
        c = jnp.dot(xb, w3_ref[...], preferred_element_type=F32)
        h = (jax.nn.silu(a) * c).astype(BF16)
        f = jnp.dot(h, w2_ref[...], preferred_element_type=F32)
        o_ref[rows, :] = _layer_norm(DN_ALPHA * x + f, g_ref[...], b_ref[...])


def _ffn_dense(x, w1, w3, w2, g, b):
    n = x.shape[0]
    t = _pick(n, (1280, 640, 512, 256))
    zero = lambda i: (0, 0)
    return pl.pallas_call(
        _ffn_dense_body,
        out_shape=jax.ShapeDtypeStruct((n, D_MODEL), F32),
        grid=(n // t,),
        in_specs=[
            pl.BlockSpec((t, D_MODEL), lambda i: (i, 0)),
            pl.BlockSpec((D_MODEL, D_FF), zero),
            pl.BlockSpec((D_MODEL, D_FF), zero),
            pl.BlockSpec((D_FF, D_MODEL), zero),
            pl.BlockSpec((1, D_MODEL), zero),
            pl.BlockSpec((1, D_MODEL), zero),
        ],
        out_specs=pl.BlockSpec((t, D_MODEL), lambda i: (i, 0)),
        compiler_params=_params(("arbitrary",)),
        name="ffn_dense",
    )(x, w1, w3, w2, g, b)


def _route(x, wr_hi_ref, wr_lo_ref, br_ref, mi_ref, mw_ref, cnt_ref, carry_ref):
    t = x.shape[0]

    @pl.when(pl.program_id(0) == 0)
    def _():
        carry_ref[...] = jnp.zeros_like(carry_ref)

    x_hi = x.astype(BF16)
    x_lo = (x - x_hi.astype(F32)).astype(BF16)
    nt = (((1,), (1,)), ((), ()))
    logits = (lax.dot_general(wr_hi_ref[...], x_hi, nt, preferred_element_type=F32)
              + lax.dot_general(wr_hi_ref[...], x_lo, nt, preferred_element_type=F32)
              + lax.dot_general(wr_lo_ref[...], x_hi, nt, preferred_element_type=F32)) + br_ref[...]
    row = lax.broadcasted_iota(jnp.int32, (ROUTE_ROWS, t), 0)
    row_f = row.astype(F32)
    logits = jnp.where(row < N_EXPERTS, logits, -jnp.inf)
    v0 = jnp.max(logits, axis=0, keepdims=True)
    e0 = jnp.min(jnp.where(logits == v0, row_f, float(ROUTE_ROWS)), axis=0, keepdims=True)
    rest = jnp.where(row_f == e0, -jnp.inf, logits)
    v1 = jnp.max(rest, axis=0, keepdims=True)
    e1 = jnp.min(jnp.where(rest == v1, row_f, float(ROUTE_ROWS)), axis=0, keepdims=True)
    ex = jnp.exp(v1 - v0)
    w0 = 1.0 / (1.0 + ex)
    w1 = ex / (1.0 + ex)
    oh0 = (row_f == e0).astype(F32)
    oh1 = (row_f == e1).astype(F32)
    before = (lax.broadcasted_iota(jnp.int32, (t, t), 0) < lax.broadcasted_iota(jnp.int32, (t, t), 1)).astype(BF16)
    pre0 = jnp.dot(oh0.astype(BF16), before, preferred_element_type=F32)
    pre1 = jnp.dot(oh1.astype(BF16), before, preferred_element_type=F32)
    cnt0 = jnp.sum(oh0, axis=1, keepdims=True)
    cnt1 = jnp.sum(oh1, axis=1, keepdims=True)
    carry = carry_ref[:, 0:1]
    rank0 = jnp.sum(oh0 * (carry + pre0), axis=0, keepdims=True)
    rank1 = jnp.sum(oh1 * (carry + cnt0 + pre1), axis=0, keepdims=True)
    carry = jnp.broadcast_to(carry + cnt0 + cnt1, carry_ref.shape)
    carry_ref[...] = carry
    cnt_ref[...] = carry.astype(jnp.int32)
    r8 = lax.broadcasted_iota(jnp.int32, (8, t), 0)
    mi = jnp.where(r8 == 0, e0, jnp.where(r8 == 1, e1, jnp.where(r8 == 2, rank0, jnp.where(r8 == 3, rank1, 0.0))))
    mi_ref[0] = mi.astype(jnp.int32)
    mw_ref[0] = jnp.where(r8 == 0, w0, jnp.where(r8 == 1, w1, 0.0))


def _router_operands(w_r, b_r):
    wr = jnp.zeros((ROUTE_ROWS, D_MODEL), F32).at[:N_EXPERTS].set(w_r.T)
    br = jnp.zeros((ROUTE_ROWS, 1), F32).at[:N_EXPERTS, 0].set(b_r)
    wr_hi = wr.astype(BF16)
    wr_lo = (wr - wr_hi.astype(F32)).astype(BF16)
    return wr_hi, wr_lo, br


def _dispatch_body(dest_ref, last_ref, nused_ref, x_ref, xs_ref, zero_ref, rows_ref, sem_ref, row_sem_ref):
    t = dest_ref.shape[0] // 2
    tm = zero_ref.shape[0]
    n_tiles = xs_ref.shape[0] // tm
    i = pl.program_id(0)

    @pl.when(i == 0)
    def _():
        zero_ref[...] = jnp.zeros_like(zero_ref)

        def fill(tile):
            cp = pltpu.make_async_copy(zero_ref, xs_ref.at[pl.ds(pl.multiple_of(tile * tm, tm), tm), :], sem_ref.at[0])
            cp.start()
            cp.wait()

        for e in range(N_EXPERTS):
            @pl.when(last_ref[e] >= 0)
            def _():
                fill(last_ref[e])

        def unused(tile, carry):
            fill(tile)
            return carry

        lax.fori_loop(nused_ref[0], n_tiles, unused, 0)

    cur = i % 2
    rows_ref[cur] = x_ref[...]

    def issue(r, carry):
        for slot in range(2):
            d = dest_ref[slot * t + r]
            pltpu.make_async_copy(rows_ref.at[cur, pl.ds(r, 1), :], xs_ref.at[pl.ds(d, 1), :],
                                  row_sem_ref.at[cur]).start(priority=slot)
        return carry

    lax.fori_loop(0, t, issue, 0, unroll=ISSUE_UNROLL)

    def wait_rows(buf):
        pltpu.make_async_copy(xs_ref.at[pl.ds(0, 2 * t), :], xs_ref.at[pl.ds(0, 2 * t), :], row_sem_ref.at[buf]).wait()

    @pl.when(i > 0)
    def _():
        wait_rows(1 - cur)

    @pl.when(i == pl.num_programs(0) - 1)
    def _():
        wait_rows(cur)


def _dispatch(x, dest, last_tile, n_used, n_tiles, t):
    n = x.shape[0]
    tm = EXPERT_TILE
    return pl.pallas_call(
        _dispatch_body,
        out_shape=jax.ShapeDtypeStruct((n_tiles * tm, D_MODEL), F32),
        grid=(n // t,),
        in_specs=[
            pl.BlockSpec((2 * t,), lambda i: (i,), memory_space=pltpu.SMEM),
            pl.BlockSpec(memory_space=pltpu.SMEM),
            pl.BlockSpec(memory_space=pltpu.SMEM),
            pl.BlockSpec((t, D_MODEL), lambda i: (i, 0)),
        ],
        out_specs=pl.BlockSpec(memory_space=pl.ANY),
        scratch_shapes=[pltpu.VMEM((tm, D_MODEL), F32), pltpu.VMEM((2, t, D_MODEL), F32),
                        pltpu.SemaphoreType.DMA((1,)), pltpu.SemaphoreType.DMA((2,))],
        compiler_params=_params(("arbitrary",)),
        name="dispatch",
    )(dest, last_tile, n_used, x)


def _experts_body(te_ref, first_ref, next_ref, slot_ref, nused_ref, x_ref, w1_hbm, w3_hbm, w2_hbm, o_ref,
                  wb1_ref, wb3_ref, wb2_ref, st1_ref, st3_ref, st2_ref, sem_ref):
    i = pl.program_id(0)

    def piece_copies(e, c):
        buf = c % 2
        cols = pl.ds(c * FF_CHUNK, FF_CHUNK)
        return (pltpu.make_async_copy(w1_hbm.at[e, :, cols], st1_ref.at[buf], sem_ref.at[buf, 0]),
                pltpu.make_async_copy(w3_hbm.at[e, :, cols], st3_ref.at[buf], sem_ref.at[buf, 1]),
                pltpu.make_async_copy(w2_hbm.at[e, cols, :], st2_ref.at[buf], sem_ref.at[buf, 2]))

    def start_piece(e, c):
        for cp in piece_copies(e, c):
            cp.start()

    def finish_piece(e, c, slot):
        for cp in piece_copies(e, c):
            cp.wait()
        cols = slice(c * FF_CHUNK, (c + 1) * FF_CHUNK)
        wb1_ref[slot, :, cols] = st1_ref[c % 2].astype(BF16)
        wb3_ref[slot, :, cols] = st3_ref[c % 2].astype(BF16)
        wb2_ref[slot, cols, :] = st2_ref[c % 2].astype(BF16)

    def ff_piece(xb, slot, c):
        cols = slice(c * FF_CHUNK, (c + 1) * FF_CHUNK)
        a = jnp.dot(xb, wb1_ref[slot, :, cols], preferred_element_type=F32)
        g = jnp.dot(xb, wb3_ref[slot, :, cols], preferred_element_type=F32)
        h = (jax.nn.silu(a) * g).astype(BF16)
        return jnp.dot(h, wb2_ref[slot, cols, :], preferred_element_type=F32)

    @pl.when(i == 0)
    def _():
        e, slot = te_ref[0], slot_ref[0]
        start_piece(e, 0)
        for c in range(FF_PIECES):
            if c + 1 < FF_PIECES:
                start_piece(e, c + 1)
            finish_piece(e, c, slot)

    live = i < nused_ref[0]
    prefetch = jnp.logical_and(live, jnp.logical_and(first_ref[i] == 1, next_ref[i] >= 0))

    @pl.when(jnp.logical_and(live, jnp.logical_not(prefetch)))
    def _():
        xb = x_ref[...].astype(BF16)
        slot = slot_ref[i]
        f = ff_piece(xb, slot, 0)
        for c in range(1, FF_PIECES):
            f = f + ff_piece(xb, slot, c)
        o_ref[...] = f

    @pl.when(prefetch)
    def _():
        xb = x_ref[...].astype(BF16)
        slot, e_next = slot_ref[i], next_ref[i]
        start_piece(e_next, 0)
        f = None
        for c in range(FF_PIECES):
            if c + 1 < FF_PIECES:
                start_piece(e_next, c + 1)
            part = ff_piece(xb, slot, c)
            f = part if f is None else f + part
            finish_piece(e_next, c, 1 - slot)
        o_ref[...] = f

    @pl.when(jnp.logical_not(live))
    def _():
        o_ref[...] = jnp.zeros_like(o_ref)


def _experts(xs, tile_expert, first, nxt, slot, n_used, w1, w3, w2):
    tm = EXPERT_TILE
    n_tiles = xs.shape[0] // tm
    hbm = pl.BlockSpec(memory_space=pl.ANY)
    return pl.pallas_call(
        _experts_body,
        out_shape=jax.ShapeDtypeStruct(xs.shape, F32),
        grid_spec=pltpu.PrefetchScalarGridSpec(
            num_scalar_prefetch=5,
            grid=(n_tiles,),
            in_specs=[pl.BlockSpec((tm, D_MODEL), lambda i, te, fi, nx, sl, nu: (jnp.minimum(i, nu[0] - 1), 0)),
                      hbm, hbm, hbm],
            out_specs=pl.BlockSpec((tm, D_MODEL), lambda i, te, fi, nx, sl, nu: (i, 0)),
            scratch_shapes=[
                pltpu.VMEM((2, D_MODEL, D_FF), BF16), pltpu.VMEM((2, D_MODEL, D_FF), BF16),
                pltpu.VMEM((2, D_FF, D_MODEL), BF16),
                pltpu.VMEM((2, D_MODEL, FF_CHUNK), F32), pltpu.VMEM((2, D_MODEL, FF_CHUNK), F32),
                pltpu.VMEM((2, FF_CHUNK, D_MODEL), F32),
                pltpu.SemaphoreType.DMA((2, 3)),
            ],
        ),
        compiler_params=_params(("arbitrary",), EXPERTS_VMEM_LIMIT),
        name="experts",
    )(tile_expert, first, nxt, slot, n_used, xs, w1, w3, w2)


def _combine_body(n_first, dest_ref, next_ref, x_ref, mw_ref, g_ref, b_ref, ys_ref, *rest):
    out_refs, (y0_ref, y1_ref, sem_ref) = rest[:-3], rest[-3:]
    t = x_ref.shape[0]
    i = pl.program_id(0)
    cur = i % 2

    def gather(idx_ref, buf):
        def issue(r, carry):
            pltpu.make_async_copy(ys_ref.at[pl.ds(idx_ref[r], 1), :], y0_ref.at[buf, pl.ds(r, 1), :],
                                  sem_ref.at[buf]).start(priority=0)
            pltpu.make_async_copy(ys_ref.at[pl.ds(idx_ref[t + r], 1), :], y1_ref.at[buf, pl.ds(r, 1), :],
                                  sem_ref.at[buf]).start(priority=1)
            return carry

        lax.fori_loop(0, t, issue, 0, unroll=ISSUE_UNROLL)

    @pl.when(i == 0)
    def _():
        gather(dest_ref, 0)

    @pl.when(i + 1 < pl.num_programs(0))
    def _():
        gather(next_ref, 1 - cur)

    pltpu.make_async_copy(ys_ref.at[pl.ds(0, t), :], y0_ref.at[cur], sem_ref.at[cur]).wait()
    pltpu.make_async_copy(ys_ref.at[pl.ds(0, t), :], y1_ref.at[cur], sem_ref.at[cur]).wait()
    mw = mw_ref[...]
    f = mw[:, 0:1] * y0_ref[cur] + mw[:, 1:2] * y1_ref[cur]
    out = _layer_norm(DN_ALPHA * x_ref[...] + f, g_ref[...], b_ref[...])
    if len(out_refs) == 1:
        out_refs[0][...] = out
    else:
        @pl.when(i < n_first)
        def _():
            out_refs[0][...] = out

        @pl.when(i == n_first)
        def _():
            out_refs[1][...] = out


def _combine(x, ys, dest, mw, g, b, t, split):
    n = x.shape[0]
    zero = lambda i: (0, 0)
    n_first = n // t - 1
    last = n // t - 1
    if split:
        out_shape = (jax.ShapeDtypeStruct((n - t, D_MODEL), F32), jax.ShapeDtypeStruct((t, D_MODEL), F32))
        out_specs = (pl.BlockSpec((t, D_MODEL), lambda i: (jnp.minimum(i, n_first - 1), 0)),
                     pl.BlockSpec((t, D_MODEL), zero))
    else:
        out_shape = jax.ShapeDtypeStruct((n, D_MODEL), F32)
        out_specs = pl.BlockSpec((t, D_MODEL), lambda i: (i, 0))
    return pl.pallas_call(
        functools.partial(_combine_body, n_first),
        out_shape=out_shape,
        grid=(n // t,),
        in_specs=[
            pl.BlockSpec((2 * t,), lambda i: (i,), memory_space=pltpu.SMEM),
            pl.BlockSpec((2 * t,), lambda i: (jnp.minimum(i + 1, last),), memory_space=pltpu.SMEM),
            pl.BlockSpec((t, D_MODEL), lambda i: (i, 0)),
            pl.BlockSpec((t, 2), lambda i: (i, 0)),
            pl.BlockSpec((1, D_MODEL), zero),
            pl.BlockSpec((1, D_MODEL), zero),
            pl.BlockSpec(memory_space=pl.ANY),
        ],
        out_specs=out_specs,
        scratch_shapes=[pltpu.VMEM((2, t, D_MODEL), F32), pltpu.VMEM((2, t, D_MODEL), F32),
                        pltpu.SemaphoreType.DMA((2,))],
        compiler_params=_params(("arbitrary",)),
        name="combine",
    )(dest, dest, x, mw, g, b, ys)


def _moe(x, mi, mw, cnt, w1, w3, w2, g, b, t, split):
    n = x.shape[0]
    tm = EXPERT_TILE
    n_tiles = (2 * n + N_EXPERTS * (tm - 1)) // tm
    counts = cnt[:N_EXPERTS, 0]
    tiles = (counts + tm - 1) // tm
    tile_end = jnp.cumsum(tiles)
    start = (tile_end - tiles) * tm
    n_used = tile_end[-1:].astype(jnp.int32)
    tile_expert = jnp.sum((tile_end[None, :] <= jnp.arange(n_tiles)[:, None]).astype(jnp.int32), axis=1)
    tile_expert = jnp.minimum(tile_expert, N_EXPERTS - 1)
    last_tile = jnp.where(tiles > 0, tile_end - 1, -1).astype(jnp.int32)
    has = tiles > 0
    next_of = [jnp.int32(-1)] * N_EXPERTS
    for e in range(N_EXPERTS - 2, -1, -1):
        next_of[e] = jnp.where(has[e + 1], e + 1, next_of[e + 1])
    per_expert = jnp.stack([tile_end - tiles, jnp.stack(next_of), (jnp.cumsum(has) - 1) % 2]).astype(jnp.int32)
    onehot = (tile_expert[None, :, None] == jnp.arange(N_EXPERTS)[None, None, :]).astype(jnp.int32)
    first_tile, nxt, slot = jnp.sum(onehot * per_expert[:, None, :], axis=2)
    first = (first_tile == jnp.arange(n_tiles)).astype(jnp.int32)
    experts = mi[:, 0:2, :]
    group_start = sum(jnp.where(experts == e, start[e], 0) for e in range(N_EXPERTS))
    dest = (group_start + mi[:, 2:4, :]).astype(jnp.int32).reshape(2 * n)
    mw = jnp.swapaxes(mw[:, 0:2, :], 1, 2).reshape(n, 2)
    xs = _dispatch(x, dest, last_tile, n_used, n_tiles, t)
    ys = _experts(xs, tile_expert.astype(jnp.int32), first, nxt, slot, n_used, w1, w3, w2)
    return _combine(x, ys, dest, mw, g, b, t, split)


def _block_diag(w):
    layers, g, c, _ = w.shape
    out = jnp.zeros((layers, g * c, g * c), w.dtype)
    for i in range(g):
        out = out.at[:, i * c:(i + 1) * c, i * c:(i + 1) * c].set(w[:, i])
    return out


def _bias_pairs(table, t_q, n_keys, offset):
    layers = table.shape[0]
    hi = t_q - 1 + offset
    span = t_q - 1 + n_keys
    cols = np.clip(hi - np.arange(span), -REL_CLIP, REL_CLIP) + REL_CLIP
    rev = table.reshape(layers * N_HEADS, -1)[:, cols].astype(F32)
    skew = jnp.tile(rev, (1, t_q + 1))[:, :t_q * (span + 1)].reshape(layers * N_HEADS, t_q, span + 1)
    slab = skew[:, ::-1, :n_keys]
    return slab.reshape(layers, HEAD_PAIRS, 2 * t_q, n_keys)


def _prompt_bias_slabs(table):
    bias = _bias_pairs(table, CHUNK, BAND, ATTN_REACH)
    col = np.arange(BAND)[None, :]
    first_valid = np.concatenate([[0], (LEFT_CHUNKS - np.arange(LEFT_CHUNKS)) * CHUNK])[:, None]
    valid = jnp.asarray(col >= first_valid)
    return jnp.where(valid[None, :, None, None, :], bias[:, None], NEG_INF)


def kernel(x_prompt, x_sample, cache_k, cache_v, state_pool, w_in, b_in, w_pool_grp, pool_scale,
           rel_table, w_pool_br, w_attn_br, w_out, ln1_g, ln1_b, ln2_g, ln2_b,
           w1_dense, w3_dense, w2_dense, w_router, b_router, w1_exp, w3_exp, w2_exp):
    bp, tp, d = x_prompt.shape
    bs, ts, _ = x_sample.shape
    n_p, n_s = bp * tp, bs * ts
    n = n_p + n_s
    depth = w_in.shape[0]
    keep_s = cache_k.shape[2]
    keep_p = min(ATTN_REACH, tp)

    x = (x_prompt.reshape(n_p, d), x_sample.reshape(n_s, d))
    ck = cache_k.reshape(depth, bs, keep_s, ATTN_WIDTH)
    cv = cache_v.reshape(depth, bs, keep_s, ATTN_WIDTH)
    hist = jnp.pad(state_pool, ((0, 0), (0, 0), (HIST_ROWS - POOL_HIST, 0), (0, 0)))
    assert tp % ATTN_REACH == 0 and keep_p == ATTN_REACH and ts >= POOL_HIST

    rows = lambda v: v[:, None, :].astype(F32)
    w_in_bf, b_in_r = w_in.astype(BF16), rows(b_in)
    wgrp, wp_bf, wa_bf, wo_bf = (a.astype(BF16) for a in (_block_diag(w_pool_grp), w_pool_br, w_attn_br, w_out))
    scale_r, g1, b1, g2, b2 = (rows(a) for a in (pool_scale, ln1_g, ln1_b, ln2_g, ln2_b))
    dense_bf = [a.astype(BF16) for a in (w1_dense, w3_dense, w2_dense)]
    bias_p = _prompt_bias_slabs(rel_table)
    bias_s = _bias_pairs(rel_table, ts, keep_s + ts, keep_s)

    k_buf = v_buf = None
    kp_new, vp_new, pp_new, ps_new = [], [], [], []
    for l in range(depth):
        weights = (wgrp[l], scale_r[l], wp_bf[l], wa_bf[l], wo_bf[l], g1[l], b1[l])
        z32, z16 = _in_proj(x, w_in_bf[l], b_in_r[l], n_s)
        attn_p, k_tail, v_tail = _attn_prompt(z32, z16, bias_p[l], bp, tp)
        attn_s, k_buf, v_buf = _attn_sample(z32, z16, ck, cv, bias_s[l], l, n_p, bs, ts, k_buf, v_buf)
        j = l // 2
        last = l == depth - 1
        if l % 2 == 0:
            x1, u_tail = _mix(z32, z16, attn_p, attn_s, x, hist[l], weights, n_p, tp)
            x = _ffn_dense(x1, dense_bf[0][j], dense_bf[1][j], dense_bf[2][j], g2[l], b2[l])
            if last:
                x = (x[:n_p], x[n_p:])
        else:
            x1, u_tail, mi, mw, cnt = _mix(z32, z16, attn_p, attn_s, x, hist[l], weights, n_p, tp,
                                           router=_router_operands(w_router[j], b_router[j]))
            x = _moe(x1, mi, mw, cnt, w1_exp[j], w3_exp[j], w2_exp[j], g2[l], b2[l], n_s, split=last)

        kp_new.append(k_tail.reshape(bp, keep_p, N_HEADS, HEAD_DIM))
        vp_new.append(v_tail.reshape(bp, keep_p, N_HEADS, HEAD_DIM))
        pp_new.append(u_tail[:, HIST_ROWS - POOL_HIST:])
        ps_new.append(z32[n_p:, :POOL_WIDTH].reshape(bs, ts, POOL_WIDTH)[:, ts - POOL_HIST:])

    shape_s = (depth, bs, keep_s, N_HEADS, HEAD_DIM)
    return (x[0].reshape(bp, tp, d), x[1].reshape(bs, ts, d),
            jnp.stack(kp_new), jnp.stack(vp_new), jnp.stack(pp_new),
            k_buf.reshape(shape_s), v_buf.reshape(shape_s), jnp.stack(ps_new))
```

```python
import functools

import jax
import jax.numpy as jnp
import numpy as np
from jax import lax
from jax.experimental import pallas as pl
from jax.experimental.pallas import tpu as pltpu

F32 = jnp.float32
BF16 = jnp.bfloat16

D_MODEL = 1024
N_HEADS = 8
HEAD_DIM = 64
ATTN_WIDTH = N_HEADS * HEAD_DIM
CHUNK = 64
LEFT_CHUNKS = 8
BAND = (LEFT_CHUNKS + 1) * CHUNK
ATTN_REACH = LEFT_CHUNKS * CHUNK
REL_CLIP = 256
ATTN_SCALE = HEAD_DIM ** -0.5
POOL_WIDTH = 512
POOL_WINDOWS = (2, 4, 8, 16)
POOL_GROUP = POOL_WIDTH // len(POOL_WINDOWS)
POOL_HIST = max(POOL_WINDOWS) - 1
HIST_ROWS = POOL_HIST + 1
LEAD_ROWS = 8
D_FF = 2816
N_EXPERTS = 8
PAST_LEN = 4096
DEPTH = 2
DN_ALPHA = (2 * DEPTH) ** 0.25
LN_EPS = 1e-5
NEG_INF = -1e30
IN_WIDTH = POOL_WIDTH + 3 * ATTN_WIDTH + 2 * D_MODEL
HEAD_PAIRS = N_HEADS // 2
PAIR_W = 2 * HEAD_DIM
LANES = 128
ROUTE_ROWS = 16

Z32_WIDTH = POOL_WIDTH + 2 * ATTN_WIDTH
Z16_WIDTH = 2 * D_MODEL + 3 * ATTN_WIDTH
COL_U, COL_K, COL_V = 0, 1, 2
COL_GP, COL_GA = 0, 1
COL_Q = 2 * D_MODEL // ATTN_WIDTH
COL_K16, COL_V16 = COL_Q + 1, COL_Q + 2
IN_SPLITS = (0, POOL_WIDTH, POOL_WIDTH + ATTN_WIDTH, POOL_WIDTH + 3 * ATTN_WIDTH, IN_WIDTH)

EXPERT_TILE = 512
FF_CHUNK = 256
FF_PIECES = D_FF // FF_CHUNK
ISSUE_UNROLL = 8
DENSE_SUBTILES = 4
VMEM_LIMIT = 56 * 1024 * 1024
EXPERTS_VMEM_LIMIT = 62 * 1024 * 1024


def _pick(n, candidates):
    for c in candidates:
        if n % c == 0:
            return c
    raise ValueError(f"no tile in {candidates} divides {n}")


def _params(sem, vmem=None):
    return pltpu.CompilerParams(dimension_semantics=sem, vmem_limit_bytes=vmem or VMEM_LIMIT)


def _layer_norm(r, g, b):
    mu = jnp.mean(r, axis=-1, keepdims=True)
    c = r - mu
    var = jnp.mean(c * c, axis=-1, keepdims=True)
    return c * lax.rsqrt(var + LN_EPS) * g + b


def _row_sources(x, t):
    if not isinstance(x, tuple):
        return [x], [pl.BlockSpec((t, x.shape[1]), lambda i: (i, 0))], (lambda refs, i: refs[0][...])
    first, last = x
    assert last.shape[0] == t and first.shape[0] % t == 0
    n_first = first.shape[0] // t
    specs = [pl.BlockSpec((t, first.shape[1]), lambda i: (jnp.minimum(i, n_first - 1), 0)),
             pl.BlockSpec((t, last.shape[1]), lambda i: (0, 0))]
    return [first, last], specs, (lambda refs, i: jnp.where(i < n_first, refs[0][...], refs[1][...]))


def _inproj_body(n_src, select, *refs):
    w_ref, b_ref, z32_ref, z16_ref = refs[n_src:]
    x = select(refs[:n_src], pl.program_id(0)).astype(BF16)
    z = jnp.dot(x, w_ref[...], preferred_element_type=F32) + b_ref[...]
    u, q, kv, gates = (z[:, a:b] for a, b in zip(IN_SPLITS[:-1], IN_SPLITS[1:]))
    z32_ref[...] = jnp.concatenate([u, kv], axis=-1)
    z16_ref[...] = jnp.concatenate([gates, q, kv], axis=-1).astype(BF16)


def _in_proj(x, w_bf, b, t):
    srcs, specs, select = _row_sources(x, t)
    n = sum(s.shape[0] for s in srcs)
    zero = lambda i: (0, 0)
    return pl.pallas_call(
        functools.partial(_inproj_body, len(srcs), select),
        out_shape=(jax.ShapeDtypeStruct((n, Z32_WIDTH), F32), jax.ShapeDtypeStruct((n, Z16_WIDTH), BF16)),
        grid=(n // t,),
        in_specs=specs + [pl.BlockSpec((D_MODEL, IN_WIDTH), zero), pl.BlockSpec((1, IN_WIDTH), zero)],
        out_specs=(pl.BlockSpec((t, Z32_WIDTH), lambda i: (i, 0)), pl.BlockSpec((t, Z16_WIDTH), lambda i: (i, 0))),
        compiler_params=_params(("arbitrary",)),
        name="in_proj",
    )(*srcs, w_bf, b)


def _pair_scores(q_pair, k_pair, bias):
    lane = lax.broadcasted_iota(jnp.int32, q_pair.shape, 1)
    qs = q_pair.astype(F32) * ATTN_SCALE
    q2 = jnp.concatenate([jnp.where(lane < HEAD_DIM, qs, 0.0), jnp.where(lane >= HEAD_DIM, qs, 0.0)], axis=0)
    s = lax.dot_general(q2.astype(BF16), k_pair, (((1,), (1,)), ((), ())), preferred_element_type=F32)
    return s + bias


def _pair_output(s, v_pair):
    rows = s.shape[0] // 2
    m = jnp.max(s, axis=-1, keepdims=True)
    e = jnp.exp(s - m)
    l = jnp.sum(e, axis=-1, keepdims=True)
    o2 = jnp.dot(e.astype(BF16), v_pair, preferred_element_type=F32) / l
    lane = lax.broadcasted_iota(jnp.int32, (rows, PAIR_W), 1)
    return jnp.where(lane < HEAD_DIM, o2[:rows], o2[rows:])


def _attn_prompt_body(q_ref, kp_ref, kc_ref, vp_ref, vc_ref, kl_ref, vl_ref, bias_ref,
                      o_ref, kt_ref, vt_ref, kext_ref, vext_ref):
    blk = q_ref.shape[0]
    j = pl.program_id(1)

    @pl.when(j == pl.num_programs(1) - 1)
    def _():
        kt_ref[0] = kl_ref[...]
        vt_ref[0] = vl_ref[...]

    kext_ref[0:blk, :] = kp_ref[...]
    kext_ref[blk:2 * blk, :] = kc_ref[...]
    vext_ref[0:blk, :] = vp_ref[...]
    vext_ref[blk:2 * blk, :] = vc_ref[...]
    def chunk(c, carry):
        q0 = pl.multiple_of(c * CHUNK, CHUNK)
        slab = jnp.where(j == 0, c + 1, 0)
        pairs = [slice(hp * PAIR_W, (hp + 1) * PAIR_W) for hp in range(HEAD_PAIRS)]
        s = jnp.concatenate([_pair_scores(q_ref[pl.ds(q0, CHUNK), lanes], kext_ref[pl.ds(q0, BAND), lanes],
                                          bias_ref[slab, hp]) for hp, lanes in enumerate(pairs)], axis=0)
        m = jnp.max(s, axis=-1, keepdims=True)
        e = jnp.exp(s - m)
        inv = 1.0 / jnp.sum(e, axis=-1, keepdims=True)
        p = e.astype(BF16)
        lane = lax.broadcasted_iota(jnp.int32, (CHUNK, PAIR_W), 1)
        for hp, lanes in enumerate(pairs):
            rows = slice(hp * 2 * CHUNK, (hp + 1) * 2 * CHUNK)
            o2 = jnp.dot(p[rows], vext_ref[pl.ds(q0, BAND), lanes], preferred_element_type=F32) * inv[rows]
            o_ref[pl.ds(q0, CHUNK), lanes] = jnp.where(lane < HEAD_DIM, o2[:CHUNK], o2[CHUNK:])
        return carry

    lax.fori_loop(0, blk // CHUNK, chunk, 0, unroll=8)


def _attn_prompt(z32, z16, bias_pairs, batch, seq):
    blk = ATTN_REACH
    per_seq = seq // blk
    rows = batch * seq

    def cur(col):
        return lambda b, j: (b * per_seq + j, col)

    def prev(col):
        return lambda b, j: (b * per_seq + jnp.maximum(j - 1, 0), col)

    def last(col):
        return lambda b, j: (b * per_seq + per_seq - 1, col)

    tail = jax.ShapeDtypeStruct((batch, blk, ATTN_WIDTH), F32)
    tail_spec = pl.BlockSpec((1, blk, ATTN_WIDTH), lambda b, j: (b, 0, 0))
    return pl.pallas_call(
        _attn_prompt_body,
        out_shape=(jax.ShapeDtypeStruct((rows, ATTN_WIDTH), F32), tail, tail),
        grid=(batch, per_seq),
        in_specs=[
            pl.BlockSpec((blk, ATTN_WIDTH), cur(COL_Q)),
            pl.BlockSpec((blk, ATTN_WIDTH), prev(COL_K16)),
            pl.BlockSpec((blk, ATTN_WIDTH), cur(COL_K16)),
            pl.BlockSpec((blk, ATTN_WIDTH), prev(COL_V16)),
            pl.BlockSpec((blk, ATTN_WIDTH), cur(COL_V16)),
            pl.BlockSpec((blk, ATTN_WIDTH), last(COL_K)),
            pl.BlockSpec((blk, ATTN_WIDTH), last(COL_V)),
            pl.BlockSpec((1 + LEFT_CHUNKS, HEAD_PAIRS, 2 * CHUNK, BAND), lambda b, j: (0, 0, 0, 0)),
        ],
        out_specs=(pl.BlockSpec((blk, ATTN_WIDTH), lambda b, j: (b * per_seq + j, 0)), tail_spec, tail_spec),
        scratch_shapes=[pltpu.VMEM((2 * blk, ATTN_WIDTH), BF16), pltpu.VMEM((2 * blk, ATTN_WIDTH), BF16)],
        compiler_params=_params(("arbitrary", "arbitrary")),
        name="attn_prompt",
    )(z16, z16, z16, z16, z16, z32, z32, bias_pairs)


def _attn_sample_body(q_ref, kn_ref, vn_ref, ck_ref, cv_ref, bias_ref, *rest):
    o_ref, ko_ref, vo_ref, kall_ref, vall_ref = rest[-5:]
    keep = ck_ref.shape[2]
    t = q_ref.shape[0]
    ck = ck_ref[0, 0]
    cv = cv_ref[0, 0]
    kn = kn_ref[...]
    vn = vn_ref[...]
    kall_ref[0:keep, :] = ck.astype(BF16)
    kall_ref[keep:keep + t, :] = kn.astype(BF16)
    vall_ref[0:keep, :] = cv.astype(BF16)
    vall_ref[keep:keep + t, :] = vn.astype(BF16)
    for hp in range(HEAD_PAIRS):
        lanes = slice(hp * PAIR_W, (hp + 1) * PAIR_W)
        s = _pair_scores(q_ref[:, lanes], kall_ref[:, lanes], bias_ref[hp])
        o_ref[:, lanes] = _pair_output(s, vall_ref[:, lanes])
    ko_ref[0, 0, 0:keep - t, :] = ck[t:keep]
    ko_ref[0, 0, keep - t:keep, :] = kn
    vo_ref[0, 0, 0:keep - t, :] = cv[t:keep]
    vo_ref[0, 0, keep - t:keep, :] = vn
    for later in range(1, ko_ref.shape[0]):
        ko_ref[later] = jnp.zeros(ko_ref.shape[1:], F32)
        vo_ref[later] = jnp.zeros(vo_ref.shape[1:], F32)


def _attn_sample(z32, z16, cache_k, cache_v, bias_pairs, layer, row0, streams, t, k_buf, v_buf):
    depth, _, keep, _ = cache_k.shape
    blk0 = row0 // t
    ins = [z16, z32, z32, cache_k, cache_v, bias_pairs]
    in_specs = [
        pl.BlockSpec((t, ATTN_WIDTH), lambda s: (blk0 + s, COL_Q)),
        pl.BlockSpec((t, ATTN_WIDTH), lambda s: (blk0 + s, COL_K)),
        pl.BlockSpec((t, ATTN_WIDTH), lambda s: (blk0 + s, COL_V)),
        pl.BlockSpec((1, 1, keep, ATTN_WIDTH), lambda s: (layer, s, 0, 0)),
        pl.BlockSpec((1, 1, keep, ATTN_WIDTH), lambda s: (layer, s, 0, 0)),
        pl.BlockSpec((HEAD_PAIRS, 2 * t, keep + t), lambda s: (0, 0, 0)),
    ]
    aliases = {}
    if k_buf is None:
        assert layer == 0
        buf_spec = pl.BlockSpec((depth, 1, keep, ATTN_WIDTH), lambda s: (0, s, 0, 0))
    else:
        aliases = {len(ins): 1, len(ins) + 1: 2}
        ins += [k_buf, v_buf]
        in_specs += [pl.BlockSpec(memory_space=pl.ANY), pl.BlockSpec(memory_space=pl.ANY)]
        buf_spec = pl.BlockSpec((1, 1, keep, ATTN_WIDTH), lambda s: (layer, s, 0, 0))
    buf = jax.ShapeDtypeStruct(cache_k.shape, F32)
    return pl.pallas_call(
        _attn_sample_body,
        out_shape=(jax.ShapeDtypeStruct((streams * t, ATTN_WIDTH), F32), buf, buf),
        grid=(streams,),
        in_specs=in_specs,
        out_specs=(
            pl.BlockSpec((t, ATTN_WIDTH), lambda s: (s, 0)),
            buf_spec,
            buf_spec,
        ),
        scratch_shapes=[pltpu.VMEM((keep + t, ATTN_WIDTH), BF16), pltpu.VMEM((keep + t, ATTN_WIDTH), BF16)],
        input_output_aliases=aliases,
        compiler_params=_params(("arbitrary",)),
        name="attn_sample",
    )(*ins)


def _merge(pooled, u, attn, gp, ga, x, wgrp_ref, scale_ref, wp_ref, wa_ref, wo_ref, g_ref, b_ref):
    pooled = pooled - u
    pool_y = jnp.dot(pooled.astype(BF16), wgrp_ref[...], preferred_element_type=F32) * scale_ref[...]
    mp = jnp.dot(pool_y.astype(BF16), wp_ref[...], preferred_element_type=F32)
    ma = jnp.dot(attn.astype(BF16), wa_ref[...], preferred_element_type=F32)
    m = jax.nn.sigmoid(gp) * mp + jax.nn.sigmoid(ga) * ma
    y = jnp.dot(m.astype(BF16), wo_ref[...], preferred_element_type=F32)
    return _layer_norm(DN_ALPHA * x + y, g_ref[...], b_ref[...])


def _window_means(read, pos, shape_out):
    outs = []
    for g, w in enumerate(POOL_WINDOWS):
        lanes = slice(g * POOL_GROUP, (g + 1) * POOL_GROUP)
        s = read(0, lanes)
        for back in range(1, w):
            s = s + read(back, lanes)
        outs.append((s / jnp.minimum(pos + 1, w).astype(F32)).reshape(shape_out))
    return jnp.concatenate(outs, axis=-1)


def _window_means_doubling(buf_ref, lvl_ref, pos, t):
    g = POOL_GROUP
    lo, hi = LEAD_ROWS, LEAD_ROWS + HIST_ROWS + t
    out0 = LEAD_ROWS + HIST_ROWS
    lvl_ref[0, lo:hi, :] = buf_ref[lo:hi, :] + buf_ref[lo - 1:hi - 1, :]
    lvl_ref[1, lo:hi, g:] = lvl_ref[0, lo:hi, g:] + lvl_ref[0, lo - 2:hi - 2, g:]
    lvl_ref[2, lo:hi, 2 * g:] = lvl_ref[1, lo:hi, 2 * g:] + lvl_ref[1, lo - 4:hi - 4, 2 * g:]
    s16 = lvl_ref[2, out0:out0 + t, 3 * g:] + lvl_ref[2, out0 - 8:out0 - 8 + t, 3 * g:]
    sums = [lvl_ref[k, out0:out0 + t, k * g:(k + 1) * g] for k in range(3)] + [s16]
    return jnp.concatenate([s / jnp.minimum(pos + 1, w).astype(F32) for s, w in zip(sums, POOL_WINDOWS)], axis=-1)


def _mix_body(n_prompt_tiles, tiles_per_seq, with_router, n_src, select,
              u_ref, up_ref, hist_ref, attn_p_ref, attn_s_ref, gp_ref, ga_ref, *rest):
    x_refs, rest = rest[:n_src], rest[n_src:]
    (wgrp_ref, scale_ref, wp_ref, wa_ref, wo_ref, g_ref, b_ref), rest = rest[:7], rest[7:]
    if with_router:
        (wr_hi_ref, wr_lo_ref, br_ref, o_ref, tail_ref, mi_ref, mw_ref, cnt_ref,
         buf_ref, lvl_ref, sbuf_ref, pooled_ref, attn_ref, carry_ref) = rest
    else:
        o_ref, tail_ref, buf_ref, lvl_ref, sbuf_ref, pooled_ref, attn_ref = rest
    t = u_ref.shape[0]
    i = pl.program_id(0)
    u = u_ref[...]

    @pl.when(i == 0)
    def _():
        buf_ref[0:LEAD_ROWS, :] = jnp.zeros((LEAD_ROWS, POOL_WIDTH), F32)
        lvl_ref[:, 0:LEAD_ROWS, :] = jnp.zeros((lvl_ref.shape[0], LEAD_ROWS, POOL_WIDTH), F32)

    @pl.when(i < n_prompt_tiles)
    def _():
        tile = i % tiles_per_seq
        h0 = LEAD_ROWS
        buf_ref[h0:h0 + HIST_ROWS, :] = jnp.where(tile == 0, 0.0, up_ref[...])
        buf_ref[h0 + HIST_ROWS:h0 + HIST_ROWS + t, :] = u
        pos = tile * t + lax.broadcasted_iota(jnp.int32, (t, 1), 0)
        pooled_ref[...] = _window_means_doubling(buf_ref, lvl_ref, pos, t)
        attn_ref[...] = attn_p_ref[...]

        @pl.when(tile == tiles_per_seq - 1)
        def _():
            tail_ref[0] = u[t - HIST_ROWS:, :]

    @pl.when(i == n_prompt_tiles)
    def _():
        streams, hrows, _ = hist_ref.shape
        ts = t // streams
        sbuf_ref[:, 0:hrows, :] = hist_ref[...]
        sbuf_ref[:, hrows:hrows + ts, :] = u.reshape(streams, ts, POOL_WIDTH)
        pos = PAST_LEN + lax.broadcasted_iota(jnp.int32, (1, ts, 1), 1)
        read = lambda back, lanes: sbuf_ref[:, hrows - back:hrows - back + ts, lanes]
        pooled_ref[...] = _window_means(read, pos, (t, POOL_GROUP))
        attn_ref[...] = attn_s_ref[...]

    x1 = _merge(pooled_ref[...], u, attn_ref[...], gp_ref[...].astype(F32), ga_ref[...].astype(F32),
                select(x_refs, i), wgrp_ref, scale_ref, wp_ref, wa_ref, wo_ref, g_ref, b_ref)
    o_ref[...] = x1
    if with_router:
        _route(x1, wr_hi_ref, wr_lo_ref, br_ref, mi_ref, mw_ref, cnt_ref, carry_ref)


def _mix(z32, z16, attn_p, attn_s, x, hist, weights, n_prompt, seq, router=None):
    n = z32.shape[0]
    t = n - n_prompt
    x_srcs, x_specs, select = _row_sources(x, t)
    n_prompt_tiles = n_prompt // t
    streams, hrows, _ = hist.shape
    hist_per_tile = t // HIST_ROWS
    last_p = n_prompt_tiles - 1
    zero = lambda i: (0, 0)
    rows = lambda width: pl.BlockSpec((t, width), lambda i: (i, 0))
    tiles_per_seq = seq // t
    out_shape = [jax.ShapeDtypeStruct((n, D_MODEL), F32),
                 jax.ShapeDtypeStruct((n_prompt // seq, HIST_ROWS, POOL_WIDTH), F32)]
    out_specs = [rows(D_MODEL),
                 pl.BlockSpec((1, HIST_ROWS, POOL_WIDTH), lambda i: (jnp.minimum(i, last_p) // tiles_per_seq, 0, 0))]
    router_specs, router_scratch = [], []
    if router is not None:
        router_specs = [pl.BlockSpec((ROUTE_ROWS, D_MODEL), zero), pl.BlockSpec((ROUTE_ROWS, D_MODEL), zero),
                        pl.BlockSpec((ROUTE_ROWS, 1), zero)]
        per_tile = lambda dtype: jax.ShapeDtypeStruct((n // t, 8, t), dtype)
        out_shape += [per_tile(jnp.int32), per_tile(F32), jax.ShapeDtypeStruct((ROUTE_ROWS, LANES), jnp.int32)]
        out_specs += [pl.BlockSpec((1, 8, t), lambda i: (i, 0, 0)), pl.BlockSpec((1, 8, t), lambda i: (i, 0, 0)),
                      pl.BlockSpec((ROUTE_ROWS, LANES), zero)]
        router_scratch = [pltpu.VMEM((ROUTE_ROWS, LANES), F32)]
    out = pl.pallas_call(
        functools.partial(_mix_body, n_prompt_tiles, seq // t, router is not None, len(x_srcs), select),
        out_shape=out_shape,
        grid=(n_prompt_tiles + 1,),
        in_specs=[
            pl.BlockSpec((t, POOL_WIDTH), lambda i: (i, COL_U)),
            pl.BlockSpec((HIST_ROWS, POOL_WIDTH), lambda i: (jnp.maximum(i * hist_per_tile - 1, 0), COL_U)),
            pl.BlockSpec((streams, hrows, POOL_WIDTH), lambda i: (0, 0, 0)),
            pl.BlockSpec((t, ATTN_WIDTH), lambda i: (jnp.minimum(i, last_p), 0)),
            pl.BlockSpec((t, ATTN_WIDTH), zero),
            pl.BlockSpec((t, D_MODEL), lambda i: (i, COL_GP)),
            pl.BlockSpec((t, D_MODEL), lambda i: (i, COL_GA)),
        ] + x_specs + [
            pl.BlockSpec((POOL_WIDTH, POOL_WIDTH), zero),
            pl.BlockSpec((1, POOL_WIDTH), zero),
            pl.BlockSpec((POOL_WIDTH, D_MODEL), zero),
            pl.BlockSpec((ATTN_WIDTH, D_MODEL), zero),
            pl.BlockSpec((D_MODEL, D_MODEL), zero),
            pl.BlockSpec((1, D_MODEL), zero),
            pl.BlockSpec((1, D_MODEL), zero),
        ] + router_specs,
        out_specs=out_specs,
        scratch_shapes=[
            pltpu.VMEM((LEAD_ROWS + HIST_ROWS + t, POOL_WIDTH), F32),
            pltpu.VMEM((3, LEAD_ROWS + HIST_ROWS + t, POOL_WIDTH), F32),
            pltpu.VMEM((streams, hrows + t // streams, POOL_WIDTH), F32),
            pltpu.VMEM((t, POOL_WIDTH), F32),
            pltpu.VMEM((t, ATTN_WIDTH), F32),
        ] + router_scratch,
        compiler_params=_params(("arbitrary",)),
        name="mix",
    )(z32, z32, hist, attn_p, attn_s, z16, z16, *x_srcs, *weights, *(router or ()))
    return out


def _ffn_dense_body(x_ref, w1_ref, w3_ref, w2_ref, g_ref, b_ref, o_ref):
    sub = x_ref.shape[0] // DENSE_SUBTILES
    for k in range(DENSE_SUBTILES):
        rows = slice(k * sub, (k + 1) * sub)
        x = x_ref[rows, :]
        xb = x.astype(BF16)
        a = jnp.dot(xb, w1_ref[...], preferred_element_type=F32)
        c = jnp.dot(xb, w3_ref[...], preferred_element_type=F32)
        h = (jax.nn.silu(a) * c).astype(BF16)
        f = jnp.dot(h, w2_ref[...], preferred_element_type=F32)
        o_ref[rows, :] = _layer_norm(DN_ALPHA * x + f, g_ref[...], b_ref[...])


def _ffn_dense(x, w1, w3, w2, g, b):
    n = x.shape[0]
    t = _pick(n, (1280, 640, 512, 256))
    zero = lambda i: (0, 0)
    return pl.pallas_call(
        _ffn_dense_body,
        out_shape=jax.ShapeDtypeStruct((n, D_MODEL), F32),
        grid=(n // t,),
        in_specs=[
            pl.BlockSpec((t, D_MODEL), lambda i: (i, 0)),
            pl.BlockSpec((D_MODEL, D_FF), zero),
            pl.BlockSpec((D_MODEL, D_FF), zero),
            pl.BlockSpec((D_FF, D_MODEL), zero),
            pl.BlockSpec((1, D_MODEL), zero),
            pl.BlockSpec((1, D_MODEL), zero),
        ],
        out_specs=pl.BlockSpec((t, D_MODEL), lambda i: (i, 0)),
        compiler_params=_params(("arbitrary",)),
        name="ffn_dense",
    )(x, w1, w3, w2, g, b)


def _route(x, wr_hi_ref, wr_lo_ref, br_ref, mi_ref, mw_ref, cnt_ref, carry_ref):
    t = x.shape[0]

    @pl.when(pl.program_id(0) == 0)
    def _():
        carry_ref[...] = jnp.zeros_like(carry_ref)

    x_hi = x.astype(BF16)
    x_lo = (x - x_hi.astype(F32)).astype(BF16)
    nt = (((1,), (1,)), ((), ()))
    logits = (lax.dot_general(wr_hi_ref[...], x_hi, nt, preferred_element_type=F32)
              + lax.dot_general(wr_hi_ref[...], x_lo, nt, preferred_element_type=F32)
              + lax.dot_general(wr_lo_ref[...], x_hi, nt, preferred_element_type=F32)) + br_ref[...]
    row = lax.broadcasted_iota(jnp.int32, (ROUTE_ROWS, t), 0)
    row_f = row.astype(F32)
    logits = jnp.where(row < N_EXPERTS, logits, -jnp.inf)
    v0 = jnp.max(logits, axis=0, keepdims=True)
    e0 = jnp.min(jnp.where(logits == v0, row_f, float(ROUTE_ROWS)), axis=0, keepdims=True)
    rest = jnp.where(row_f == e0, -jnp.inf, logits)
    v1 = jnp.max(rest, axis=0, keepdims=True)
    e1 = jnp.min(jnp.where(rest == v1, row_f, float(ROUTE_ROWS)), axis=0, keepdims=True)
    ex = jnp.exp(v1 - v0)
    w0 = 1.0 / (1.0 + ex)
    w1 = ex / (1.0 + ex)
    oh0 = (row_f == e0).astype(F32)
    oh1 = (row_f == e1).astype(F32)
    before = (lax.broadcasted_iota(jnp.int32, (t, t), 0) < lax.broadcasted_iota(jnp.int32, (t, t), 1)).astype(BF16)
    pre0 = jnp.dot(oh0.astype(BF16), before, preferred_element_type=F32)
    pre1 = jnp.dot(oh1.astype(BF16), before, preferred_element_type=F32)
    cnt0 = jnp.sum(oh0, axis=1, keepdims=True)
    cnt1 = jnp.sum(oh1, axis=1, keepdims=True)
    carry = carry_ref[:, 0:1]
    rank0 = jnp.sum(oh0 * (carry + pre0), axis=0, keepdims=True)
    rank1 = jnp.sum(oh1 * (carry + cnt0 + pre1), axis=0, keepdims=True)
    carry = jnp.broadcast_to(carry + cnt0 + cnt1, carry_ref.shape)
    carry_ref[...] = carry
    cnt_ref[...] = carry.astype(jnp.int32)
    r8 = lax.broadcasted_iota(jnp.int32, (8, t), 0)
    mi = jnp.where(r8 == 0, e0, jnp.where(r8 == 1, e1, jnp.where(r8 == 2, rank0, jnp.where(r8 == 3, rank1, 0.0))))
    mi_ref[0] = mi.astype(jnp.int32)
    mw_ref[0] = jnp.where(r8 == 0, w0, jnp.where(r8 == 1, w1, 0.0))


def _router_operands(w_r, b_r):
    wr = jnp.zeros((ROUTE_ROWS, D_MODEL), F32).at[:N_EXPERTS].set(w_r.T)
    br = jnp.zeros((ROUTE_ROWS, 1), F32).at[:N_EXPERTS, 0].set(b_r)
    wr_hi = wr.astype(BF16)
    wr_lo = (wr - wr_hi.astype(F32)).astype(BF16)
    return wr_hi, wr_lo, br


def _dispatch_body(dest_ref, last_ref, nused_ref, x_ref, xs_ref, zero_ref, rows_ref, sem_ref, row_sem_ref):
    t = dest_ref.shape[0] // 2
    tm = zero_ref.shape[0]
    n_tiles = xs_ref.shape[0] // tm
    i = pl.program_id(0)

    @pl.when(i == 0)
    def _():
        zero_ref[...] = jnp.zeros_like(zero_ref)

        def fill(tile):
            return pltpu.make_async_copy(zero_ref, xs_ref.at[pl.ds(pl.multiple_of(tile * tm, tm), tm), :],
                                         sem_ref.at[0])

        for wait in (False, True):
            def one(tile, wait=wait):
                fill(tile).wait() if wait else fill(tile).start()

            for e in range(N_EXPERTS):
                @pl.when(last_ref[e] >= 0)
                def _(e=e, one=one):
                    one(last_ref[e])

            def unused(tile, carry, one=one):
                one(tile)
                return carry

            lax.fori_loop(nused_ref[0], n_tiles, unused, 0)

    cur = i % 2
    rows_ref[cur] = x_ref[...]

    def issue(r, carry):
        for slot in range(2):
            d = dest_ref[slot * t + r]
            pltpu.make_async_copy(rows_ref.at[cur, pl.ds(r, 1), :], xs_ref.at[pl.ds(d, 1), :],
                                  row_sem_ref.at[cur]).start(priority=slot)
        return carry

    lax.fori_loop(0, t, issue, 0, unroll=ISSUE_UNROLL)

    def wait_rows(buf):
        pltpu.make_async_copy(xs_ref.at[pl.ds(0, 2 * t), :], xs_ref.at[pl.ds(0, 2 * t), :], row_sem_ref.at[buf]).wait()

    @pl.when(i > 0)
    def _():
        wait_rows(1 - cur)

    @pl.when(i == pl.num_programs(0) - 1)
    def _():
        wait_rows(cur)


def _dispatch(x, dest, last_tile, n_used, n_tiles, t):
    n = x.shape[0]
    tm = EXPERT_TILE
    return pl.pallas_call(
        _dispatch_body,
        out_shape=jax.ShapeDtypeStruct((n_tiles * tm, D_MODEL), F32),
        grid=(n // t,),
        in_specs=[
            pl.BlockSpec((2 * t,), lambda i: (i,), memory_space=pltpu.SMEM),
            pl.BlockSpec(memory_space=pltpu.SMEM),
            pl.BlockSpec(memory_space=pltpu.SMEM),
            pl.BlockSpec((t, D_MODEL), lambda i: (i, 0)),
        ],
        out_specs=pl.BlockSpec(memory_space=pl.ANY),
        scratch_shapes=[pltpu.VMEM((tm, D_MODEL), F32), pltpu.VMEM((2, t, D_MODEL), F32),
                        pltpu.SemaphoreType.DMA((1,)), pltpu.SemaphoreType.DMA((2,))],
        compiler_params=_params(("arbitrary",)),
        name="dispatch",
    )(dest, last_tile, n_used, x)


def _experts_body(te_ref, first_ref, next_ref, slot_ref, nused_ref, x_ref, w1_hbm, w3_hbm, w2_hbm, o_ref,
                  wb1_ref, wb3_ref, wb2_ref, st1_ref, st3_ref, st2_ref, sem_ref):
    i = pl.program_id(0)

    def piece_copies(e, c):
        buf = c % 2
        cols = pl.ds(c * FF_CHUNK, FF_CHUNK)
        return (pltpu.make_async_copy(w1_hbm.at[e, :, cols], st1_ref.at[buf], sem_ref.at[buf, 0]),
                pltpu.make_async_copy(w3_hbm.at[e, :, cols], st3_ref.at[buf], sem_ref.at[buf, 1]),
                pltpu.make_async_copy(w2_hbm.at[e, cols, :], st2_ref.at[buf], sem_ref.at[buf, 2]))

    def start_piece(e, c):
        for cp in piece_copies(e, c):
            cp.start()

    def finish_piece(e, c, slot):
        for cp in piece_copies(e, c):
            cp.wait()
        cols = slice(c * FF_CHUNK, (c + 1) * FF_CHUNK)
        wb1_ref[slot, :, cols] = st1_ref[c % 2].astype(BF16)
        wb3_ref[slot, :, cols] = st3_ref[c % 2].astype(BF16)
        wb2_ref[slot, cols, :] = st2_ref[c % 2].astype(BF16)

    def ff_piece(xb, slot, c):
        cols = slice(c * FF_CHUNK, (c + 1) * FF_CHUNK)
        a = jnp.dot(xb, wb1_ref[slot, :, cols], preferred_element_type=F32)
        g = jnp.dot(xb, wb3_ref[slot, :, cols], preferred_element_type=F32)
        h = (jax.nn.silu(a) * g).astype(BF16)
        return jnp.dot(h, wb2_ref[slot, cols, :], preferred_element_type=F32)

    @pl.when(i == 0)
    def _():
        e, slot = te_ref[0], slot_ref[0]
        start_piece(e, 0)
        for c in range(FF_PIECES):
            if c + 1 < FF_PIECES:
                start_piece(e, c + 1)
            finish_piece(e, c, slot)

    live = i < nused_ref[0]
    prefetch = jnp.logical_and(live, jnp.logical_and(first_ref[i] == 1, next_ref[i] >= 0))

    @pl.when(jnp.logical_and(live, jnp.logical_not(prefetch)))
    def _():
        xb = x_ref[...].astype(BF16)
        slot = slot_ref[i]
        f = ff_piece(xb, slot, 0)
        for c in range(1, FF_PIECES):
            f = f + ff_piece(xb, slot, c)
        o_ref[...] = f

    @pl.when(prefetch)
    def _():
        xb = x_ref[...].astype(BF16)
        slot, e_next = slot_ref[i], next_ref[i]
        start_piece(e_next, 0)
        f = None
        for c in range(FF_PIECES):
            if c + 1 < FF_PIECES:
                start_piece(e_next, c + 1)
            part = ff_piece(xb, slot, c)
            f = part if f is None else f + part
            finish_piece(e_next, c, 1 - slot)
        o_ref[...] = f

    @pl.when(jnp.logical_not(live))
    def _():
        o_ref[...] = jnp.zeros_like(o_ref)


def _experts(xs, tile_expert, first, nxt, slot, n_used, w1, w3, w2):
    tm = EXPERT_TILE
    n_tiles = xs.shape[0] // tm
    hbm = pl.BlockSpec(memory_space=pl.ANY)
    return pl.pallas_call(
        _experts_body,
        out_shape=jax.ShapeDtypeStruct(xs.shape, F32),
        grid_spec=pltpu.PrefetchScalarGridSpec(
            num_scalar_prefetch=5,
            grid=(n_tiles,),
            in_specs=[pl.BlockSpec((tm, D_MODEL), lambda i, te, fi, nx, sl, nu: (jnp.minimum(i, nu[0] - 1), 0)),
                      hbm, hbm, hbm],
            out_specs=pl.BlockSpec((tm, D_MODEL), lambda i, te, fi, nx, sl, nu: (i, 0)),
            scratch_shapes=[
                pltpu.VMEM((2, D_MODEL, D_FF), BF16), pltpu.VMEM((2, D_MODEL, D_FF), BF16),
                pltpu.VMEM((2, D_FF, D_MODEL), BF16),
                pltpu.VMEM((2, D_MODEL, FF_CHUNK), F32), pltpu.VMEM((2, D_MODEL, FF_CHUNK), F32),
                pltpu.VMEM((2, FF_CHUNK, D_MODEL), F32),
                pltpu.SemaphoreType.DMA((2, 3)),
            ],
        ),
        compiler_params=_params(("arbitrary",), EXPERTS_VMEM_LIMIT),
        name="experts",
    )(tile_expert, first, nxt, slot, n_used, xs, w1, w3, w2)


def _combine_body(n_first, dest_ref, next_ref, x_ref, mw_ref, g_ref, b_ref, ys_ref, *rest):
    out_refs, (y0_ref, y1_ref, sem_ref) = rest[:-3], rest[-3:]
    t = x_ref.shape[0]
    i = pl.program_id(0)
    cur = i % 2

    def gather(idx_ref, buf):
        def issue(r, carry):
            pltpu.make_async_copy(ys_ref.at[pl.ds(idx_ref[r], 1), :], y0_ref.at[buf, pl.ds(r, 1), :],
                                  sem_ref.at[buf]).start(priority=0)
            pltpu.make_async_copy(ys_ref.at[pl.ds(idx_ref[t + r], 1), :], y1_ref.at[buf, pl.ds(r, 1), :],
                                  sem_ref.at[buf]).start(priority=1)
            return carry

        lax.fori_loop(0, t, issue, 0, unroll=ISSUE_UNROLL)

    @pl.when(i == 0)
    def _():
        gather(dest_ref, 0)

    @pl.when(i + 1 < pl.num_programs(0))
    def _():
        gather(next_ref, 1 - cur)

    pltpu.make_async_copy(ys_ref.at[pl.ds(0, t), :], y0_ref.at[cur], sem_ref.at[cur]).wait()
    pltpu.make_async_copy(ys_ref.at[pl.ds(0, t), :], y1_ref.at[cur], sem_ref.at[cur]).wait()
    mw = mw_ref[...]
    f = mw[:, 0:1] * y0_ref[cur] + mw[:, 1:2] * y1_ref[cur]
    out = _layer_norm(DN_ALPHA * x_ref[...] + f, g_ref[...], b_ref[...])
    if len(out_refs) == 1:
        out_refs[0][...] = out
    else:
        @pl.when(i < n_first)
        def _():
            out_refs[0][...] = out

        @pl.when(i == n_first)
        def _():
            out_refs[1][...] = out


def _combine(x, ys, dest, mw, g, b, t, split):
    n = x.shape[0]
    zero = lambda i: (0, 0)
    n_first = n // t - 1
    last = n // t - 1
    if split:
        out_shape = (jax.ShapeDtypeStruct((n - t, D_MODEL), F32), jax.ShapeDtypeStruct((t, D_MODEL), F32))
        out_specs = (pl.BlockSpec((t, D_MODEL), lambda i: (jnp.minimum(i, n_first - 1), 0)),
                     pl.BlockSpec((t, D_MODEL), zero))
    else:
        out_shape = jax.ShapeDtypeStruct((n, D_MODEL), F32)
        out_specs = pl.BlockSpec((t, D_MODEL), lambda i: (i, 0))
    return pl.pallas_call(
        functools.partial(_combine_body, n_first),
        out_shape=out_shape,
        grid=(n // t,),
        in_specs=[
            pl.BlockSpec((2 * t,), lambda i: (i,), memory_space=pltpu.SMEM),
            pl.BlockSpec((2 * t,), lambda i: (jnp.minimum(i + 1, last),), memory_space=pltpu.SMEM),
            pl.BlockSpec((t, D_MODEL), lambda i: (i, 0)),
            pl.BlockSpec((t, 2), lambda i: (i, 0)),
            pl.BlockSpec((1, D_MODEL), zero),
            pl.BlockSpec((1, D_MODEL), zero),
            pl.BlockSpec(memory_space=pl.ANY),
        ],
        out_specs=out_specs,
        scratch_shapes=[pltpu.VMEM((2, t, D_MODEL), F32), pltpu.VMEM((2, t, D_MODEL), F32),
                        pltpu.SemaphoreType.DMA((2,))],
        compiler_params=_params(("arbitrary",)),
        name="combine",
    )(dest, dest, x, mw, g, b, ys)


def _moe(x, mi, mw, cnt, w1, w3, w2, g, b, t, split):
    n = x.shape[0]
    tm = EXPERT_TILE
    n_tiles = (2 * n + N_EXPERTS * (tm - 1)) // tm
    counts = cnt[:N_EXPERTS, 0]
    tiles = (counts + tm - 1) // tm
    tile_end = jnp.cumsum(tiles)
    start = (tile_end - tiles) * tm
    n_used = tile_end[-1:].astype(jnp.int32)
    tile_expert = jnp.sum((tile_end[None, :] <= jnp.arange(n_tiles)[:, None]).astype(jnp.int32), axis=1)
    tile_expert = jnp.minimum(tile_expert, N_EXPERTS - 1)
    last_tile = jnp.where(tiles > 0, tile_end - 1, -1).astype(jnp.int32)
    has = tiles > 0
    next_of = [jnp.int32(-1)] * N_EXPERTS
    for e in range(N_EXPERTS - 2, -1, -1):
        next_of[e] = jnp.where(has[e + 1], e + 1, next_of[e + 1])
    per_expert = jnp.stack([tile_end - tiles, jnp.stack(next_of), (jnp.cumsum(has) - 1) % 2]).astype(jnp.int32)
    onehot = (tile_expert[None, :, None] == jnp.arange(N_EXPERTS)[None, None, :]).astype(jnp.int32)
    first_tile, nxt, slot = jnp.sum(onehot * per_expert[:, None, :], axis=2)
    first = (first_tile == jnp.arange(n_tiles)).astype(jnp.int32)
    experts = mi[:, 0:2, :]
    group_start = sum(jnp.where(experts == e, start[e], 0) for e in range(N_EXPERTS))
    dest = (group_start + mi[:, 2:4, :]).astype(jnp.int32).reshape(2 * n)
    mw = jnp.swapaxes(mw[:, 0:2, :], 1, 2).reshape(n, 2)
    xs = _dispatch(x, dest, last_tile, n_used, n_tiles, t)
    ys = _experts(xs, tile_expert.astype(jnp.int32), first, nxt, slot, n_used, w1, w3, w2)
    return _combine(x, ys, dest, mw, g, b, t, split)


def _block_diag(w):
    layers, g, c, _ = w.shape
    out = jnp.zeros((layers, g * c, g * c), w.dtype)
    for i in range(g):
        out = out.at[:, i * c:(i + 1) * c, i * c:(i + 1) * c].set(w[:, i])
    return out


def _bias_pairs(table, t_q, n_keys, offset):
    layers = table.shape[0]
    hi = t_q - 1 + offset
    span = t_q - 1 + n_keys
    cols = np.clip(hi - np.arange(span), -REL_CLIP, REL_CLIP) + REL_CLIP
    rev = table.reshape(layers * N_HEADS, -1)[:, cols].astype(F32)
    skew = jnp.tile(rev, (1, t_q + 1))[:, :t_q * (span + 1)].reshape(layers * N_HEADS, t_q, span + 1)
    slab = skew[:, ::-1, :n_keys]
    return slab.reshape(layers, HEAD_PAIRS, 2 * t_q, n_keys)


def _prompt_bias_slabs(table):
    bias = _bias_pairs(table, CHUNK, BAND, ATTN_REACH)
    col = np.arange(BAND)[None, :]
    first_valid = np.concatenate([[0], (LEFT_CHUNKS - np.arange(LEFT_CHUNKS)) * CHUNK])[:, None]
    valid = jnp.asarray(col >= first_valid)
    return jnp.where(valid[None, :, None, None, :], bias[:, None], NEG_INF)


def kernel(x_prompt, x_sample, cache_k, cache_v, state_pool, w_in, b_in, w_pool_grp, pool_scale,
           rel_table, w_pool_br, w_attn_br, w_out, ln1_g, ln1_b, ln2_g, ln2_b,
           w1_dense, w3_dense, w2_dense, w_router, b_router, w1_exp, w3_exp, w2_exp):
    bp, tp, d = x_prompt.shape
    bs, ts, _ = x_sample.shape
    n_p, n_s = bp * tp, bs * ts
    n = n_p + n_s
    depth = w_in.shape[0]
    keep_s = cache_k.shape[2]
    keep_p = min(ATTN_REACH, tp)

    x = (x_prompt.reshape(n_p, d), x_sample.reshape(n_s, d))
    ck = cache_k.reshape(depth, bs, keep_s, ATTN_WIDTH)
    cv = cache_v.reshape(depth, bs, keep_s, ATTN_WIDTH)
    hist = jnp.pad(state_pool, ((0, 0), (0, 0), (HIST_ROWS - POOL_HIST, 0), (0, 0)))
    assert tp % ATTN_REACH == 0 and keep_p == ATTN_REACH and ts >= POOL_HIST

    rows = lambda v: v[:, None, :].astype(F32)
    w_in_bf, b_in_r = w_in.astype(BF16), rows(b_in)
    wgrp, wp_bf, wa_bf, wo_bf = (a.astype(BF16) for a in (_block_diag(w_pool_grp), w_pool_br, w_attn_br, w_out))
    scale_r, g1, b1, g2, b2 = (rows(a) for a in (pool_scale, ln1_g, ln1_b, ln2_g, ln2_b))
    dense_bf = [a.astype(BF16) for a in (w1_dense, w3_dense, w2_dense)]
    bias_p = _prompt_bias_slabs(rel_table)
    bias_s = _bias_pairs(rel_table, ts, keep_s + ts, keep_s)

    k_buf = v_buf = None
    kp_new, vp_new, pp_new, ps_new = [], [], [], []
    for l in range(depth):
        weights = (wgrp[l], scale_r[l], wp_bf[l], wa_bf[l], wo_bf[l], g1[l], b1[l])
        z32, z16 = _in_proj(x, w_in_bf[l], b_in_r[l], n_s)
        attn_p, k_tail, v_tail = _attn_prompt(z32, z16, bias_p[l], bp, tp)
        attn_s, k_buf, v_buf = _attn_sample(z32, z16, ck, cv, bias_s[l], l, n_p, bs, ts, k_buf, v_buf)
        j = l // 2
        last = l == depth - 1
        if l % 2 == 0:
            x1, u_tail = _mix(z32, z16, attn_p, attn_s, x, hist[l], weights, n_p, tp)
            x = _ffn_dense(x1, dense_bf[0][j], dense_bf[1][j], dense_bf[2][j], g2[l], b2[l])
            if last:
                x = (x[:n_p], x[n_p:])
        else:
            x1, u_tail, mi, mw, cnt = _mix(z32, z16, attn_p, attn_s, x, hist[l], weights, n_p, tp,
                                           router=_router_operands(w_router[j], b_router[j]))
            x = _moe(x1, mi, mw, cnt, w1_exp[j], w3_exp[j], w2_exp[j], g2[l], b2[l], n_s, split=last)

        kp_new.append(k_tail.reshape(bp, keep_p, N_HEADS, HEAD_DIM))
        vp_new.append(v_tail.reshape(bp, keep_p, N_HEADS, HEAD_DIM))
        pp_new.append(u_tail[:, HIST_ROWS - POOL_HIST:])
        ps_new.append(z32[n_p:, :POOL_WIDTH].reshape(bs, ts, POOL_WIDTH)[:, ts - POOL_HIST:])

    shape_s = (depth, bs, keep_s, N_HEADS, HEAD_DIM)
    return (x[0].reshape(bp, tp, d), x[1].reshape(bs, ts, d),
            jnp.stack(kp_new), jnp.stack(vp_new), jnp.stack(pp_new),
            k_buf.reshape(shape_s), v_buf.reshape(shape_s), jnp.stack(ps_new))
```

```python
import functools

import jax
import jax.numpy as jnp
import numpy as np
from jax import lax
from jax.experimental import pallas as pl
from jax.experimental.pallas import tpu as pltpu

F32 = jnp.float32
BF16 = jnp.bfloat16

D_MODEL = 1024
N_HEADS = 8
HEAD_DIM = 64
ATTN_WIDTH = N_HEADS * HEAD_DIM
CHUNK = 64
LEFT_CHUNKS = 8
BAND = (LEFT_CHUNKS + 1) * CHUNK
ATTN_REACH = LEFT_CHUNKS * CHUNK
REL_CLIP = 256
ATTN_SCALE = HEAD_DIM ** -0.5
POOL_WIDTH = 512
POOL_WINDOWS = (2, 4, 8, 16)
POOL_GROUP = POOL_WIDTH // len(POOL_WINDOWS)
POOL_HIST = max(POOL_WINDOWS) - 1
HIST_ROWS = POOL_HIST + 1
LEAD_ROWS = 8
D_FF = 2816
N_EXPERTS = 8
PAST_LEN = 4096
DEPTH = 2
DN_ALPHA = (2 * DEPTH) ** 0.25
LN_EPS = 1e-5
NEG_INF = -1e30
IN_WIDTH = POOL_WIDTH + 3 * ATTN_WIDTH + 2 * D_MODEL
HEAD_PAIRS = N_HEADS // 2
PAIR_W = 2 * HEAD_DIM
LANES = 128
ROUTE_ROWS = 16

Z32_WIDTH = POOL_WIDTH + 2 * ATTN_WIDTH
Z16_WIDTH = 2 * D_MODEL + 3 * ATTN_WIDTH
COL_U, COL_K, COL_V = 0, 1, 2
COL_GP, COL_GA = 0, 1
COL_Q = 2 * D_MODEL // ATTN_WIDTH
COL_K16, COL_V16 = COL_Q + 1, COL_Q + 2
IN_SPLITS = (0, POOL_WIDTH, POOL_WIDTH + ATTN_WIDTH, POOL_WIDTH + 3 * ATTN_WIDTH, IN_WIDTH)

EXPERT_TILE = 512
FF_CHUNK = 256
FF_PIECES = D_FF // FF_CHUNK
ISSUE_UNROLL = 8
DENSE_SUBTILES = 4
INPROJ_BLOCK = 640
VMEM_LIMIT = 56 * 1024 * 1024
EXPERTS_VMEM_LIMIT = 62 * 1024 * 1024


def _pick(n, candidates):
    for c in candidates:
        if n % c == 0:
            return c
    raise ValueError(f"no tile in {candidates} divides {n}")


def _params(sem, vmem=None):
    return pltpu.CompilerParams(dimension_semantics=sem, vmem_limit_bytes=vmem or VMEM_LIMIT)


def _layer_norm(r, g, b):
    mu = jnp.mean(r, axis=-1, keepdims=True)
    c = r - mu
    var = jnp.mean(c * c, axis=-1, keepdims=True)
    return c * lax.rsqrt(var + LN_EPS) * g + b


def _row_sources(x, t):
    if not isinstance(x, tuple):
        return [x], [pl.BlockSpec((t, x.shape[1]), lambda i: (i, 0))], (lambda refs, i: refs[0][...])
    first, last = x
    assert last.shape[0] == t and first.shape[0] % t == 0
    n_first = first.shape[0] // t
    specs = [pl.BlockSpec((t, first.shape[1]), lambda i: (jnp.minimum(i, n_first - 1), 0)),
             pl.BlockSpec((t, last.shape[1]), lambda i: (0, 0))]
    return [first, last], specs, (lambda refs, i: jnp.where(i < n_first, refs[0][...], refs[1][...]))


def _inproj_body(n_src, select, n_sub, *refs):
    w_ref, b_ref, z32_ref, z16_ref = refs[n_src:]
    x = select(refs[:n_src], pl.program_id(0))
    sub = x.shape[0] // n_sub
    for k in range(n_sub):
        rows = slice(k * sub, (k + 1) * sub)
        z = jnp.dot(x[rows].astype(BF16), w_ref[...], preferred_element_type=F32) + b_ref[...]
        u, q, kv, gates = (z[:, a:b] for a, b in zip(IN_SPLITS[:-1], IN_SPLITS[1:]))
        z32_ref[rows, :] = jnp.concatenate([u, kv], axis=-1)
        z16_ref[rows, :] = jnp.concatenate([gates, q, kv], axis=-1).astype(BF16)


def _in_proj(x, w_bf, b, t):
    n_sub = 1
    if not isinstance(x, tuple) and x.shape[0] % INPROJ_BLOCK == 0:
        t, n_sub = INPROJ_BLOCK, 2
    srcs, specs, select = _row_sources(x, t)
    n = sum(s.shape[0] for s in srcs)
    zero = lambda i: (0, 0)
    return pl.pallas_call(
        functools.partial(_inproj_body, len(srcs), select, n_sub),
        out_shape=(jax.ShapeDtypeStruct((n, Z32_WIDTH), F32), jax.ShapeDtypeStruct((n, Z16_WIDTH), BF16)),
        grid=(n // t,),
        in_specs=specs + [pl.BlockSpec((D_MODEL, IN_WIDTH), zero), pl.BlockSpec((1, IN_WIDTH), zero)],
        out_specs=(pl.BlockSpec((t, Z32_WIDTH), lambda i: (i, 0)), pl.BlockSpec((t, Z16_WIDTH), lambda i: (i, 0))),
        compiler_params=_params(("arbitrary",)),
        name="in_proj",
    )(*srcs, w_bf, b)


def _pair_scores(q_pair, k_pair, bias):
    lane = lax.broadcasted_iota(jnp.int32, q_pair.shape, 1)
    qs = q_pair.astype(F32) * ATTN_SCALE
    q2 = jnp.concatenate([jnp.where(lane < HEAD_DIM, qs, 0.0), jnp.where(lane >= HEAD_DIM, qs, 0.0)], axis=0)
    s = lax.dot_general(q2.astype(BF16), k_pair, (((1,), (1,)), ((), ())), preferred_element_type=F32)
    return s + bias


def _pair_output(s, v_pair):
    rows = s.shape[0] // 2
    m = jnp.max(s, axis=-1, keepdims=True)
    e = jnp.exp(s - m)
    l = jnp.sum(e, axis=-1, keepdims=True)
    o2 = jnp.dot(e.astype(BF16), v_pair, preferred_element_type=F32) / l
    lane = lax.broadcasted_iota(jnp.int32, (rows, PAIR_W), 1)
    return jnp.where(lane < HEAD_DIM, o2[:rows], o2[rows:])


def _attn_prompt_body(q_ref, kp_ref, kc_ref, vp_ref, vc_ref, kl_ref, vl_ref, bias_ref,
                      o_ref, kt_ref, vt_ref, kext_ref, vext_ref):
    blk = q_ref.shape[0]
    j = pl.program_id(1)

    @pl.when(j == pl.num_programs(1) - 1)
    def _():
        kt_ref[0] = kl_ref[...]
        vt_ref[0] = vl_ref[...]

    kext_ref[0:blk, :] = kp_ref[...]
    kext_ref[blk:2 * blk, :] = kc_ref[...]
    vext_ref[0:blk, :] = vp_ref[...]
    vext_ref[blk:2 * blk, :] = vc_ref[...]
    def chunk(c, carry):
        q0 = pl.multiple_of(c * CHUNK, CHUNK)
        slab = jnp.where(j == 0, c + 1, 0)
        pairs = [slice(hp * PAIR_W, (hp + 1) * PAIR_W) for hp in range(HEAD_PAIRS)]
        s = jnp.concatenate([_pair_scores(q_ref[pl.ds(q0, CHUNK), lanes], kext_ref[pl.ds(q0, BAND), lanes],
                                          bias_ref[slab, hp]) for hp, lanes in enumerate(pairs)], axis=0)
        m = jnp.max(s, axis=-1, keepdims=True)
        e = jnp.exp(s - m)
        inv = 1.0 / jnp.sum(e, axis=-1, keepdims=True)
        p = e.astype(BF16)
        lane = lax.broadcasted_iota(jnp.int32, (CHUNK, PAIR_W), 1)
        for hp, lanes in enumerate(pairs):
            rows = slice(hp * 2 * CHUNK, (hp + 1) * 2 * CHUNK)
            o2 = jnp.dot(p[rows], vext_ref[pl.ds(q0, BAND), lanes], preferred_element_type=F32) * inv[rows]
            o_ref[pl.ds(q0, CHUNK), lanes] = jnp.where(lane < HEAD_DIM, o2[:CHUNK], o2[CHUNK:])
        return carry

    lax.fori_loop(0, blk // CHUNK, chunk, 0, unroll=8)


def _attn_prompt(z32, z16, bias_pairs, batch, seq):
    blk = ATTN_REACH
    per_seq = seq // blk
    rows = batch * seq

    def cur(col):
        return lambda b, j: (b * per_seq + j, col)

    def prev(col):
        return lambda b, j: (b * per_seq + jnp.maximum(j - 1, 0), col)

    def last(col):
        return lambda b, j: (b * per_seq + per_seq - 1, col)

    tail = jax.ShapeDtypeStruct((batch, blk, ATTN_WIDTH), F32)
    tail_spec = pl.BlockSpec((1, blk, ATTN_WIDTH), lambda b, j: (b, 0, 0))
    return pl.pallas_call(
        _attn_prompt_body,
        out_shape=(jax.ShapeDtypeStruct((rows, ATTN_WIDTH), F32), tail, tail),
        grid=(batch, per_seq),
        in_specs=[
            pl.BlockSpec((blk, ATTN_WIDTH), cur(COL_Q)),
            pl.BlockSpec((blk, ATTN_WIDTH), prev(COL_K16)),
            pl.BlockSpec((blk, ATTN_WIDTH), cur(COL_K16)),
            pl.BlockSpec((blk, ATTN_WIDTH), prev(COL_V16)),
            pl.BlockSpec((blk, ATTN_WIDTH), cur(COL_V16)),
            pl.BlockSpec((blk, ATTN_WIDTH), last(COL_K)),
            pl.BlockSpec((blk, ATTN_WIDTH), last(COL_V)),
            pl.BlockSpec((1 + LEFT_CHUNKS, HEAD_PAIRS, 2 * CHUNK, BAND), lambda b, j: (0, 0, 0, 0)),
        ],
        out_specs=(pl.BlockSpec((blk, ATTN_WIDTH), lambda b, j: (b * per_seq + j, 0)), tail_spec, tail_spec),
        scratch_shapes=[pltpu.VMEM((2 * blk, ATTN_WIDTH), BF16), pltpu.VMEM((2 * blk, ATTN_WIDTH), BF16)],
        compiler_params=_params(("arbitrary", "arbitrary")),
        name="attn_prompt",
    )(z16, z16, z16, z16, z16, z32, z32, bias_pairs)


def _attn_sample_body(q_ref, kn_ref, vn_ref, ck_ref, cv_ref, bias_ref, *rest):
    o_ref, ko_ref, vo_ref, kall_ref, vall_ref = rest[-5:]
    keep = ck_ref.shape[2]
    t = q_ref.shape[0]
    ck = ck_ref[0, 0]
    cv = cv_ref[0, 0]
    kn = kn_ref[...]
    vn = vn_ref[...]
    kall_ref[0:keep, :] = ck.astype(BF16)
    kall_ref[keep:keep + t, :] = kn.astype(BF16)
    vall_ref[0:keep, :] = cv.astype(BF16)
    vall_ref[keep:keep + t, :] = vn.astype(BF16)
    for hp in range(HEAD_PAIRS):
        lanes = slice(hp * PAIR_W, (hp + 1) * PAIR_W)
        s = _pair_scores(q_ref[:, lanes], kall_ref[:, lanes], bias_ref[hp])
        o_ref[:, lanes] = _pair_output(s, vall_ref[:, lanes])
    ko_ref[0, 0, 0:keep - t, :] = ck[t:keep]
    ko_ref[0, 0, keep - t:keep, :] = kn
    vo_ref[0, 0, 0:keep - t, :] = cv[t:keep]
    vo_ref[0, 0, keep - t:keep, :] = vn
    for later in range(1, ko_ref.shape[0]):
        ko_ref[later] = jnp.zeros(ko_ref.shape[1:], F32)
        vo_ref[later] = jnp.zeros(vo_ref.shape[1:], F32)


def _attn_sample(z32, z16, cache_k, cache_v, bias_pairs, layer, row0, streams, t, k_buf, v_buf):
    depth, _, keep, _ = cache_k.shape
    blk0 = row0 // t
    ins = [z16, z32, z32, cache_k, cache_v, bias_pairs]
    in_specs = [
        pl.BlockSpec((t, ATTN_WIDTH), lambda s: (blk0 + s, COL_Q)),
        pl.BlockSpec((t, ATTN_WIDTH), lambda s: (blk0 + s, COL_K)),
        pl.BlockSpec((t, ATTN_WIDTH), lambda s: (blk0 + s, COL_V)),
        pl.BlockSpec((1, 1, keep, ATTN_WIDTH), lambda s: (layer, s, 0, 0)),
        pl.BlockSpec((1, 1, keep, ATTN_WIDTH), lambda s: (layer, s, 0, 0)),
        pl.BlockSpec((HEAD_PAIRS, 2 * t, keep + t), lambda s: (0, 0, 0)),
    ]
    aliases = {}
    if k_buf is None:
        assert layer == 0
        buf_spec = pl.BlockSpec((depth, 1, keep, ATTN_WIDTH), lambda s: (0, s, 0, 0))
    else:
        aliases = {len(ins): 1, len(ins) + 1: 2}
        ins += [k_buf, v_buf]
        in_specs += [pl.BlockSpec(memory_space=pl.ANY), pl.BlockSpec(memory_space=pl.ANY)]
        buf_spec = pl.BlockSpec((1, 1, keep, ATTN_WIDTH), lambda s: (layer, s, 0, 0))
    buf = jax.ShapeDtypeStruct(cache_k.shape, F32)
    return pl.pallas_call(
        _attn_sample_body,
        out_shape=(jax.ShapeDtypeStruct((streams * t, ATTN_WIDTH), F32), buf, buf),
        grid=(streams,),
        in_specs=in_specs,
        out_specs=(
            pl.BlockSpec((t, ATTN_WIDTH), lambda s: (s, 0)),
            buf_spec,
            buf_spec,
        ),
        scratch_shapes=[pltpu.VMEM((keep + t, ATTN_WIDTH), BF16), pltpu.VMEM((keep + t, ATTN_WIDTH), BF16)],
        input_output_aliases=aliases,
        compiler_params=_params(("arbitrary",)),
        name="attn_sample",
    )(*ins)


def _merge(pooled, u, attn, gp, ga, x, wgrp_ref, scale_ref, wp_ref, wa_ref, wo_ref, g_ref, b_ref):
    pooled = pooled - u
    pool_y = jnp.dot(pooled.astype(BF16), wgrp_ref[...], preferred_element_type=F32) * scale_ref[...]
    mp = jnp.dot(pool_y.astype(BF16), wp_ref[...], preferred_element_type=F32)
    ma = jnp.dot(attn.astype(BF16), wa_ref[...], preferred_element_type=F32)
    m = jax.nn.sigmoid(gp) * mp + jax.nn.sigmoid(ga) * ma
    y = jnp.dot(m.astype(BF16), wo_ref[...], preferred_element_type=F32)
    return _layer_norm(DN_ALPHA * x + y, g_ref[...], b_ref[...])


def _window_means(read, pos, shape_out):
    outs = []
    for g, w in enumerate(POOL_WINDOWS):
        lanes = slice(g * POOL_GROUP, (g + 1) * POOL_GROUP)
        s = read(0, lanes)
        for back in range(1, w):
            s = s + read(back, lanes)
        outs.append((s / jnp.minimum(pos + 1, w).astype(F32)).reshape(shape_out))
    return jnp.concatenate(outs, axis=-1)


def _window_means_doubling(buf_ref, lvl_ref, pos, t):
    g = POOL_GROUP
    lo, hi = LEAD_ROWS, LEAD_ROWS + HIST_ROWS + t
    out0 = LEAD_ROWS + HIST_ROWS
    lvl_ref[0, lo:hi, :] = buf_ref[lo:hi, :] + buf_ref[lo - 1:hi - 1, :]
    lvl_ref[1, lo:hi, g:] = lvl_ref[0, lo:hi, g:] + lvl_ref[0, lo - 2:hi - 2, g:]
    lvl_ref[2, lo:hi, 2 * g:] = lvl_ref[1, lo:hi, 2 * g:] + lvl_ref[1, lo - 4:hi - 4, 2 * g:]
    s16 = lvl_ref[2, out0:out0 + t, 3 * g:] + lvl_ref[2, out0 - 8:out0 - 8 + t, 3 * g:]
    sums = [lvl_ref[k, out0:out0 + t, k * g:(k + 1) * g] for k in range(3)] + [s16]
    return jnp.concatenate([s / jnp.minimum(pos + 1, w).astype(F32) for s, w in zip(sums, POOL_WINDOWS)], axis=-1)


def _mix_body(n_prompt_tiles, tiles_per_seq, with_router, n_src, select,
              u_ref, up_ref, hist_ref, attn_p_ref, attn_s_ref, gp_ref, ga_ref, *rest):
    x_refs, rest = rest[:n_src], rest[n_src:]
    (wgrp_ref, scale_ref, wp_ref, wa_ref, wo_ref, g_ref, b_ref), rest = rest[:7], rest[7:]
    if with_router:
        (wr_hi_ref, wr_lo_ref, br_ref, o_ref, tail_ref, mi_ref, mw_ref, cnt_ref,
         buf_ref, lvl_ref, sbuf_ref, pooled_ref, attn_ref, carry_ref) = rest
    else:
        o_ref, tail_ref, buf_ref, lvl_ref, sbuf_ref, pooled_ref, attn_ref = rest
    t = u_ref.shape[0]
    i = pl.program_id(0)
    u = u_ref[...]

    @pl.when(i == 0)
    def _():
        buf_ref[0:LEAD_ROWS, :] = jnp.zeros((LEAD_ROWS, POOL_WIDTH), F32)
        lvl_ref[:, 0:LEAD_ROWS, :] = jnp.zeros((lvl_ref.shape[0], LEAD_ROWS, POOL_WIDTH), F32)

    @pl.when(i < n_prompt_tiles)
    def _():
        tile = i % tiles_per_seq
        h0 = LEAD_ROWS
        buf_ref[h0:h0 + HIST_ROWS, :] = jnp.where(tile == 0, 0.0, up_ref[...])
        buf_ref[h0 + HIST_ROWS:h0 + HIST_ROWS + t, :] = u
        pos = tile * t + lax.broadcasted_iota(jnp.int32, (t, 1), 0)
        pooled_ref[...] = _window_means_doubling(buf_ref, lvl_ref, pos, t)
        attn_ref[...] = attn_p_ref[...]

        @pl.when(tile == tiles_per_seq - 1)
        def _():
            tail_ref[0] = u[t - HIST_ROWS:, :]

    @pl.when(i == n_prompt_tiles)
    def _():
        streams, hrows, _ = hist_ref.shape
        ts = t // streams
        sbuf_ref[:, 0:hrows, :] = hist_ref[...]
        sbuf_ref[:, hrows:hrows + ts, :] = u.reshape(streams, ts, POOL_WIDTH)
        pos = PAST_LEN + lax.broadcasted_iota(jnp.int32, (1, ts, 1), 1)
        read = lambda back, lanes: sbuf_ref[:, hrows - back:hrows - back + ts, lanes]
        pooled_ref[...] = _window_means(read, pos, (t, POOL_GROUP))
        attn_ref[...] = attn_s_ref[...]

    x1 = _merge(pooled_ref[...], u, attn_ref[...], gp_ref[...].astype(F32), ga_ref[...].astype(F32),
                select(x_refs, i), wgrp_ref, scale_ref, wp_ref, wa_ref, wo_ref, g_ref, b_ref)
    o_ref[...] = x1
    if with_router:
        _route(x1, wr_hi_ref, wr_lo_ref, br_ref, mi_ref, mw_ref, cnt_ref, carry_ref)


def _mix(z32, z16, attn_p, attn_s, x, hist, weights, n_prompt, seq, router=None):
    n = z32.shape[0]
    t = n - n_prompt
    x_srcs, x_specs, select = _row_sources(x, t)
    n_prompt_tiles = n_prompt // t
    streams, hrows, _ = hist.shape
    hist_per_tile = t // HIST_ROWS
    last_p = n_prompt_tiles - 1
    zero = lambda i: (0, 0)
    rows = lambda width: pl.BlockSpec((t, width), lambda i: (i, 0))
    tiles_per_seq = seq // t
    out_shape = [jax.ShapeDtypeStruct((n, D_MODEL), F32),
                 jax.ShapeDtypeStruct((n_prompt // seq, HIST_ROWS, POOL_WIDTH), F32)]
    out_specs = [rows(D_MODEL),
                 pl.BlockSpec((1, HIST_ROWS, POOL_WIDTH), lambda i: (jnp.minimum(i, last_p) // tiles_per_seq, 0, 0))]
    router_specs, router_scratch = [], []
    if router is not None:
        router_specs = [pl.BlockSpec((ROUTE_ROWS, D_MODEL), zero), pl.BlockSpec((ROUTE_ROWS, D_MODEL), zero),
                        pl.BlockSpec((ROUTE_ROWS, 1), zero)]
        per_tile = lambda dtype: jax.ShapeDtypeStruct((n // t, 8, t), dtype)
        out_shape += [per_tile(jnp.int32), per_tile(F32), jax.ShapeDtypeStruct((ROUTE_ROWS, LANES), jnp.int32)]
        out_specs += [pl.BlockSpec((1, 8, t), lambda i: (i, 0, 0)), pl.BlockSpec((1, 8, t), lambda i: (i, 0, 0)),
                      pl.BlockSpec((ROUTE_ROWS, LANES), zero)]
        router_scratch = [pltpu.VMEM((ROUTE_ROWS, LANES), F32)]
    out = pl.pallas_call(
        functools.partial(_mix_body, n_prompt_tiles, seq // t, router is not None, len(x_srcs), select),
        out_shape=out_shape,
        grid=(n_prompt_tiles + 1,),
        in_specs=[
            pl.BlockSpec((t, POOL_WIDTH), lambda i: (i, COL_U)),
            pl.BlockSpec((HIST_ROWS, POOL_WIDTH), lambda i: (jnp.maximum(i * hist_per_tile - 1, 0), COL_U)),
            pl.BlockSpec((streams, hrows, POOL_WIDTH), lambda i: (0, 0, 0)),
            pl.BlockSpec((t, ATTN_WIDTH), lambda i: (jnp.minimum(i, last_p), 0)),
            pl.BlockSpec((t, ATTN_WIDTH), zero),
            pl.BlockSpec((t, D_MODEL), lambda i: (i, COL_GP)),
            pl.BlockSpec((t, D_MODEL), lambda i: (i, COL_GA)),
        ] + x_specs + [
            pl.BlockSpec((POOL_WIDTH, POOL_WIDTH), zero),
            pl.BlockSpec((1, POOL_WIDTH), zero),
            pl.BlockSpec((POOL_WIDTH, D_MODEL), zero),
            pl.BlockSpec((ATTN_WIDTH, D_MODEL), zero),
            pl.BlockSpec((D_MODEL, D_MODEL), zero),
            pl.BlockSpec((1, D_MODEL), zero),
            pl.BlockSpec((1, D_MODEL), zero),
        ] + router_specs,
        out_specs=out_specs,
        scratch_shapes=[
            pltpu.VMEM((LEAD_ROWS + HIST_ROWS + t, POOL_WIDTH), F32),
            pltpu.VMEM((3, LEAD_ROWS + HIST_ROWS + t, POOL_WIDTH), F32),
            pltpu.VMEM((streams, hrows + t // streams, POOL_WIDTH), F32),
            pltpu.VMEM((t, POOL_WIDTH), F32),
            pltpu.VMEM((t, ATTN_WIDTH), F32),
        ] + router_scratch,
        compiler_params=_params(("arbitrary",)),
        name="mix",
    )(z32, z32, hist, attn_p, attn_s, z16, z16, *x_srcs, *weights, *(router or ()))
    return out


def _ffn_dense_body(x_ref, w1_ref, w3_ref, w2_ref, g_ref, b_ref, o_ref):
    sub = x_ref.shape[0] // DENSE_SUBTILES
    for k in range(DENSE_SUBTILES):
        rows = slice(k * sub, (k + 1) * sub)
        x = x_ref[rows, :]
        xb = x.astype(BF16)
        a = jnp.dot(xb, w1_ref[...], preferred_element_type=F32)
        c = jnp.dot(xb, w3_ref[...], preferred_element_type=F32)
        h = (jax.nn.silu(a) * c).astype(BF16)
        f = jnp.dot(h, w2_ref[...], preferred_element_type=F32)
        o_ref[rows, :] = _layer_norm(DN_ALPHA * x + f, g_ref[...], b_ref[...])


def _ffn_dense(x, w1, w3, w2, g, b):
    n = x.shape[0]
    t = _pick(n, (1280, 640, 512, 256))
    zero = lambda i: (0, 0)
    return pl.pallas_call(
        _ffn_dense_body,
        out_shape=jax.ShapeDtypeStruct((n, D_MODEL), F32),
        grid=(n // t,),
        in_specs=[
            pl.BlockSpec((t, D_MODEL), lambda i: (i, 0)),
            pl.BlockSpec((D_MODEL, D_FF), zero),
            pl.BlockSpec((D_MODEL, D_FF), zero),
            pl.BlockSpec((D_FF, D_MODEL), zero),
            pl.BlockSpec((1, D_MODEL), zero),
            pl.BlockSpec((1, D_MODEL), zero),
        ],
        out_specs=pl.BlockSpec((t, D_MODEL), lambda i: (i, 0)),
        compiler_params=_params(("arbitrary",)),
        name="ffn_dense",
    )(x, w1, w3, w2, g, b)


def _route(x, wr_hi_ref, wr_lo_ref, br_ref, mi_ref, mw_ref, cnt_ref, carry_ref):
    t = x.shape[0]

    @pl.when(pl.program_id(0) == 0)
    def _():
        carry_ref[...] = jnp.zeros_like(carry_ref)

    x_hi = x.astype(BF16)
    x_lo = (x - x_hi.astype(F32)).astype(BF16)
    nt = (((1,), (1,)), ((), ()))
    logits = (lax.dot_general(wr_hi_ref[...], x_hi, nt, preferred_element_type=F32)
              + lax.dot_general(wr_hi_ref[...], x_lo, nt, preferred_element_type=F32)
              + lax.dot_general(wr_lo_ref[...], x_hi, nt, preferred_element_type=F32)) + br_ref[...]
    row = lax.broadcasted_iota(jnp.int32, (ROUTE_ROWS, t), 0)
    row_f = row.astype(F32)
    logits = jnp.where(row < N_EXPERTS, logits, -jnp.inf)
    v0 = jnp.max(logits, axis=0, keepdims=True)
    e0 = jnp.min(jnp.where(logits == v0, row_f, float(ROUTE_ROWS)), axis=0, keepdims=True)
    rest = jnp.where(row_f == e0, -jnp.inf, logits)
    v1 = jnp.max(rest, axis=0, keepdims=True)
    e1 = jnp.min(jnp.where(rest == v1, row_f, float(ROUTE_ROWS)), axis=0, keepdims=True)
    ex = jnp.exp(v1 - v0)
    w0 = 1.0 / (1.0 + ex)
    w1 = ex / (1.0 + ex)
    oh0 = (row_f == e0).astype(F32)
    oh1 = (row_f == e1).astype(F32)
    before = (lax.broadcasted_iota(jnp.int32, (t, t), 0) < lax.broadcasted_iota(jnp.int32, (t, t), 1)).astype(BF16)
    pre0 = jnp.dot(oh0.astype(BF16), before, preferred_element_type=F32)
    pre1 = jnp.dot(oh1.astype(BF16), before, preferred_element_type=F32)
    cnt0 = jnp.sum(oh0, axis=1, keepdims=True)
    cnt1 = jnp.sum(oh1, axis=1, keepdims=True)
    carry = carry_ref[:, 0:1]
    rank0 = jnp.sum(oh0 * (carry + pre0), axis=0, keepdims=True)
    rank1 = jnp.sum(oh1 * (carry + cnt0 + pre1), axis=0, keepdims=True)
    carry = jnp.broadcast_to(carry + cnt0 + cnt1, carry_ref.shape)
    carry_ref[...] = carry
    cnt_ref[...] = carry.astype(jnp.int32)
    r8 = lax.broadcasted_iota(jnp.int32, (8, t), 0)
    mi = jnp.where(r8 == 0, e0, jnp.where(r8 == 1, e1, jnp.where(r8 == 2, rank0, jnp.where(r8 == 3, rank1, 0.0))))
    mi_ref[0] = mi.astype(jnp.int32)
    mw_ref[0] = jnp.where(r8 == 0, w0, jnp.where(r8 == 1, w1, 0.0))


def _router_operands(w_r, b_r):
    wr = jnp.zeros((ROUTE_ROWS, D_MODEL), F32).at[:N_EXPERTS].set(w_r.T)
    br = jnp.zeros((ROUTE_ROWS, 1), F32).at[:N_EXPERTS, 0].set(b_r)
    wr_hi = wr.astype(BF16)
    wr_lo = (wr - wr_hi.astype(F32)).astype(BF16)
    return wr_hi, wr_lo, br


def _dispatch_body(dest_ref, last_ref, nused_ref, x_ref, xs_ref, zero_ref, rows_ref, sem_ref, row_sem_ref):
    t = dest_ref.shape[0] // 2
    tm = zero_ref.shape[0]
    n_tiles = xs_ref.shape[0] // tm
    i = pl.program_id(0)

    @pl.when(i == 0)
    def _():
        zero_ref[...] = jnp.zeros_like(zero_ref)

        def fill(tile):
            return pltpu.make_async_copy(zero_ref, xs_ref.at[pl.ds(pl.multiple_of(tile * tm, tm), tm), :],
                                         sem_ref.at[0])

        for wait in (False, True):
            def one(tile, wait=wait):
                fill(tile).wait() if wait else fill(tile).start()

            for e in range(N_EXPERTS):
                @pl.when(last_ref[e] >= 0)
                def _(e=e, one=one):
                    one(last_ref[e])

            def unused(tile, carry, one=one):
                one(tile)
                return carry

            lax.fori_loop(nused_ref[0], n_tiles, unused, 0)

    cur = i % 2
    rows_ref[cur] = x_ref[...]

    def issue(r, carry):
        for slot in range(2):
            d = dest_ref[slot * t + r]
            pltpu.make_async_copy(rows_ref.at[cur, pl.ds(r, 1), :], xs_ref.at[pl.ds(d, 1), :],
                                  row_sem_ref.at[cur]).start(priority=slot)
        return carry

    lax.fori_loop(0, t, issue, 0, unroll=ISSUE_UNROLL)

    def wait_rows(buf):
        pltpu.make_async_copy(xs_ref.at[pl.ds(0, 2 * t), :], xs_ref.at[pl.ds(0, 2 * t), :], row_sem_ref.at[buf]).wait()

    @pl.when(i > 0)
    def _():
        wait_rows(1 - cur)

    @pl.when(i == pl.num_programs(0) - 1)
    def _():
        wait_rows(cur)


def _dispatch(x, dest, last_tile, n_used, n_tiles, t):
    n = x.shape[0]
    tm = EXPERT_TILE
    return pl.pallas_call(
        _dispatch_body,
        out_shape=jax.ShapeDtypeStruct((n_tiles * tm, D_MODEL), F32),
        grid=(n // t,),
        in_specs=[
            pl.BlockSpec((2 * t,), lambda i: (i,), memory_space=pltpu.SMEM),
            pl.BlockSpec(memory_space=pltpu.SMEM),
            pl.BlockSpec(memory_space=pltpu.SMEM),
            pl.BlockSpec((t, D_MODEL), lambda i: (i, 0)),
        ],
        out_specs=pl.BlockSpec(memory_space=pl.ANY),
        scratch_shapes=[pltpu.VMEM((tm, D_MODEL), F32), pltpu.VMEM((2, t, D_MODEL), F32),
                        pltpu.SemaphoreType.DMA((1,)), pltpu.SemaphoreType.DMA((2,))],
        compiler_params=_params(("arbitrary",)),
        name="dispatch",
    )(dest, last_tile, n_used, x)


def _experts_body(te_ref, first_ref, next_ref, slot_ref, nused_ref, x_ref, w1_hbm, w3_hbm, w2_hbm, o_ref,
                  wb1_ref, wb3_ref, wb2_ref, st1_ref, st3_ref, st2_ref, sem_ref):
    i = pl.program_id(0)

    def piece_copies(e, c):
        buf = c % 2
        cols = pl.ds(c * FF_CHUNK, FF_CHUNK)
        return (pltpu.make_async_copy(w1_hbm.at[e, :, cols], st1_ref.at[buf], sem_ref.at[buf, 0]),
                pltpu.make_async_copy(w3_hbm.at[e, :, cols], st3_ref.at[buf], sem_ref.at[buf, 1]),
                pltpu.make_async_copy(w2_hbm.at[e, cols, :], st2_ref.at[buf], sem_ref.at[buf, 2]))

    def start_piece(e, c):
        for cp in piece_copies(e, c):
            cp.start()

    def finish_piece(e, c, slot):
        for cp in piece_copies(e, c):
            cp.wait()
        cols = slice(c * FF_CHUNK, (c + 1) * FF_CHUNK)
        wb1_ref[slot, :, cols] = st1_ref[c % 2].astype(BF16)
        wb3_ref[slot, :, cols] = st3_ref[c % 2].astype(BF16)
        wb2_ref[slot, cols, :] = st2_ref[c % 2].astype(BF16)

    def ff_piece(xb, slot, c):
        cols = slice(c * FF_CHUNK, (c + 1) * FF_CHUNK)
        a = jnp.dot(xb, wb1_ref[slot, :, cols], preferred_element_type=F32)
        g = jnp.dot(xb, wb3_ref[slot, :, cols], preferred_element_type=F32)
        h = (jax.nn.silu(a) * g).astype(BF16)
        return jnp.dot(h, wb2_ref[slot, cols, :], preferred_element_type=F32)

    @pl.when(i == 0)
    def _():
        e, slot = te_ref[0], slot_ref[0]
        start_piece(e, 0)
        for c in range(FF_PIECES):
            if c + 1 < FF_PIECES:
                start_piece(e, c + 1)
            finish_piece(e, c, slot)

    live = i < nused_ref[0]
    prefetch = jnp.logical_and(live, jnp.logical_and(first_ref[i] == 1, next_ref[i] >= 0))

    @pl.when(jnp.logical_and(live, jnp.logical_not(prefetch)))
    def _():
        xb = x_ref[...].astype(BF16)
        slot = slot_ref[i]
        f = ff_piece(xb, slot, 0)
        for c in range(1, FF_PIECES):
            f = f + ff_piece(xb, slot, c)
        o_ref[...] = f

    @pl.when(prefetch)
    def _():
        xb = x_ref[...].astype(BF16)
        slot, e_next = slot_ref[i], next_ref[i]
        start_piece(e_next, 0)
        f = None
        for c in range(FF_PIECES):
            if c + 1 < FF_PIECES:
                start_piece(e_next, c + 1)
            part = ff_piece(xb, slot, c)
            f = part if f is None else f + part
            finish_piece(e_next, c, 1 - slot)
        o_ref[...] = f

    @pl.when(jnp.logical_not(live))
    def _():
        o_ref[...] = jnp.zeros_like(o_ref)


def _experts(xs, tile_expert, first, nxt, slot, n_used, w1, w3, w2):
    tm = EXPERT_TILE
    n_tiles = xs.shape[0] // tm
    hbm = pl.BlockSpec(memory_space=pl.ANY)
    return pl.pallas_call(
        _experts_body,
        out_shape=jax.ShapeDtypeStruct(xs.shape, F32),
        grid_spec=pltpu.PrefetchScalarGridSpec(
            num_scalar_prefetch=5,
            grid=(n_tiles,),
            in_specs=[pl.BlockSpec((tm, D_MODEL), lambda i, te, fi, nx, sl, nu: (jnp.minimum(i, nu[0] - 1), 0)),
                      hbm, hbm, hbm],
            out_specs=pl.BlockSpec((tm, D_MODEL), lambda i, te, fi, nx, sl, nu: (i, 0)),
            scratch_shapes=[
                pltpu.VMEM((2, D_MODEL, D_FF), BF16), pltpu.VMEM((2, D_MODEL, D_FF), BF16),
                pltpu.VMEM((2, D_FF, D_MODEL), BF16),
                pltpu.VMEM((2, D_MODEL, FF_CHUNK), F32), pltpu.VMEM((2, D_MODEL, FF_CHUNK), F32),
                pltpu.VMEM((2, FF_CHUNK, D_MODEL), F32),
                pltpu.SemaphoreType.DMA((2, 3)),
            ],
        ),
        compiler_params=_params(("arbitrary",), EXPERTS_VMEM_LIMIT),
        name="experts",
    )(tile_expert, first, nxt, slot, n_used, xs, w1, w3, w2)


def _combine_body(n_first, dest_ref, next_ref, x_ref, mw_ref, g_ref, b_ref, ys_ref, *rest):
    out_refs, (y0_ref, y1_ref, sem_ref) = rest[:-3], rest[-3:]
    t = x_ref.shape[0]
    i = pl.program_id(0)
    cur = i % 2

    def gather(idx_ref, buf):
        def issue(r, carry):
            pltpu.make_async_copy(ys_ref.at[pl.ds(idx_ref[r], 1), :], y0_ref.at[buf, pl.ds(r, 1), :],
                                  sem_ref.at[buf]).start(priority=0)
            pltpu.make_async_copy(ys_ref.at[pl.ds(idx_ref[t + r], 1), :], y1_ref.at[buf, pl.ds(r, 1), :],
                                  sem_ref.at[buf]).start(priority=1)
            return carry

        lax.fori_loop(0, t, issue, 0, unroll=ISSUE_UNROLL)

    @pl.when(i == 0)
    def _():
        gather(dest_ref, 0)

    @pl.when(i + 1 < pl.num_programs(0))
    def _():
        gather(next_ref, 1 - cur)

    pltpu.make_async_copy(ys_ref.at[pl.ds(0, t), :], y0_ref.at[cur], sem_ref.at[cur]).wait()
    pltpu.make_async_copy(ys_ref.at[pl.ds(0, t), :], y1_ref.at[cur], sem_ref.at[cur]).wait()
    mw = mw_ref[...]
    f = mw[:, 0:1] * y0_ref[cur] + mw[:, 1:2] * y1_ref[cur]
    out = _layer_norm(DN_ALPHA * x_ref[...] + f, g_ref[...], b_ref[...])
    if len(out_refs) == 1:
        out_refs[0][...] = out
    else:
        @pl.when(i < n_first)
        def _():
            out_refs[0][...] = out

        @pl.when(i == n_first)
        def _():
            out_refs[1][...] = out


def _combine(x, ys, dest, mw, g, b, t, split):
    n = x.shape[0]
    zero = lambda i: (0, 0)
    n_first = n // t - 1
    last = n // t - 1
    if split:
        out_shape = (jax.ShapeDtypeStruct((n - t, D_MODEL), F32), jax.ShapeDtypeStruct((t, D_MODEL), F32))
        out_specs = (pl.BlockSpec((t, D_MODEL), lambda i: (jnp.minimum(i, n_first - 1), 0)),
                     pl.BlockSpec((t, D_MODEL), zero))
    else:
        out_shape = jax.ShapeDtypeStruct((n, D_MODEL), F32)
        out_specs = pl.BlockSpec((t, D_MODEL), lambda i: (i, 0))
    return pl.pallas_call(
        functools.partial(_combine_body, n_first),
        out_shape=out_shape,
        grid=(n // t,),
        in_specs=[
            pl.BlockSpec((2 * t,), lambda i: (i,), memory_space=pltpu.SMEM),
            pl.BlockSpec((2 * t,), lambda i: (jnp.minimum(i + 1, last),), memory_space=pltpu.SMEM),
            pl.BlockSpec((t, D_MODEL), lambda i: (i, 0)),
            pl.BlockSpec((t, 2), lambda i: (i, 0)),
            pl.BlockSpec((1, D_MODEL), zero),
            pl.BlockSpec((1, D_MODEL), zero),
            pl.BlockSpec(memory_space=pl.ANY),
        ],
        out_specs=out_specs,
        scratch_shapes=[pltpu.VMEM((2, t, D_MODEL), F32), pltpu.VMEM((2, t, D_MODEL), F32),
                        pltpu.SemaphoreType.DMA((2,))],
        compiler_params=_params(("arbitrary",)),
        name="combine",
    )(dest, dest, x, mw, g, b, ys)


def _moe(x, mi, mw, cnt, w1, w3, w2, g, b, t, split):
    n = x.shape[0]
    tm = EXPERT_TILE
    n_tiles = (2 * n + N_EXPERTS * (tm - 1)) // tm
    counts = cnt[:N_EXPERTS, 0]
    tiles = (counts + tm - 1) // tm
    tile_end = jnp.cumsum(tiles)
    start = (tile_end - tiles) * tm
    n_used = tile_end[-1:].astype(jnp.int32)
    tile_expert = jnp.sum((tile_end[None, :] <= jnp.arange(n_tiles)[:, None]).astype(jnp.int32), axis=1)
    tile_expert = jnp.minimum(tile_expert, N_EXPERTS - 1)
    last_tile = jnp.where(tiles > 0, tile_end - 1, -1).astype(jnp.int32)
    has = tiles > 0
    next_of = [jnp.int32(-1)] * N_EXPERTS
    for e in range(N_EXPERTS - 2, -1, -1):
        next_of[e] = jnp.where(has[e + 1], e + 1, next_of[e + 1])
    per_expert = jnp.stack([tile_end - tiles, jnp.stack(next_of), (jnp.cumsum(has) - 1) % 2]).astype(jnp.int32)
    onehot = (tile_expert[None, :, None] == jnp.arange(N_EXPERTS)[None, None, :]).astype(jnp.int32)
    first_tile, nxt, slot = jnp.sum(onehot * per_expert[:, None, :], axis=2)
    first = (first_tile == jnp.arange(n_tiles)).astype(jnp.int32)
    experts = mi[:, 0:2, :]
    group_start = sum(jnp.where(experts == e, start[e], 0) for e in range(N_EXPERTS))
    dest = (group_start + mi[:, 2:4, :]).astype(jnp.int32).reshape(2 * n)
    mw = jnp.swapaxes(mw[:, 0:2, :], 1, 2).reshape(n, 2)
    xs = _dispatch(x, dest, last_tile, n_used, n_tiles, t)
    ys = _experts(xs, tile_expert.astype(jnp.int32), first, nxt, slot, n_used, w1, w3, w2)
    return _combine(x, ys, dest, mw, g, b, t, split)


def _block_diag(w):
    layers, g, c, _ = w.shape
    out = jnp.zeros((layers, g * c, g * c), w.dtype)
    for i in range(g):
        out = out.at[:, i * c:(i + 1) * c, i * c:(i + 1) * c].set(w[:, i])
    return out


def _bias_pairs(table, t_q, n_keys, offset):
    layers = table.shape[0]
    hi = t_q - 1 + offset
    span = t_q - 1 + n_keys
    cols = np.clip(hi - np.arange(span), -REL_CLIP, REL_CLIP) + REL_CLIP
    rev = table.reshape(layers * N_HEADS, -1)[:, cols].astype(F32)
    skew = jnp.tile(rev, (1, t_q + 1))[:, :t_q * (span + 1)].reshape(layers * N_HEADS, t_q, span + 1)
    slab = skew[:, ::-1, :n_keys]
    return slab.reshape(layers, HEAD_PAIRS, 2 * t_q, n_keys)


def _prompt_bias_slabs(table):
    bias = _bias_pairs(table, CHUNK, BAND, ATTN_REACH)
    col = np.arange(BAND)[None, :]
    first_valid = np.concatenate([[0], (LEFT_CHUNKS - np.arange(LEFT_CHUNKS)) * CHUNK])[:, None]
    valid = jnp.asarray(col >= first_valid)
    return jnp.where(valid[None, :, None, None, :], bias[:, None], NEG_INF)


def kernel(x_prompt, x_sample, cache_k, cache_v, state_pool, w_in, b_in, w_pool_grp, pool_scale,
           rel_table, w_pool_br, w_attn_br, w_out, ln1_g, ln1_b, ln2_g, ln2_b,
           w1_dense, w3_dense, w2_dense, w_router, b_router, w1_exp, w3_exp, w2_exp):
    bp, tp, d = x_prompt.shape
    bs, ts, _ = x_sample.shape
    n_p, n_s = bp * tp, bs * ts
    n = n_p + n_s
    depth = w_in.shape[0]
    keep_s = cache_k.shape[2]
    keep_p = min(ATTN_REACH, tp)

    x = (x_prompt.reshape(n_p, d), x_sample.reshape(n_s, d))
    ck = cache_k.reshape(depth, bs, keep_s, ATTN_WIDTH)
    cv = cache_v.reshape(depth, bs, keep_s, ATTN_WIDTH)
    hist = jnp.pad(state_pool, ((0, 0), (0, 0), (HIST_ROWS - POOL_HIST, 0), (0, 0)))
    assert tp % ATTN_REACH == 0 and keep_p == ATTN_REACH and ts >= POOL_HIST

    rows = lambda v: v[:, None, :].astype(F32)
    w_in_bf, b_in_r = w_in.astype(BF16), rows(b_in)
    wgrp, wp_bf, wa_bf, wo_bf = (a.astype(BF16) for a in (_block_diag(w_pool_grp), w_pool_br, w_attn_br, w_out))
    scale_r, g1, b1, g2, b2 = (rows(a) for a in (pool_scale, ln1_g, ln1_b, ln2_g, ln2_b))
    dense_bf = [a.astype(BF16) for a in (w1_dense, w3_dense, w2_dense)]
    bias_p = _prompt_bias_slabs(rel_table)
    bias_s = _bias_pairs(rel_table, ts, keep_s + ts, keep_s)

    k_buf = v_buf = None
    kp_new, vp_new, pp_new, ps_new = [], [], [], []
    for l in range(depth):
        weights = (wgrp[l], scale_r[l], wp_bf[l], wa_bf[l], wo_bf[l], g1[l], b1[l])
        z32, z16 = _in_proj(x, w_in_bf[l], b_in_r[l], n_s)
        attn_p, k_tail, v_tail = _attn_prompt(z32, z16, bias_p[l], bp, tp)
        attn_s, k_buf, v_buf = _attn_sample(z32, z16, ck, cv, bias_s[l], l, n_p, bs, ts, k_buf, v_buf)
        j = l // 2
        last = l == depth - 1
        if l % 2 == 0:
            x1, u_tail = _mix(z32, z16, attn_p, attn_s, x, hist[l], weights, n_p, tp)
            x = _ffn_dense(x1, dense_bf[0][j], dense_bf[1][j], dense_bf[2][j], g2[l], b2[l])
            if last:
                x = (x[:n_p], x[n_p:])
        else:
            x1, u_tail, mi, mw, cnt = _mix(z32, z16, attn_p, attn_s, x, hist[l], weights, n_p, tp,
                                           router=_router_operands(w_router[j], b_router[j]))
            x = _moe(x1, mi, mw, cnt, w1_exp[j], w3_exp[j], w2_exp[j], g2[l], b2[l], n_s, split=last)

        kp_new.append(k_tail.reshape(bp, keep_p, N_HEADS, HEAD_DIM))
        vp_new.append(v_tail.reshape(bp, keep_p, N_HEADS, HEAD_DIM))
        pp_new.append(u_tail[:, HIST_ROWS - POOL_HIST:])
        ps_new.append(z32[n_p:, :POOL_WIDTH].reshape(bs, ts, POOL_WIDTH)[:, ts - POOL_HIST:])

    shape_s = (depth, bs, keep_s, N_HEADS, HEAD_DIM)
    return (x[0].reshape(bp, tp, d), x[1].reshape(bs, ts, d),
            jnp.stack(kp_new), jnp.stack(vp_new), jnp.stack(pp_new),
            k_buf.reshape(shape_s), v_buf.reshape(shape_s), jnp.stack(ps_new))
```
